```python
import jax, jax.numpy as jnp
from jax import lax
import numpy as np

D_MODEL = 1024
BATCH = 2
SEQ = 8192
DEPTH = 1

ATTN_GROUPS = ((128, 1), (512, 4), (2048, 16))
ATTN_HEADS_PER_GROUP = 8
ATTN_HEAD_DIM = 64
ATTN_BLOCK = 128
N_ATTN_HEADS = len(ATTN_GROUPS) * ATTN_HEADS_PER_GROUP
ATTN_QKV_WIDTH = 3 * N_ATTN_HEADS * ATTN_HEAD_DIM
ATTN_OUT = ATTN_HEADS_PER_GROUP * ATTN_HEAD_DIM

GLA_HEADS = 4
GLA_DK = D_MODEL // 2
GLA_DV = D_MODEL
GLA_HK = GLA_DK // GLA_HEADS
GLA_HV = GLA_DV // GLA_HEADS
GLA_GATE_RANK = 16
GLA_TAU = 16.0
GLA_CHUNK = 64

D_FF = 4 * D_MODEL

N_BRANCHES = 2
EPS = 1e-6

_IN_SIZES = (ATTN_QKV_WIDTH, GLA_DK, GLA_DK, GLA_DV, GLA_DV, GLA_GATE_RANK, N_BRANCHES * D_MODEL)
D_IN = sum(_IN_SIZES)
_IN_OFFSETS = tuple(int(v) for v in np.cumsum(_IN_SIZES)[:-1])

kernel_name = 'hybrid_dilated_attn_gla_gated_block'


def rms_norm(x, g):
    xf = x.astype(jnp.float32)
    y = xf * lax.rsqrt(jnp.mean(jnp.square(xf), axis=-1, keepdims=True) + EPS)
    return (y * g.astype(jnp.float32)).astype(x.dtype)


def _dilated_group(q, k, v, window, dilation):
    b, s, h, e = q.shape
    n_back = window // dilation
    blk = ATTN_BLOCK
    assert n_back <= blk
    seg = dilation * blk
    s_pad = -(-s // seg) * seg
    L = s_pad // dilation
    nb = L // blk

    def to_blocks(t):
        t = jnp.pad(t, ((0, 0), (0, s_pad - s), (0, 0), (0, 0)))
        t = t.reshape(b, L, dilation, h, e).transpose(0, 2, 3, 1, 4)
        return t.reshape(b, dilation, h, nb, blk, e)

    def with_prev(t):
        prev = jnp.pad(t[:, :, :, :-1], ((0, 0), (0, 0), (0, 0), (1, 0), (0, 0), (0, 0)))
        return jnp.concatenate([prev, t], axis=4)

    qb, kb, vb = to_blocks(q), to_blocks(k), to_blocks(v)
    kw, vw = with_prev(kb), with_prev(vb)
    scores = jnp.einsum('bdhnqe,bdhnke->bdhnqk', qb, kw).astype(jnp.float32) * (e ** -0.5)
    qi = jnp.arange(blk)[:, None]
    ki = jnp.arange(2 * blk)[None, :]
    dist = blk + qi - ki
    first = (jnp.arange(nb) == 0)[:, None, None]
    valid = (dist >= 0) & (dist <= n_back) & ~(first & (ki < blk))
    scores = jnp.where(valid, scores, -jnp.inf)
    mx = jnp.max(scores, axis=-1, keepdims=True)
    p = jnp.exp(scores - mx)
    den = jnp.sum(p, axis=-1, keepdims=True)
    o = jnp.einsum('bdhnqk,bdhnke->bdhnqe', p, vw.astype(jnp.float32)) / den
    lse = (mx + jnp.log(den))[..., 0]
    o = o.reshape(b, dilation, h, L, e).transpose(0, 3, 1, 2, 4).reshape(b, s_pad, h, e)[:, :s]
    lse = lse.reshape(b, dilation, h, L).transpose(0, 3, 1, 2).reshape(b, s_pad, h)[:, :s]
    return o, lse


def dilated_attention(q, k, v, gq, gk):
    b, s = q.shape[:2]
    q = rms_norm(q, gq)
    k = rms_norm(k, gk)
    outs, lses = [], []
    for gi, (window, dilation) in enumerate(ATTN_GROUPS):
        sl = slice(gi * ATTN_HEADS_PER_GROUP, (gi + 1) * ATTN_HEADS_PER_GROUP)
        o, l = _dilated_group(q[:, :, sl], k[:, :, sl], v[:, :, sl], window, dilation)
        outs.append(o)
        lses.append(l)
    wts = jax.nn.softmax(jnp.stack(lses, axis=0), axis=0)
    o = jnp.sum(wts[..., None] * jnp.stack(outs, axis=0), axis=0)
    return o.reshape(b, s, ATTN_OUT).astype(v.dtype)


def gated_linear_attention(q, k, v, log_a):
    b, s, h, dk = q.shape
    dv = v.shape[-1]
    c = GLA_CHUNK
    n = s // c

    def chunks(t):
        return t.reshape(b, n, c, h, t.shape[-1]).transpose(1, 0, 3, 2, 4).astype(jnp.float32)

    qc = chunks(q) * (dk ** -0.5)
    kc, vc, ac = chunks(k), chunks(v), chunks(log_a)
    causal = jnp.tril(jnp.ones((c, c), dtype=bool))[:, :, None]

    def step(state, inp):
        qt, kt, vt, at = inp
        bcum = jnp.cumsum(at, axis=2)
        o_inter = jnp.einsum('bhtk,bhkv->bhtv', qt * jnp.exp(bcum), state)
        diff = bcum[:, :, :, None, :] - bcum[:, :, None, :, :]
        decay = jnp.exp(jnp.where(causal, diff, -jnp.inf))
        attn = jnp.einsum('bhtk,bhsk,bhtsk->bhts', qt, kt, decay)
        o_intra = jnp.einsum('bhts,bhsv->bhtv', attn, vt)
        blast = bcum[:, :, -1:, :]
        state = (jnp.exp(blast[:, :, 0, :])[..., None] * state
                 + jnp.einsum('bhsk,bhsv->bhkv', kt * jnp.exp(blast - bcum), vt))
        return state, o_inter + o_intra

    state0 = jnp.zeros((b, h, dk, dv), jnp.float32)
    _, o = lax.scan(step, state0, (qc, kc, vc, ac))
    return o.transpose(1, 0, 3, 2, 4).reshape(b, s, h, dv).astype(v.dtype)


def setup_inputs(seed: int = 0) -> dict:
    key = jax.random.key(seed)
    ks = jax.random.split(key, 16)
    f32 = jnp.float32

    def dense(k, fan_in, fan_out):
        return jax.random.normal(k, (DEPTH, fan_in, fan_out), f32) * fan_in ** -0.5

    def gain(k, n):
        return 1.0 + 0.02 * jax.random.normal(k, (DEPTH, n), f32)

    return {
        'x': jax.random.normal(ks[0], (BATCH, SEQ, D_MODEL), f32),
        'norm1_g': gain(ks[1], D_MODEL),
        'w_in': dense(ks[2], D_MODEL, D_IN),
        'attn_q_norm_g': gain(ks[3], ATTN_HEAD_DIM),
        'attn_k_norm_g': gain(ks[4], ATTN_HEAD_DIM),
        'gla_gate_up': dense(ks[5], GLA_GATE_RANK, GLA_DK),
        'gla_gate_bias': 0.1 * jax.random.normal(ks[6], (DEPTH, GLA_DK), f32),
        'gla_out_norm_g': gain(ks[7], GLA_HV),
        'branch_gate_bias': 0.1 * jax.random.normal(ks[8], (DEPTH, N_BRANCHES * D_MODEL), f32),
        'w_attn_branch': dense(ks[9], ATTN_OUT, D_MODEL),
        'w_gla_branch': dense(ks[10], GLA_DV, D_MODEL),
        'w_out': dense(ks[11], D_MODEL, D_MODEL),
        'norm2_g': gain(ks[12], D_MODEL),
        'w_ff_up': dense(ks[13], D_MODEL, D_FF),
        'w_ff_down': dense(ks[14], D_FF, D_MODEL),
    }


def reference(x, norm1_g, w_in, attn_q_norm_g, attn_k_norm_g, gla_gate_up, gla_gate_bias,
              gla_out_norm_g, branch_gate_bias, w_attn_branch, w_gla_branch, w_out,
              norm2_g, w_ff_up, w_ff_down):
    b, s, _ = x.shape
    for l in range(DEPTH):
        h = rms_norm(x, norm1_g[l])
        proj = jnp.einsum('bsd,de->bse', h, w_in[l])
        p_attn, p_q, p_k, p_v, p_r, p_a, p_gate = jnp.split(proj, _IN_OFFSETS, axis=-1)

        qkv = p_attn.reshape(b, s, 3, N_ATTN_HEADS, ATTN_HEAD_DIM)
        o_attn = dilated_attention(qkv[:, :, 0], qkv[:, :, 1], qkv[:, :, 2],
                                   attn_q_norm_g[l], attn_k_norm_g[l])

        gate_logits = (p_a @ gla_gate_up[l] + gla_gate_bias[l]).astype(jnp.float32)
        log_a = jax.nn.log_sigmoid(gate_logits) / GLA_TAU
        o_gla = gated_linear_attention(
            p_q.reshape(b, s, GLA_HEADS, GLA_HK),
            p_k.reshape(b, s, GLA_HEADS, GLA_HK),
            p_v.reshape(b, s, GLA_HEADS, GLA_HV),
            log_a.reshape(b, s, GLA_HEADS, GLA_HK))
        o_gla = rms_norm(o_gla, gla_out_norm_g[l]).reshape(b, s, GLA_DV) * jax.nn.silu(p_r)

        gates = jax.nn.sigmoid(p_gate + branch_gate_bias[l]).reshape(b, s, N_BRANCHES, D_MODEL)
        mixed = (gates[:, :, 0] * (o_attn @ w_attn_branch[l])
                 + gates[:, :, 1] * (o_gla @ w_gla_branch[l]))
        x = x + mixed @ w_out[l]

        h2 = rms_norm(x, norm2_g[l])
        x = x + jnp.square(jax.nn.relu(h2 @ w_ff_up[l])) @ w_ff_down[l]
    return x
```

```python
import functools

import numpy as np
import jax
import jax.numpy as jnp
from jax import lax
from jax.experimental import pallas as pl
from jax.experimental.pallas import tpu as pltpu

F32 = jnp.float32
BF16 = jnp.bfloat16

D_MODEL = 1024
ATTN_GROUPS = ((128, 1), (512, 4), (2048, 16))
N_GROUPS = len(ATTN_GROUPS)
HEADS_PER_GROUP = 8
HEAD_DIM = 64
ATTN_BLOCK = 128
GROUP_WIDTH = HEADS_PER_GROUP * HEAD_DIM
ATTN_WIDTH = 3 * N_GROUPS * GROUP_WIDTH

GLA_HEADS = 4
GLA_DK = 512
GLA_DV = 1024
GLA_HK = GLA_DK // GLA_HEADS
GLA_HV = GLA_DV // GLA_HEADS
GLA_RANK = 16
GLA_TAU = 16.0
GLA_CHUNK = 64
GLA_SUB = 16
GLA_STEP = 256

D_FF = 4 * D_MODEL
EPS = 1e-6

LANES = 128
MXU_DIM = 256
VMEM_LIMIT_BYTES = 56 * 1024 * 1024

_ORIG_SIZES = (ATTN_WIDTH, GLA_DK, GLA_DK, GLA_DV, GLA_DV, GLA_RANK, 2 * D_MODEL)
_ORIG_OFF = tuple(int(v) for v in np.cumsum((0,) + _ORIG_SIZES))
O_ATTN, O_GQ, O_GK, O_GV, O_GR, O_PA, O_GATE = _ORIG_OFF[:7]

P_GATE = 0
P_GV = P_GATE + 2 * D_MODEL
P_GR = P_GV + GLA_DV
P_GQ = P_GR + GLA_DV
P_GK = P_GQ + GLA_DK
P_ATTN = P_GK + GLA_DK
P_WIDTH = P_ATTN + ATTN_WIDTH
PROJ_TN = 512
N_PROJ_TILES = P_WIDTH // PROJ_TN
ATTN_TILE0 = P_ATTN // PROJ_TN
N_QK_TILES = 2 * N_GROUPS * GROUP_WIDTH // PROJ_TN
PA_PAD = LANES


def _proj_kernel(x_ref, g1_ref, w_ref, wpa_ref, qkg_ref, bd_ref, o_ref, pa_ref, h_ref):
    j = pl.program_id(1)

    @pl.when(j == 0)
    def _():
        x = x_ref[...]
        ms = jnp.mean(x * x, axis=-1, keepdims=True)
        h = (x * lax.rsqrt(ms + EPS) * g1_ref[...]).astype(BF16)
        h_ref[...] = h
        pa_ref[...] = jnp.dot(h, wpa_ref[...], preferred_element_type=F32)

    acc = jnp.dot(h_ref[...], w_ref[...], preferred_element_type=F32)
    is_qk = jnp.logical_and(j >= ATTN_TILE0, j < ATTN_TILE0 + N_QK_TILES)

    @pl.when(is_qk)
    def _():
        sq = (acc * acc).astype(BF16)
        bd = bd_ref[...]
        ss = jnp.concatenate(
            [jnp.dot(sq[:, c * MXU_DIM:(c + 1) * MXU_DIM], bd, preferred_element_type=F32)
             for c in range(PROJ_TN // MXU_DIM)], axis=1)
        y = acc * lax.rsqrt(ss * (1.0 / HEAD_DIM) + EPS) * qkg_ref[0]
        o_ref[...] = y.astype(BF16)

    @pl.when(jnp.logical_not(is_qk))
    def _():
        o_ref[...] = acc.astype(BF16)


def _proj_call(x2d, g1, w_re, w_pa, qk_gain, bd, tm):
    t = x2d.shape[0]
    return pl.pallas_call(
        _proj_kernel,
        grid=(t // tm, N_PROJ_TILES),
        in_specs=[
            pl.BlockSpec((tm, D_MODEL), lambda i, j: (i, 0)),
            pl.BlockSpec((1, D_MODEL), lambda i, j: (0, 0)),
            pl.BlockSpec((D_MODEL, PROJ_TN), lambda i, j: (0, j)),
            pl.BlockSpec((D_MODEL, PA_PAD), lambda i, j: (0, 0)),
            pl.BlockSpec((1, 1, PROJ_TN),
                         lambda i, j: (jnp.clip(j - ATTN_TILE0, 0, N_QK_TILES - 1), 0, 0)),
            pl.BlockSpec((MXU_DIM, MXU_DIM), lambda i, j: (0, 0)),
        ],
        out_specs=[
            pl.BlockSpec((tm, PROJ_TN), lambda i, j: (i, j)),
            pl.BlockSpec((tm, PA_PAD), lambda i, j: (i, 0)),
        ],
        out_shape=[
            jax.ShapeDtypeStruct((t, P_WIDTH), BF16),
            jax.ShapeDtypeStruct((t, PA_PAD), F32),
        ],
        scratch_shapes=[pltpu.VMEM((tm, D_MODEL), BF16)],
        compiler_params=pltpu.CompilerParams(
            dimension_semantics=("parallel", "arbitrary"),
            vmem_limit_bytes=VMEM_LIMIT_BYTES),
        name="proj",
    )(x2d, g1, w_re, w_pa, qk_gain, bd)


def _attn_kernel(q_ref, kp_ref, kc_ref, vp_ref, vc_ref, o_ref, lse_ref):
    n = pl.program_id(2)
    blk = ATTN_BLOCK
    qi = lax.broadcasted_iota(jnp.int32, (blk, 2 * blk), 0)
    ki = lax.broadcasted_iota(jnp.int32, (blk, 2 * blk), 1)
    valid = (ki >= qi) & (ki <= qi + blk) & ((ki >= blk) | (n > 0))
    lane = lax.broadcasted_iota(jnp.int32, (blk, LANES), 1)
    first_head = lane < HEAD_DIM

    for pair in range(GROUP_WIDTH // LANES):
        sl = slice(pair * LANES, (pair + 1) * LANES)
        q = q_ref[0, :, sl]
        k = jnp.concatenate([kp_ref[0, :, sl], kc_ref[0, :, sl]], axis=0)
        v = jnp.concatenate([vp_ref[0, :, sl], vc_ref[0, :, sl]], axis=0)
        zero = jnp.zeros_like(q)
        res = []
        for sel in (first_head, jnp.logical_not(first_head)):
            qm = jnp.where(sel, q, zero)
            s = lax.dot_general(qm, k, (((1,), (1,)), ((), ())), preferred_element_type=F32)
            s = jnp.where(valid, s, -jnp.inf)
            m = jnp.max(s, axis=-1, keepdims=True)
            p = jnp.exp(s - m)
            l = jnp.sum(p, axis=-1, keepdims=True)
            pv = jnp.dot(p.astype(BF16), v, preferred_element_type=F32)
            res.append((pv / l, m + jnp.log(l)))
        o = jnp.where(first_head, res[0][0], res[1][0])
        lse = jnp.where(first_head, res[0][1], res[1][1])
        o_ref[0, :, sl] = o.astype(BF16)
        lse_ref[0, :, sl] = lse


def _attn_call(proj3d, group, dilation):
    b, s, _ = proj3d.shape
    d = dilation
    blk = ATTN_BLOCK
    assert s % (d * blk) == 0
    sub_len = s // d
    nb = sub_len // blk
    view = proj3d.reshape(b, sub_len, d * P_WIDTH)
    tiles_per_tok = P_WIDTH // GROUP_WIDTH
    qt = ATTN_TILE0 + group
    kt = ATTN_TILE0 + N_GROUPS + group
    vt = ATTN_TILE0 + 2 * N_GROUPS + group

    def cur(tile):
        return lambda bi, r, n: (bi, n, r * tiles_per_tok + tile)

    def prev(tile):
        return lambda bi, r, n: (bi, jnp.maximum(n - 1, 0), r * tiles_per_tok + tile)

    blk_shape = (1, blk, GROUP_WIDTH)
    o, lse = pl.pallas_call(
        _attn_kernel,
        grid=(b, d, nb),
        in_specs=[
            pl.BlockSpec(blk_shape, cur(qt)),
            pl.BlockSpec(blk_shape, prev(kt)),
            pl.BlockSpec(blk_shape, cur(kt)),
            pl.BlockSpec(blk_shape, prev(vt)),
            pl.BlockSpec(blk_shape, cur(vt)),
        ],
        out_specs=[
            pl.BlockSpec(blk_shape, lambda bi, r, n: (bi, n, r)),
            pl.BlockSpec(blk_shape, lambda bi, r, n: (bi, n, r)),
        ],
        out_shape=[
            jax.ShapeDtypeStruct((b, sub_len, d * GROUP_WIDTH), BF16),
            jax.ShapeDtypeStruct((b, sub_len, d * GROUP_WIDTH), F32),
        ],
        compiler_params=pltpu.CompilerParams(
            dimension_semantics=("parallel", "parallel", "arbitrary"),
            vmem_limit_bytes=VMEM_LIMIT_BYTES),
        name=f"attn_d{d}",
    )(view, view, view, view, view)
    return o.reshape(b * s, GROUP_WIDTH), lse.reshape(b * s, GROUP_WIDTH)


def _gla_kernel(q_ref, k_ref, v_ref, r_ref, pa_ref, u_ref, bias_ref, gn_ref, ltri_ref, e_ref,
                o_ref, st_ref, b_scr, qs_scr, kf_scr, pcat_scr):
    c_len, sub = GLA_CHUNK, GLA_SUB

    @pl.when(pl.program_id(2) == 0)
    def _():
        st_ref[...] = jnp.zeros_like(st_ref)

    logits = jnp.dot(pa_ref[...].astype(BF16), u_ref[...], preferred_element_type=F32) + bias_ref[...]
    log_sig = jnp.minimum(logits, 0.0) - jnp.log1p(jnp.exp(-jnp.abs(logits)))
    la = log_sig * (1.0 / GLA_TAU)

    hi = la.astype(BF16)
    r1 = la - hi.astype(F32)
    mid = r1.astype(BF16)
    lo = (r1 - mid.astype(F32)).astype(BF16)
    ltri = ltri_ref[...]
    b = (jnp.dot(ltri, hi, preferred_element_type=F32)
         + jnp.dot(ltri, mid, preferred_element_type=F32)
         + jnp.dot(ltri, lo, preferred_element_type=F32))

    qs = q_ref[...].astype(F32) * (GLA_HK ** -0.5)
    kf = k_ref[...].astype(F32)
    b_scr[...] = b
    qs_scr[...] = qs
    kf_scr[...] = kf

    half = sub // 2
    tl = lax.broadcasted_iota(jnp.int32, (half, GLA_HK), 0)
    zeros_half = jnp.zeros((half, GLA_HK), F32)
    for blk_i in range(GLA_STEP // sub):
        r0 = blk_i * sub
        b_lo, b_hi = b_scr[r0:r0 + half, :], b_scr[r0 + half:r0 + sub, :]
        q_lo, q_hi = qs_scr[r0:r0 + half, :], qs_scr[r0 + half:r0 + sub, :]
        for s in range(sub):
            kb = kf_scr[r0 + s:r0 + s + 1, :]
            bb = b_scr[r0 + s:r0 + s + 1, :]
            d_hi = b_hi - bb
            if s > half:
                d_hi = jnp.where(tl + half >= s, d_hi, -jnp.inf)
            p_hi = q_hi * kb * jnp.exp(d_hi)
            if s < half:
                d_lo = b_lo - bb
                if s > 0:
                    d_lo = jnp.where(tl >= s, d_lo, -jnp.inf)
                p_lo = q_lo * kb * jnp.exp(d_lo)
            else:
                p_lo = zeros_half
            pcat_scr[r0:r0 + sub, s * GLA_HK:(s + 1) * GLA_HK] = (
                jnp.concatenate([p_lo, p_hi], axis=0).astype(BF16))
    a_diag = jnp.dot(pcat_scr[...], e_ref[...], preferred_element_type=F32)

    ri = lax.broadcasted_iota(jnp.int32, (c_len, c_len), 0)
    ci = lax.broadcasted_iota(jnp.int32, (c_len, c_len), 1)
    same_sub = (ri // sub) == (ci // sub)
    gn = gn_ref[...]

    def z(nrows):
        return jnp.zeros((nrows, GLA_HK), F32)

    for c in range(GLA_STEP // c_len):
        base = c * c_len
        bc = b[base:base + c_len]
        qc = qs[base:base + c_len]
        kc = kf[base:base + c_len]
        b15, b31, b47, b63 = bc[15:16], bc[31:32], bc[47:48], bc[63:64]
        q1 = jnp.concatenate([z(16), qc[16:32] * jnp.exp(bc[16:32] - b15), z(32)], axis=0)
        q2 = jnp.concatenate([z(32), qc[32:64] * jnp.exp(bc[32:64] - b31)], axis=0)
        q3 = jnp.concatenate([z(48), qc[48:64] * jnp.exp(bc[48:64] - b47)], axis=0)
        k1 = jnp.concatenate([kc[0:16] * jnp.exp(b15 - bc[0:16]), z(48)], axis=0)
        k2 = jnp.concatenate([kc[0:32] * jnp.exp(b31 - bc[0:32]), z(32)], axis=0)
        k3 = jnp.concatenate([z(32), kc[32:48] * jnp.exp(b47 - bc[32:48]), z(16)], axis=0)
        q_cat = jnp.concatenate([q1, q2, q3], axis=1).astype(BF16)
        k_cat = jnp.concatenate([k1, k2, k3], axis=1).astype(BF16)
        a_off = lax.dot_general(q_cat, k_cat, (((1,), (1,)), ((), ())),
                                preferred_element_type=F32)
        a = a_off + jnp.where(same_sub, a_diag[base:base + c_len], 0.0)

        vc = v_ref[base:base + c_len, :]
        o_intra = jnp.dot(a.astype(BF16), vc, preferred_element_type=F32)

        st = st_ref[...]
        q_in = (qc * jnp.exp(bc)).astype(BF16)
        o_inter = lax.dot_general(q_in, st.astype(BF16), (((1,), (1,)), ((), ())),
                                  preferred_element_type=F32)
        k_st = (kc * jnp.exp(b63 - bc)).astype(BF16)
        upd = lax.dot_general(vc, k_st, (((0,), (0,)), ((), ())),
                              preferred_element_type=F32)
        st_ref[...] = st * jnp.exp(b63) + upd

        o = o_inter + o_intra
        ms = jnp.mean(o * o, axis=-1, keepdims=True)
        y = o * lax.rsqrt(ms + EPS) * gn
        r = r_ref[base:base + c_len, :].astype(F32)
        y = y * (r * jax.nn.sigmoid(r))
        o_ref[base:base + c_len, :] = y.astype(BF16)


def _gla_constants():
    n = GLA_STEP
    idx = np.arange(n)
    ltri = ((idx[:, None] >= idx[None, :]) & (idx[:, None] // GLA_CHUNK == idx[None, :] // GLA_CHUNK))
    rows = np.arange(GLA_SUB * GLA_HK)
    cols = np.arange(GLA_CHUNK)
    e = (rows[:, None] // GLA_HK) == (cols[None, :] % GLA_SUB)
    return jnp.asarray(ltri, BF16), jnp.asarray(e, BF16)


def _gla_call(proj2d, pa, u_pad, bias, gn, batch):
    t = proj2d.shape[0]
    steps = t // batch // GLA_STEP
    ltri, e = _gla_constants()

    def row(bi, h, i):
        return bi * steps + i

    return pl.pallas_call(
        _gla_kernel,
        grid=(batch, GLA_HEADS, steps),
        in_specs=[
            pl.BlockSpec((GLA_STEP, GLA_HK), lambda bi, h, i: (row(bi, h, i), P_GQ // GLA_HK + h)),
            pl.BlockSpec((GLA_STEP, GLA_HK), lambda bi, h, i: (row(bi, h, i), P_GK // GLA_HK + h)),
            pl.BlockSpec((GLA_STEP, GLA_HV), lambda bi, h, i: (row(bi, h, i), P_GV // GLA_HV + h)),
            pl.BlockSpec((GLA_STEP, GLA_HV), lambda bi, h, i: (row(bi, h, i), P_GR // GLA_HV + h)),
            pl.BlockSpec((GLA_STEP, PA_PAD), lambda bi, h, i: (row(bi, h, i), 0)),
            pl.BlockSpec((PA_PAD, GLA_HK), lambda bi, h, i: (0, h)),
            pl.BlockSpec((1, GLA_HK), lambda bi, h, i: (0, h)),
            pl.BlockSpec((1, GLA_HV), lambda bi, h, i: (0, 0)),
            pl.BlockSpec((GLA_STEP, GLA_STEP), lambda bi, h, i: (0, 0)),
            pl.BlockSpec((GLA_SUB * GLA_HK, GLA_CHUNK), lambda bi, h, i: (0, 0)),
        ],
        out_specs=pl.BlockSpec((GLA_STEP, GLA_HV), lambda bi, h, i: (row(bi, h, i), h)),
        out_shape=jax.ShapeDtypeStruct((t, GLA_DV), BF16),
        scratch_shapes=[
            pltpu.VMEM((GLA_HV, GLA_HK), F32),
            pltpu.VMEM((GLA_STEP, GLA_HK), F32),
            pltpu.VMEM((GLA_STEP, GLA_HK), F32),
            pltpu.VMEM((GLA_STEP, GLA_HK), F32),
            pltpu.VMEM((GLA_STEP, GLA_SUB * GLA_HK), BF16),
        ],
        compiler_params=pltpu.CompilerParams(
            dimension_semantics=("parallel", "parallel", "arbitrary"),
            vmem_limit_bytes=VMEM_LIMIT_BYTES),
        name="gla",
    )(proj2d, proj2d, proj2d, proj2d, pa, u_pad, bias, gn, ltri, e)


FF_CHUNK = 1024


def _out_kernel(x_ref, o0_ref, o1_ref, o2_ref, l0_ref, l1_ref, l2_ref, og_ref, gate_ref, gbias_ref,
                wa_ref, wb_ref, wo_ref, g2_ref, wup_ref, wdn_ref, out_ref):
    l0, l1, l2 = l0_ref[...], l1_ref[...], l2_ref[...]
    mx = jnp.maximum(jnp.maximum(l0, l1), l2)
    e0, e1, e2 = jnp.exp(l0 - mx), jnp.exp(l1 - mx), jnp.exp(l2 - mx)
    o_attn = ((e0 * o0_ref[...].astype(F32) + e1 * o1_ref[...].astype(F32)
               + e2 * o2_ref[...].astype(F32)) / (e0 + e1 + e2))
    a = jnp.dot(o_attn.astype(BF16), wa_ref[...], preferred_element_type=F32)
    g = jnp.dot(og_ref[...], wb_ref[...], preferred_element_type=F32)
    gate_a = jax.nn.sigmoid(gate_ref[:, :D_MODEL].astype(F32) + gbias_ref[:, :D_MODEL])
    gate_g = jax.nn.sigmoid(gate_ref[:, D_MODEL:].astype(F32) + gbias_ref[:, D_MODEL:])
    mixed = gate_a * a + gate_g * g
    x1 = x_ref[...] + jnp.dot(mixed.astype(BF16), wo_ref[...], preferred_element_type=F32)

    ms = jnp.mean(x1 * x1, axis=-1, keepdims=True)
    h2 = (x1 * lax.rsqrt(ms + EPS) * g2_ref[...]).astype(BF16)
    acc = x1
    for c in range(D_FF // FF_CHUNK):
        u = jnp.dot(h2, wup_ref[:, c * FF_CHUNK:(c + 1) * FF_CHUNK], preferred_element_type=F32)
        u = jnp.maximum(u, 0.0)
        u = (u * u).astype(BF16)
        acc = acc + jnp.dot(u, wdn_ref[c * FF_CHUNK:(c + 1) * FF_CHUNK, :],
                            preferred_element_type=F32)
    out_ref[...] = acc


def _out_call(x2d, o_groups, lse_groups, o_gla, proj2d, gbias, wa, wb, wo, g2, wup, wdn, tm):
    t = x2d.shape[0]

    def tok(width):
        return pl.BlockSpec((tm, width), lambda i: (i, 0))

    def const(shape):
        return pl.BlockSpec(shape, lambda i: (0, 0), pipeline_mode=pl.Buffered(1))

    return pl.pallas_call(
        _out_kernel,
        grid=(t // tm,),
        in_specs=[
            tok(D_MODEL),
            tok(GROUP_WIDTH), tok(GROUP_WIDTH), tok(GROUP_WIDTH),
            tok(GROUP_WIDTH), tok(GROUP_WIDTH), tok(GROUP_WIDTH),
            tok(GLA_DV),
            pl.BlockSpec((tm, 2 * D_MODEL), lambda i: (i, P_GATE // (2 * D_MODEL))),
            const((1, 2 * D_MODEL)),
            const((GROUP_WIDTH, D_MODEL)),
            const((GLA_DV, D_MODEL)),
            const((D_MODEL, D_MODEL)),
            const((1, D_MODEL)),
            const((D_MODEL, D_FF)),
            const((D_FF, D_MODEL)),
        ],
        out_specs=tok(D_MODEL),
        out_shape=jax.ShapeDtypeStruct((t, D_MODEL), F32),
        compiler_params=pltpu.CompilerParams(
            dimension_semantics=("parallel",),
            vmem_limit_bytes=VMEM_LIMIT_BYTES),
        name="out",
    )(x2d, *o_groups, *lse_groups, o_gla, proj2d, gbias, wa, wb, wo, g2, wup, wdn)


def _layer(x2d, batch, norm1_g, w_in, gq, gk, gate_up, gate_bias, gla_norm_g, branch_bias,
           w_a, w_b, w_out, norm2_g, w_up, w_down, proj_tm=1024, out_tm=512):
    t = x2d.shape[0]
    s = t // batch

    def cols(off, size):
        return w_in[:, off:off + size]

    w_re = jnp.concatenate([
        cols(O_GATE, 2 * D_MODEL), cols(O_GV, GLA_DV), cols(O_GR, GLA_DV),
        cols(O_GQ, GLA_DK), cols(O_GK, GLA_DK), cols(O_ATTN, ATTN_WIDTH)], axis=1).astype(BF16)
    w_pa = jnp.pad(cols(O_PA, GLA_RANK), ((0, 0), (0, PA_PAD - GLA_RANK))).astype(BF16)
    q_gain = jnp.tile(gq, HEADS_PER_GROUP) * (HEAD_DIM ** -0.5)
    k_gain = jnp.tile(gk, HEADS_PER_GROUP)
    qk_gain = jnp.stack([q_gain] * N_GROUPS + [k_gain] * N_GROUPS).reshape(N_QK_TILES, 1, PROJ_TN)
    idx = np.arange(MXU_DIM)
    bd = jnp.asarray((idx[:, None] // HEAD_DIM) == (idx[None, :] // HEAD_DIM), BF16)

    proj, pa = _proj_call(x2d, norm1_g.reshape(1, D_MODEL), w_re, w_pa, qk_gain, bd, proj_tm)

    proj3d = proj.reshape(batch, s, P_WIDTH)
    o_groups, lse_groups = [], []
    for gi, (_, dilation) in enumerate(ATTN_GROUPS):
        o, lse = _attn_call(proj3d, gi, dilation)
        o_groups.append(o)
        lse_groups.append(lse)

    u_pad = jnp.pad(gate_up, ((0, PA_PAD - GLA_RANK), (0, 0))).astype(BF16)
    o_gla = _gla_call(proj, pa, u_pad, gate_bias.reshape(1, GLA_DK),
                      gla_norm_g.reshape(1, GLA_HV), batch)

    return _out_call(x2d, o_groups, lse_groups, o_gla, proj,
                     branch_bias.reshape(1, 2 * D_MODEL),
                     w_a.astype(BF16), w_b.astype(BF16), w_out.astype(BF16),
                     norm2_g.reshape(1, D_MODEL), w_up.astype(BF16), w_down.astype(BF16), out_tm)


def kernel(x, norm1_g, w_in, attn_q_norm_g, attn_k_norm_g, gla_gate_up, gla_gate_bias, gla_out_norm_g, branch_gate_bias, w_attn_branch, w_gla_branch, w_out, norm2_g, w_ff_up, w_ff_down):
    b, s, d = x.shape
    x2d = x.reshape(b * s, d)
    for l in range(norm1_g.shape[0]):
        x2d = _layer(x2d, b, norm1_g[l], w_in[l], attn_q_norm_g[l], attn_k_norm_g[l],
                     gla_gate_up[l], gla_gate_bias[l], gla_out_norm_g[l], branch_gate_bias[l],
                     w_attn_branch[l], w_gla_branch[l], w_out[l], norm2_g[l],
                     w_ff_up[l], w_ff_down[l])
    return x2d.reshape(b, s, d)
```

```python
import numpy as np
import jax
import jax.numpy as jnp
from jax import lax
from jax.experimental import pallas as pl
from jax.experimental.pallas import tpu as pltpu

F32 = jnp.float32
BF16 = jnp.bfloat16

D_MODEL = 1024
ATTN_GROUPS = ((128, 1), (512, 4), (2048, 16))
N_GROUPS = len(ATTN_GROUPS)
HEADS_PER_GROUP = 8
HEAD_DIM = 64
ATTN_BLOCK = 128
GROUP_WIDTH = HEADS_PER_GROUP * HEAD_DIM
ATTN_WIDTH = 3 * N_GROUPS * GROUP_WIDTH
N_PAIRS = GROUP_WIDTH // 128

GLA_HEADS = 4
GLA_DK = 512
GLA_DV = 1024
GLA_HK = GLA_DK // GLA_HEADS
GLA_HV = GLA_DV // GLA_HEADS
GLA_RANK = 16
GLA_TAU = 16.0
GLA_CHUNK = 64
GLA_SUB = 16
GLA_STEP = 256

D_FF = 4 * D_MODEL
EPS = 1e-6

LANES = 128
MXU_DIM = 256
VMEM_LIMIT_BYTES = 56 * 1024 * 1024

_ORIG_SIZES = (ATTN_WIDTH, GLA_DK, GLA_DK, GLA_DV, GLA_DV, GLA_RANK, 2 * D_MODEL)
_ORIG_OFF = tuple(int(v) for v in np.cumsum((0,) + _ORIG_SIZES))
O_ATTN, O_GQ, O_GK, O_GV, O_GR, O_PA, O_GATE = _ORIG_OFF[:7]

P_GATE = 0
P_GV = P_GATE + 2 * D_MODEL
P_GR = P_GV + GLA_DV
P_GQ = P_GR + GLA_DV
P_GK = P_GQ + GLA_DK
P_MAIN = P_GK + GLA_DK
PROJ_TN = GROUP_WIDTH
N_MAIN_TILES = P_MAIN // PROJ_TN
N_KINDS = 3
N_PROJ_TILES = N_MAIN_TILES + N_KINDS * N_GROUPS
PA_PAD = LANES


def _proj_kernel(x_ref, g1_ref, w_ref, wpa_ref, qkg_ref, bd_ref,
                 main_ref, a0_ref, a1_ref, a2_ref, pa_ref, h_ref, y_scr):
    j = pl.program_id(1)
    tm = x_ref.shape[0]

    @pl.when(j == 0)
    def _():
        x = x_ref[...]
        ms = jnp.mean(x * x, axis=-1, keepdims=True)
        h = (x * lax.rsqrt(ms + EPS) * g1_ref[...]).astype(BF16)
        h_ref[...] = h
        pa_ref[...] = jnp.dot(h, wpa_ref[...], preferred_element_type=F32)

    acc = jnp.dot(h_ref[...], w_ref[...], preferred_element_type=F32)

    @pl.when(j < N_MAIN_TILES)
    def _():
        main_ref[...] = acc.astype(BF16)

    @pl.when(j >= N_MAIN_TILES)
    def _():
        slab = j - N_MAIN_TILES
        kind = lax.rem(slab, N_KINDS)
        group = lax.div(slab, N_KINDS)

        def store_y(y):
            for c in range(PROJ_TN // LANES):
                y_scr[c] = y[:, c * LANES:(c + 1) * LANES]

        @pl.when(kind < 2)
        def _():
            sq = (acc * acc).astype(BF16)
            bd = bd_ref[...]
            ss = jnp.concatenate(
                [jnp.dot(sq[:, c * MXU_DIM:(c + 1) * MXU_DIM], bd, preferred_element_type=F32)
                 for c in range(PROJ_TN // MXU_DIM)], axis=1)
            store_y(acc * lax.rsqrt(ss * (1.0 / HEAD_DIM) + EPS) * qkg_ref[0])

        @pl.when(kind == 2)
        def _():
            store_y(acc)

        for g, ((_, d), ref) in enumerate(zip(ATTN_GROUPS, (a0_ref, a1_ref, a2_ref))):
            @pl.when(group == g)
            def _(d=d, ref=ref):
                for r in range(d):
                    for c in range(PROJ_TN // LANES):
                        ref[0, r, :, c * LANES:(c + 1) * LANES] = (
                            y_scr[c, pl.ds(r, tm // d, stride=d), :].astype(BF16))


def _proj_call(x2d, g1, w_re, w_pa, qk_gain, bd, batch, tm):
    t = x2d.shape[0]
    s = t // batch
    tiles_per_seq = s // tm

    def attn_spec(g, d):
        def index(i, j):
            kind = jnp.clip(j - N_MAIN_TILES - N_KINDS * g, 0, N_KINDS - 1)
            return (i // tiles_per_seq, 0, i % tiles_per_seq, kind)
        return pl.BlockSpec((1, d, tm // d, PROJ_TN), index)

    return pl.pallas_call(
        _proj_kernel,
        grid=(t // tm, N_PROJ_TILES),
        in_specs=[
            pl.BlockSpec((tm, D_MODEL), lambda i, j: (i, 0)),
            pl.BlockSpec((1, D_MODEL), lambda i, j: (0, 0)),
            pl.BlockSpec((D_MODEL, PROJ_TN), lambda i, j: (0, j)),
            pl.BlockSpec((D_MODEL, PA_PAD), lambda i, j: (0, 0)),
            pl.BlockSpec((1, 1, PROJ_TN), lambda i, j: (jnp.minimum((j + 2) % N_KINDS, 1), 0, 0)),
            pl.BlockSpec((MXU_DIM, MXU_DIM), lambda i, j: (0, 0)),
        ],
        out_specs=[
            pl.BlockSpec((tm, PROJ_TN), lambda i, j: (i, jnp.minimum(j, N_MAIN_TILES - 1))),
            *[attn_spec(g, d) for g, (_, d) in enumerate(ATTN_GROUPS)],
            pl.BlockSpec((tm, PA_PAD), lambda i, j: (i, 0)),
        ],
        out_shape=[
            jax.ShapeDtypeStruct((t, P_MAIN), BF16),
            *[jax.ShapeDtypeStruct((batch, d, s // d, N_KINDS * GROUP_WIDTH), BF16)
              for _, d in ATTN_GROUPS],
            jax.ShapeDtypeStruct((t, PA_PAD), F32),
        ],
        scratch_shapes=[pltpu.VMEM((tm, D_MODEL), BF16), pltpu.VMEM((PROJ_TN // LANES, tm, LANES), F32)],
        compiler_params=pltpu.CompilerParams(
            dimension_semantics=("parallel", "arbitrary"),
            vmem_limit_bytes=VMEM_LIMIT_BYTES),
        name="proj",
    )(x2d, g1, w_re, w_pa, qk_gain, bd)


def _attn_kernel(q_ref, kp_ref, kc_ref, vp_ref, vc_ref, o_ref, lse_ref):
    n = pl.program_id(1)
    d = q_ref.shape[1]
    blk = ATTN_BLOCK
    qi = lax.broadcasted_iota(jnp.int32, (blk, 2 * blk), 0)
    ki = lax.broadcasted_iota(jnp.int32, (blk, 2 * blk), 1)
    valid = (ki >= qi) & (ki <= qi + blk) & ((ki >= blk) | (n > 0))
    lane = lax.broadcasted_iota(jnp.int32, (blk, LANES), 1)
    first_head = lane < HEAD_DIM

    def residue(r, carry):
        lse_tile = jnp.zeros((blk, LANES), F32)
        for pair in range(N_PAIRS):
            sl = slice(pair * LANES, (pair + 1) * LANES)
            q = q_ref[0, r, :, sl]
            k = jnp.concatenate([kp_ref[0, r, :, sl], kc_ref[0, r, :, sl]], axis=0)
            v = jnp.concatenate([vp_ref[0, r, :, sl], vc_ref[0, r, :, sl]], axis=0)
            zero = jnp.zeros_like(q)
            outs = []
            for half, sel in enumerate((first_head, jnp.logical_not(first_head))):
                qm = jnp.where(sel, q, zero)
                s = lax.dot_general(qm, k, (((1,), (1,)), ((), ())), preferred_element_type=F32)
                s = jnp.where(valid, s, -jnp.inf)
                m = jnp.max(s, axis=-1, keepdims=True)
                p = jnp.exp(s - m)
                l = jnp.sum(p, axis=-1, keepdims=True)
                pv = jnp.dot(p.astype(BF16), v, preferred_element_type=F32)
                outs.append(pv / l)
                lse_tile = jnp.where(lane == 2 * pair + half, m + jnp.log(l), lse_tile)
            o_ref[0, pair, pl.ds(r, blk, stride=d), :] = jnp.where(first_head, outs[0], outs[1])
        lse_ref[0, pl.ds(r, blk, stride=d), :] = lse_tile
        return carry

    lax.fori_loop(0, d, residue, 0)


def _attn_call(qkv, dilation):
    b, d, sub_len, _ = qkv.shape
    assert d == dilation
    blk = ATTN_BLOCK
    nb = sub_len // blk
    s = sub_len * d
    blk_shape = (1, d, blk, GROUP_WIDTH)

    def cur(kind):
        return lambda bi, n: (bi, 0, n, kind)

    def prev(kind):
        return lambda bi, n: (bi, 0, jnp.maximum(n - 1, 0), kind)

    o, lse = pl.pallas_call(
        _attn_kernel,
        grid=(b, nb),
        in_specs=[
            pl.BlockSpec(blk_shape, cur(0)),
            pl.BlockSpec(blk_shape, prev(1)),
            pl.BlockSpec(blk_shape, cur(1)),
            pl.BlockSpec(blk_shape, prev(2)),
            pl.BlockSpec(blk_shape, cur(2)),
        ],
        out_specs=[
            pl.BlockSpec((1, N_PAIRS, d * blk, LANES), lambda bi, n: (bi, 0, n, 0)),
            pl.BlockSpec((1, d * blk, LANES), lambda bi, n: (bi, n, 0)),
        ],
        out_shape=[
            jax.ShapeDtypeStruct((b, N_PAIRS, s, LANES), F32),
            jax.ShapeDtypeStruct((b, s, LANES), F32),
        ],
        compiler_params=pltpu.CompilerParams(
            dimension_semantics=("parallel", "arbitrary"),
            vmem_limit_bytes=VMEM_LIMIT_BYTES),
        name=f"attn_d{d}",
    )(qkv, qkv, qkv, qkv, qkv)
    return o, lse


def _gla_kernel(q_ref, k_ref, v_ref, r_ref, pa_ref, u_ref, bias_ref, gn_ref, ltri_ref, e_ref,
                o_ref, st_ref, b_scr, qs_scr, kf_scr, pcat_scr):
    c_len, sub = GLA_CHUNK, GLA_SUB

    @pl.when(pl.program_id(2) == 0)
    def _():
        st_ref[...] = jnp.zeros_like(st_ref)

    logits = jnp.dot(pa_ref[...].astype(BF16), u_ref[...], preferred_element_type=F32) + bias_ref[...]
    log_sig = jnp.minimum(logits, 0.0) - jnp.log1p(jnp.exp(-jnp.abs(logits)))
    la = log_sig * (1.0 / GLA_TAU)

    hi = la.astype(BF16)
    r1 = la - hi.astype(F32)
    mid = r1.astype(BF16)
    lo = (r1 - mid.astype(F32)).astype(BF16)
    ltri = ltri_ref[...]
    b = (jnp.dot(ltri, hi, preferred_element_type=F32)
         + jnp.dot(ltri, mid, preferred_element_type=F32)
         + jnp.dot(ltri, lo, preferred_element_type=F32))

    qs = q_ref[...].astype(F32) * (GLA_HK ** -0.5)
    kf = k_ref[...].astype(F32)
    b_scr[...] = b
    qs_scr[...] = qs
    kf_scr[...] = kf

    half = sub // 2
    tl = lax.broadcasted_iota(jnp.int32, (half, GLA_HK), 0)
    zeros_half = jnp.zeros((half, GLA_HK), F32)
    for blk_i in range(GLA_STEP // sub):
        r0 = blk_i * sub
        b_lo, b_hi = b_scr[r0:r0 + half, :], b_scr[r0 + half:r0 + sub, :]
        q_lo, q_hi = qs_scr[r0:r0 + half, :], qs_scr[r0 + half:r0 + sub, :]
        for s in range(sub):
            kb = kf_scr[r0 + s:r0 + s + 1, :]
            bb = b_scr[r0 + s:r0 + s + 1, :]
            d_hi = b_hi - bb
            if s > half:
                d_hi = jnp.where(tl + half >= s, d_hi, -jnp.inf)
            p_hi = q_hi * kb * jnp.exp(d_hi)
            if s < half:
                d_lo = b_lo - bb
                if s > 0:
                    d_lo = jnp.where(tl >= s, d_lo, -jnp.inf)
                p_lo = q_lo * kb * jnp.exp(d_lo)
            else:
                p_lo = zeros_half
            pcat_scr[r0:r0 + sub, s * GLA_HK:(s + 1) * GLA_HK] = (
                jnp.concatenate([p_lo, p_hi], axis=0).astype(BF16))
    a_diag = jnp.dot(pcat_scr[...], e_ref[...], preferred_element_type=F32)

    ri = lax.broadcasted_iota(jnp.int32, (c_len, c_len), 0)
    ci = lax.broadcasted_iota(jnp.int32, (c_len, c_len), 1)
    same_sub = (ri // sub) == (ci // sub)
    gn = gn_ref[...]

    def z(nrows):
        return jnp.zeros((nrows, GLA_HK), F32)

    for c in range(GLA_STEP // c_len):
        base = c * c_len
        bc = b[base:base + c_len]
        qc = qs[base:base + c_len]
        kc = kf[base:base + c_len]
        b15, b31, b47, b63 = bc[15:16], bc[31:32], bc[47:48], bc[63:64]
        q1 = jnp.concatenate([z(16), qc[16:32] * jnp.exp(bc[16:32] - b15), z(32)], axis=0)
        q2 = jnp.concatenate([z(32), qc[32:64] * jnp.exp(bc[32:64] - b31)], axis=0)
        q3 = jnp.concatenate([z(48), qc[48:64] * jnp.exp(bc[48:64] - b47)], axis=0)
        k1 = jnp.concatenate([kc[0:16] * jnp.exp(b15 - bc[0:16]), z(48)], axis=0)
        k2 = jnp.concatenate([kc[0:32] * jnp.exp(b31 - bc[0:32]), z(32)], axis=0)
        k3 = jnp.concatenate([z(32), kc[32:48] * jnp.exp(b47 - bc[32:48]), z(16)], axis=0)
        q_cat = jnp.concatenate([q1, q2, q3], axis=1).astype(BF16)
        k_cat = jnp.concatenate([k1, k2, k3], axis=1).astype(BF16)
        a_off = lax.dot_general(q_cat, k_cat, (((1,), (1,)), ((), ())),
                                preferred_element_type=F32)
        a = a_off + jnp.where(same_sub, a_diag[base:base + c_len], 0.0)

        vc = v_ref[base:base + c_len, :]
        o_intra = jnp.dot(a.astype(BF16), vc, preferred_element_type=F32)

        st = st_ref[...]
        q_in = (qc * jnp.exp(bc)).astype(BF16)
        o_inter = lax.dot_general(q_in, st.astype(BF16), (((1,), (1,)), ((), ())),
                                  preferred_element_type=F32)
        k_st = (kc * jnp.exp(b63 - bc)).astype(BF16)
        upd = lax.dot_general(vc, k_st, (((0,), (0,)), ((), ())),
                              preferred_element_type=F32)
        st_ref[...] = st * jnp.exp(b63) + upd

        o = o_inter + o_intra
        ms = jnp.mean(o * o, axis=-1, keepdims=True)
        y = o * lax.rsqrt(ms + EPS) * gn
        r = r_ref[base:base + c_len, :].astype(F32)
        y = y * (r * jax.nn.sigmoid(r))
        o_ref[base:base + c_len, :] = y.astype(BF16)


def _gla_constants():
    n = GLA_STEP
    idx = np.arange(n)
    ltri = ((idx[:, None] >= idx[None, :]) & (idx[:, None] // GLA_CHUNK == idx[None, :] // GLA_CHUNK))
    rows = np.arange(GLA_SUB * GLA_HK)
    cols = np.arange(GLA_CHUNK)
    e = (rows[:, None] // GLA_HK) == (cols[None, :] % GLA_SUB)
    return jnp.asarray(ltri, BF16), jnp.asarray(e, BF16)


def _gla_call(proj2d, pa, u_pad, bias, gn, batch):
    t = proj2d.shape[0]
    steps = t // batch // GLA_STEP
    ltri, e = _gla_constants()

    def row(bi, h, i):
        return bi * steps + i

    return pl.pallas_call(
        _gla_kernel,
        grid=(batch, GLA_HEADS, steps),
        in_specs=[
            pl.BlockSpec((GLA_STEP, GLA_HK), lambda bi, h, i: (row(bi, h, i), P_GQ // GLA_HK + h)),
            pl.BlockSpec((GLA_STEP, GLA_HK), lambda bi, h, i: (row(bi, h, i), P_GK // GLA_HK + h)),
            pl.BlockSpec((GLA_STEP, GLA_HV), lambda bi, h, i: (row(bi, h, i), P_GV // GLA_HV + h)),
            pl.BlockSpec((GLA_STEP, GLA_HV), lambda bi, h, i: (row(bi, h, i), P_GR // GLA_HV + h)),
            pl.BlockSpec((GLA_STEP, PA_PAD), lambda bi, h, i: (row(bi, h, i), 0)),
            pl.BlockSpec((PA_PAD, GLA_HK), lambda bi, h, i: (0, h)),
            pl.BlockSpec((1, GLA_HK), lambda bi, h, i: (0, h)),
            pl.BlockSpec((1, GLA_HV), lambda bi, h, i: (0, 0)),
            pl.BlockSpec((GLA_STEP, GLA_STEP), lambda bi, h, i: (0, 0)),
            pl.BlockSpec((GLA_SUB * GLA_HK, GLA_CHUNK), lambda bi, h, i: (0, 0)),
        ],
        out_specs=pl.BlockSpec((GLA_STEP, GLA_HV), lambda bi, h, i: (row(bi, h, i), h)),
        out_shape=jax.ShapeDtypeStruct((t, GLA_DV), BF16),
        scratch_shapes=[
            pltpu.VMEM((GLA_HV, GLA_HK), F32),
            pltpu.VMEM((GLA_STEP, GLA_HK), F32),
            pltpu.VMEM((GLA_STEP, GLA_HK), F32),
            pltpu.VMEM((GLA_STEP, GLA_HK), F32),
            pltpu.VMEM((GLA_STEP, GLA_SUB * GLA_HK), BF16),
        ],
        compiler_params=pltpu.CompilerParams(
            dimension_semantics=("parallel", "parallel", "arbitrary"),
            vmem_limit_bytes=VMEM_LIMIT_BYTES),
        name="gla",
    )(proj2d, proj2d, proj2d, proj2d, pa, u_pad, bias, gn, ltri, e)


FF_CHUNK = 1024


def _split_dot(w, rhs):
    hi = w.astype(BF16)
    lo = (w - hi.astype(F32)).astype(BF16)
    return (jnp.dot(hi, rhs, preferred_element_type=F32)
            + jnp.dot(lo, rhs, preferred_element_type=F32))


def _out_kernel(x_ref, o0_ref, o1_ref, o2_ref, l0_ref, l1_ref, l2_ref, og_ref, gate_ref, gbias_ref,
                hx_ref, wa_ref, wb_ref, wo_ref, g2_ref, wup_ref, wdn_ref, out_ref):
    l0, l1, l2 = l0_ref[0], l1_ref[0], l2_ref[0]
    mx = jnp.maximum(jnp.maximum(l0, l1), l2)
    e0, e1, e2 = jnp.exp(l0 - mx), jnp.exp(l1 - mx), jnp.exp(l2 - mx)
    inv = 1.0 / (e0 + e1 + e2)
    hx = hx_ref[...]
    def pairs(ref):
        return jnp.concatenate([ref[0, p] for p in range(N_PAIRS)], axis=1)

    o_attn = (_split_dot(e0 * inv, hx) * pairs(o0_ref) + _split_dot(e1 * inv, hx) * pairs(o1_ref)
              + _split_dot(e2 * inv, hx) * pairs(o2_ref))
    a = jnp.dot(o_attn.astype(BF16), wa_ref[...], preferred_element_type=F32)
    g = jnp.dot(og_ref[...], wb_ref[...], preferred_element_type=F32)
    gate_a = jax.nn.sigmoid(gate_ref[:, :D_MODEL].astype(F32) + gbias_ref[:, :D_MODEL])
    gate_g = jax.nn.sigmoid(gate_ref[:, D_MODEL:].astype(F32) + gbias_ref[:, D_MODEL:])
    mixed = gate_a * a + gate_g * g
    x1 = x_ref[...] + jnp.dot(mixed.astype(BF16), wo_ref[...], preferred_element_type=F32)

    ms = jnp.mean(x1 * x1, axis=-1, keepdims=True)
    h2 = (x1 * lax.rsqrt(ms + EPS) * g2_ref[...]).astype(BF16)
    acc = x1
    for c in range(D_FF // FF_CHUNK):
        u = jnp.dot(h2, wup_ref[:, c * FF_CHUNK:(c + 1) * FF_CHUNK], preferred_element_type=F32)
        u = jnp.maximum(u, 0.0)
        u = (u * u).astype(BF16)
        acc = acc + jnp.dot(u, wdn_ref[c * FF_CHUNK:(c + 1) * FF_CHUNK, :],
                            preferred_element_type=F32)
    out_ref[...] = acc


def _out_call(x2d, o_groups, lse_groups, o_gla, proj2d, gbias, wa, wb, wo, g2, wup, wdn, batch, tm):
    t = x2d.shape[0]
    lanes = np.arange(LANES)
    cols = np.arange(GROUP_WIDTH)
    head_expand = jnp.asarray(lanes[:, None] == cols[None, :] // HEAD_DIM, BF16)

    tiles_per_seq = t // batch // tm

    def tok(width):
        return pl.BlockSpec((tm, width), lambda i: (i, 0))

    attn_o = pl.BlockSpec((1, N_PAIRS, tm, LANES),
                          lambda i: (i // tiles_per_seq, 0, i % tiles_per_seq, 0))
    attn_lse = pl.BlockSpec((1, tm, LANES), lambda i: (i // tiles_per_seq, i % tiles_per_seq, 0))

    def const(shape):
        return pl.BlockSpec(shape, lambda i: (0, 0), pipeline_mode=pl.Buffered(1))

    return pl.pallas_call(
        _out_kernel,
        grid=(t // tm,),
        in_specs=[
            tok(D_MODEL),
            attn_o, attn_o, attn_o,
            attn_lse, attn_lse, attn_lse,
            tok(GLA_DV),
            pl.BlockSpec((tm, 2 * D_MODEL), lambda i: (i, P_GATE // (2 * D_MODEL))),
            const((1, 2 * D_MODEL)),
            const((LANES, GROUP_WIDTH)),
            const((GROUP_WIDTH, D_MODEL)),
            const((GLA_DV, D_MODEL)),
            const((D_MODEL, D_MODEL)),
            const((1, D_MODEL)),
            const((D_MODEL, D_FF)),
            const((D_FF, D_MODEL)),
        ],
        out_specs=tok(D_MODEL),
        out_shape=jax.ShapeDtypeStruct((t, D_MODEL), F32),
        compiler_params=pltpu.CompilerParams(
            dimension_semantics=("parallel",),
            vmem_limit_bytes=VMEM_LIMIT_BYTES),
        name="out",
    )(x2d, *o_groups, *lse_groups, o_gla, proj2d, gbias, head_expand, wa, wb, wo, g2, wup, wdn)


def _layer(x2d, batch, norm1_g, w_in, gq, gk, gate_up, gate_bias, gla_norm_g, branch_bias,
           w_a, w_b, w_out, norm2_g, w_up, w_down, proj_tm=1024, out_tm=512):
    def cols(off, size):
        return w_in[:, off:off + size]

    attn_cols = []
    for g in range(N_GROUPS):
        for kind in range(N_KINDS):
            attn_cols.append(cols(O_ATTN + (kind * N_GROUPS + g) * GROUP_WIDTH, GROUP_WIDTH))
    w_re = jnp.concatenate([
        cols(O_GATE, 2 * D_MODEL), cols(O_GV, GLA_DV), cols(O_GR, GLA_DV),
        cols(O_GQ, GLA_DK), cols(O_GK, GLA_DK)] + attn_cols, axis=1).astype(BF16)
    w_pa = jnp.pad(cols(O_PA, GLA_RANK), ((0, 0), (0, PA_PAD - GLA_RANK))).astype(BF16)
    q_gain = jnp.tile(gq, HEADS_PER_GROUP) * (HEAD_DIM ** -0.5)
    k_gain = jnp.tile(gk, HEADS_PER_GROUP)
    qk_gain = jnp.stack([q_gain, k_gain]).reshape(2, 1, PROJ_TN)
    idx = np.arange(MXU_DIM)
    bd = jnp.asarray((idx[:, None] // HEAD_DIM) == (idx[None, :] // HEAD_DIM), BF16)

    proj, qkv0, qkv1, qkv2, pa = _proj_call(
        x2d, norm1_g.reshape(1, D_MODEL), w_re, w_pa, qk_gain, bd, batch, proj_tm)

    o_groups, lse_groups = [], []
    for qkv, (_, dilation) in zip((qkv0, qkv1, qkv2), ATTN_GROUPS):
        o, lse = _attn_call(qkv, dilation)
        o_groups.append(o)
        lse_groups.append(lse)

    u_pad = jnp.pad(gate_up, ((0, PA_PAD - GLA_RANK), (0, 0))).astype(BF16)
    o_gla = _gla_call(proj, pa, u_pad, gate_bias.reshape(1, GLA_DK),
                      gla_norm_g.reshape(1, GLA_HV), batch)

    return _out_call(x2d, o_groups, lse_groups, o_gla, proj,
                     branch_bias.reshape(1, 2 * D_MODEL),
                     w_a.astype(BF16), w_b.astype(BF16), w_out.astype(BF16),
                     norm2_g.reshape(1, D_MODEL), w_up.astype(BF16), w_down.astype(BF16), batch, out_tm)


def kernel(x, norm1_g, w_in, attn_q_norm_g, attn_k_norm_g, gla_gate_up, gla_gate_bias, gla_out_norm_g, branch_gate_bias, w_attn_branch, w_gla_branch, w_out, norm2_g, w_ff_up, w_ff_down):
    b, s, d = x.shape
    x2d = x.reshape(b * s, d)
    for l in range(norm1_g.shape[0]):
        x2d = _layer(x2d, b, norm1_g[l], w_in[l], attn_q_norm_g[l], attn_k_norm_g[l],
                     gla_gate_up[l], gla_gate_bias[l], gla_out_norm_g[l], branch_gate_bias[l],
                     w_attn_branch[l], w_gla_branch[l], w_out[l], norm2_g[l],
                     w_ff_up[l], w_ff_down[l])
    return x2d.reshape(b, s, d)
```

```python
import numpy as np
import jax
import jax.numpy as jnp
from jax import lax
from jax.experimental import pallas as pl
from jax.experimental.pallas import tpu as pltpu

F32 = jnp.float32
BF16 = jnp.bfloat16

D_MODEL = 1024
ATTN_GROUPS = ((128, 1), (512, 4), (2048, 16))
N_GROUPS = len(ATTN_GROUPS)
HEADS_PER_GROUP = 8
HEAD_DIM = 64
ATTN_BLOCK = 128
GROUP_WIDTH = HEADS_PER_GROUP * HEAD_DIM
ATTN_WIDTH = 3 * N_GROUPS * GROUP_WIDTH
N_PAIRS = GROUP_WIDTH // 128

GLA_HEADS = 4
GLA_DK = 512
GLA_DV = 1024
GLA_HK = GLA_DK // GLA_HEADS
GLA_HV = GLA_DV // GLA_HEADS
GLA_RANK = 16
GLA_TAU = 16.0
GLA_CHUNK = 64
GLA_SUB = 16
GLA_STEP = 256

D_FF = 4 * D_MODEL
EPS = 1e-6

LANES = 128
MXU_DIM = 256
VMEM_LIMIT_BYTES = 56 * 1024 * 1024

_ORIG_SIZES = (ATTN_WIDTH, GLA_DK, GLA_DK, GLA_DV, GLA_DV, GLA_RANK, 2 * D_MODEL)
_ORIG_OFF = tuple(int(v) for v in np.cumsum((0,) + _ORIG_SIZES))
O_ATTN, O_GQ, O_GK, O_GV, O_GR, O_PA, O_GATE = _ORIG_OFF[:7]

P_GATE = 0
P_GV = P_GATE + 2 * D_MODEL
P_GR = P_GV + GLA_DV
P_GQ = P_GR + GLA_DV
P_GK = P_GQ + GLA_DK
P_MAIN = P_GK + GLA_DK
N_KINDS = 3
PA_PAD = LANES
NORM_TM = 1024


def _norm_kernel(x_ref, g1_ref, wpa_ref, h_ref, h4_ref, h16_ref, pa_ref, scr):
    tm = x_ref.shape[0]
    x = x_ref[...]
    ms = jnp.mean(x * x, axis=-1, keepdims=True)
    hf = x * lax.rsqrt(ms + EPS) * g1_ref[...]
    h = hf.astype(BF16)
    h_ref[...] = h
    pa_ref[...] = jnp.dot(h, wpa_ref[...], preferred_element_type=F32)
    n_col = D_MODEL // LANES
    for c in range(n_col):
        scr[c] = hf[:, c * LANES:(c + 1) * LANES]
    for ref in (h4_ref, h16_ref):
        d = ref.shape[0]
        for r in range(d):
            for c in range(n_col):
                ref[r, :, c * LANES:(c + 1) * LANES] = (
                    scr[c, pl.ds(r, tm // d, stride=d), :].astype(BF16))


def _norm_call(x2d, g1, w_pa):
    t = x2d.shape[0]
    tm = NORM_TM
    n_tiles = t // tm
    d4, d16 = ATTN_GROUPS[1][1], ATTN_GROUPS[2][1]
    h, h4, h16, pa = pl.pallas_call(
        _norm_kernel,
        grid=(n_tiles,),
        in_specs=[
            pl.BlockSpec((tm, D_MODEL), lambda i: (i, 0)),
            pl.BlockSpec((1, D_MODEL), lambda i: (0, 0)),
            pl.BlockSpec((D_MODEL, PA_PAD), lambda i: (0, 0)),
        ],
        out_specs=[
            pl.BlockSpec((tm, D_MODEL), lambda i: (i, 0)),
            pl.BlockSpec((d4, tm // d4, D_MODEL), lambda i: (i, 0, 0)),
            pl.BlockSpec((d16, tm // d16, D_MODEL), lambda i: (i, 0, 0)),
            pl.BlockSpec((tm, PA_PAD), lambda i: (i, 0)),
        ],
        out_shape=[
            jax.ShapeDtypeStruct((t, D_MODEL), BF16),
            jax.ShapeDtypeStruct((n_tiles * d4, tm // d4, D_MODEL), BF16),
            jax.ShapeDtypeStruct((n_tiles * d16, tm // d16, D_MODEL), BF16),
            jax.ShapeDtypeStruct((t, PA_PAD), F32),
        ],
        scratch_shapes=[pltpu.VMEM((D_MODEL // LANES, tm, LANES), F32)],
        compiler_params=pltpu.CompilerParams(
            dimension_semantics=("parallel",),
            vmem_limit_bytes=VMEM_LIMIT_BYTES),
        name="norm",
    )(x2d, g1, w_pa)
    return h, h4.reshape(t, D_MODEL), h16.reshape(t, D_MODEL), pa


def _matmul_kernel(h_ref, w_ref, o_ref):
    o_ref[...] = jnp.dot(h_ref[...], w_ref[...], preferred_element_type=F32).astype(o_ref.dtype)


def _main_proj_call(h, w_main, tm, tn):
    t = h.shape[0]
    n = w_main.shape[1]
    return pl.pallas_call(
        _matmul_kernel,
        grid=(t // tm, n // tn),
        in_specs=[
            pl.BlockSpec((tm, D_MODEL), lambda i, j: (i, 0)),
            pl.BlockSpec((D_MODEL, tn), lambda i, j: (0, j)),
        ],
        out_specs=pl.BlockSpec((tm, tn), lambda i, j: (i, j)),
        out_shape=jax.ShapeDtypeStruct((t, n), BF16),
        compiler_params=pltpu.CompilerParams(
            dimension_semantics=("parallel", "arbitrary"),
            vmem_limit_bytes=VMEM_LIMIT_BYTES),
        name="proj_main",
    )(h, w_main)


def _attn_proj_kernel(h_ref, w_ref, gain_ref, bd_ref, o_ref):
    d = o_ref.shape[1]
    rows = o_ref.shape[2]
    h = h_ref[...]
    bd = bd_ref[...]
    for kind in range(N_KINDS):
        cols = slice(kind * GROUP_WIDTH, (kind + 1) * GROUP_WIDTH)
        acc = jnp.dot(h, w_ref[:, cols], preferred_element_type=F32)
        if kind < 2:
            sq = (acc * acc).astype(BF16)
            ss = jnp.concatenate(
                [jnp.dot(sq[:, c * MXU_DIM:(c + 1) * MXU_DIM], bd, preferred_element_type=F32)
                 for c in range(GROUP_WIDTH // MXU_DIM)], axis=1)
            acc = acc * lax.rsqrt(ss * (1.0 / HEAD_DIM) + EPS) * gain_ref[kind]
        y = acc.astype(BF16)
        for r in range(d):
            o_ref[0, r, :, cols] = y[r * rows:(r + 1) * rows, :]


def _attn_proj_call(h_perm, w_qkv, qk_gain, bd, batch, dilation):
    t = h_perm.shape[0]
    s = t // batch
    d = dilation
    tm = NORM_TM
    tiles_per_seq = s // tm
    width = N_KINDS * GROUP_WIDTH
    return pl.pallas_call(
        _attn_proj_kernel,
        grid=(t // tm,),
        in_specs=[
            pl.BlockSpec((tm, D_MODEL), lambda i: (i, 0)),
            pl.BlockSpec((D_MODEL, width), lambda i: (0, 0)),
            pl.BlockSpec((2, 1, GROUP_WIDTH), lambda i: (0, 0, 0)),
            pl.BlockSpec((MXU_DIM, MXU_DIM), lambda i: (0, 0)),
        ],
        out_specs=pl.BlockSpec((1, d, tm // d, width),
                               lambda i: (i // tiles_per_seq, 0, i % tiles_per_seq, 0)),
        out_shape=jax.ShapeDtypeStruct((batch, d, s // d, width), BF16),
        compiler_params=pltpu.CompilerParams(
            dimension_semantics=("parallel",),
            vmem_limit_bytes=VMEM_LIMIT_BYTES),
        name=f"proj_attn_d{d}",
    )(h_perm, w_qkv, qk_gain, bd)


def _attn_kernel(q_ref, kp_ref, kc_ref, vp_ref, vc_ref, o_ref, lse_ref):
    n = pl.program_id(1)
    d = q_ref.shape[1]
    blk = ATTN_BLOCK
    qi = lax.broadcasted_iota(jnp.int32, (blk, 2 * blk), 0)
    ki = lax.broadcasted_iota(jnp.int32, (blk, 2 * blk), 1)
    valid = (ki >= qi) & (ki <= qi + blk) & ((ki >= blk) | (n > 0))
    lane = lax.broadcasted_iota(jnp.int32, (blk, LANES), 1)
    first_head = lane < HEAD_DIM

    def residue(r, carry):
        lse_tile = jnp.zeros((blk, LANES), F32)
        for pair in range(N_PAIRS):
            sl = slice(pair * LANES, (pair + 1) * LANES)
            q = q_ref[0, r, :, sl]
            k = jnp.concatenate([kp_ref[0, r, :, sl], kc_ref[0, r, :, sl]], axis=0)
            v = jnp.concatenate([vp_ref[0, r, :, sl], vc_ref[0, r, :, sl]], axis=0)
            zero = jnp.zeros_like(q)
            outs = []
            for half, sel in enumerate((first_head, jnp.logical_not(first_head))):
                qm = jnp.where(sel, q, zero)
                s = lax.dot_general(qm, k, (((1,), (1,)), ((), ())), preferred_element_type=F32)
                s = jnp.where(valid, s, -jnp.inf)
                m = jnp.max(s, axis=-1, keepdims=True)
                p = jnp.exp(s - m)
                l = jnp.sum(p, axis=-1, keepdims=True)
                pv = jnp.dot(p.astype(BF16), v, preferred_element_type=F32)
                outs.append(pv / l)
                lse_tile = jnp.where(lane == 2 * pair + half, m + jnp.log(l), lse_tile)
            o_ref[0, pair, pl.ds(r, blk, stride=d), :] = jnp.where(first_head, outs[0], outs[1])
        lse_ref[0, pl.ds(r, blk, stride=d), :] = lse_tile
        return carry

    lax.fori_loop(0, d, residue, 0)


def _attn_call(qkv, dilation):
    b, d, sub_len, _ = qkv.shape
    assert d == dilation
    blk = ATTN_BLOCK
    nb = sub_len // blk
    s = sub_len * d
    blk_shape = (1, d, blk, GROUP_WIDTH)

    def cur(kind):
        return lambda bi, n: (bi, 0, n, kind)

    def prev(kind):
        return lambda bi, n: (bi, 0, jnp.maximum(n - 1, 0), kind)

    o, lse = pl.pallas_call(
        _attn_kernel,
        grid=(b, nb),
        in_specs=[
            pl.BlockSpec(blk_shape, cur(0)),
            pl.BlockSpec(blk_shape, prev(1)),
            pl.BlockSpec(blk_shape, cur(1)),
            pl.BlockSpec(blk_shape, prev(2)),
            pl.BlockSpec(blk_shape, cur(2)),
        ],
        out_specs=[
            pl.BlockSpec((1, N_PAIRS, d * blk, LANES), lambda bi, n: (bi, 0, n, 0)),
            pl.BlockSpec((1, d * blk, LANES), lambda bi, n: (bi, n, 0)),
        ],
        out_shape=[
            jax.ShapeDtypeStruct((b, N_PAIRS, s, LANES), F32),
            jax.ShapeDtypeStruct((b, s, LANES), F32),
        ],
        compiler_params=pltpu.CompilerParams(
            dimension_semantics=("parallel", "arbitrary"),
            vmem_limit_bytes=VMEM_LIMIT_BYTES),
        name=f"attn_d{d}",
    )(qkv, qkv, qkv, qkv, qkv)
    return o, lse


def _gla_kernel(q_ref, k_ref, v_ref, r_ref, pa_ref, u_ref, bias_ref, gn_ref, ltri_ref, e_ref,
                o_ref, st_ref, b_scr, qs_scr, kf_scr, pcat_scr):
    c_len, sub = GLA_CHUNK, GLA_SUB

    @pl.when(pl.program_id(2) == 0)
    def _():
        st_ref[...] = jnp.zeros_like(st_ref)

    logits = jnp.dot(pa_ref[...].astype(BF16), u_ref[...], preferred_element_type=F32) + bias_ref[...]
    log_sig = jnp.minimum(logits, 0.0) - jnp.log1p(jnp.exp(-jnp.abs(logits)))
    la = log_sig * (1.0 / GLA_TAU)

    hi = la.astype(BF16)
    r1 = la - hi.astype(F32)
    mid = r1.astype(BF16)
    lo = (r1 - mid.astype(F32)).astype(BF16)
    ltri = ltri_ref[...]
    b = (jnp.dot(ltri, hi, preferred_element_type=F32)
         + jnp.dot(ltri, mid, preferred_element_type=F32)
         + jnp.dot(ltri, lo, preferred_element_type=F32))

    qs = q_ref[...].astype(F32) * (GLA_HK ** -0.5)
    kf = k_ref[...].astype(F32)
    b_scr[...] = b
    qs_scr[...] = qs
    kf_scr[...] = kf

    half = sub // 2
    tl = lax.broadcasted_iota(jnp.int32, (half, GLA_HK), 0)
    zeros_half = jnp.zeros((half, GLA_HK), F32)
    for blk_i in range(GLA_STEP // sub):
        r0 = blk_i * sub
        b_lo, b_hi = b_scr[r0:r0 + half, :], b_scr[r0 + half:r0 + sub, :]
        q_lo, q_hi = qs_scr[r0:r0 + half, :], qs_scr[r0 + half:r0 + sub, :]
        for s in range(sub):
            kb = kf_scr[r0 + s:r0 + s + 1, :]
            bb = b_scr[r0 + s:r0 + s + 1, :]
            d_hi = b_hi - bb
            if s > half:
                d_hi = jnp.where(tl + half >= s, d_hi, -jnp.inf)
            p_hi = q_hi * kb * jnp.exp(d_hi)
            if s < half:
                d_lo = b_lo - bb
                if s > 0:
                    d_lo = jnp.where(tl >= s, d_lo, -jnp.inf)
                p_lo = q_lo * kb * jnp.exp(d_lo)
            else:
                p_lo = zeros_half
            pcat_scr[r0:r0 + sub, s * GLA_HK:(s + 1) * GLA_HK] = (
                jnp.concatenate([p_lo, p_hi], axis=0).astype(BF16))
    a_diag = jnp.dot(pcat_scr[...], e_ref[...], preferred_element_type=F32)

    ri = lax.broadcasted_iota(jnp.int32, (c_len, c_len), 0)
    ci = lax.broadcasted_iota(jnp.int32, (c_len, c_len), 1)
    same_sub = (ri // sub) == (ci // sub)
    gn = gn_ref[...]

    def z(nrows):
        return jnp.zeros((nrows, GLA_HK), F32)

    for c in range(GLA_STEP // c_len):
        base = c * c_len
        bc = b[base:base + c_len]
        qc = qs[base:base + c_len]
        kc = kf[base:base + c_len]
        b15, b31, b47, b63 = bc[15:16], bc[31:32], bc[47:48], bc[63:64]
        q1 = jnp.concatenate([z(16), qc[16:32] * jnp.exp(bc[16:32] - b15), z(32)], axis=0)
        q2 = jnp.concatenate([z(32), qc[32:64] * jnp.exp(bc[32:64] - b31)], axis=0)
        q3 = jnp.concatenate([z(48), qc[48:64] * jnp.exp(bc[48:64] - b47)], axis=0)
        k1 = jnp.concatenate([kc[0:16] * jnp.exp(b15 - bc[0:16]), z(48)], axis=0)
        k2 = jnp.concatenate([kc[0:32] * jnp.exp(b31 - bc[0:32]), z(32)], axis=0)
        k3 = jnp.concatenate([z(32), kc[32:48] * jnp.exp(b47 - bc[32:48]), z(16)], axis=0)
        q_cat = jnp.concatenate([q1, q2, q3], axis=1).astype(BF16)
        k_cat = jnp.concatenate([k1, k2, k3], axis=1).astype(BF16)
        a_off = lax.dot_general(q_cat, k_cat, (((1,), (1,)), ((), ())),
                                preferred_element_type=F32)
        a = a_off + jnp.where(same_sub, a_diag[base:base + c_len], 0.0)

        vc = v_ref[base:base + c_len, :]
        o_intra = jnp.dot(a.astype(BF16), vc, preferred_element_type=F32)

        st = st_ref[...]
        q_in = (qc * jnp.exp(bc)).astype(BF16)
        o_inter = lax.dot_general(q_in, st.astype(BF16), (((1,), (1,)), ((), ())),
                                  preferred_element_type=F32)
        k_st = (kc * jnp.exp(b63 - bc)).astype(BF16)
        upd = lax.dot_general(vc, k_st, (((0,), (0,)), ((), ())),
                              preferred_element_type=F32)
        st_ref[...] = st * jnp.exp(b63) + upd

        o = o_inter + o_intra
        ms = jnp.mean(o * o, axis=-1, keepdims=True)
        y = o * lax.rsqrt(ms + EPS) * gn
        r = r_ref[base:base + c_len, :].astype(F32)
        y = y * (r * jax.nn.sigmoid(r))
        o_ref[base:base + c_len, :] = y.astype(BF16)


def _gla_constants():
    n = GLA_STEP
    idx = np.arange(n)
    ltri = ((idx[:, None] >= idx[None, :]) & (idx[:, None] // GLA_CHUNK == idx[None, :] // GLA_CHUNK))
    rows = np.arange(GLA_SUB * GLA_HK)
    cols = np.arange(GLA_CHUNK)
    e = (rows[:, None] // GLA_HK) == (cols[None, :] % GLA_SUB)
    return jnp.asarray(ltri, BF16), jnp.asarray(e, BF16)


def _gla_call(proj2d, pa, u_pad, bias, gn, batch):
    t = proj2d.shape[0]
    steps = t // batch // GLA_STEP
    ltri, e = _gla_constants()

    def row(bi, h, i):
        return bi * steps + i

    return pl.pallas_call(
        _gla_kernel,
        grid=(batch, GLA_HEADS, steps),
        in_specs=[
            pl.BlockSpec((GLA_STEP, GLA_HK), lambda bi, h, i: (row(bi, h, i), P_GQ // GLA_HK + h)),
            pl.BlockSpec((GLA_STEP, GLA_HK), lambda bi, h, i: (row(bi, h, i), P_GK // GLA_HK + h)),
            pl.BlockSpec((GLA_STEP, GLA_HV), lambda bi, h, i: (row(bi, h, i), P_GV // GLA_HV + h)),
            pl.BlockSpec((GLA_STEP, GLA_HV), lambda bi, h, i: (row(bi, h, i), P_GR // GLA_HV + h)),
            pl.BlockSpec((GLA_STEP, PA_PAD), lambda bi, h, i: (row(bi, h, i), 0)),
            pl.BlockSpec((PA_PAD, GLA_HK), lambda bi, h, i: (0, h)),
            pl.BlockSpec((1, GLA_HK), lambda bi, h, i: (0, h)),
            pl.BlockSpec((1, GLA_HV), lambda bi, h, i: (0, 0)),
            pl.BlockSpec((GLA_STEP, GLA_STEP), lambda bi, h, i: (0, 0)),
            pl.BlockSpec((GLA_SUB * GLA_HK, GLA_CHUNK), lambda bi, h, i: (0, 0)),
        ],
        out_specs=pl.BlockSpec((GLA_STEP, GLA_HV), lambda bi, h, i: (row(bi, h, i), h)),
        out_shape=jax.ShapeDtypeStruct((t, GLA_DV), BF16),
        scratch_shapes=[
            pltpu.VMEM((GLA_HV, GLA_HK), F32),
            pltpu.VMEM((GLA_STEP, GLA_HK), F32),
            pltpu.VMEM((GLA_STEP, GLA_HK), F32),
            pltpu.VMEM((GLA_STEP, GLA_HK), F32),
            pltpu.VMEM((GLA_STEP, GLA_SUB * GLA_HK), BF16),
        ],
        compiler_params=pltpu.CompilerParams(
            dimension_semantics=("parallel", "parallel", "arbitrary"),
            vmem_limit_bytes=VMEM_LIMIT_BYTES),
        name="gla",
    )(proj2d, proj2d, proj2d, proj2d, pa, u_pad, bias, gn, ltri, e)


FF_CHUNK = 1024


def _split_dot(w, rhs):
    hi = w.astype(BF16)
    lo = (w - hi.astype(F32)).astype(BF16)
    return (jnp.dot(hi, rhs, preferred_element_type=F32)
            + jnp.dot(lo, rhs, preferred_element_type=F32))


def _out_kernel(x_ref, o0_ref, o1_ref, o2_ref, l0_ref, l1_ref, l2_ref, og_ref, gate_ref, gbias_ref,
                hx_ref, wa_ref, wb_ref, wo_ref, g2_ref, wup_ref, wdn_ref, out_ref):
    l0, l1, l2 = l0_ref[0], l1_ref[0], l2_ref[0]
    mx = jnp.maximum(jnp.maximum(l0, l1), l2)
    e0, e1, e2 = jnp.exp(l0 - mx), jnp.exp(l1 - mx), jnp.exp(l2 - mx)
    inv = 1.0 / (e0 + e1 + e2)
    hx = hx_ref[...]
    def pairs(ref):
        return jnp.concatenate([ref[0, p] for p in range(N_PAIRS)], axis=1)

    o_attn = (_split_dot(e0 * inv, hx) * pairs(o0_ref) + _split_dot(e1 * inv, hx) * pairs(o1_ref)
              + _split_dot(e2 * inv, hx) * pairs(o2_ref))
    a = jnp.dot(o_attn.astype(BF16), wa_ref[...], preferred_element_type=F32)
    g = jnp.dot(og_ref[...], wb_ref[...], preferred_element_type=F32)
    gate_a = jax.nn.sigmoid(gate_ref[:, :D_MODEL].astype(F32) + gbias_ref[:, :D_MODEL])
    gate_g = jax.nn.sigmoid(gate_ref[:, D_MODEL:].astype(F32) + gbias_ref[:, D_MODEL:])
    mixed = gate_a * a + gate_g * g
    x1 = x_ref[...] + jnp.dot(mixed.astype(BF16), wo_ref[...], preferred_element_type=F32)

    ms = jnp.mean(x1 * x1, axis=-1, keepdims=True)
    h2 = (x1 * lax.rsqrt(ms + EPS) * g2_ref[...]).astype(BF16)
    acc = x1
    for c in range(D_FF // FF_CHUNK):
        u = jnp.dot(h2, wup_ref[:, c * FF_CHUNK:(c + 1) * FF_CHUNK], preferred_element_type=F32)
        u = jnp.maximum(u, 0.0)
        u = (u * u).astype(BF16)
        acc = acc + jnp.dot(u, wdn_ref[c * FF_CHUNK:(c + 1) * FF_CHUNK, :],
                            preferred_element_type=F32)
    out_ref[...] = acc


def _out_call(x2d, o_groups, lse_groups, o_gla, proj2d, gbias, wa, wb, wo, g2, wup, wdn, batch, tm):
    t = x2d.shape[0]
    lanes = np.arange(LANES)
    cols = np.arange(GROUP_WIDTH)
    head_expand = jnp.asarray(lanes[:, None] == cols[None, :] // HEAD_DIM, BF16)

    tiles_per_seq = t // batch // tm

    def tok(width):
        return pl.BlockSpec((tm, width), lambda i: (i, 0))

    attn_o = pl.BlockSpec((1, N_PAIRS, tm, LANES),
                          lambda i: (i // tiles_per_seq, 0, i % tiles_per_seq, 0))
    attn_lse = pl.BlockSpec((1, tm, LANES), lambda i: (i // tiles_per_seq, i % tiles_per_seq, 0))

    def const(shape):
        return pl.BlockSpec(shape, lambda i: (0, 0), pipeline_mode=pl.Buffered(1))

    return pl.pallas_call(
        _out_kernel,
        grid=(t // tm,),
        in_specs=[
            tok(D_MODEL),
            attn_o, attn_o, attn_o,
            attn_lse, attn_lse, attn_lse,
            tok(GLA_DV),
            pl.BlockSpec((tm, 2 * D_MODEL), lambda i: (i, P_GATE // (2 * D_MODEL))),
            const((1, 2 * D_MODEL)),
            const((LANES, GROUP_WIDTH)),
            const((GROUP_WIDTH, D_MODEL)),
            const((GLA_DV, D_MODEL)),
            const((D_MODEL, D_MODEL)),
            const((1, D_MODEL)),
            const((D_MODEL, D_FF)),
            const((D_FF, D_MODEL)),
        ],
        out_specs=tok(D_MODEL),
        out_shape=jax.ShapeDtypeStruct((t, D_MODEL), F32),
        compiler_params=pltpu.CompilerParams(
            dimension_semantics=("parallel",),
            vmem_limit_bytes=VMEM_LIMIT_BYTES),
        name="out",
    )(x2d, *o_groups, *lse_groups, o_gla, proj2d, gbias, head_expand, wa, wb, wo, g2, wup, wdn)


def _layer(x2d, batch, norm1_g, w_in, gq, gk, gate_up, gate_bias, gla_norm_g, branch_bias,
           w_a, w_b, w_out, norm2_g, w_up, w_down, out_tm=512):
    def cols(off, size):
        return w_in[:, off:off + size]

    w_main = jnp.concatenate([
        cols(O_GATE, 2 * D_MODEL), cols(O_GV, GLA_DV), cols(O_GR, GLA_DV),
        cols(O_GQ, GLA_DK), cols(O_GK, GLA_DK)], axis=1).astype(BF16)
    w_pa = jnp.pad(cols(O_PA, GLA_RANK), ((0, 0), (0, PA_PAD - GLA_RANK))).astype(BF16)
    q_gain = jnp.tile(gq, HEADS_PER_GROUP) * (HEAD_DIM ** -0.5)
    k_gain = jnp.tile(gk, HEADS_PER_GROUP)
    qk_gain = jnp.stack([q_gain, k_gain]).reshape(2, 1, GROUP_WIDTH)
    idx = np.arange(MXU_DIM)
    bd = jnp.asarray((idx[:, None] // HEAD_DIM) == (idx[None, :] // HEAD_DIM), BF16)

    h, h4, h16, pa = _norm_call(x2d, norm1_g.reshape(1, D_MODEL), w_pa)
    proj = _main_proj_call(h, w_main, tm=1024, tn=1024)

    o_groups, lse_groups = [], []
    for g, (h_perm, (_, dilation)) in enumerate(zip((h, h4, h16), ATTN_GROUPS)):
        w_qkv = jnp.concatenate(
            [cols(O_ATTN + (kind * N_GROUPS + g) * GROUP_WIDTH, GROUP_WIDTH)
             for kind in range(N_KINDS)], axis=1).astype(BF16)
        qkv = _attn_proj_call(h_perm, w_qkv, qk_gain, bd, batch, dilation)
        o, lse = _attn_call(qkv, dilation)
        o_groups.append(o)
        lse_groups.append(lse)

    u_pad = jnp.pad(gate_up, ((0, PA_PAD - GLA_RANK), (0, 0))).astype(BF16)
    o_gla = _gla_call(proj, pa, u_pad, gate_bias.reshape(1, GLA_DK),
                      gla_norm_g.reshape(1, GLA_HV), batch)

    return _out_call(x2d, o_groups, lse_groups, o_gla, proj,
                     branch_bias.reshape(1, 2 * D_MODEL),
                     w_a.astype(BF16), w_b.astype(BF16), w_out.astype(BF16),
                     norm2_g.reshape(1, D_MODEL), w_up.astype(BF16), w_down.astype(BF16), batch, out_tm)


def kernel(x, norm1_g, w_in, attn_q_norm_g, attn_k_norm_g, gla_gate_up, gla_gate_bias, gla_out_norm_g, branch_gate_bias, w_attn_branch, w_gla_branch, w_out, norm2_g, w_ff_up, w_ff_down):
    b, s, d = x.shape
    x2d = x.reshape(b * s, d)
    for l in range(norm1_g.shape[0]):
        x2d = _layer(x2d, b, norm1_g[l], w_in[l], attn_q_norm_g[l], attn_k_norm_g[l],
                     gla_gate_up[l], gla_gate_bias[l], gla_out_norm_g[l], branch_gate_bias[l],
                     w_attn_branch[l], w_gla_branch[l], w_out[l], norm2_g[l],
                     w_ff_up[l], w_ff_down[l])
    return x2d.reshape(b, s, d)
```

```python
import functools

import numpy as np
import jax
import jax.numpy as jnp
from jax import lax
from jax.experimental import pallas as pl
from jax.experimental.pallas import tpu as pltpu

F32 = jnp.float32
BF16 = jnp.bfloat16

D_MODEL = 1024
ATTN_GROUPS = ((128, 1), (512, 4), (2048, 16))
N_GROUPS = len(ATTN_GROUPS)
HEADS_PER_GROUP = 8
HEAD_DIM = 64
ATTN_BLOCK = 128
GROUP_WIDTH = HEADS_PER_GROUP * HEAD_DIM
ATTN_WIDTH = 3 * N_GROUPS * GROUP_WIDTH
N_PAIRS = GROUP_WIDTH // 128

GLA_HEADS = 4
GLA_DK = 512
GLA_DV = 1024
GLA_HK = GLA_DK // GLA_HEADS
GLA_HV = GLA_DV // GLA_HEADS
GLA_RANK = 16
GLA_TAU = 16.0
GLA_CHUNK = 64
GLA_SUB = 16
GLA_STEP = 256
GLA_HPS = 4

D_FF = 4 * D_MODEL
EPS = 1e-6

LANES = 128
MXU_DIM = 256
VMEM_LIMIT_BYTES = 56 * 1024 * 1024

_ORIG_SIZES = (ATTN_WIDTH, GLA_DK, GLA_DK, GLA_DV, GLA_DV, GLA_RANK, 2 * D_MODEL)
_ORIG_OFF = tuple(int(v) for v in np.cumsum((0,) + _ORIG_SIZES))
O_ATTN, O_GQ, O_GK, O_GV, O_GR, O_PA, O_GATE = _ORIG_OFF[:7]

P_GATE = 0
P_GV = P_GATE + 2 * D_MODEL
P_GR = P_GV + GLA_DV
P_GQ = P_GR + GLA_DV
P_GK = P_GQ + GLA_DK
P_MAIN = P_GK + GLA_DK
N_KINDS = 3
PA_PAD = LANES
NORM_TM = 1024


def _norm_kernel(x_ref, g1_ref, wpa_ref, h_ref, h4_ref, h16_ref, pa_ref, scr):
    tm = x_ref.shape[0]
    x = x_ref[...]
    ms = jnp.mean(x * x, axis=-1, keepdims=True)
    hf = x * lax.rsqrt(ms + EPS) * g1_ref[...]
    h = hf.astype(BF16)
    h_ref[...] = h
    pa_ref[...] = jnp.dot(h, wpa_ref[...], preferred_element_type=F32)
    n_col = D_MODEL // LANES
    for c in range(n_col):
        scr[c] = hf[:, c * LANES:(c + 1) * LANES]
    for ref in (h4_ref, h16_ref):
        d = ref.shape[0]
        for r in range(d):
            for c in range(n_col):
                ref[r, :, c * LANES:(c + 1) * LANES] = (
                    scr[c, pl.ds(r, tm // d, stride=d), :].astype(BF16))


def _norm_call(x2d, g1, w_pa):
    t = x2d.shape[0]
    tm = NORM_TM
    n_tiles = t // tm
    d4, d16 = ATTN_GROUPS[1][1], ATTN_GROUPS[2][1]
    h, h4, h16, pa = pl.pallas_call(
        _norm_kernel,
        grid=(n_tiles,),
        in_specs=[
            pl.BlockSpec((tm, D_MODEL), lambda i: (i, 0)),
            pl.BlockSpec((1, D_MODEL), lambda i: (0, 0)),
            pl.BlockSpec((D_MODEL, PA_PAD), lambda i: (0, 0)),
        ],
        out_specs=[
            pl.BlockSpec((tm, D_MODEL), lambda i: (i, 0)),
            pl.BlockSpec((d4, tm // d4, D_MODEL), lambda i: (i, 0, 0)),
            pl.BlockSpec((d16, tm // d16, D_MODEL), lambda i: (i, 0, 0)),
            pl.BlockSpec((tm, PA_PAD), lambda i: (i, 0)),
        ],
        out_shape=[
            jax.ShapeDtypeStruct((t, D_MODEL), BF16),
            jax.ShapeDtypeStruct((n_tiles * d4, tm // d4, D_MODEL), BF16),
            jax.ShapeDtypeStruct((n_tiles * d16, tm // d16, D_MODEL), BF16),
            jax.ShapeDtypeStruct((t, PA_PAD), F32),
        ],
        scratch_shapes=[pltpu.VMEM((D_MODEL // LANES, tm, LANES), F32)],
        compiler_params=pltpu.CompilerParams(
            dimension_semantics=("parallel",),
            vmem_limit_bytes=VMEM_LIMIT_BYTES),
        name="norm",
    )(x2d, g1, w_pa)
    return h, h4.reshape(t, D_MODEL), h16.reshape(t, D_MODEL), pa


def _matmul_kernel(h_ref, w_ref, o_ref, *, tn):
    h = h_ref[...]
    for c in range(w_ref.shape[1] // tn):
        cols = slice(c * tn, (c + 1) * tn)
        o_ref[:, cols] = jnp.dot(h, w_ref[:, cols], preferred_element_type=F32).astype(o_ref.dtype)


def _main_proj_call(h, w_main, tm, tn):
    t = h.shape[0]
    n = w_main.shape[1]
    return pl.pallas_call(
        functools.partial(_matmul_kernel, tn=tn),
        grid=(t // tm,),
        in_specs=[
            pl.BlockSpec((tm, D_MODEL), lambda i: (i, 0)),
            pl.BlockSpec((D_MODEL, n), lambda i: (0, 0), pipeline_mode=pl.Buffered(1)),
        ],
        out_specs=pl.BlockSpec((tm, n), lambda i: (i, 0)),
        out_shape=jax.ShapeDtypeStruct((t, n), BF16),
        compiler_params=pltpu.CompilerParams(
            dimension_semantics=("parallel",),
            vmem_limit_bytes=VMEM_LIMIT_BYTES),
        name="proj_main",
    )(h, w_main)


def _attn_proj_kernel(h_ref, w_ref, gain_ref, bd_ref, o_ref):
    d = o_ref.shape[1]
    rows = o_ref.shape[2]
    h = h_ref[...]
    bd = bd_ref[...]
    for kind in range(N_KINDS):
        cols = slice(kind * GROUP_WIDTH, (kind + 1) * GROUP_WIDTH)
        acc = jnp.dot(h, w_ref[:, cols], preferred_element_type=F32)
        if kind < 2:
            sq = (acc * acc).astype(BF16)
            ss = jnp.concatenate(
                [jnp.dot(sq[:, c * MXU_DIM:(c + 1) * MXU_DIM], bd, preferred_element_type=F32)
                 for c in range(GROUP_WIDTH // MXU_DIM)], axis=1)
            acc = acc * lax.rsqrt(ss * (1.0 / HEAD_DIM) + EPS) * gain_ref[kind]
        y = acc.astype(BF16)
        for r in range(d):
            o_ref[0, r, :, cols] = y[r * rows:(r + 1) * rows, :]


def _attn_proj_call(h_perm, w_qkv, qk_gain, bd, batch, dilation):
    t = h_perm.shape[0]
    s = t // batch
    d = dilation
    tm = NORM_TM
    tiles_per_seq = s // tm
    width = N_KINDS * GROUP_WIDTH
    return pl.pallas_call(
        _attn_proj_kernel,
        grid=(t // tm,),
        in_specs=[
            pl.BlockSpec((tm, D_MODEL), lambda i: (i, 0)),
            pl.BlockSpec((D_MODEL, width), lambda i: (0, 0)),
            pl.BlockSpec((2, 1, GROUP_WIDTH), lambda i: (0, 0, 0)),
            pl.BlockSpec((MXU_DIM, MXU_DIM), lambda i: (0, 0)),
        ],
        out_specs=pl.BlockSpec((1, d, tm // d, width),
                               lambda i: (i // tiles_per_seq, 0, i % tiles_per_seq, 0)),
        out_shape=jax.ShapeDtypeStruct((batch, d, s // d, width), BF16),
        compiler_params=pltpu.CompilerParams(
            dimension_semantics=("parallel",),
            vmem_limit_bytes=VMEM_LIMIT_BYTES),
        name=f"proj_attn_d{d}",
    )(h_perm, w_qkv, qk_gain, bd)


def _attn_kernel(q_ref, kp_ref, kc_ref, vp_ref, vc_ref, o_ref, lse_ref):
    n = pl.program_id(1)
    d = q_ref.shape[1]
    blk = ATTN_BLOCK
    qi = lax.broadcasted_iota(jnp.int32, (blk, 2 * blk), 0)
    ki = lax.broadcasted_iota(jnp.int32, (blk, 2 * blk), 1)
    valid = (ki >= qi) & (ki <= qi + blk) & ((ki >= blk) | (n > 0))
    lane = lax.broadcasted_iota(jnp.int32, (blk, LANES), 1)
    first_head = lane < HEAD_DIM

    def residue(r, carry):
        lse_tile = jnp.zeros((blk, LANES), F32)
        for pair in range(N_PAIRS):
            sl = slice(pair * LANES, (pair + 1) * LANES)
            q = q_ref[0, r, :, sl]
            k = jnp.concatenate([kp_ref[0, r, :, sl], kc_ref[0, r, :, sl]], axis=0)
            v = jnp.concatenate([vp_ref[0, r, :, sl], vc_ref[0, r, :, sl]], axis=0)
            zero = jnp.zeros_like(q)
            outs = []
            for half, sel in enumerate((first_head, jnp.logical_not(first_head))):
                qm = jnp.where(sel, q, zero)
                s = lax.dot_general(qm, k, (((1,), (1,)), ((), ())), preferred_element_type=F32)
                s = jnp.where(valid, s, -jnp.inf)
                m = jnp.max(s, axis=-1, keepdims=True)
                p = jnp.exp(s - m)
                l = jnp.sum(p, axis=-1, keepdims=True)
                pv = jnp.dot(p.astype(BF16), v, preferred_element_type=F32)
                outs.append(pv / l)
                lse_tile = jnp.where(lane == 2 * pair + half, m + jnp.log(l), lse_tile)
            o_ref[0, pair, pl.ds(r, blk, stride=d), :] = jnp.where(first_head, outs[0], outs[1])
        lse_ref[0, pl.ds(r, blk, stride=d), :] = lse_tile
        return carry

    lax.fori_loop(0, d, residue, 0)


def _attn_call(qkv, dilation):
    b, d, sub_len, _ = qkv.shape
    assert d == dilation
    blk = ATTN_BLOCK
    nb = sub_len // blk
    s = sub_len * d
    blk_shape = (1, d, blk, GROUP_WIDTH)

    def cur(kind):
        return lambda bi, n: (bi, 0, n, kind)

    def prev(kind):
        return lambda bi, n: (bi, 0, jnp.maximum(n - 1, 0), kind)

    o, lse = pl.pallas_call(
        _attn_kernel,
        grid=(b, nb),
        in_specs=[
            pl.BlockSpec(blk_shape, cur(0)),
            pl.BlockSpec(blk_shape, prev(1)),
            pl.BlockSpec(blk_shape, cur(1)),
            pl.BlockSpec(blk_shape, prev(2)),
            pl.BlockSpec(blk_shape, cur(2)),
        ],
        out_specs=[
            pl.BlockSpec((1, N_PAIRS, d * blk, LANES), lambda bi, n: (bi, 0, n, 0)),
            pl.BlockSpec((1, d * blk, LANES), lambda bi, n: (bi, n, 0)),
        ],
        out_shape=[
            jax.ShapeDtypeStruct((b, N_PAIRS, s, LANES), F32),
            jax.ShapeDtypeStruct((b, s, LANES), F32),
        ],
        compiler_params=pltpu.CompilerParams(
            dimension_semantics=("parallel", "arbitrary"),
            vmem_limit_bytes=VMEM_LIMIT_BYTES),
        name=f"attn_d{d}",
    )(qkv, qkv, qkv, qkv, qkv)
    return o, lse


def _gla_kernel(q_ref, k_ref, v_ref, r_ref, pa_ref, u_ref, bias_ref, gn_ref, ltri_ref, e_ref,
                o_ref, st_ref, b_scr, qs_scr, kf_scr, pcat_scr):
    @pl.when(pl.program_id(2) == 0)
    def _():
        st_ref[...] = jnp.zeros_like(st_ref)

    for hh in range(GLA_HPS):
        kcols = slice(hh * GLA_HK, (hh + 1) * GLA_HK)
        vcols = slice(hh * GLA_HV, (hh + 1) * GLA_HV)
        _gla_head(q_ref.at[:, kcols], k_ref.at[:, kcols], v_ref.at[:, vcols], r_ref.at[:, vcols],
                  pa_ref, u_ref.at[:, kcols], bias_ref.at[:, kcols], gn_ref, ltri_ref, e_ref,
                  o_ref.at[:, vcols], st_ref.at[hh], b_scr.at[hh], qs_scr.at[hh], kf_scr.at[hh],
                  pcat_scr.at[hh])


def _gla_head(q_ref, k_ref, v_ref, r_ref, pa_ref, u_ref, bias_ref, gn_ref, ltri_ref, e_ref,
              o_ref, st_ref, b_scr, qs_scr, kf_scr, pcat_scr):
    c_len, sub = GLA_CHUNK, GLA_SUB

    logits = jnp.dot(pa_ref[...].astype(BF16), u_ref[...], preferred_element_type=F32) + bias_ref[...]
    log_sig = jnp.minimum(logits, 0.0) - jnp.log1p(jnp.exp(-jnp.abs(logits)))
    la = log_sig * (1.0 / GLA_TAU)

    hi = la.astype(BF16)
    r1 = la - hi.astype(F32)
    mid = r1.astype(BF16)
    lo = (r1 - mid.astype(F32)).astype(BF16)
    ltri = ltri_ref[...]
    b = (jnp.dot(ltri, hi, preferred_element_type=F32)
         + jnp.dot(ltri, mid, preferred_element_type=F32)
         + jnp.dot(ltri, lo, preferred_element_type=F32))

    qs = q_ref[...].astype(F32) * (GLA_HK ** -0.5)
    kf = k_ref[...].astype(F32)
    b_scr[...] = b
    qs_scr[...] = qs
    kf_scr[...] = kf

    half = sub // 2
    tl = lax.broadcasted_iota(jnp.int32, (half, GLA_HK), 0)
    zeros_half = jnp.zeros((half, GLA_HK), F32)
    for blk_i in range(GLA_STEP // sub):
        r0 = blk_i * sub
        b_lo, b_hi = b_scr[r0:r0 + half, :], b_scr[r0 + half:r0 + sub, :]
        q_lo, q_hi = qs_scr[r0:r0 + half, :], qs_scr[r0 + half:r0 + sub, :]
        for s in range(sub):
            kb = kf_scr[r0 + s:r0 + s + 1, :]
            bb = b_scr[r0 + s:r0 + s + 1, :]
            d_hi = b_hi - bb
            if s > half:
                d_hi = jnp.where(tl + half >= s, d_hi, -jnp.inf)
            p_hi = q_hi * kb * jnp.exp(d_hi)
            if s < half:
                d_lo = b_lo - bb
                if s > 0:
                    d_lo = jnp.where(tl >= s, d_lo, -jnp.inf)
                p_lo = q_lo * kb * jnp.exp(d_lo)
            else:
                p_lo = zeros_half
            pcat_scr[r0:r0 + sub, s * GLA_HK:(s + 1) * GLA_HK] = (
                jnp.concatenate([p_lo, p_hi], axis=0).astype(BF16))
    a_diag = jnp.dot(pcat_scr[...], e_ref[...], preferred_element_type=F32)

    ri = lax.broadcasted_iota(jnp.int32, (c_len, c_len), 0)
    ci = lax.broadcasted_iota(jnp.int32, (c_len, c_len), 1)
    same_sub = (ri // sub) == (ci // sub)
    gn = gn_ref[...]

    def z(nrows):
        return jnp.zeros((nrows, GLA_HK), F32)

    for c in range(GLA_STEP // c_len):
        base = c * c_len
        bc = b[base:base + c_len]
        qc = qs[base:base + c_len]
        kc = kf[base:base + c_len]
        b15, b31, b47, b63 = bc[15:16], bc[31:32], bc[47:48], bc[63:64]
        q1 = jnp.concatenate([z(16), qc[16:32] * jnp.exp(bc[16:32] - b15), z(32)], axis=0)
        q2 = jnp.concatenate([z(32), qc[32:64] * jnp.exp(bc[32:64] - b31)], axis=0)
        q3 = jnp.concatenate([z(48), qc[48:64] * jnp.exp(bc[48:64] - b47)], axis=0)
        k1 = jnp.concatenate([kc[0:16] * jnp.exp(b15 - bc[0:16]), z(48)], axis=0)
        k2 = jnp.concatenate([kc[0:32] * jnp.exp(b31 - bc[0:32]), z(32)], axis=0)
        k3 = jnp.concatenate([z(32), kc[32:48] * jnp.exp(b47 - bc[32:48]), z(16)], axis=0)
        q_cat = jnp.concatenate([q1, q2, q3], axis=1).astype(BF16)
        k_cat = jnp.concatenate([k1, k2, k3], axis=1).astype(BF16)
        a_off = lax.dot_general(q_cat, k_cat, (((1,), (1,)), ((), ())),
                                preferred_element_type=F32)
        a = a_off + jnp.where(same_sub, a_diag[base:base + c_len], 0.0)

        vc = v_ref[base:base + c_len, :]
        o_intra = jnp.dot(a.astype(BF16), vc, preferred_element_type=F32)

        st = st_ref[...]
        q_in = (qc * jnp.exp(bc)).astype(BF16)
        o_inter = lax.dot_general(q_in, st.astype(BF16), (((1,), (1,)), ((), ())),
                                  preferred_element_type=F32)
        k_st = (kc * jnp.exp(b63 - bc)).astype(BF16)
        upd = lax.dot_general(vc, k_st, (((0,), (0,)), ((), ())),
                              preferred_element_type=F32)
        st_ref[...] = st * jnp.exp(b63) + upd

        o = o_inter + o_intra
        ms = jnp.mean(o * o, axis=-1, keepdims=True)
        y = o * lax.rsqrt(ms + EPS) * gn
        r = r_ref[base:base + c_len, :].astype(F32)
        y = y * (r * jax.nn.sigmoid(r))
        o_ref[base:base + c_len, :] = y.astype(BF16)


def _gla_constants():
    n = GLA_STEP
    idx = np.arange(n)
    ltri = ((idx[:, None] >= idx[None, :]) & (idx[:, None] // GLA_CHUNK == idx[None, :] // GLA_CHUNK))
    rows = np.arange(GLA_SUB * GLA_HK)
    cols = np.arange(GLA_CHUNK)
    e = (rows[:, None] // GLA_HK) == (cols[None, :] % GLA_SUB)
    return jnp.asarray(ltri, BF16), jnp.asarray(e, BF16)


def _gla_call(proj2d, pa, u_pad, bias, gn, batch):
    t = proj2d.shape[0]
    steps = t // batch // GLA_STEP
    ltri, e = _gla_constants()

    def row(bi, h, i):
        return bi * steps + i

    hk, hv = GLA_HPS * GLA_HK, GLA_HPS * GLA_HV
    return pl.pallas_call(
        _gla_kernel,
        grid=(batch, GLA_HEADS // GLA_HPS, steps),
        in_specs=[
            pl.BlockSpec((GLA_STEP, hk), lambda bi, h, i: (row(bi, h, i), P_GQ // hk + h)),
            pl.BlockSpec((GLA_STEP, hk), lambda bi, h, i: (row(bi, h, i), P_GK // hk + h)),
            pl.BlockSpec((GLA_STEP, hv), lambda bi, h, i: (row(bi, h, i), P_GV // hv + h)),
            pl.BlockSpec((GLA_STEP, hv), lambda bi, h, i: (row(bi, h, i), P_GR // hv + h)),
            pl.BlockSpec((GLA_STEP, PA_PAD), lambda bi, h, i: (row(bi, h, i), 0)),
            pl.BlockSpec((PA_PAD, hk), lambda bi, h, i: (0, h)),
            pl.BlockSpec((1, hk), lambda bi, h, i: (0, h)),
            pl.BlockSpec((1, GLA_HV), lambda bi, h, i: (0, 0)),
            pl.BlockSpec((GLA_STEP, GLA_STEP), lambda bi, h, i: (0, 0)),
            pl.BlockSpec((GLA_SUB * GLA_HK, GLA_CHUNK), lambda bi, h, i: (0, 0)),
        ],
        out_specs=pl.BlockSpec((GLA_STEP, hv), lambda bi, h, i: (row(bi, h, i), h)),
        out_shape=jax.ShapeDtypeStruct((t, GLA_DV), BF16),
        scratch_shapes=[
            pltpu.VMEM((GLA_HPS, GLA_HV, GLA_HK), F32),
            pltpu.VMEM((GLA_HPS, GLA_STEP, GLA_HK), F32),
            pltpu.VMEM((GLA_HPS, GLA_STEP, GLA_HK), F32),
            pltpu.VMEM((GLA_HPS, GLA_STEP, GLA_HK), F32),
            pltpu.VMEM((GLA_HPS, GLA_STEP, GLA_SUB * GLA_HK), BF16),
        ],
        compiler_params=pltpu.CompilerParams(
            dimension_semantics=("parallel", "parallel", "arbitrary"),
            vmem_limit_bytes=VMEM_LIMIT_BYTES),
        name="gla",
    )(proj2d, proj2d, proj2d, proj2d, pa, u_pad, bias, gn, ltri, e)


FF_CHUNK = 1024


def _split_dot(w, rhs):
    hi = w.astype(BF16)
    lo = (w - hi.astype(F32)).astype(BF16)
    return (jnp.dot(hi, rhs, preferred_element_type=F32)
            + jnp.dot(lo, rhs, preferred_element_type=F32))


def _out_kernel(x_ref, o0_ref, o1_ref, o2_ref, l0_ref, l1_ref, l2_ref, og_ref, gate_ref, gbias_ref,
                hx_ref, wa_ref, wb_ref, wo_ref, g2_ref, wup_ref, wdn_ref, out_ref):
    l0, l1, l2 = l0_ref[0], l1_ref[0], l2_ref[0]
    mx = jnp.maximum(jnp.maximum(l0, l1), l2)
    e0, e1, e2 = jnp.exp(l0 - mx), jnp.exp(l1 - mx), jnp.exp(l2 - mx)
    inv = 1.0 / (e0 + e1 + e2)
    hx = hx_ref[...]
    def pairs(ref):
        return jnp.concatenate([ref[0, p] for p in range(N_PAIRS)], axis=1)

    o_attn = (_split_dot(e0 * inv, hx) * pairs(o0_ref) + _split_dot(e1 * inv, hx) * pairs(o1_ref)
              + _split_dot(e2 * inv, hx) * pairs(o2_ref))
    a = jnp.dot(o_attn.astype(BF16), wa_ref[...], preferred_element_type=F32)
    g = jnp.dot(og_ref[...], wb_ref[...], preferred_element_type=F32)
    gate_a = jax.nn.sigmoid(gate_ref[:, :D_MODEL].astype(F32) + gbias_ref[:, :D_MODEL])
    gate_g = jax.nn.sigmoid(gate_ref[:, D_MODEL:].astype(F32) + gbias_ref[:, D_MODEL:])
    mixed = gate_a * a + gate_g * g
    x1 = x_ref[...] + jnp.dot(mixed.astype(BF16), wo_ref[...], preferred_element_type=F32)

    ms = jnp.mean(x1 * x1, axis=-1, keepdims=True)
    h2 = (x1 * lax.rsqrt(ms + EPS) * g2_ref[...]).astype(BF16)
    acc = x1
    for c in range(D_FF // FF_CHUNK):
        u = jnp.dot(h2, wup_ref[:, c * FF_CHUNK:(c + 1) * FF_CHUNK], preferred_element_type=F32)
        u = jnp.maximum(u, 0.0)
        u = (u * u).astype(BF16)
        acc = acc + jnp.dot(u, wdn_ref[c * FF_CHUNK:(c + 1) * FF_CHUNK, :],
                            preferred_element_type=F32)
    out_ref[...] = acc


def _out_call(x2d, o_groups, lse_groups, o_gla, proj2d, gbias, wa, wb, wo, g2, wup, wdn, batch, tm):
    t = x2d.shape[0]
    lanes = np.arange(LANES)
    cols = np.arange(GROUP_WIDTH)
    head_expand = jnp.asarray(lanes[:, None] == cols[None, :] // HEAD_DIM, BF16)

    tiles_per_seq = t // batch // tm

    def tok(width):
        return pl.BlockSpec((tm, width), lambda i: (i, 0))

    attn_o = pl.BlockSpec((1, N_PAIRS, tm, LANES),
                          lambda i: (i // tiles_per_seq, 0, i % tiles_per_seq, 0))
    attn_lse = pl.BlockSpec((1, tm, LANES), lambda i: (i // tiles_per_seq, i % tiles_per_seq, 0))

    def const(shape):
        return pl.BlockSpec(shape, lambda i: (0, 0), pipeline_mode=pl.Buffered(1))

    return pl.pallas_call(
        _out_kernel,
        grid=(t // tm,),
        in_specs=[
            tok(D_MODEL),
            attn_o, attn_o, attn_o,
            attn_lse, attn_lse, attn_lse,
            tok(GLA_DV),
            pl.BlockSpec((tm, 2 * D_MODEL), lambda i: (i, P_GATE // (2 * D_MODEL))),
            const((1, 2 * D_MODEL)),
            const((LANES, GROUP_WIDTH)),
            const((GROUP_WIDTH, D_MODEL)),
            const((GLA_DV, D_MODEL)),
            const((D_MODEL, D_MODEL)),
            const((1, D_MODEL)),
            const((D_MODEL, D_FF)),
            const((D_FF, D_MODEL)),
        ],
        out_specs=tok(D_MODEL),
        out_shape=jax.ShapeDtypeStruct((t, D_MODEL), F32),
        compiler_params=pltpu.CompilerParams(
            dimension_semantics=("parallel",),
            vmem_limit_bytes=VMEM_LIMIT_BYTES),
        name="out",
    )(x2d, *o_groups, *lse_groups, o_gla, proj2d, gbias, head_expand, wa, wb, wo, g2, wup, wdn)


def _layer(x2d, batch, norm1_g, w_in, gq, gk, gate_up, gate_bias, gla_norm_g, branch_bias,
           w_a, w_b, w_out, norm2_g, w_up, w_down, out_tm=512):
    def cols(off, size):
        return w_in[:, off:off + size]

    w_main = jnp.concatenate([
        cols(O_GATE, 2 * D_MODEL), cols(O_GV, GLA_DV), cols(O_GR, GLA_DV),
        cols(O_GQ, GLA_DK), cols(O_GK, GLA_DK)], axis=1).astype(BF16)
    w_pa = jnp.pad(cols(O_PA, GLA_RANK), ((0, 0), (0, PA_PAD - GLA_RANK))).astype(BF16)
    q_gain = jnp.tile(gq, HEADS_PER_GROUP) * (HEAD_DIM ** -0.5)
    k_gain = jnp.tile(gk, HEADS_PER_GROUP)
    qk_gain = jnp.stack([q_gain, k_gain]).reshape(2, 1, GROUP_WIDTH)
    idx = np.arange(MXU_DIM)
    bd = jnp.asarray((idx[:, None] // HEAD_DIM) == (idx[None, :] // HEAD_DIM), BF16)

    h, h4, h16, pa = _norm_call(x2d, norm1_g.reshape(1, D_MODEL), w_pa)
    proj = _main_proj_call(h, w_main, tm=1024, tn=1024)

    o_groups, lse_groups = [], []
    for g, (h_perm, (_, dilation)) in enumerate(zip((h, h4, h16), ATTN_GROUPS)):
        w_qkv = jnp.concatenate(
            [cols(O_ATTN + (kind * N_GROUPS + g) * GROUP_WIDTH, GROUP_WIDTH)
             for kind in range(N_KINDS)], axis=1).astype(BF16)
        qkv = _attn_proj_call(h_perm, w_qkv, qk_gain, bd, batch, dilation)
        o, lse = _attn_call(qkv, dilation)
        o_groups.append(o)
        lse_groups.append(lse)

    u_pad = jnp.pad(gate_up, ((0, PA_PAD - GLA_RANK), (0, 0))).astype(BF16)
    o_gla = _gla_call(proj, pa, u_pad, gate_bias.reshape(1, GLA_DK),
                      gla_norm_g.reshape(1, GLA_HV), batch)

    return _out_call(x2d, o_groups, lse_groups, o_gla, proj,
                     branch_bias.reshape(1, 2 * D_MODEL),
                     w_a.astype(BF16), w_b.astype(BF16), w_out.astype(BF16),
                     norm2_g.reshape(1, D_MODEL), w_up.astype(BF16), w_down.astype(BF16), batch, out_tm)


def kernel(x, norm1_g, w_in, attn_q_norm_g, attn_k_norm_g, gla_gate_up, gla_gate_bias, gla_out_norm_g, branch_gate_bias, w_attn_branch, w_gla_branch, w_out, norm2_g, w_ff_up, w_ff_down):
    b, s, d = x.shape
    x2d = x.reshape(b * s, d)
    for l in range(norm1_g.shape[0]):
        x2d = _layer(x2d, b, norm1_g[l], w_in[l], attn_q_norm_g[l], attn_k_norm_g[l],
                     gla_gate_up[l], gla_gate_bias[l], gla_out_norm_g[l], branch_gate_bias[l],
                     w_attn_branch[l], w_gla_branch[l], w_out[l], norm2_g[l],
                     w_ff_up[l], w_ff_down[l])
    return x2d.reshape(b, s, d)
```

```python
import functools

import numpy as np
import jax
import jax.numpy as jnp
from jax import lax
from jax.experimental import pallas as pl
from jax.experimental.pallas import tpu as pltpu

F32 = jnp.float32
BF16 = jnp.bfloat16

D_MODEL = 1024
ATTN_GROUPS = ((128, 1), (512, 4), (2048, 16))
N_GROUPS = len(ATTN_GROUPS)
HEADS_PER_GROUP = 8
HEAD_DIM = 64
ATTN_BLOCK = 128
GROUP_WIDTH = HEADS_PER_GROUP * HEAD_DIM
ATTN_WIDTH = 3 * N_GROUPS * GROUP_WIDTH
N_PAIRS = GROUP_WIDTH // 128

GLA_HEADS = 4
GLA_DK = 512
GLA_DV = 1024
GLA_HK = GLA_DK // GLA_HEADS
GLA_HV = GLA_DV // GLA_HEADS
GLA_RANK = 16
GLA_TAU = 16.0
GLA_CHUNK = 64
GLA_SUB = 16
GLA_STEP = 256
GLA_HPS = 4

D_FF = 4 * D_MODEL
EPS = 1e-6

LANES = 128
MXU_DIM = 256
VMEM_LIMIT_BYTES = 56 * 1024 * 1024

_ORIG_SIZES = (ATTN_WIDTH, GLA_DK, GLA_DK, GLA_DV, GLA_DV, GLA_RANK, 2 * D_MODEL)
_ORIG_OFF = tuple(int(v) for v in np.cumsum((0,) + _ORIG_SIZES))
O_ATTN, O_GQ, O_GK, O_GV, O_GR, O_PA, O_GATE = _ORIG_OFF[:7]

P_GATE = 0
P_GV = P_GATE + 2 * D_MODEL
P_GR = P_GV + GLA_DV
P_GQ = P_GR + GLA_DV
P_GK = P_GQ + GLA_DK
P_MAIN = P_GK + GLA_DK
N_KINDS = 3
PA_PAD = LANES


def _rms_norm_rows(x, gain):
    ms = jnp.mean(x * x, axis=-1, keepdims=True)
    return x * lax.rsqrt(ms + EPS) * gain


def _main_proj_kernel(x_ref, g1_ref, w_ref, wpa_ref, o_ref, pa_ref, *, rows, tn):
    gain = g1_ref[...]
    for rc in range(x_ref.shape[0] // rows):
        rs = slice(rc * rows, (rc + 1) * rows)
        h = _rms_norm_rows(x_ref[rs, :], gain).astype(BF16)
        pa_ref[rs, :] = jnp.dot(h, wpa_ref[...], preferred_element_type=F32)
        for c in range(w_ref.shape[1] // tn):
            cols = slice(c * tn, (c + 1) * tn)
            o_ref[rs, cols] = jnp.dot(h, w_ref[:, cols], preferred_element_type=F32).astype(BF16)


def _main_proj_call(x2d, g1, w_main, w_pa, tm, rows, tn):
    t = x2d.shape[0]
    n = w_main.shape[1]

    def const(shape):
        return pl.BlockSpec(shape, lambda i: (0, 0), pipeline_mode=pl.Buffered(1))

    return pl.pallas_call(
        functools.partial(_main_proj_kernel, rows=rows, tn=tn),
        grid=(t // tm,),
        in_specs=[
            pl.BlockSpec((tm, D_MODEL), lambda i: (i, 0)),
            const((1, D_MODEL)),
            const((D_MODEL, n)),
            const((D_MODEL, PA_PAD)),
        ],
        out_specs=[
            pl.BlockSpec((tm, n), lambda i: (i, 0)),
            pl.BlockSpec((tm, PA_PAD), lambda i: (i, 0)),
        ],
        out_shape=[
            jax.ShapeDtypeStruct((t, n), BF16),
            jax.ShapeDtypeStruct((t, PA_PAD), F32),
        ],
        compiler_params=pltpu.CompilerParams(
            dimension_semantics=("parallel",),
            vmem_limit_bytes=VMEM_LIMIT_BYTES),
        name="proj_main",
    )(x2d, g1, w_main, w_pa)


def _qkv_project(h, w_ref, gain_ref, bd, o_ref):
    d = o_ref.shape[1]
    rows = o_ref.shape[2]
    for kind in range(N_KINDS):
        cols = slice(kind * GROUP_WIDTH, (kind + 1) * GROUP_WIDTH)
        acc = jnp.dot(h, w_ref[:, cols], preferred_element_type=F32)
        if kind < 2:
            sq = (acc * acc).astype(BF16)
            ss = jnp.concatenate(
                [jnp.dot(sq[:, c * MXU_DIM:(c + 1) * MXU_DIM], bd, preferred_element_type=F32)
                 for c in range(GROUP_WIDTH // MXU_DIM)], axis=1)
            acc = acc * lax.rsqrt(ss * (1.0 / HEAD_DIM) + EPS) * gain_ref[kind]
        y = acc.astype(BF16)
        for r in range(d):
            o_ref[0, r, :, cols] = y[r * rows:(r + 1) * rows, :]


def _attn_proj_kernel(x_ref, g1_ref, w0_ref, w1_ref, w2_ref, gain_ref, bd_ref,
                      o0_ref, o1_ref, o2_ref, col_scr, perm1_scr, perm2_scr):
    tm = x_ref.shape[0]
    n_col = D_MODEL // LANES
    bd = bd_ref[...]
    hf = _rms_norm_rows(x_ref[...], g1_ref[...])
    for c in range(n_col):
        col_scr[c] = hf[:, c * LANES:(c + 1) * LANES]
    _qkv_project(hf.astype(BF16), w0_ref, gain_ref, bd, o0_ref)
    for w_ref, o_ref, perm_scr in ((w1_ref, o1_ref, perm1_scr), (w2_ref, o2_ref, perm2_scr)):
        d = o_ref.shape[1]
        rows = tm // d
        for r in range(d):
            for c in range(n_col):
                perm_scr[r * rows:(r + 1) * rows, c * LANES:(c + 1) * LANES] = (
                    col_scr[c, pl.ds(r, rows, stride=d), :].astype(BF16))
        _qkv_project(perm_scr[...], w_ref, gain_ref, bd, o_ref)


def _attn_proj_call(x2d, g1, w_qkv, qk_gain, bd, batch, tm):
    t = x2d.shape[0]
    s = t // batch
    tiles_per_seq = s // tm
    width = N_KINDS * GROUP_WIDTH

    def const(shape):
        return pl.BlockSpec(shape, lambda i: (0,) * len(shape), pipeline_mode=pl.Buffered(1))

    return pl.pallas_call(
        _attn_proj_kernel,
        grid=(t // tm,),
        in_specs=[
            pl.BlockSpec((tm, D_MODEL), lambda i: (i, 0)),
            const((1, D_MODEL)),
            const((D_MODEL, width)), const((D_MODEL, width)), const((D_MODEL, width)),
            const((2, 1, GROUP_WIDTH)),
            const((MXU_DIM, MXU_DIM)),
        ],
        out_specs=[
            pl.BlockSpec((1, d, tm // d, width),
                         lambda i: (i // tiles_per_seq, 0, i % tiles_per_seq, 0))
            for _, d in ATTN_GROUPS],
        out_shape=[jax.ShapeDtypeStruct((batch, d, s // d, width), BF16) for _, d in ATTN_GROUPS],
        scratch_shapes=[
            pltpu.VMEM((D_MODEL // LANES, tm, LANES), F32),
            pltpu.VMEM((tm, D_MODEL), BF16),
            pltpu.VMEM((tm, D_MODEL), BF16),
        ],
        compiler_params=pltpu.CompilerParams(
            dimension_semantics=("parallel",),
            vmem_limit_bytes=VMEM_LIMIT_BYTES),
        name="proj_attn",
    )(x2d, g1, *w_qkv, qk_gain, bd)


def _attn_kernel(q_ref, kp_ref, kc_ref, vp_ref, vc_ref, o_ref, lse_ref):
    n = pl.program_id(1)
    d = q_ref.shape[1]
    blk = ATTN_BLOCK
    qi = lax.broadcasted_iota(jnp.int32, (blk, 2 * blk), 0)
    ki = lax.broadcasted_iota(jnp.int32, (blk, 2 * blk), 1)
    valid = (ki >= qi) & (ki <= qi + blk) & ((ki >= blk) | (n > 0))
    lane = lax.broadcasted_iota(jnp.int32, (blk, LANES), 1)
    first_head = lane < HEAD_DIM

    def residue(r, carry):
        lse_tile = jnp.zeros((blk, LANES), F32)
        for pair in range(N_PAIRS):
            sl = slice(pair * LANES, (pair + 1) * LANES)
            q = q_ref[0, r, :, sl]
            k = jnp.concatenate([kp_ref[0, r, :, sl], kc_ref[0, r, :, sl]], axis=0)
            v = jnp.concatenate([vp_ref[0, r, :, sl], vc_ref[0, r, :, sl]], axis=0)
            zero = jnp.zeros_like(q)
            outs = []
            for half, sel in enumerate((first_head, jnp.logical_not(first_head))):
                qm = jnp.where(sel, q, zero)
                s = lax.dot_general(qm, k, (((1,), (1,)), ((), ())), preferred_element_type=F32)
                s = jnp.where(valid, s, -jnp.inf)
                m = jnp.max(s, axis=-1, keepdims=True)
                p = jnp.exp(s - m)
                l = jnp.sum(p, axis=-1, keepdims=True)
                pv = jnp.dot(p.astype(BF16), v, preferred_element_type=F32)
                outs.append(pv / l)
                lse_tile = jnp.where(lane == 2 * pair + half, m + jnp.log(l), lse_tile)
            o_ref[0, pair, pl.ds(r, blk, stride=d), :] = jnp.where(first_head, outs[0], outs[1])
        lse_ref[0, pl.ds(r, blk, stride=d), :] = lse_tile
        return carry

    lax.fori_loop(0, d, residue, 0)


def _attn_call(qkv, dilation):
    b, d, sub_len, _ = qkv.shape
    assert d == dilation
    blk = ATTN_BLOCK
    nb = sub_len // blk
    s = sub_len * d
    blk_shape = (1, d, blk, GROUP_WIDTH)

    def cur(kind):
        return lambda bi, n: (bi, 0, n, kind)

    def prev(kind):
        return lambda bi, n: (bi, 0, jnp.maximum(n - 1, 0), kind)

    o, lse = pl.pallas_call(
        _attn_kernel,
        grid=(b, nb),
        in_specs=[
            pl.BlockSpec(blk_shape, cur(0)),
            pl.BlockSpec(blk_shape, prev(1)),
            pl.BlockSpec(blk_shape, cur(1)),
            pl.BlockSpec(blk_shape, prev(2)),
            pl.BlockSpec(blk_shape, cur(2)),
        ],
        out_specs=[
            pl.BlockSpec((1, N_PAIRS, d * blk, LANES), lambda bi, n: (bi, 0, n, 0)),
            pl.BlockSpec((1, d * blk, LANES), lambda bi, n: (bi, n, 0)),
        ],
        out_shape=[
            jax.ShapeDtypeStruct((b, N_PAIRS, s, LANES), F32),
            jax.ShapeDtypeStruct((b, s, LANES), F32),
        ],
        compiler_params=pltpu.CompilerParams(
            dimension_semantics=("parallel", "arbitrary"),
            vmem_limit_bytes=VMEM_LIMIT_BYTES),
        name=f"attn_d{d}",
    )(qkv, qkv, qkv, qkv, qkv)
    return o, lse


def _gla_kernel(q_ref, k_ref, v_ref, r_ref, pa_ref, u_ref, bias_ref, gn_ref, ltri_ref, e_ref,
                o_ref, st_ref, b_scr, qs_scr, kf_scr, pcat_scr):
    @pl.when(pl.program_id(2) == 0)
    def _():
        st_ref[...] = jnp.zeros_like(st_ref)

    for hh in range(GLA_HPS):
        kcols = slice(hh * GLA_HK, (hh + 1) * GLA_HK)
        vcols = slice(hh * GLA_HV, (hh + 1) * GLA_HV)
        _gla_head(q_ref.at[:, kcols], k_ref.at[:, kcols], v_ref.at[:, vcols], r_ref.at[:, vcols],
                  pa_ref, u_ref.at[:, kcols], bias_ref.at[:, kcols], gn_ref, ltri_ref, e_ref,
                  o_ref.at[:, vcols], st_ref.at[hh], b_scr.at[hh], qs_scr.at[hh], kf_scr.at[hh],
                  pcat_scr.at[hh])


def _gla_head(q_ref, k_ref, v_ref, r_ref, pa_ref, u_ref, bias_ref, gn_ref, ltri_ref, e_ref,
              o_ref, st_ref, b_scr, qs_scr, kf_scr, pcat_scr):
    c_len, sub = GLA_CHUNK, GLA_SUB

    logits = jnp.dot(pa_ref[...].astype(BF16), u_ref[...], preferred_element_type=F32) + bias_ref[...]
    log_sig = jnp.minimum(logits, 0.0) - jnp.log1p(jnp.exp(-jnp.abs(logits)))
    la = log_sig * (1.0 / GLA_TAU)

    hi = la.astype(BF16)
    r1 = la - hi.astype(F32)
    mid = r1.astype(BF16)
    lo = (r1 - mid.astype(F32)).astype(BF16)
    ltri = ltri_ref[...]
    b = (jnp.dot(ltri, hi, preferred_element_type=F32)
         + jnp.dot(ltri, mid, preferred_element_type=F32)
         + jnp.dot(ltri, lo, preferred_element_type=F32))

    qs = q_ref[...].astype(F32) * (GLA_HK ** -0.5)
    kf = k_ref[...].astype(F32)
    b_scr[...] = b
    qs_scr[...] = qs
    kf_scr[...] = kf

    half = sub // 2
    tl = lax.broadcasted_iota(jnp.int32, (half, GLA_HK), 0)
    zeros_half = jnp.zeros((half, GLA_HK), F32)
    for blk_i in range(GLA_STEP // sub):
        r0 = blk_i * sub
        b_lo, b_hi = b_scr[r0:r0 + half, :], b_scr[r0 + half:r0 + sub, :]
        q_lo, q_hi = qs_scr[r0:r0 + half, :], qs_scr[r0 + half:r0 + sub, :]
        for s in range(sub):
            kb = kf_scr[r0 + s:r0 + s + 1, :]
            bb = b_scr[r0 + s:r0 + s + 1, :]
            d_hi = b_hi - bb
            if s > half:
                d_hi = jnp.where(tl + half >= s, d_hi, -jnp.inf)
            p_hi = q_hi * kb * jnp.exp(d_hi)
            if s < half:
                d_lo = b_lo - bb
                if s > 0:
                    d_lo = jnp.where(tl >= s, d_lo, -jnp.inf)
                p_lo = q_lo * kb * jnp.exp(d_lo)
            else:
                p_lo = zeros_half
            pcat_scr[r0:r0 + sub, s * GLA_HK:(s + 1) * GLA_HK] = (
                jnp.concatenate([p_lo, p_hi], axis=0).astype(BF16))
    a_diag = jnp.dot(pcat_scr[...], e_ref[...], preferred_element_type=F32)

    ri = lax.broadcasted_iota(jnp.int32, (c_len, c_len), 0)
    ci = lax.broadcasted_iota(jnp.int32, (c_len, c_len), 1)
    same_sub = (ri // sub) == (ci // sub)
    gn = gn_ref[...]

    def z(nrows):
        return jnp.zeros((nrows, GLA_HK), F32)

    for c in range(GLA_STEP // c_len):
        base = c * c_len
        bc = b[base:base + c_len]
        qc = qs[base:base + c_len]
        kc = kf[base:base + c_len]
        b15, b31, b47, b63 = bc[15:16], bc[31:32], bc[47:48], bc[63:64]
        q1 = jnp.concatenate([z(16), qc[16:32] * jnp.exp(bc[16:32] - b15), z(32)], axis=0)
        q2 = jnp.concatenate([z(32), qc[32:64] * jnp.exp(bc[32:64] - b31)], axis=0)
        q3 = jnp.concatenate([z(48), qc[48:64] * jnp.exp(bc[48:64] - b47)], axis=0)
        k1 = jnp.concatenate([kc[0:16] * jnp.exp(b15 - bc[0:16]), z(48)], axis=0)
        k2 = jnp.concatenate([kc[0:32] * jnp.exp(b31 - bc[0:32]), z(32)], axis=0)
        k3 = jnp.concatenate([z(32), kc[32:48] * jnp.exp(b47 - bc[32:48]), z(16)], axis=0)
        q_cat = jnp.concatenate([q1, q2, q3], axis=1).astype(BF16)
        k_cat = jnp.concatenate([k1, k2, k3], axis=1).astype(BF16)
        a_off = lax.dot_general(q_cat, k_cat, (((1,), (1,)), ((), ())),
                                preferred_element_type=F32)
        a = a_off + jnp.where(same_sub, a_diag[base:base + c_len], 0.0)

        vc = v_ref[base:base + c_len, :]
        o_intra = jnp.dot(a.astype(BF16), vc, preferred_element_type=F32)

        st = st_ref[...]
        q_in = (qc * jnp.exp(bc)).astype(BF16)
        o_inter = lax.dot_general(q_in, st.astype(BF16), (((1,), (1,)), ((), ())),
                                  preferred_element_type=F32)
        k_st = (kc * jnp.exp(b63 - bc)).astype(BF16)
        upd = lax.dot_general(vc, k_st, (((0,), (0,)), ((), ())),
                              preferred_element_type=F32)
        st_ref[...] = st * jnp.exp(b63) + upd

        o = o_inter + o_intra
        ms = jnp.mean(o * o, axis=-1, keepdims=True)
        y = o * lax.rsqrt(ms + EPS) * gn
        r = r_ref[base:base + c_len, :].astype(F32)
        y = y * (r * jax.nn.sigmoid(r))
        o_ref[base:base + c_len, :] = y.astype(BF16)


def _gla_constants():
    n = GLA_STEP
    idx = np.arange(n)
    ltri = ((idx[:, None] >= idx[None, :]) & (idx[:, None] // GLA_CHUNK == idx[None, :] // GLA_CHUNK))
    rows = np.arange(GLA_SUB * GLA_HK)
    cols = np.arange(GLA_CHUNK)
    e = (rows[:, None] // GLA_HK) == (cols[None, :] % GLA_SUB)
    return jnp.asarray(ltri, BF16), jnp.asarray(e, BF16)


def _gla_call(proj2d, pa, u_pad, bias, gn, batch):
    t = proj2d.shape[0]
    steps = t // batch // GLA_STEP
    ltri, e = _gla_constants()

    def row(bi, h, i):
        return bi * steps + i

    hk, hv = GLA_HPS * GLA_HK, GLA_HPS * GLA_HV
    return pl.pallas_call(
        _gla_kernel,
        grid=(batch, GLA_HEADS // GLA_HPS, steps),
        in_specs=[
            pl.BlockSpec((GLA_STEP, hk), lambda bi, h, i: (row(bi, h, i), P_GQ // hk + h)),
            pl.BlockSpec((GLA_STEP, hk), lambda bi, h, i: (row(bi, h, i), P_GK // hk + h)),
            pl.BlockSpec((GLA_STEP, hv), lambda bi, h, i: (row(bi, h, i), P_GV // hv + h)),
            pl.BlockSpec((GLA_STEP, hv), lambda bi, h, i: (row(bi, h, i), P_GR // hv + h)),
            pl.BlockSpec((GLA_STEP, PA_PAD), lambda bi, h, i: (row(bi, h, i), 0)),
            pl.BlockSpec((PA_PAD, hk), lambda bi, h, i: (0, h)),
            pl.BlockSpec((1, hk), lambda bi, h, i: (0, h)),
            pl.BlockSpec((1, GLA_HV), lambda bi, h, i: (0, 0)),
            pl.BlockSpec((GLA_STEP, GLA_STEP), lambda bi, h, i: (0, 0)),
            pl.BlockSpec((GLA_SUB * GLA_HK, GLA_CHUNK), lambda bi, h, i: (0, 0)),
        ],
        out_specs=pl.BlockSpec((GLA_STEP, hv), lambda bi, h, i: (row(bi, h, i), h)),
        out_shape=jax.ShapeDtypeStruct((t, GLA_DV), BF16),
        scratch_shapes=[
            pltpu.VMEM((GLA_HPS, GLA_HV, GLA_HK), F32),
            pltpu.VMEM((GLA_HPS, GLA_STEP, GLA_HK), F32),
            pltpu.VMEM((GLA_HPS, GLA_STEP, GLA_HK), F32),
            pltpu.VMEM((GLA_HPS, GLA_STEP, GLA_HK), F32),
            pltpu.VMEM((GLA_HPS, GLA_STEP, GLA_SUB * GLA_HK), BF16),
        ],
        compiler_params=pltpu.CompilerParams(
            dimension_semantics=("parallel", "parallel", "arbitrary"),
            vmem_limit_bytes=VMEM_LIMIT_BYTES),
        name="gla",
    )(proj2d, proj2d, proj2d, proj2d, pa, u_pad, bias, gn, ltri, e)


FF_CHUNK = 1024


def _split_dot(w, rhs):
    hi = w.astype(BF16)
    lo = (w - hi.astype(F32)).astype(BF16)
    return (jnp.dot(hi, rhs, preferred_element_type=F32)
            + jnp.dot(lo, rhs, preferred_element_type=F32))


def _out_kernel(x_ref, o0_ref, o1_ref, o2_ref, l0_ref, l1_ref, l2_ref, og_ref, gate_ref, gbias_ref,
                hx_ref, wa_ref, wb_ref, wo_ref, g2_ref, wup_ref, wdn_ref, out_ref):
    l0, l1, l2 = l0_ref[0], l1_ref[0], l2_ref[0]
    mx = jnp.maximum(jnp.maximum(l0, l1), l2)
    e0, e1, e2 = jnp.exp(l0 - mx), jnp.exp(l1 - mx), jnp.exp(l2 - mx)
    inv = 1.0 / (e0 + e1 + e2)
    hx = hx_ref[...]
    def pairs(ref):
        return jnp.concatenate([ref[0, p] for p in range(N_PAIRS)], axis=1)

    o_attn = (_split_dot(e0 * inv, hx) * pairs(o0_ref) + _split_dot(e1 * inv, hx) * pairs(o1_ref)
              + _split_dot(e2 * inv, hx) * pairs(o2_ref))
    a = jnp.dot(o_attn.astype(BF16), wa_ref[...], preferred_element_type=F32)
    g = jnp.dot(og_ref[...], wb_ref[...], preferred_element_type=F32)
    gate_a = jax.nn.sigmoid(gate_ref[:, :D_MODEL].astype(F32) + gbias_ref[:, :D_MODEL])
    gate_g = jax.nn.sigmoid(gate_ref[:, D_MODEL:].astype(F32) + gbias_ref[:, D_MODEL:])
    mixed = gate_a * a + gate_g * g
    x1 = x_ref[...] + jnp.dot(mixed.astype(BF16), wo_ref[...], preferred_element_type=F32)

    ms = jnp.mean(x1 * x1, axis=-1, keepdims=True)
    h2 = (x1 * lax.rsqrt(ms + EPS) * g2_ref[...]).astype(BF16)
    acc = x1
    for c in range(D_FF // FF_CHUNK):
        u = jnp.dot(h2, wup_ref[:, c * FF_CHUNK:(c + 1) * FF_CHUNK], preferred_element_type=F32)
        u = jnp.maximum(u, 0.0)
        u = (u * u).astype(BF16)
        acc = acc + jnp.dot(u, wdn_ref[c * FF_CHUNK:(c + 1) * FF_CHUNK, :],
                            preferred_element_type=F32)
    out_ref[...] = acc


def _out_call(x2d, o_groups, lse_groups, o_gla, proj2d, gbias, wa, wb, wo, g2, wup, wdn, batch, tm):
    t = x2d.shape[0]
    lanes = np.arange(LANES)
    cols = np.arange(GROUP_WIDTH)
    head_expand = jnp.asarray(lanes[:, None] == cols[None, :] // HEAD_DIM, BF16)

    tiles_per_seq = t // batch // tm

    def tok(width):
        return pl.BlockSpec((tm, width), lambda i: (i, 0))

    attn_o = pl.BlockSpec((1, N_PAIRS, tm, LANES),
                          lambda i: (i // tiles_per_seq, 0, i % tiles_per_seq, 0))
    attn_lse = pl.BlockSpec((1, tm, LANES), lambda i: (i // tiles_per_seq, i % tiles_per_seq, 0))

    def const(shape):
        return pl.BlockSpec(shape, lambda i: (0, 0), pipeline_mode=pl.Buffered(1))

    return pl.pallas_call(
        _out_kernel,
        grid=(t // tm,),
        in_specs=[
            tok(D_MODEL),
            attn_o, attn_o, attn_o,
            attn_lse, attn_lse, attn_lse,
            tok(GLA_DV),
            pl.BlockSpec((tm, 2 * D_MODEL), lambda i: (i, P_GATE // (2 * D_MODEL))),
            const((1, 2 * D_MODEL)),
            const((LANES, GROUP_WIDTH)),
            const((GROUP_WIDTH, D_MODEL)),
            const((GLA_DV, D_MODEL)),
            const((D_MODEL, D_MODEL)),
            const((1, D_MODEL)),
            const((D_MODEL, D_FF)),
            const((D_FF, D_MODEL)),
        ],
        out_specs=tok(D_MODEL),
        out_shape=jax.ShapeDtypeStruct((t, D_MODEL), F32),
        compiler_params=pltpu.CompilerParams(
            dimension_semantics=("parallel",),
            vmem_limit_bytes=VMEM_LIMIT_BYTES),
        name="out",
    )(x2d, *o_groups, *lse_groups, o_gla, proj2d, gbias, head_expand, wa, wb, wo, g2, wup, wdn)


def _layer(x2d, batch, norm1_g, w_in, gq, gk, gate_up, gate_bias, gla_norm_g, branch_bias,
           w_a, w_b, w_out, norm2_g, w_up, w_down, out_tm=512):
    def cols(off, size):
        return w_in[:, off:off + size].astype(BF16)

    g1 = norm1_g.reshape(1, D_MODEL)
    w_main = jnp.concatenate([
        cols(O_GATE, 2 * D_MODEL), cols(O_GV, GLA_DV), cols(O_GR, GLA_DV),
        cols(O_GQ, GLA_DK), cols(O_GK, GLA_DK)], axis=1)
    w_pa = jnp.pad(cols(O_PA, GLA_RANK), ((0, 0), (0, PA_PAD - GLA_RANK)))
    q_gain = jnp.tile(gq, HEADS_PER_GROUP) * (HEAD_DIM ** -0.5)
    k_gain = jnp.tile(gk, HEADS_PER_GROUP)
    qk_gain = jnp.stack([q_gain, k_gain]).reshape(2, 1, GROUP_WIDTH)
    idx = np.arange(MXU_DIM)
    bd = jnp.asarray((idx[:, None] // HEAD_DIM) == (idx[None, :] // HEAD_DIM), BF16)

    proj, pa = _main_proj_call(x2d, g1, w_main, w_pa, tm=1024, rows=512, tn=1024)

    w_qkv = [jnp.concatenate(
        [cols(O_ATTN + (kind * N_GROUPS + g) * GROUP_WIDTH, GROUP_WIDTH) for kind in range(N_KINDS)],
        axis=1) for g in range(N_GROUPS)]
    qkv_groups = _attn_proj_call(x2d, g1, w_qkv, qk_gain, bd, batch, tm=512)

    o_groups, lse_groups = [], []
    for qkv, (_, dilation) in zip(qkv_groups, ATTN_GROUPS):
        o, lse = _attn_call(qkv, dilation)
        o_groups.append(o)
        lse_groups.append(lse)

    u_pad = jnp.pad(gate_up, ((0, PA_PAD - GLA_RANK), (0, 0))).astype(BF16)
    o_gla = _gla_call(proj, pa, u_pad, gate_bias.reshape(1, GLA_DK),
                      gla_norm_g.reshape(1, GLA_HV), batch)

    return _out_call(x2d, o_groups, lse_groups, o_gla, proj,
                     branch_bias.reshape(1, 2 * D_MODEL),
                     w_a.astype(BF16), w_b.astype(BF16), w_out.astype(BF16),
                     norm2_g.reshape(1, D_MODEL), w_up.astype(BF16), w_down.astype(BF16), batch, out_tm)


def kernel(x, norm1_g, w_in, attn_q_norm_g, attn_k_norm_g, gla_gate_up, gla_gate_bias, gla_out_norm_g, branch_gate_bias, w_attn_branch, w_gla_branch, w_out, norm2_g, w_ff_up, w_ff_down):
    b, s, d = x.shape
    x2d = x.reshape(b * s, d)
    for l in range(norm1_g.shape[0]):
        x2d = _layer(x2d, b, norm1_g[l], w_in[l], attn_q_norm_g[l], attn_k_norm_g[l],
                     gla_gate_up[l], gla_gate_bias[l], gla_out_norm_g[l], branch_gate_bias[l],
                     w_attn_branch[l], w_gla_branch[l], w_out[l], norm2_g[l],
                     w_ff_up[l], w_ff_down[l])
    return x2d.reshape(b, s, d)
```

```python
import functools

import numpy as np
import jax
import jax.numpy as jnp
from jax import lax
from jax.experimental import pallas as pl
from jax.experimental.pallas import tpu as pltpu

F32 = jnp.float32
BF16 = jnp.bfloat16

D_MODEL = 1024
ATTN_GROUPS = ((128, 1), (512, 4), (2048, 16))
N_GROUPS = len(ATTN_GROUPS)
HEADS_PER_GROUP = 8
HEAD_DIM = 64
ATTN_BLOCK = 128
GROUP_WIDTH = HEADS_PER_GROUP * HEAD_DIM
ATTN_WIDTH = 3 * N_GROUPS * GROUP_WIDTH
N_PAIRS = GROUP_WIDTH // 128
ATTN_HEADS_PER_DOT = 4
ATTN_UNROLL = 4

GLA_HEADS = 4
GLA_DK = 512
GLA_DV = 1024
GLA_HK = GLA_DK // GLA_HEADS
GLA_HV = GLA_DV // GLA_HEADS
GLA_RANK = 16
GLA_TAU = 16.0
GLA_CHUNK = 64
GLA_SUB = 16
GLA_STEP = 256
GLA_HPS = 4

D_FF = 4 * D_MODEL
EPS = 1e-6

LANES = 128
MXU_DIM = 256
VMEM_LIMIT_BYTES = 56 * 1024 * 1024

_ORIG_SIZES = (ATTN_WIDTH, GLA_DK, GLA_DK, GLA_DV, GLA_DV, GLA_RANK, 2 * D_MODEL)
_ORIG_OFF = tuple(int(v) for v in np.cumsum((0,) + _ORIG_SIZES))
O_ATTN, O_GQ, O_GK, O_GV, O_GR, O_PA, O_GATE = _ORIG_OFF[:7]

P_GATE = 0
P_GV = P_GATE + 2 * D_MODEL
P_GR = P_GV + GLA_DV
P_GQ = P_GR + GLA_DV
P_GK = P_GQ + GLA_DK
P_MAIN = P_GK + GLA_DK
N_KINDS = 3
PA_PAD = LANES


def _rms_norm_rows(x, gain):
    ms = jnp.mean(x * x, axis=-1, keepdims=True)
    return x * lax.rsqrt(ms + EPS) * gain


def _main_proj_kernel(x_ref, g1_ref, w_ref, wpa_ref, o_ref, pa_ref, *, rows, tn):
    gain = g1_ref[...]
    for rc in range(x_ref.shape[0] // rows):
        rs = slice(rc * rows, (rc + 1) * rows)
        h = _rms_norm_rows(x_ref[rs, :], gain).astype(BF16)
        pa_ref[rs, :] = jnp.dot(h, wpa_ref[...], preferred_element_type=F32)
        for c in range(w_ref.shape[1] // tn):
            cols = slice(c * tn, (c + 1) * tn)
            o_ref[rs, cols] = jnp.dot(h, w_ref[:, cols], preferred_element_type=F32).astype(BF16)


def _main_proj_call(x2d, g1, w_main, w_pa, tm, rows, tn):
    t = x2d.shape[0]
    n = w_main.shape[1]

    def const(shape):
        return pl.BlockSpec(shape, lambda i: (0, 0), pipeline_mode=pl.Buffered(1))

    return pl.pallas_call(
        functools.partial(_main_proj_kernel, rows=rows, tn=tn),
        grid=(t // tm,),
        in_specs=[
            pl.BlockSpec((tm, D_MODEL), lambda i: (i, 0)),
            const((1, D_MODEL)),
            const((D_MODEL, n)),
            const((D_MODEL, PA_PAD)),
        ],
        out_specs=[
            pl.BlockSpec((tm, n), lambda i: (i, 0)),
            pl.BlockSpec((tm, PA_PAD), lambda i: (i, 0)),
        ],
        out_shape=[
            jax.ShapeDtypeStruct((t, n), BF16),
            jax.ShapeDtypeStruct((t, PA_PAD), F32),
        ],
        compiler_params=pltpu.CompilerParams(
            dimension_semantics=("parallel",),
            vmem_limit_bytes=VMEM_LIMIT_BYTES),
        name="proj_main",
    )(x2d, g1, w_main, w_pa)


def _qkv_project(h, w_ref, gain_ref, bd, o_ref):
    d = o_ref.shape[1]
    rows = o_ref.shape[2]
    for kind in range(N_KINDS):
        cols = slice(kind * GROUP_WIDTH, (kind + 1) * GROUP_WIDTH)
        acc = jnp.dot(h, w_ref[:, cols], preferred_element_type=F32)
        if kind < 2:
            sq = (acc * acc).astype(BF16)
            ss = jnp.concatenate(
                [jnp.dot(sq[:, c * MXU_DIM:(c + 1) * MXU_DIM], bd, preferred_element_type=F32)
                 for c in range(GROUP_WIDTH // MXU_DIM)], axis=1)
            acc = acc * lax.rsqrt(ss * (1.0 / HEAD_DIM) + EPS) * gain_ref[kind]
        y = acc.astype(BF16)
        for r in range(d):
            o_ref[0, r, :, cols] = y[r * rows:(r + 1) * rows, :]


def _attn_proj_kernel(x_ref, g1_ref, w0_ref, w1_ref, w2_ref, gain_ref, bd_ref,
                      o0_ref, o1_ref, o2_ref, col_scr, perm1_scr, perm2_scr):
    tm = x_ref.shape[0]
    n_col = D_MODEL // LANES
    bd = bd_ref[...]
    hf = _rms_norm_rows(x_ref[...], g1_ref[...])
    for c in range(n_col):
        col_scr[c] = hf[:, c * LANES:(c + 1) * LANES]
    _qkv_project(hf.astype(BF16), w0_ref, gain_ref, bd, o0_ref)
    for w_ref, o_ref, perm_scr in ((w1_ref, o1_ref, perm1_scr), (w2_ref, o2_ref, perm2_scr)):
        d = o_ref.shape[1]
        rows = tm // d
        for r in range(d):
            for c in range(n_col):
                perm_scr[r * rows:(r + 1) * rows, c * LANES:(c + 1) * LANES] = (
                    col_scr[c, pl.ds(r, rows, stride=d), :].astype(BF16))
        _qkv_project(perm_scr[...], w_ref, gain_ref, bd, o_ref)


def _attn_proj_call(x2d, g1, w_qkv, qk_gain, bd, batch, tm):
    t = x2d.shape[0]
    s = t // batch
    tiles_per_seq = s // tm
    width = N_KINDS * GROUP_WIDTH

    def const(shape):
        return pl.BlockSpec(shape, lambda i: (0,) * len(shape), pipeline_mode=pl.Buffered(1))

    return pl.pallas_call(
        _attn_proj_kernel,
        grid=(t // tm,),
        in_specs=[
            pl.BlockSpec((tm, D_MODEL), lambda i: (i, 0)),
            const((1, D_MODEL)),
            const((D_MODEL, width)), const((D_MODEL, width)), const((D_MODEL, width)),
            const((2, 1, GROUP_WIDTH)),
            const((MXU_DIM, MXU_DIM)),
        ],
        out_specs=[
            pl.BlockSpec((1, d, tm // d, width),
                         lambda i: (i // tiles_per_seq, 0, i % tiles_per_seq, 0))
            for _, d in ATTN_GROUPS],
        out_shape=[jax.ShapeDtypeStruct((batch, d, s // d, width), BF16) for _, d in ATTN_GROUPS],
        scratch_shapes=[
            pltpu.VMEM((D_MODEL // LANES, tm, LANES), F32),
            pltpu.VMEM((tm, D_MODEL), BF16),
            pltpu.VMEM((tm, D_MODEL), BF16),
        ],
        compiler_params=pltpu.CompilerParams(
            dimension_semantics=("parallel",),
            vmem_limit_bytes=VMEM_LIMIT_BYTES),
        name="proj_attn",
    )(x2d, g1, *w_qkv, qk_gain, bd)


def _attn_kernel(q_ref, kp_ref, kc_ref, vp_ref, vc_ref, o_ref, lse_ref):
    n = pl.program_id(1)
    d = q_ref.shape[1]
    blk = ATTN_BLOCK
    n_sub = q_ref.shape[2] // blk
    qi = lax.broadcasted_iota(jnp.int32, (blk, 2 * blk), 0)
    ki = lax.broadcasted_iota(jnp.int32, (blk, 2 * blk), 1)
    band = (ki >= qi) & (ki <= qi + blk)
    band_first = band & ((ki >= blk) | (n > 0))
    lane = lax.broadcasted_iota(jnp.int32, (blk, LANES), 1)
    first_head = lane < HEAD_DIM
    hb = ATTN_HEADS_PER_DOT
    width = hb * HEAD_DIM
    lane_w = lax.broadcasted_iota(jnp.int32, (blk, width), 1)
    head_lanes = [(lane_w >= h * HEAD_DIM) & (lane_w < (h + 1) * HEAD_DIM) for h in range(hb)]
    valid_by_sub = [jnp.concatenate([band_first if u == 0 else band] * hb, axis=0)
                    for u in range(min(n_sub, 2))]

    def unit(r, u):
        rows = slice(u * blk, (u + 1) * blk)
        valid_b = valid_by_sub[min(u, 1)]
        lse_tile = jnp.zeros((blk, LANES), F32)
        for g in range(HEADS_PER_GROUP // hb):
            sl = slice(g * width, (g + 1) * width)
            q = q_ref[0, r, rows, sl]
            if u == 0:
                k_prev, v_prev = kp_ref[0, r, :, sl], vp_ref[0, r, :, sl]
            else:
                prev_rows = slice((u - 1) * blk, u * blk)
                k_prev, v_prev = kc_ref[0, r, prev_rows, sl], vc_ref[0, r, prev_rows, sl]
            k = jnp.concatenate([k_prev, kc_ref[0, r, rows, sl]], axis=0)
            v = jnp.concatenate([v_prev, vc_ref[0, r, rows, sl]], axis=0)
            zero = jnp.zeros_like(q)
            q_rows = jnp.concatenate([jnp.where(head_lanes[h], q, zero) for h in range(hb)],
                                     axis=0)
            s = lax.dot_general(q_rows, k, (((1,), (1,)), ((), ())), preferred_element_type=F32)
            s = jnp.where(valid_b, s, -jnp.inf)
            m = jnp.max(s, axis=-1, keepdims=True)
            p = jnp.exp(s - m)
            l = jnp.sum(p, axis=-1, keepdims=True)
            pv = jnp.dot(p.astype(BF16), v, preferred_element_type=F32)
            o_all = pv / l
            lse_all = m + jnp.log(l)
            tok = pl.ds(u * blk * d + r, blk, stride=d)
            for h in range(hb):
                head = g * hb + h
                lse_tile = jnp.where(lane == head, lse_all[h * blk:(h + 1) * blk], lse_tile)
            for pp in range(hb // 2):
                cols = slice(pp * LANES, (pp + 1) * LANES)
                lo = o_all[(2 * pp) * blk:(2 * pp + 1) * blk, cols]
                hi = o_all[(2 * pp + 1) * blk:(2 * pp + 2) * blk, cols]
                o_ref[0, g * (hb // 2) + pp, tok, :] = jnp.where(first_head, lo, hi)
        lse_ref[0, tok, :] = lse_tile

    if d * n_sub <= ATTN_UNROLL:
        for r in range(d):
            for u in range(n_sub):
                unit(r, u)
    else:
        def residue(r, carry):
            for u in range(n_sub):
                unit(r, u)
            return carry
        lax.fori_loop(0, d, residue, 0, unroll=ATTN_UNROLL // n_sub)


def _attn_call(qkv, dilation):
    b, d, sub_len, _ = qkv.shape
    assert d == dilation
    blk = ATTN_BLOCK
    n_sub = max(1, ATTN_UNROLL // d)
    rows = n_sub * blk
    steps = sub_len // rows
    s = sub_len * d

    def cur(kind):
        return lambda bi, n: (bi, 0, n, kind)

    def prev(kind):
        return lambda bi, n: (bi, 0, jnp.maximum(n * n_sub - 1, 0), kind)

    cur_shape = (1, d, rows, GROUP_WIDTH)
    prev_shape = (1, d, blk, GROUP_WIDTH)
    o, lse = pl.pallas_call(
        _attn_kernel,
        grid=(b, steps),
        in_specs=[
            pl.BlockSpec(cur_shape, cur(0)),
            pl.BlockSpec(prev_shape, prev(1)),
            pl.BlockSpec(cur_shape, cur(1)),
            pl.BlockSpec(prev_shape, prev(2)),
            pl.BlockSpec(cur_shape, cur(2)),
        ],
        out_specs=[
            pl.BlockSpec((1, N_PAIRS, d * rows, LANES), lambda bi, n: (bi, 0, n, 0)),
            pl.BlockSpec((1, d * rows, LANES), lambda bi, n: (bi, n, 0)),
        ],
        out_shape=[
            jax.ShapeDtypeStruct((b, N_PAIRS, s, LANES), F32),
            jax.ShapeDtypeStruct((b, s, LANES), F32),
        ],
        compiler_params=pltpu.CompilerParams(
            dimension_semantics=("parallel", "arbitrary"),
            vmem_limit_bytes=VMEM_LIMIT_BYTES),
        name=f"attn_d{d}",
    )(qkv, qkv, qkv, qkv, qkv)
    return o, lse


def _gla_kernel(q_ref, k_ref, v_ref, r_ref, pa_ref, u_ref, bias_ref, gn_ref, ltri_ref, e_ref,
                o_ref, st_ref, b_scr, qs_scr, kf_scr, pcat_scr):
    @pl.when(pl.program_id(2) == 0)
    def _():
        st_ref[...] = jnp.zeros_like(st_ref)

    for hh in range(GLA_HPS):
        kcols = slice(hh * GLA_HK, (hh + 1) * GLA_HK)
        vcols = slice(hh * GLA_HV, (hh + 1) * GLA_HV)
        _gla_head(q_ref.at[:, kcols], k_ref.at[:, kcols], v_ref.at[:, vcols], r_ref.at[:, vcols],
                  pa_ref, u_ref.at[:, kcols], bias_ref.at[:, kcols], gn_ref, ltri_ref, e_ref,
                  o_ref.at[:, vcols], st_ref.at[hh], b_scr.at[hh], qs_scr.at[hh], kf_scr.at[hh],
                  pcat_scr.at[hh])


def _gla_head(q_ref, k_ref, v_ref, r_ref, pa_ref, u_ref, bias_ref, gn_ref, ltri_ref, e_ref,
              o_ref, st_ref, b_scr, qs_scr, kf_scr, pcat_scr):
    c_len, sub = GLA_CHUNK, GLA_SUB

    logits = jnp.dot(pa_ref[...].astype(BF16), u_ref[...], preferred_element_type=F32) + bias_ref[...]
    log_sig = jnp.minimum(logits, 0.0) - jnp.log1p(jnp.exp(-jnp.abs(logits)))
    la = log_sig * (1.0 / GLA_TAU)

    hi = la.astype(BF16)
    r1 = la - hi.astype(F32)
    mid = r1.astype(BF16)
    lo = (r1 - mid.astype(F32)).astype(BF16)
    ltri = ltri_ref[...]
    b = (jnp.dot(ltri, hi, preferred_element_type=F32)
         + jnp.dot(ltri, mid, preferred_element_type=F32)
         + jnp.dot(ltri, lo, preferred_element_type=F32))

    qs = q_ref[...].astype(F32) * (GLA_HK ** -0.5)
    kf = k_ref[...].astype(F32)
    b_scr[...] = b
    qs_scr[...] = qs
    kf_scr[...] = kf

    half = sub // 2
    tl = lax.broadcasted_iota(jnp.int32, (half, GLA_HK), 0)
    zeros_half = jnp.zeros((half, GLA_HK), F32)
    for blk_i in range(GLA_STEP // sub):
        r0 = blk_i * sub
        b_lo, b_hi = b_scr[r0:r0 + half, :], b_scr[r0 + half:r0 + sub, :]
        q_lo, q_hi = qs_scr[r0:r0 + half, :], qs_scr[r0 + half:r0 + sub, :]
        for s in range(sub):
            kb = kf_scr[r0 + s:r0 + s + 1, :]
            bb = b_scr[r0 + s:r0 + s + 1, :]
            d_hi = b_hi - bb
            if s > half:
                d_hi = jnp.where(tl + half >= s, d_hi, -jnp.inf)
            p_hi = q_hi * kb * jnp.exp(d_hi)
            if s < half:
                d_lo = b_lo - bb
                if s > 0:
                    d_lo = jnp.where(tl >= s, d_lo, -jnp.inf)
                p_lo = q_lo * kb * jnp.exp(d_lo)
            else:
                p_lo = zeros_half
            pcat_scr[r0:r0 + sub, s * GLA_HK:(s + 1) * GLA_HK] = (
                jnp.concatenate([p_lo, p_hi], axis=0).astype(BF16))
    a_diag = jnp.dot(pcat_scr[...], e_ref[...], preferred_element_type=F32)

    ri = lax.broadcasted_iota(jnp.int32, (c_len, c_len), 0)
    ci = lax.broadcasted_iota(jnp.int32, (c_len, c_len), 1)
    same_sub = (ri // sub) == (ci // sub)
    gn = gn_ref[...]

    def z(nrows):
        return jnp.zeros((nrows, GLA_HK), F32)

    for c in range(GLA_STEP // c_len):
        base = c * c_len
        bc = b[base:base + c_len]
        qc = qs[base:base + c_len]
        kc = kf[base:base + c_len]
        b15, b31, b47, b63 = bc[15:16], bc[31:32], bc[47:48], bc[63:64]
        q1 = jnp.concatenate([z(16), qc[16:32] * jnp.exp(bc[16:32] - b15), z(32)], axis=0)
        q2 = jnp.concatenate([z(32), qc[32:64] * jnp.exp(bc[32:64] - b31)], axis=0)
        q3 = jnp.concatenate([z(48), qc[48:64] * jnp.exp(bc[48:64] - b47)], axis=0)
        k1 = jnp.concatenate([kc[0:16] * jnp.exp(b15 - bc[0:16]), z(48)], axis=0)
        k2 = jnp.concatenate([kc[0:32] * jnp.exp(b31 - bc[0:32]), z(32)], axis=0)
        k3 = jnp.concatenate([z(32), kc[32:48] * jnp.exp(b47 - bc[32:48]), z(16)], axis=0)
        q_cat = jnp.concatenate([q1, q2, q3], axis=1).astype(BF16)
        k_cat = jnp.concatenate([k1, k2, k3], axis=1).astype(BF16)
        a_off = lax.dot_general(q_cat, k_cat, (((1,), (1,)), ((), ())),
                                preferred_element_type=F32)
        a = a_off + jnp.where(same_sub, a_diag[base:base + c_len], 0.0)

        vc = v_ref[base:base + c_len, :]
        o_intra = jnp.dot(a.astype(BF16), vc, preferred_element_type=F32)

        st = st_ref[...]
        q_in = (qc * jnp.exp(bc)).astype(BF16)
        o_inter = lax.dot_general(q_in, st.astype(BF16), (((1,), (1,)), ((), ())),
                                  preferred_element_type=F32)
        k_st = (kc * jnp.exp(b63 - bc)).astype(BF16)
        upd = lax.dot_general(vc, k_st, (((0,), (0,)), ((), ())),
                              preferred_element_type=F32)
        st_ref[...] = st * jnp.exp(b63) + upd

        o = o_inter + o_intra
        ms = jnp.mean(o * o, axis=-1, keepdims=True)
        y = o * lax.rsqrt(ms + EPS) * gn
        r = r_ref[base:base + c_len, :].astype(F32)
        y = y * (r * jax.nn.sigmoid(r))
        o_ref[base:base + c_len, :] = y.astype(BF16)


def _gla_constants():
    n = GLA_STEP
    idx = np.arange(n)
    ltri = ((idx[:, None] >= idx[None, :]) & (idx[:, None] // GLA_CHUNK == idx[None, :] // GLA_CHUNK))
    rows = np.arange(GLA_SUB * GLA_HK)
    cols = np.arange(GLA_CHUNK)
    e = (rows[:, None] // GLA_HK) == (cols[None, :] % GLA_SUB)
    return jnp.asarray(ltri, BF16), jnp.asarray(e, BF16)


def _gla_call(proj2d, pa, u_pad, bias, gn, batch):
    t = proj2d.shape[0]
    steps = t // batch // GLA_STEP
    ltri, e = _gla_constants()

    def row(bi, h, i):
        return bi * steps + i

    hk, hv = GLA_HPS * GLA_HK, GLA_HPS * GLA_HV
    return pl.pallas_call(
        _gla_kernel,
        grid=(batch, GLA_HEADS // GLA_HPS, steps),
        in_specs=[
            pl.BlockSpec((GLA_STEP, hk), lambda bi, h, i: (row(bi, h, i), P_GQ // hk + h)),
            pl.BlockSpec((GLA_STEP, hk), lambda bi, h, i: (row(bi, h, i), P_GK // hk + h)),
            pl.BlockSpec((GLA_STEP, hv), lambda bi, h, i: (row(bi, h, i), P_GV // hv + h)),
            pl.BlockSpec((GLA_STEP, hv), lambda bi, h, i: (row(bi, h, i), P_GR // hv + h)),
            pl.BlockSpec((GLA_STEP, PA_PAD), lambda bi, h, i: (row(bi, h, i), 0)),
            pl.BlockSpec((PA_PAD, hk), lambda bi, h, i: (0, h)),
            pl.BlockSpec((1, hk), lambda bi, h, i: (0, h)),
            pl.BlockSpec((1, GLA_HV), lambda bi, h, i: (0, 0)),
            pl.BlockSpec((GLA_STEP, GLA_STEP), lambda bi, h, i: (0, 0)),
            pl.BlockSpec((GLA_SUB * GLA_HK, GLA_CHUNK), lambda bi, h, i: (0, 0)),
        ],
        out_specs=pl.BlockSpec((GLA_STEP, hv), lambda bi, h, i: (row(bi, h, i), h)),
        out_shape=jax.ShapeDtypeStruct((t, GLA_DV), BF16),
        scratch_shapes=[
            pltpu.VMEM((GLA_HPS, GLA_HV, GLA_HK), F32),
            pltpu.VMEM((GLA_HPS, GLA_STEP, GLA_HK), F32),
            pltpu.VMEM((GLA_HPS, GLA_STEP, GLA_HK), F32),
            pltpu.VMEM((GLA_HPS, GLA_STEP, GLA_HK), F32),
            pltpu.VMEM((GLA_HPS, GLA_STEP, GLA_SUB * GLA_HK), BF16),
        ],
        compiler_params=pltpu.CompilerParams(
            dimension_semantics=("parallel", "parallel", "arbitrary"),
            vmem_limit_bytes=VMEM_LIMIT_BYTES),
        name="gla",
    )(proj2d, proj2d, proj2d, proj2d, pa, u_pad, bias, gn, ltri, e)


FF_CHUNK = 1024


def _split_dot(w, rhs):
    hi = w.astype(BF16)
    lo = (w - hi.astype(F32)).astype(BF16)
    return (jnp.dot(hi, rhs, preferred_element_type=F32)
            + jnp.dot(lo, rhs, preferred_element_type=F32))


def _out_kernel(x_ref, o0_ref, o1_ref, o2_ref, l0_ref, l1_ref, l2_ref, og_ref, gate_ref, gbias_ref,
                hx_ref, wa_ref, wb_ref, wo_ref, g2_ref, wup_ref, wdn_ref, out_ref):
    l0, l1, l2 = l0_ref[0], l1_ref[0], l2_ref[0]
    mx = jnp.maximum(jnp.maximum(l0, l1), l2)
    e0, e1, e2 = jnp.exp(l0 - mx), jnp.exp(l1 - mx), jnp.exp(l2 - mx)
    inv = 1.0 / (e0 + e1 + e2)
    hx = hx_ref[...]
    def pairs(ref):
        return jnp.concatenate([ref[0, p] for p in range(N_PAIRS)], axis=1)

    o_attn = (_split_dot(e0 * inv, hx) * pairs(o0_ref) + _split_dot(e1 * inv, hx) * pairs(o1_ref)
              + _split_dot(e2 * inv, hx) * pairs(o2_ref))
    a = jnp.dot(o_attn.astype(BF16), wa_ref[...], preferred_element_type=F32)
    g = jnp.dot(og_ref[...], wb_ref[...], preferred_element_type=F32)
    gate_a = jax.nn.sigmoid(gate_ref[:, :D_MODEL].astype(F32) + gbias_ref[:, :D_MODEL])
    gate_g = jax.nn.sigmoid(gate_ref[:, D_MODEL:].astype(F32) + gbias_ref[:, D_MODEL:])
    mixed = gate_a * a + gate_g * g
    x1 = x_ref[...] + jnp.dot(mixed.astype(BF16), wo_ref[...], preferred_element_type=F32)

    ms = jnp.mean(x1 * x1, axis=-1, keepdims=True)
    h2 = (x1 * lax.rsqrt(ms + EPS) * g2_ref[...]).astype(BF16)
    acc = x1
    for c in range(D_FF // FF_CHUNK):
        u = jnp.dot(h2, wup_ref[:, c * FF_CHUNK:(c + 1) * FF_CHUNK], preferred_element_type=F32)
        u = jnp.maximum(u, 0.0)
        u = (u * u).astype(BF16)
        acc = acc + jnp.dot(u, wdn_ref[c * FF_CHUNK:(c + 1) * FF_CHUNK, :],
                            preferred_element_type=F32)
    out_ref[...] = acc


def _out_call(x2d, o_groups, lse_groups, o_gla, proj2d, gbias, wa, wb, wo, g2, wup, wdn, batch, tm):
    t = x2d.shape[0]
    lanes = np.arange(LANES)
    cols = np.arange(GROUP_WIDTH)
    head_expand = jnp.asarray(lanes[:, None] == cols[None, :] // HEAD_DIM, BF16)

    tiles_per_seq = t // batch // tm

    def tok(width):
        return pl.BlockSpec((tm, width), lambda i: (i, 0))

    attn_o = pl.BlockSpec((1, N_PAIRS, tm, LANES),
                          lambda i: (i // tiles_per_seq, 0, i % tiles_per_seq, 0))
    attn_lse = pl.BlockSpec((1, tm, LANES), lambda i: (i // tiles_per_seq, i % tiles_per_seq, 0))

    def const(shape):
        return pl.BlockSpec(shape, lambda i: (0, 0), pipeline_mode=pl.Buffered(1))

    return pl.pallas_call(
        _out_kernel,
        grid=(t // tm,),
        in_specs=[
            tok(D_MODEL),
            attn_o, attn_o, attn_o,
            attn_lse, attn_lse, attn_lse,
            tok(GLA_DV),
            pl.BlockSpec((tm, 2 * D_MODEL), lambda i: (i, P_GATE // (2 * D_MODEL))),
            const((1, 2 * D_MODEL)),
            const((LANES, GROUP_WIDTH)),
            const((GROUP_WIDTH, D_MODEL)),
            const((GLA_DV, D_MODEL)),
            const((D_MODEL, D_MODEL)),
            const((1, D_MODEL)),
            const((D_MODEL, D_FF)),
            const((D_FF, D_MODEL)),
        ],
        out_specs=tok(D_MODEL),
        out_shape=jax.ShapeDtypeStruct((t, D_MODEL), F32),
        compiler_params=pltpu.CompilerParams(
            dimension_semantics=("parallel",),
            vmem_limit_bytes=VMEM_LIMIT_BYTES),
        name="out",
    )(x2d, *o_groups, *lse_groups, o_gla, proj2d, gbias, head_expand, wa, wb, wo, g2, wup, wdn)


def _layer(x2d, batch, norm1_g, w_in, gq, gk, gate_up, gate_bias, gla_norm_g, branch_bias,
           w_a, w_b, w_out, norm2_g, w_up, w_down, out_tm=512):
    def cols(off, size):
        return w_in[:, off:off + size].astype(BF16)

    g1 = norm1_g.reshape(1, D_MODEL)
    w_main = jnp.concatenate([
        cols(O_GATE, 2 * D_MODEL), cols(O_GV, GLA_DV), cols(O_GR, GLA_DV),
        cols(O_GQ, GLA_DK), cols(O_GK, GLA_DK)], axis=1)
    w_pa = jnp.pad(cols(O_PA, GLA_RANK), ((0, 0), (0, PA_PAD - GLA_RANK)))
    q_gain = jnp.tile(gq, HEADS_PER_GROUP) * (HEAD_DIM ** -0.5)
    k_gain = jnp.tile(gk, HEADS_PER_GROUP)
    qk_gain = jnp.stack([q_gain, k_gain]).reshape(2, 1, GROUP_WIDTH)
    idx = np.arange(MXU_DIM)
    bd = jnp.asarray((idx[:, None] // HEAD_DIM) == (idx[None, :] // HEAD_DIM), BF16)

    proj, pa = _main_proj_call(x2d, g1, w_main, w_pa, tm=1024, rows=512, tn=1024)

    w_qkv = [jnp.concatenate(
        [cols(O_ATTN + (kind * N_GROUPS + g) * GROUP_WIDTH, GROUP_WIDTH) for kind in range(N_KINDS)],
        axis=1) for g in range(N_GROUPS)]
    qkv_groups = _attn_proj_call(x2d, g1, w_qkv, qk_gain, bd, batch, tm=512)

    o_groups, lse_groups = [], []
    for qkv, (_, dilation) in zip(qkv_groups, ATTN_GROUPS):
        o, lse = _attn_call(qkv, dilation)
        o_groups.append(o)
        lse_groups.append(lse)

    u_pad = jnp.pad(gate_up, ((0, PA_PAD - GLA_RANK), (0, 0))).astype(BF16)
    o_gla = _gla_call(proj, pa, u_pad, gate_bias.reshape(1, GLA_DK),
                      gla_norm_g.reshape(1, GLA_HV), batch)

    return _out_call(x2d, o_groups, lse_groups, o_gla, proj,
                     branch_bias.reshape(1, 2 * D_MODEL),
                     w_a.astype(BF16), w_b.astype(BF16), w_out.astype(BF16),
                     norm2_g.reshape(1, D_MODEL), w_up.astype(BF16), w_down.astype(BF16), batch, out_tm)


def kernel(x, norm1_g, w_in, attn_q_norm_g, attn_k_norm_g, gla_gate_up, gla_gate_bias, gla_out_norm_g, branch_gate_bias, w_attn_branch, w_gla_branch, w_out, norm2_g, w_ff_up, w_ff_down):
    b, s, d = x.shape
    x2d = x.reshape(b * s, d)
    for l in range(norm1_g.shape[0]):
        x2d = _layer(x2d, b, norm1_g[l], w_in[l], attn_q_norm_g[l], attn_k_norm_g[l],
                     gla_gate_up[l], gla_gate_bias[l], gla_out_norm_g[l], branch_gate_bias[l],
                     w_attn_branch[l], w_gla_branch[l], w_out[l], norm2_g[l],
                     w_ff_up[l], w_ff_down[l])
    return x2d.reshape(b, s, d)
```

```python
import functools

import numpy as np
import jax
import jax.numpy as jnp
from jax import lax
from jax.experimental import pallas as pl
from jax.experimental.pallas import tpu as pltpu

F32 = jnp.float32
BF16 = jnp.bfloat16

D_MODEL = 1024
ATTN_GROUPS = ((128, 1), (512, 4), (2048, 16))
N_GROUPS = len(ATTN_GROUPS)
HEADS_PER_GROUP = 8
HEAD_DIM = 64
ATTN_BLOCK = 128
GROUP_WIDTH = HEADS_PER_GROUP * HEAD_DIM
ATTN_WIDTH = 3 * N_GROUPS * GROUP_WIDTH
N_PAIRS = GROUP_WIDTH // 128
ATTN_HEADS_PER_DOT = 4
ATTN_UNROLL = 4

GLA_HEADS = 4
GLA_DK = 512
GLA_DV = 1024
GLA_HK = GLA_DK // GLA_HEADS
GLA_HV = GLA_DV // GLA_HEADS
GLA_RANK = 16
GLA_TAU = 16.0
GLA_CHUNK = 64
GLA_SUB = 16
GLA_STEP = 256

D_FF = 4 * D_MODEL
EPS = 1e-6

LANES = 128
MXU_DIM = 256
VMEM_LIMIT_BYTES = 56 * 1024 * 1024

_ORIG_SIZES = (ATTN_WIDTH, GLA_DK, GLA_DK, GLA_DV, GLA_DV, GLA_RANK, 2 * D_MODEL)
_ORIG_OFF = tuple(int(v) for v in np.cumsum((0,) + _ORIG_SIZES))
O_ATTN, O_GQ, O_GK, O_GV, O_GR, O_PA, O_GATE = _ORIG_OFF[:7]

P_GATE = 0
P_GV = P_GATE + 2 * D_MODEL
P_GR = P_GV + GLA_DV
P_GQ = P_GR + GLA_DV
P_GK = P_GQ + GLA_DK
P_MAIN = P_GK + GLA_DK
N_KINDS = 3
PA_PAD = LANES


def _rms_norm_rows(x, gain):
    ms = jnp.mean(x * x, axis=-1, keepdims=True)
    return x * lax.rsqrt(ms + EPS) * gain


def _main_proj_kernel(x_ref, g1_ref, w_ref, wpa_ref, o_ref, pa_ref, *, rows, tn):
    gain = g1_ref[...]
    for rc in range(x_ref.shape[0] // rows):
        rs = slice(rc * rows, (rc + 1) * rows)
        h = _rms_norm_rows(x_ref[rs, :], gain).astype(BF16)
        pa_ref[rs, :] = jnp.dot(h, wpa_ref[...], preferred_element_type=F32)
        for c in range(w_ref.shape[1] // tn):
            cols = slice(c * tn, (c + 1) * tn)
            o_ref[rs, cols] = jnp.dot(h, w_ref[:, cols], preferred_element_type=F32).astype(BF16)


def _main_proj_call(x2d, g1, w_main, w_pa, tm, rows, tn):
    t = x2d.shape[0]
    n = w_main.shape[1]

    def const(shape):
        return pl.BlockSpec(shape, lambda i: (0, 0), pipeline_mode=pl.Buffered(1))

    return pl.pallas_call(
        functools.partial(_main_proj_kernel, rows=rows, tn=tn),
        grid=(t // tm,),
        in_specs=[
            pl.BlockSpec((tm, D_MODEL), lambda i: (i, 0)),
            const((1, D_MODEL)),
            const((D_MODEL, n)),
            const((D_MODEL, PA_PAD)),
        ],
        out_specs=[
            pl.BlockSpec((tm, n), lambda i: (i, 0)),
            pl.BlockSpec((tm, PA_PAD), lambda i: (i, 0)),
        ],
        out_shape=[
            jax.ShapeDtypeStruct((t, n), BF16),
            jax.ShapeDtypeStruct((t, PA_PAD), F32),
        ],
        compiler_params=pltpu.CompilerParams(
            dimension_semantics=("parallel",),
            vmem_limit_bytes=VMEM_LIMIT_BYTES),
        name="proj_main",
    )(x2d, g1, w_main, w_pa)


def _qkv_project(h, w_ref, gain_ref, bd, o_ref):
    d = o_ref.shape[1]
    rows = o_ref.shape[2]
    for kind in range(N_KINDS):
        cols = slice(kind * GROUP_WIDTH, (kind + 1) * GROUP_WIDTH)
        acc = jnp.dot(h, w_ref[:, cols], preferred_element_type=F32)
        if kind < 2:
            sq = (acc * acc).astype(BF16)
            ss = jnp.concatenate(
                [jnp.dot(sq[:, c * MXU_DIM:(c + 1) * MXU_DIM], bd, preferred_element_type=F32)
                 for c in range(GROUP_WIDTH // MXU_DIM)], axis=1)
            acc = acc * lax.rsqrt(ss * (1.0 / HEAD_DIM) + EPS) * gain_ref[kind]
        y = acc.astype(BF16)
        for r in range(d):
            o_ref[0, r, :, cols] = y[r * rows:(r + 1) * rows, :]


def _attn_proj_kernel(x_ref, g1_ref, w0_ref, w1_ref, w2_ref, gain_ref, bd_ref,
                      o0_ref, o1_ref, o2_ref, col_scr, perm1_scr, perm2_scr):
    tm = x_ref.shape[0]
    n_col = D_MODEL // LANES
    bd = bd_ref[...]
    hf = _rms_norm_rows(x_ref[...], g1_ref[...])
    for c in range(n_col):
        col_scr[c] = hf[:, c * LANES:(c + 1) * LANES]
    _qkv_project(hf.astype(BF16), w0_ref, gain_ref, bd, o0_ref)
    for w_ref, o_ref, perm_scr in ((w1_ref, o1_ref, perm1_scr), (w2_ref, o2_ref, perm2_scr)):
        d = o_ref.shape[1]
        rows = tm // d
        for r in range(d):
            for c in range(n_col):
                perm_scr[r * rows:(r + 1) * rows, c * LANES:(c + 1) * LANES] = (
                    col_scr[c, pl.ds(r, rows, stride=d), :].astype(BF16))
        _qkv_project(perm_scr[...], w_ref, gain_ref, bd, o_ref)


def _attn_proj_call(x2d, g1, w_qkv, qk_gain, bd, batch, tm):
    t = x2d.shape[0]
    s = t // batch
    tiles_per_seq = s // tm
    width = N_KINDS * GROUP_WIDTH

    def const(shape):
        return pl.BlockSpec(shape, lambda i: (0,) * len(shape), pipeline_mode=pl.Buffered(1))

    return pl.pallas_call(
        _attn_proj_kernel,
        grid=(t // tm,),
        in_specs=[
            pl.BlockSpec((tm, D_MODEL), lambda i: (i, 0)),
            const((1, D_MODEL)),
            const((D_MODEL, width)), const((D_MODEL, width)), const((D_MODEL, width)),
            const((2, 1, GROUP_WIDTH)),
            const((MXU_DIM, MXU_DIM)),
        ],
        out_specs=[
            pl.BlockSpec((1, d, tm // d, width),
                         lambda i: (i // tiles_per_seq, 0, i % tiles_per_seq, 0))
            for _, d in ATTN_GROUPS],
        out_shape=[jax.ShapeDtypeStruct((batch, d, s // d, width), BF16) for _, d in ATTN_GROUPS],
        scratch_shapes=[
            pltpu.VMEM((D_MODEL // LANES, tm, LANES), F32),
            pltpu.VMEM((tm, D_MODEL), BF16),
            pltpu.VMEM((tm, D_MODEL), BF16),
        ],
        compiler_params=pltpu.CompilerParams(
            dimension_semantics=("parallel",),
            vmem_limit_bytes=VMEM_LIMIT_BYTES),
        name="proj_attn",
    )(x2d, g1, *w_qkv, qk_gain, bd)


def _attn_kernel(q_ref, kp_ref, kc_ref, vp_ref, vc_ref, o_ref, lse_ref):
    n = pl.program_id(1)
    d = q_ref.shape[1]
    blk = ATTN_BLOCK
    n_sub = q_ref.shape[2] // blk
    qi = lax.broadcasted_iota(jnp.int32, (blk, 2 * blk), 0)
    ki = lax.broadcasted_iota(jnp.int32, (blk, 2 * blk), 1)
    band = (ki >= qi) & (ki <= qi + blk)
    band_first = band & ((ki >= blk) | (n > 0))
    lane = lax.broadcasted_iota(jnp.int32, (blk, LANES), 1)
    first_head = lane < HEAD_DIM
    hb = ATTN_HEADS_PER_DOT
    width = hb * HEAD_DIM
    lane_w = lax.broadcasted_iota(jnp.int32, (blk, width), 1)
    head_lanes = [(lane_w >= h * HEAD_DIM) & (lane_w < (h + 1) * HEAD_DIM) for h in range(hb)]
    valid_by_sub = [jnp.concatenate([band_first if u == 0 else band] * hb, axis=0)
                    for u in range(min(n_sub, 2))]

    def unit(r, u):
        rows = slice(u * blk, (u + 1) * blk)
        valid_b = valid_by_sub[min(u, 1)]
        lse_tile = jnp.zeros((blk, LANES), F32)
        for g in range(HEADS_PER_GROUP // hb):
            sl = slice(g * width, (g + 1) * width)
            q = q_ref[0, r, rows, sl]
            if u == 0:
                k_prev, v_prev = kp_ref[0, r, :, sl], vp_ref[0, r, :, sl]
            else:
                prev_rows = slice((u - 1) * blk, u * blk)
                k_prev, v_prev = kc_ref[0, r, prev_rows, sl], vc_ref[0, r, prev_rows, sl]
            k = jnp.concatenate([k_prev, kc_ref[0, r, rows, sl]], axis=0)
            v = jnp.concatenate([v_prev, vc_ref[0, r, rows, sl]], axis=0)
            zero = jnp.zeros_like(q)
            q_rows = jnp.concatenate([jnp.where(head_lanes[h], q, zero) for h in range(hb)],
                                     axis=0)
            s = lax.dot_general(q_rows, k, (((1,), (1,)), ((), ())), preferred_element_type=F32)
            s = jnp.where(valid_b, s, -jnp.inf)
            m = jnp.max(s, axis=-1, keepdims=True)
            p = jnp.exp(s - m)
            l = jnp.sum(p, axis=-1, keepdims=True)
            pv = jnp.dot(p.astype(BF16), v, preferred_element_type=F32)
            o_all = pv / l
            lse_all = m + jnp.log(l)
            tok = pl.ds(u * blk * d + r, blk, stride=d)
            for h in range(hb):
                head = g * hb + h
                lse_tile = jnp.where(lane == head, lse_all[h * blk:(h + 1) * blk], lse_tile)
            for pp in range(hb // 2):
                cols = slice(pp * LANES, (pp + 1) * LANES)
                lo = o_all[(2 * pp) * blk:(2 * pp + 1) * blk, cols]
                hi = o_all[(2 * pp + 1) * blk:(2 * pp + 2) * blk, cols]
                o_ref[0, g * (hb // 2) + pp, tok, :] = jnp.where(first_head, lo, hi)
        lse_ref[0, tok, :] = lse_tile

    if d * n_sub <= ATTN_UNROLL:
        for r in range(d):
            for u in range(n_sub):
                unit(r, u)
    else:
        def residue(r, carry):
            for u in range(n_sub):
                unit(r, u)
            return carry
        lax.fori_loop(0, d, residue, 0, unroll=ATTN_UNROLL // n_sub)


def _attn_call(qkv, dilation):
    b, d, sub_len, _ = qkv.shape
    assert d == dilation
    blk = ATTN_BLOCK
    n_sub = max(1, ATTN_UNROLL // d)
    rows = n_sub * blk
    steps = sub_len // rows
    s = sub_len * d

    def cur(kind):
        return lambda bi, n: (bi, 0, n, kind)

    def prev(kind):
        return lambda bi, n: (bi, 0, jnp.maximum(n * n_sub - 1, 0), kind)

    cur_shape = (1, d, rows, GROUP_WIDTH)
    prev_shape = (1, d, blk, GROUP_WIDTH)
    o, lse = pl.pallas_call(
        _attn_kernel,
        grid=(b, steps),
        in_specs=[
            pl.BlockSpec(cur_shape, cur(0)),
            pl.BlockSpec(prev_shape, prev(1)),
            pl.BlockSpec(cur_shape, cur(1)),
            pl.BlockSpec(prev_shape, prev(2)),
            pl.BlockSpec(cur_shape, cur(2)),
        ],
        out_specs=[
            pl.BlockSpec((1, N_PAIRS, d * rows, LANES), lambda bi, n: (bi, 0, n, 0)),
            pl.BlockSpec((1, d * rows, LANES), lambda bi, n: (bi, n, 0)),
        ],
        out_shape=[
            jax.ShapeDtypeStruct((b, N_PAIRS, s, LANES), F32),
            jax.ShapeDtypeStruct((b, s, LANES), F32),
        ],
        compiler_params=pltpu.CompilerParams(
            dimension_semantics=("parallel", "arbitrary"),
            vmem_limit_bytes=VMEM_LIMIT_BYTES),
        name=f"attn_d{d}",
    )(qkv, qkv, qkv, qkv, qkv)
    return o, lse


def _log_sigmoid(x):
    return jnp.minimum(x, 0.0) - jnp.log1p(jnp.exp(-jnp.abs(x)))


def _split3(x):
    hi = x.astype(BF16)
    r1 = x - hi.astype(F32)
    mid = r1.astype(BF16)
    lo = (r1 - mid.astype(F32)).astype(BF16)
    return hi, mid, lo


def _gla_pairwise_products(b_scr, qs_scr, kf_scr, pcat_scr):
    sub = GLA_SUB
    half = sub // 2
    tl = lax.broadcasted_iota(jnp.int32, (half, GLA_HK), 0)
    zeros_half = jnp.zeros((half, GLA_HK), F32)
    for blk_i in range(GLA_STEP // sub):
        r0 = blk_i * sub
        b_lo, b_hi = b_scr[r0:r0 + half, :], b_scr[r0 + half:r0 + sub, :]
        q_lo, q_hi = qs_scr[r0:r0 + half, :], qs_scr[r0 + half:r0 + sub, :]
        for s in range(sub):
            kb = kf_scr[r0 + s:r0 + s + 1, :]
            bb = b_scr[r0 + s:r0 + s + 1, :]
            d_hi = b_hi - bb
            if s > half:
                d_hi = jnp.where(tl + half >= s, d_hi, -jnp.inf)
            p_hi = q_hi * kb * jnp.exp(d_hi)
            if s < half:
                d_lo = b_lo - bb
                if s > 0:
                    d_lo = jnp.where(tl >= s, d_lo, -jnp.inf)
                p_lo = q_lo * kb * jnp.exp(d_lo)
            else:
                p_lo = zeros_half
            pcat_scr[r0:r0 + sub, s * GLA_HK:(s + 1) * GLA_HK] = (
                jnp.concatenate([p_lo, p_hi], axis=0).astype(BF16))


def _gla_reference_factors(b, qs, kf):
    c_len = GLA_CHUNK
    n_chunks = GLA_STEP // c_len

    def z(nrows):
        return jnp.zeros((nrows, GLA_HK), F32)

    qw, kw = [], []
    for c in range(n_chunks):
        bc, qc, kc = (x[c * c_len:(c + 1) * c_len] for x in (b, qs, kf))
        b15, b31, b47 = bc[15:16], bc[31:32], bc[47:48]
        q1 = jnp.concatenate([z(16), qc[16:32] * jnp.exp(bc[16:32] - b15), z(32)], axis=0)
        q2 = jnp.concatenate([z(32), qc[32:64] * jnp.exp(bc[32:64] - b31)], axis=0)
        q3 = jnp.concatenate([z(48), qc[48:64] * jnp.exp(bc[48:64] - b47)], axis=0)
        k1 = jnp.concatenate([kc[0:16] * jnp.exp(b15 - bc[0:16]), z(48)], axis=0)
        k2 = jnp.concatenate([kc[0:32] * jnp.exp(b31 - bc[0:32]), z(32)], axis=0)
        k3 = jnp.concatenate([z(32), kc[32:48] * jnp.exp(b47 - bc[32:48]), z(16)], axis=0)
        qw.append(jnp.concatenate([q1, q2, q3], axis=1))
        kw.append(jnp.concatenate([k1, k2, k3], axis=1))
    q_within = jnp.concatenate(qw, axis=0).astype(BF16)
    k_within = jnp.concatenate(kw, axis=0).astype(BF16)

    qx, kx = [], []
    for j in range(n_chunks - 1):
        lo, hi = j * c_len, (j + 1) * c_len
        b_ref = b[hi - 1:hi]
        qx.append(jnp.concatenate([z(hi), qs[hi:] * jnp.exp(b[hi:] - b_ref)], axis=0))
        parts = [kf[lo:hi] * jnp.exp(b_ref - b[lo:hi])]
        if lo:
            parts.insert(0, z(lo))
        parts.append(z(GLA_STEP - hi))
        kx.append(jnp.concatenate(parts, axis=0))
    q_cross = jnp.concatenate(qx, axis=1).astype(BF16)
    k_cross = jnp.concatenate(kx, axis=1).astype(BF16)
    return q_within, k_within, q_cross, k_cross


def _gla_kernel(q_ref, k_ref, v_ref, r_ref, pa_ref, u_ref, bias_ref, gn_ref, ltri_ref, e_ref,
                o_ref, st_ref, b_scr, qs_scr, kf_scr, pcat_scr):
    n = GLA_STEP
    nt = (((1,), (1,)), ((), ()))

    @pl.when(pl.program_id(1) == 0)
    def _():
        st_ref[...] = jnp.zeros_like(st_ref)

    logits = jnp.dot(pa_ref[...].astype(BF16), u_ref[...], preferred_element_type=F32) + bias_ref[...]
    la = _log_sigmoid(logits) * (1.0 / GLA_TAU)
    parts = jnp.dot(ltri_ref[...], jnp.concatenate(_split3(la), axis=1), preferred_element_type=F32)
    b_all = parts[:, :GLA_DK] + parts[:, GLA_DK:2 * GLA_DK] + parts[:, 2 * GLA_DK:]

    heads = []
    for h in range(GLA_HEADS):
        kcols = slice(h * GLA_HK, (h + 1) * GLA_HK)
        b = b_all[:, kcols]
        qs = q_ref[:, kcols].astype(F32) * (GLA_HK ** -0.5)
        kf = k_ref[:, kcols].astype(F32)
        b_scr[h], qs_scr[h], kf_scr[h] = b, qs, kf
        _gla_pairwise_products(b_scr.at[h], qs_scr.at[h], kf_scr.at[h], pcat_scr.at[h * n:(h + 1) * n])
        heads.append((b, qs, kf))
    a_diag = jnp.dot(pcat_scr[...], e_ref[...], preferred_element_type=F32)

    ri = lax.broadcasted_iota(jnp.int32, (n, n), 0)
    ci = lax.broadcasted_iota(jnp.int32, (n, n), 1)
    same_sub = (ri // GLA_SUB) == (ci // GLA_SUB)
    same_chunk = (ri // GLA_CHUNK) == (ci // GLA_CHUNK)
    gn = gn_ref[...]

    for h, (b, qs, kf) in enumerate(heads):
        vcols = slice(h * GLA_HV, (h + 1) * GLA_HV)
        q_w, k_w, q_x, k_x = _gla_reference_factors(b, qs, kf)
        a_within = lax.dot_general(q_w, k_w, nt, preferred_element_type=F32)
        a_cross = lax.dot_general(q_x, k_x, nt, preferred_element_type=F32)
        a = (jnp.where(same_chunk, a_within, 0.0) + a_cross
             + jnp.where(same_sub, a_diag[h * n:(h + 1) * n], 0.0))
        v = v_ref[:, vcols]
        st = st_ref[h]
        b_last = b[n - 1:n]
        q_in = (qs * jnp.exp(b)).astype(BF16)
        o = (jnp.dot(a.astype(BF16), v, preferred_element_type=F32)
             + lax.dot_general(q_in, st.astype(BF16), nt, preferred_element_type=F32))
        k_st = (kf * jnp.exp(b_last - b)).astype(BF16)
        upd = lax.dot_general(v, k_st, (((0,), (0,)), ((), ())), preferred_element_type=F32)
        st_ref[h] = st * jnp.exp(b_last) + upd

        ms = jnp.mean(o * o, axis=-1, keepdims=True)
        y = o * lax.rsqrt(ms + EPS) * gn
        r = r_ref[:, vcols].astype(F32)
        o_ref[:, vcols] = (y * (r * jax.nn.sigmoid(r))).astype(BF16)


def _gla_constants():
    n = GLA_STEP
    idx = np.arange(n)
    ltri = idx[:, None] >= idx[None, :]
    rows = np.arange(GLA_SUB * GLA_HK)
    cols = np.arange(n)
    e = (rows[:, None] // GLA_HK) == (cols[None, :] % GLA_SUB)
    return jnp.asarray(ltri, BF16), jnp.asarray(e, BF16)


def _gla_call(proj2d, pa, u_pad, bias, gn, batch):
    t = proj2d.shape[0]
    steps = t // batch // GLA_STEP
    n = GLA_STEP
    ltri, e = _gla_constants()

    def tok(width, col_block):
        return pl.BlockSpec((n, width), lambda bi, i: (bi * steps + i, col_block))

    def const(shape):
        return pl.BlockSpec(shape, lambda bi, i: (0, 0))

    return pl.pallas_call(
        _gla_kernel,
        grid=(batch, steps),
        in_specs=[
            tok(GLA_DK, P_GQ // GLA_DK),
            tok(GLA_DK, P_GK // GLA_DK),
            tok(GLA_DV, P_GV // GLA_DV),
            tok(GLA_DV, P_GR // GLA_DV),
            tok(PA_PAD, 0),
            const((PA_PAD, GLA_DK)),
            const((1, GLA_DK)),
            const((1, GLA_HV)),
            const((n, n)),
            const((GLA_SUB * GLA_HK, n)),
        ],
        out_specs=tok(GLA_DV, 0),
        out_shape=jax.ShapeDtypeStruct((t, GLA_DV), BF16),
        scratch_shapes=[
            pltpu.VMEM((GLA_HEADS, GLA_HV, GLA_HK), F32),
            pltpu.VMEM((GLA_HEADS, n, GLA_HK), F32),
            pltpu.VMEM((GLA_HEADS, n, GLA_HK), F32),
            pltpu.VMEM((GLA_HEADS, n, GLA_HK), F32),
            pltpu.VMEM((GLA_HEADS * n, GLA_SUB * GLA_HK), BF16),
        ],
        compiler_params=pltpu.CompilerParams(
            dimension_semantics=("parallel", "arbitrary"),
            vmem_limit_bytes=VMEM_LIMIT_BYTES),
        name="gla",
    )(proj2d, proj2d, proj2d, proj2d, pa, u_pad, bias, gn, ltri, e)


FF_CHUNK = 1024


def _split_dot(w, rhs):
    hi = w.astype(BF16)
    lo = (w - hi.astype(F32)).astype(BF16)
    return (jnp.dot(hi, rhs, preferred_element_type=F32)
            + jnp.dot(lo, rhs, preferred_element_type=F32))


def _out_kernel(x_ref, o0_ref, o1_ref, o2_ref, l0_ref, l1_ref, l2_ref, og_ref, gate_ref, gbias_ref,
                hx_ref, wa_ref, wb_ref, wo_ref, g2_ref, wup_ref, wdn_ref, out_ref):
    l0, l1, l2 = l0_ref[0], l1_ref[0], l2_ref[0]
    mx = jnp.maximum(jnp.maximum(l0, l1), l2)
    e0, e1, e2 = jnp.exp(l0 - mx), jnp.exp(l1 - mx), jnp.exp(l2 - mx)
    inv = 1.0 / (e0 + e1 + e2)
    hx = hx_ref[...]
    def pairs(ref):
        return jnp.concatenate([ref[0, p] for p in range(N_PAIRS)], axis=1)

    o_attn = (_split_dot(e0 * inv, hx) * pairs(o0_ref) + _split_dot(e1 * inv, hx) * pairs(o1_ref)
              + _split_dot(e2 * inv, hx) * pairs(o2_ref))
    a = jnp.dot(o_attn.astype(BF16), wa_ref[...], preferred_element_type=F32)
    g = jnp.dot(og_ref[...], wb_ref[...], preferred_element_type=F32)
    gate_a = jax.nn.sigmoid(gate_ref[:, :D_MODEL].astype(F32) + gbias_ref[:, :D_MODEL])
    gate_g = jax.nn.sigmoid(gate_ref[:, D_MODEL:].astype(F32) + gbias_ref[:, D_MODEL:])
    mixed = gate_a * a + gate_g * g
    x1 = x_ref[...] + jnp.dot(mixed.astype(BF16), wo_ref[...], preferred_element_type=F32)

    ms = jnp.mean(x1 * x1, axis=-1, keepdims=True)
    h2 = (x1 * lax.rsqrt(ms + EPS) * g2_ref[...]).astype(BF16)
    acc = x1
    for c in range(D_FF // FF_CHUNK):
        u = jnp.dot(h2, wup_ref[:, c * FF_CHUNK:(c + 1) * FF_CHUNK], preferred_element_type=F32)
        u = jnp.maximum(u, 0.0)
        u = (u * u).astype(BF16)
        acc = acc + jnp.dot(u, wdn_ref[c * FF_CHUNK:(c + 1) * FF_CHUNK, :],
                            preferred_element_type=F32)
    out_ref[...] = acc


def _out_call(x2d, o_groups, lse_groups, o_gla, proj2d, gbias, wa, wb, wo, g2, wup, wdn, batch, tm):
    t = x2d.shape[0]
    lanes = np.arange(LANES)
    cols = np.arange(GROUP_WIDTH)
    head_expand = jnp.asarray(lanes[:, None] == cols[None, :] // HEAD_DIM, BF16)

    tiles_per_seq = t // batch // tm

    def tok(width):
        return pl.BlockSpec((tm, width), lambda i: (i, 0))

    attn_o = pl.BlockSpec((1, N_PAIRS, tm, LANES),
                          lambda i: (i // tiles_per_seq, 0, i % tiles_per_seq, 0))
    attn_lse = pl.BlockSpec((1, tm, LANES), lambda i: (i // tiles_per_seq, i % tiles_per_seq, 0))

    def const(shape):
        return pl.BlockSpec(shape, lambda i: (0, 0), pipeline_mode=pl.Buffered(1))

    return pl.pallas_call(
        _out_kernel,
        grid=(t // tm,),
        in_specs=[
            tok(D_MODEL),
            attn_o, attn_o, attn_o,
            attn_lse, attn_lse, attn_lse,
            tok(GLA_DV),
            pl.BlockSpec((tm, 2 * D_MODEL), lambda i: (i, P_GATE // (2 * D_MODEL))),
            const((1, 2 * D_MODEL)),
            const((LANES, GROUP_WIDTH)),
            const((GROUP_WIDTH, D_MODEL)),
            const((GLA_DV, D_MODEL)),
            const((D_MODEL, D_MODEL)),
            const((1, D_MODEL)),
            const((D_MODEL, D_FF)),
            const((D_FF, D_MODEL)),
        ],
        out_specs=tok(D_MODEL),
        out_shape=jax.ShapeDtypeStruct((t, D_MODEL), F32),
        compiler_params=pltpu.CompilerParams(
            dimension_semantics=("parallel",),
            vmem_limit_bytes=VMEM_LIMIT_BYTES),
        name="out",
    )(x2d, *o_groups, *lse_groups, o_gla, proj2d, gbias, head_expand, wa, wb, wo, g2, wup, wdn)


def _layer(x2d, batch, norm1_g, w_in, gq, gk, gate_up, gate_bias, gla_norm_g, branch_bias,
           w_a, w_b, w_out, norm2_g, w_up, w_down, out_tm=512):
    def cols(off, size):
        return w_in[:, off:off + size].astype(BF16)

    g1 = norm1_g.reshape(1, D_MODEL)
    w_main = jnp.concatenate([
        cols(O_GATE, 2 * D_MODEL), cols(O_GV, GLA_DV), cols(O_GR, GLA_DV),
        cols(O_GQ, GLA_DK), cols(O_GK, GLA_DK)], axis=1)
    w_pa = jnp.pad(cols(O_PA, GLA_RANK), ((0, 0), (0, PA_PAD - GLA_RANK)))
    q_gain = jnp.tile(gq, HEADS_PER_GROUP) * (HEAD_DIM ** -0.5)
    k_gain = jnp.tile(gk, HEADS_PER_GROUP)
    qk_gain = jnp.stack([q_gain, k_gain]).reshape(2, 1, GROUP_WIDTH)
    idx = np.arange(MXU_DIM)
    bd = jnp.asarray((idx[:, None] // HEAD_DIM) == (idx[None, :] // HEAD_DIM), BF16)

    proj, pa = _main_proj_call(x2d, g1, w_main, w_pa, tm=1024, rows=512, tn=1024)

    w_qkv = [jnp.concatenate(
        [cols(O_ATTN + (kind * N_GROUPS + g) * GROUP_WIDTH, GROUP_WIDTH) for kind in range(N_KINDS)],
        axis=1) for g in range(N_GROUPS)]
    qkv_groups = _attn_proj_call(x2d, g1, w_qkv, qk_gain, bd, batch, tm=512)

    o_groups, lse_groups = [], []
    for qkv, (_, dilation) in zip(qkv_groups, ATTN_GROUPS):
        o, lse = _attn_call(qkv, dilation)
        o_groups.append(o)
        lse_groups.append(lse)

    u_pad = jnp.pad(gate_up, ((0, PA_PAD - GLA_RANK), (0, 0))).astype(BF16)
    o_gla = _gla_call(proj, pa, u_pad, gate_bias.reshape(1, GLA_DK),
                      gla_norm_g.reshape(1, GLA_HV), batch)

    return _out_call(x2d, o_groups, lse_groups, o_gla, proj,
                     branch_bias.reshape(1, 2 * D_MODEL),
                     w_a.astype(BF16), w_b.astype(BF16), w_out.astype(BF16),
                     norm2_g.reshape(1, D_MODEL), w_up.astype(BF16), w_down.astype(BF16), batch, out_tm)


def kernel(x, norm1_g, w_in, attn_q_norm_g, attn_k_norm_g, gla_gate_up, gla_gate_bias, gla_out_norm_g, branch_gate_bias, w_attn_branch, w_gla_branch, w_out, norm2_g, w_ff_up, w_ff_down):
    b, s, d = x.shape
    x2d = x.reshape(b * s, d)
    for l in range(norm1_g.shape[0]):
        x2d = _layer(x2d, b, norm1_g[l], w_in[l], attn_q_norm_g[l], attn_k_norm_g[l],
                     gla_gate_up[l], gla_gate_bias[l], gla_out_norm_g[l], branch_gate_bias[l],
                     w_attn_branch[l], w_gla_branch[l], w_out[l], norm2_g[l],
                     w_ff_up[l], w_ff_down[l])
    return x2d.reshape(b, s, d)
```

```python
import functools

import numpy as np
import jax
import jax.numpy as jnp
from jax import lax
from jax.experimental import pallas as pl
from jax.experimental.pallas import tpu as pltpu

F32 = jnp.float32
BF16 = jnp.bfloat16

D_MODEL = 1024
ATTN_GROUPS = ((128, 1), (512, 4), (2048, 16))
N_GROUPS = len(ATTN_GROUPS)
HEADS_PER_GROUP = 8
HEAD_DIM = 64
ATTN_BLOCK = 128
GROUP_WIDTH = HEADS_PER_GROUP * HEAD_DIM
ATTN_WIDTH = 3 * N_GROUPS * GROUP_WIDTH
N_PAIRS = GROUP_WIDTH // 128
ATTN_HEADS_PER_DOT = 4
ATTN_UNROLL = 4

GLA_HEADS = 4
GLA_DK = 512
GLA_DV = 1024
GLA_HK = GLA_DK // GLA_HEADS
GLA_HV = GLA_DV // GLA_HEADS
GLA_RANK = 16
GLA_TAU = 16.0
GLA_CHUNK = 64
GLA_SUB = 16
GLA_STEP = 256

D_FF = 4 * D_MODEL
EPS = 1e-6
LOG2_E = 1.4426950408889634
LN_2 = 0.6931471805599453

LANES = 128
MXU_DIM = 256
VMEM_LIMIT_BYTES = 56 * 1024 * 1024

_ORIG_SIZES = (ATTN_WIDTH, GLA_DK, GLA_DK, GLA_DV, GLA_DV, GLA_RANK, 2 * D_MODEL)
_ORIG_OFF = tuple(int(v) for v in np.cumsum((0,) + _ORIG_SIZES))
O_ATTN, O_GQ, O_GK, O_GV, O_GR, O_PA, O_GATE = _ORIG_OFF[:7]

P_GATE = 0
P_GV = P_GATE + 2 * D_MODEL
P_GR = P_GV + GLA_DV
P_GQ = P_GR + GLA_DV
P_GK = P_GQ + GLA_DK
P_MAIN = P_GK + GLA_DK
N_KINDS = 3
PA_PAD = LANES


def _rms_norm_rows(x, gain):
    ms = jnp.mean(x * x, axis=-1, keepdims=True)
    return x * lax.rsqrt(ms + EPS) * gain


def _main_proj_kernel(x_ref, g1_ref, w_ref, wpa_ref, o_ref, pa_ref, *, rows, tn):
    gain = g1_ref[...]
    for rc in range(x_ref.shape[0] // rows):
        rs = slice(rc * rows, (rc + 1) * rows)
        h = _rms_norm_rows(x_ref[rs, :], gain).astype(BF16)
        pa_ref[rs, :] = jnp.dot(h, wpa_ref[...], preferred_element_type=F32)
        for c in range(w_ref.shape[1] // tn):
            cols = slice(c * tn, (c + 1) * tn)
            o_ref[rs, cols] = jnp.dot(h, w_ref[:, cols], preferred_element_type=F32).astype(BF16)


def _main_proj_call(x2d, g1, w_main, w_pa, tm, rows, tn):
    t = x2d.shape[0]
    n = w_main.shape[1]

    def const(shape):
        return pl.BlockSpec(shape, lambda i: (0, 0), pipeline_mode=pl.Buffered(1))

    return pl.pallas_call(
        functools.partial(_main_proj_kernel, rows=rows, tn=tn),
        grid=(t // tm,),
        in_specs=[
            pl.BlockSpec((tm, D_MODEL), lambda i: (i, 0)),
            const((1, D_MODEL)),
            const((D_MODEL, n)),
            const((D_MODEL, PA_PAD)),
        ],
        out_specs=[
            pl.BlockSpec((tm, n), lambda i: (i, 0)),
            pl.BlockSpec((tm, PA_PAD), lambda i: (i, 0)),
        ],
        out_shape=[
            jax.ShapeDtypeStruct((t, n), BF16),
            jax.ShapeDtypeStruct((t, PA_PAD), F32),
        ],
        compiler_params=pltpu.CompilerParams(
            dimension_semantics=("parallel",),
            vmem_limit_bytes=VMEM_LIMIT_BYTES),
        name="proj_main",
    )(x2d, g1, w_main, w_pa)


def _qkv_project(h, w_ref, gain_ref, bd, o_ref):
    d = o_ref.shape[1]
    rows = o_ref.shape[2]
    for kind in range(N_KINDS):
        cols = slice(kind * GROUP_WIDTH, (kind + 1) * GROUP_WIDTH)
        acc = jnp.dot(h, w_ref[:, cols], preferred_element_type=F32)
        if kind < 2:
            sq = (acc * acc).astype(BF16)
            ss = jnp.concatenate(
                [jnp.dot(sq[:, c * MXU_DIM:(c + 1) * MXU_DIM], bd, preferred_element_type=F32)
                 for c in range(GROUP_WIDTH // MXU_DIM)], axis=1)
            acc = acc * lax.rsqrt(ss * (1.0 / HEAD_DIM) + EPS) * gain_ref[kind]
        y = acc.astype(BF16)
        for r in range(d):
            o_ref[0, r, :, cols] = y[r * rows:(r + 1) * rows, :]


def _attn_proj_kernel(x_ref, g1_ref, w0_ref, w1_ref, w2_ref, gain_ref, bd_ref,
                      o0_ref, o1_ref, o2_ref, col_scr, perm1_scr, perm2_scr):
    tm = x_ref.shape[0]
    n_col = D_MODEL // LANES
    bd = bd_ref[...]
    hf = _rms_norm_rows(x_ref[...], g1_ref[...])
    for c in range(n_col):
        col_scr[c] = hf[:, c * LANES:(c + 1) * LANES]
    _qkv_project(hf.astype(BF16), w0_ref, gain_ref, bd, o0_ref)
    for w_ref, o_ref, perm_scr in ((w1_ref, o1_ref, perm1_scr), (w2_ref, o2_ref, perm2_scr)):
        d = o_ref.shape[1]
        rows = tm // d
        for r in range(d):
            for c in range(n_col):
                perm_scr[r * rows:(r + 1) * rows, c * LANES:(c + 1) * LANES] = (
                    col_scr[c, pl.ds(r, rows, stride=d), :].astype(BF16))
        _qkv_project(perm_scr[...], w_ref, gain_ref, bd, o_ref)


def _attn_proj_call(x2d, g1, w_qkv, qk_gain, bd, batch, tm):
    t = x2d.shape[0]
    s = t // batch
    tiles_per_seq = s // tm
    width = N_KINDS * GROUP_WIDTH

    def const(shape):
        return pl.BlockSpec(shape, lambda i: (0,) * len(shape), pipeline_mode=pl.Buffered(1))

    return pl.pallas_call(
        _attn_proj_kernel,
        grid=(t // tm,),
        in_specs=[
            pl.BlockSpec((tm, D_MODEL), lambda i: (i, 0)),
            const((1, D_MODEL)),
            const((D_MODEL, width)), const((D_MODEL, width)), const((D_MODEL, width)),
            const((2, 1, GROUP_WIDTH)),
            const((MXU_DIM, MXU_DIM)),
        ],
        out_specs=[
            pl.BlockSpec((1, d, tm // d, width),
                         lambda i: (i // tiles_per_seq, 0, i % tiles_per_seq, 0))
            for _, d in ATTN_GROUPS],
        out_shape=[jax.ShapeDtypeStruct((batch, d, s // d, width), BF16) for _, d in ATTN_GROUPS],
        scratch_shapes=[
            pltpu.VMEM((D_MODEL // LANES, tm, LANES), F32),
            pltpu.VMEM((tm, D_MODEL), BF16),
            pltpu.VMEM((tm, D_MODEL), BF16),
        ],
        compiler_params=pltpu.CompilerParams(
            dimension_semantics=("parallel",),
            vmem_limit_bytes=VMEM_LIMIT_BYTES),
        name="proj_attn",
    )(x2d, g1, *w_qkv, qk_gain, bd)


def _attn_kernel(q_ref, kp_ref, kc_ref, vp_ref, vc_ref, o_ref, lse_ref):
    n = pl.program_id(1)
    d = q_ref.shape[1]
    blk = ATTN_BLOCK
    n_sub = q_ref.shape[2] // blk
    qi = lax.broadcasted_iota(jnp.int32, (blk, 2 * blk), 0)
    ki = lax.broadcasted_iota(jnp.int32, (blk, 2 * blk), 1)
    band = (ki >= qi) & (ki <= qi + blk)
    band_first = band & ((ki >= blk) | (n > 0))
    lane = lax.broadcasted_iota(jnp.int32, (blk, LANES), 1)
    first_head = lane < HEAD_DIM
    hb = ATTN_HEADS_PER_DOT
    width = hb * HEAD_DIM
    lane_w = lax.broadcasted_iota(jnp.int32, (blk, width), 1)
    head_lanes = [(lane_w >= h * HEAD_DIM) & (lane_w < (h + 1) * HEAD_DIM) for h in range(hb)]
    valid_by_sub = [jnp.concatenate([band_first if u == 0 else band] * hb, axis=0)
                    for u in range(min(n_sub, 2))]

    def unit(r, u):
        rows = slice(u * blk, (u + 1) * blk)
        valid_b = valid_by_sub[min(u, 1)]
        lse_tile = jnp.zeros((blk, LANES), F32)
        for g in range(HEADS_PER_GROUP // hb):
            sl = slice(g * width, (g + 1) * width)
            q = q_ref[0, r, rows, sl]
            if u == 0:
                k_prev, v_prev = kp_ref[0, r, :, sl], vp_ref[0, r, :, sl]
            else:
                prev_rows = slice((u - 1) * blk, u * blk)
                k_prev, v_prev = kc_ref[0, r, prev_rows, sl], vc_ref[0, r, prev_rows, sl]
            k = jnp.concatenate([k_prev, kc_ref[0, r, rows, sl]], axis=0)
            v = jnp.concatenate([v_prev, vc_ref[0, r, rows, sl]], axis=0)
            zero = jnp.zeros_like(q)
            q_rows = jnp.concatenate([jnp.where(head_lanes[h], q, zero) for h in range(hb)],
                                     axis=0)
            s = lax.dot_general(q_rows, k, (((1,), (1,)), ((), ())), preferred_element_type=F32)
            s = jnp.where(valid_b, s, -jnp.inf)
            m = jnp.max(s, axis=-1, keepdims=True)
            p = jnp.exp2(s - m)
            l = jnp.sum(p, axis=-1, keepdims=True)
            pv = jnp.dot(p.astype(BF16), v, preferred_element_type=F32)
            o_all = pv / l
            lse_all = m * LN_2 + jnp.log(l)
            tok = pl.ds(u * blk * d + r, blk, stride=d)
            for h in range(hb):
                head = g * hb + h
                lse_tile = jnp.where(lane == head, lse_all[h * blk:(h + 1) * blk], lse_tile)
            for pp in range(hb // 2):
                cols = slice(pp * LANES, (pp + 1) * LANES)
                lo = o_all[(2 * pp) * blk:(2 * pp + 1) * blk, cols]
                hi = o_all[(2 * pp + 1) * blk:(2 * pp + 2) * blk, cols]
                o_ref[0, g * (hb // 2) + pp, tok, :] = jnp.where(first_head, lo, hi)
        lse_ref[0, tok, :] = lse_tile

    if d * n_sub <= ATTN_UNROLL:
        for r in range(d):
            for u in range(n_sub):
                unit(r, u)
    else:
        def residue(r, carry):
            for u in range(n_sub):
                unit(r, u)
            return carry
        lax.fori_loop(0, d, residue, 0, unroll=ATTN_UNROLL // n_sub)


def _attn_call(qkv, dilation):
    b, d, sub_len, _ = qkv.shape
    assert d == dilation
    blk = ATTN_BLOCK
    n_sub = max(1, ATTN_UNROLL // d)
    rows = n_sub * blk
    steps = sub_len // rows
    s = sub_len * d

    def cur(kind):
        return lambda bi, n: (bi, 0, n, kind)

    def prev(kind):
        return lambda bi, n: (bi, 0, jnp.maximum(n * n_sub - 1, 0), kind)

    cur_shape = (1, d, rows, GROUP_WIDTH)
    prev_shape = (1, d, blk, GROUP_WIDTH)
    o, lse = pl.pallas_call(
        _attn_kernel,
        grid=(b, steps),
        in_specs=[
            pl.BlockSpec(cur_shape, cur(0)),
            pl.BlockSpec(prev_shape, prev(1)),
            pl.BlockSpec(cur_shape, cur(1)),
            pl.BlockSpec(prev_shape, prev(2)),
            pl.BlockSpec(cur_shape, cur(2)),
        ],
        out_specs=[
            pl.BlockSpec((1, N_PAIRS, d * rows, LANES), lambda bi, n: (bi, 0, n, 0)),
            pl.BlockSpec((1, d * rows, LANES), lambda bi, n: (bi, n, 0)),
        ],
        out_shape=[
            jax.ShapeDtypeStruct((b, N_PAIRS, s, LANES), F32),
            jax.ShapeDtypeStruct((b, s, LANES), F32),
        ],
        compiler_params=pltpu.CompilerParams(
            dimension_semantics=("parallel", "arbitrary"),
            vmem_limit_bytes=VMEM_LIMIT_BYTES),
        name=f"attn_d{d}",
    )(qkv, qkv, qkv, qkv, qkv)
    return o, lse


def _log_sigmoid(x):
    return jnp.minimum(x, 0.0) - jnp.log1p(jnp.exp(-jnp.abs(x)))


def _split3(x):
    hi = x.astype(BF16)
    r1 = x - hi.astype(F32)
    mid = r1.astype(BF16)
    lo = (r1 - mid.astype(F32)).astype(BF16)
    return hi, mid, lo


def _gla_pairwise_products(b_scr, qs_scr, kf_scr, pcat_scr):
    sub = GLA_SUB
    half = sub // 2
    tl = lax.broadcasted_iota(jnp.int32, (half, GLA_HK), 0)
    zeros_half = jnp.zeros((half, GLA_HK), F32)
    for blk_i in range(GLA_STEP // sub):
        r0 = blk_i * sub
        b_lo, b_hi = b_scr[r0:r0 + half, :], b_scr[r0 + half:r0 + sub, :]
        q_lo, q_hi = qs_scr[r0:r0 + half, :], qs_scr[r0 + half:r0 + sub, :]
        for s in range(sub):
            kb = kf_scr[r0 + s:r0 + s + 1, :]
            bb = b_scr[r0 + s:r0 + s + 1, :]
            d_hi = b_hi - bb
            if s > half:
                d_hi = jnp.where(tl + half >= s, d_hi, -jnp.inf)
            p_hi = q_hi * kb * jnp.exp2(d_hi)
            if s < half:
                d_lo = b_lo - bb
                if s > 0:
                    d_lo = jnp.where(tl >= s, d_lo, -jnp.inf)
                p_lo = q_lo * kb * jnp.exp2(d_lo)
            else:
                p_lo = zeros_half
            pcat_scr[r0:r0 + sub, s * GLA_HK:(s + 1) * GLA_HK] = (
                jnp.concatenate([p_lo, p_hi], axis=0).astype(BF16))


def _gla_reference_factors(b, qs, kf):
    c_len = GLA_CHUNK
    n_chunks = GLA_STEP // c_len

    def z(nrows):
        return jnp.zeros((nrows, GLA_HK), F32)

    qw, kw = [], []
    for c in range(n_chunks):
        bc, qc, kc = (x[c * c_len:(c + 1) * c_len] for x in (b, qs, kf))
        b15, b31, b47 = bc[15:16], bc[31:32], bc[47:48]
        q1 = jnp.concatenate([z(16), qc[16:32] * jnp.exp2(bc[16:32] - b15), z(32)], axis=0)
        q2 = jnp.concatenate([z(32), qc[32:64] * jnp.exp2(bc[32:64] - b31)], axis=0)
        q3 = jnp.concatenate([z(48), qc[48:64] * jnp.exp2(bc[48:64] - b47)], axis=0)
        k1 = jnp.concatenate([kc[0:16] * jnp.exp2(b15 - bc[0:16]), z(48)], axis=0)
        k2 = jnp.concatenate([kc[0:32] * jnp.exp2(b31 - bc[0:32]), z(32)], axis=0)
        k3 = jnp.concatenate([z(32), kc[32:48] * jnp.exp2(b47 - bc[32:48]), z(16)], axis=0)
        qw.append(jnp.concatenate([q1, q2, q3], axis=1))
        kw.append(jnp.concatenate([k1, k2, k3], axis=1))
    q_within = jnp.concatenate(qw, axis=0).astype(BF16)
    k_within = jnp.concatenate(kw, axis=0).astype(BF16)

    qx, kx = [], []
    for j in range(n_chunks - 1):
        lo, hi = j * c_len, (j + 1) * c_len
        b_ref = b[hi - 1:hi]
        qx.append(jnp.concatenate([z(hi), qs[hi:] * jnp.exp2(b[hi:] - b_ref)], axis=0))
        parts = [kf[lo:hi] * jnp.exp2(b_ref - b[lo:hi])]
        if lo:
            parts.insert(0, z(lo))
        parts.append(z(GLA_STEP - hi))
        kx.append(jnp.concatenate(parts, axis=0))
    q_cross = jnp.concatenate(qx, axis=1).astype(BF16)
    k_cross = jnp.concatenate(kx, axis=1).astype(BF16)
    return q_within, k_within, q_cross, k_cross


def _gla_kernel(q_ref, k_ref, v_ref, r_ref, pa_ref, u_ref, bias_ref, gn_ref, ltri_ref, e_ref,
                o_ref, st_ref, b_scr, qs_scr, kf_scr, pcat_scr):
    n = GLA_STEP
    nt = (((1,), (1,)), ((), ()))

    @pl.when(pl.program_id(1) == 0)
    def _():
        st_ref[...] = jnp.zeros_like(st_ref)

    logits = jnp.dot(pa_ref[...].astype(BF16), u_ref[...], preferred_element_type=F32) + bias_ref[...]
    la = _log_sigmoid(logits) * (1.0 / GLA_TAU)
    parts = jnp.dot(ltri_ref[...], jnp.concatenate(_split3(la), axis=1), preferred_element_type=F32)
    b_all = (parts[:, :GLA_DK] + parts[:, GLA_DK:2 * GLA_DK] + parts[:, 2 * GLA_DK:]) * LOG2_E

    heads = []
    for h in range(GLA_HEADS):
        kcols = slice(h * GLA_HK, (h + 1) * GLA_HK)
        b = b_all[:, kcols]
        qs = q_ref[:, kcols].astype(F32) * (GLA_HK ** -0.5)
        kf = k_ref[:, kcols].astype(F32)
        b_scr[h], qs_scr[h], kf_scr[h] = b, qs, kf
        _gla_pairwise_products(b_scr.at[h], qs_scr.at[h], kf_scr.at[h], pcat_scr.at[h * n:(h + 1) * n])
        heads.append((b, qs, kf))
    a_diag = jnp.dot(pcat_scr[...], e_ref[...], preferred_element_type=F32)

    ri = lax.broadcasted_iota(jnp.int32, (n, n), 0)
    ci = lax.broadcasted_iota(jnp.int32, (n, n), 1)
    same_sub = (ri // GLA_SUB) == (ci // GLA_SUB)
    same_chunk = (ri // GLA_CHUNK) == (ci // GLA_CHUNK)
    gn = gn_ref[...]

    for h, (b, qs, kf) in enumerate(heads):
        vcols = slice(h * GLA_HV, (h + 1) * GLA_HV)
        q_w, k_w, q_x, k_x = _gla_reference_factors(b, qs, kf)
        a_within = lax.dot_general(q_w, k_w, nt, preferred_element_type=F32)
        a_cross = lax.dot_general(q_x, k_x, nt, preferred_element_type=F32)
        a = (jnp.where(same_chunk, a_within, 0.0) + a_cross
             + jnp.where(same_sub, a_diag[h * n:(h + 1) * n], 0.0))
        v = v_ref[:, vcols]
        st = st_ref[h]
        b_last = b[n - 1:n]
        q_in = (qs * jnp.exp2(b)).astype(BF16)
        o = (jnp.dot(a.astype(BF16), v, preferred_element_type=F32)
             + lax.dot_general(q_in, st.astype(BF16), nt, preferred_element_type=F32))
        k_st = (kf * jnp.exp2(b_last - b)).astype(BF16)
        upd = lax.dot_general(v, k_st, (((0,), (0,)), ((), ())), preferred_element_type=F32)
        st_ref[h] = st * jnp.exp2(b_last) + upd

        ms = jnp.mean(o * o, axis=-1, keepdims=True)
        y = o * lax.rsqrt(ms + EPS) * gn
        r = r_ref[:, vcols].astype(F32)
        o_ref[:, vcols] = (y * (r * jax.nn.sigmoid(r))).astype(BF16)


def _gla_constants():
    n = GLA_STEP
    idx = np.arange(n)
    ltri = idx[:, None] >= idx[None, :]
    rows = np.arange(GLA_SUB * GLA_HK)
    cols = np.arange(n)
    e = (rows[:, None] // GLA_HK) == (cols[None, :] % GLA_SUB)
    return jnp.asarray(ltri, BF16), jnp.asarray(e, BF16)


def _gla_call(proj2d, pa, u_pad, bias, gn, batch):
    t = proj2d.shape[0]
    steps = t // batch // GLA_STEP
    n = GLA_STEP
    ltri, e = _gla_constants()

    def tok(width, col_block):
        return pl.BlockSpec((n, width), lambda bi, i: (bi * steps + i, col_block))

    def const(shape):
        return pl.BlockSpec(shape, lambda bi, i: (0, 0))

    return pl.pallas_call(
        _gla_kernel,
        grid=(batch, steps),
        in_specs=[
            tok(GLA_DK, P_GQ // GLA_DK),
            tok(GLA_DK, P_GK // GLA_DK),
            tok(GLA_DV, P_GV // GLA_DV),
            tok(GLA_DV, P_GR // GLA_DV),
            tok(PA_PAD, 0),
            const((PA_PAD, GLA_DK)),
            const((1, GLA_DK)),
            const((1, GLA_HV)),
            const((n, n)),
            const((GLA_SUB * GLA_HK, n)),
        ],
        out_specs=tok(GLA_DV, 0),
        out_shape=jax.ShapeDtypeStruct((t, GLA_DV), BF16),
        scratch_shapes=[
            pltpu.VMEM((GLA_HEADS, GLA_HV, GLA_HK), F32),
            pltpu.VMEM((GLA_HEADS, n, GLA_HK), F32),
            pltpu.VMEM((GLA_HEADS, n, GLA_HK), F32),
            pltpu.VMEM((GLA_HEADS, n, GLA_HK), F32),
            pltpu.VMEM((GLA_HEADS * n, GLA_SUB * GLA_HK), BF16),
        ],
        compiler_params=pltpu.CompilerParams(
            dimension_semantics=("parallel", "arbitrary"),
            vmem_limit_bytes=VMEM_LIMIT_BYTES),
        name="gla",
    )(proj2d, proj2d, proj2d, proj2d, pa, u_pad, bias, gn, ltri, e)


FF_CHUNK = 1024


def _out_kernel(x_ref, o0_ref, o1_ref, o2_ref, l0_ref, l1_ref, l2_ref, og_ref, gate_ref, gbias_ref,
                hx_ref, wa_ref, wb_ref, wo_ref, g2_ref, wup_ref, wdn_ref, out_ref):
    l0, l1, l2 = l0_ref[0], l1_ref[0], l2_ref[0]
    mx = jnp.maximum(jnp.maximum(l0, l1), l2)
    e0, e1, e2 = jnp.exp(l0 - mx), jnp.exp(l1 - mx), jnp.exp(l2 - mx)
    inv = 1.0 / (e0 + e1 + e2)
    hx = hx_ref[...]
    def pairs(ref):
        return jnp.concatenate([ref[0, p] for p in range(N_PAIRS)], axis=1)

    def expand(w):
        return jnp.dot(w.astype(BF16), hx, preferred_element_type=F32)

    o_attn = (expand(e0 * inv) * pairs(o0_ref) + expand(e1 * inv) * pairs(o1_ref)
              + expand(e2 * inv) * pairs(o2_ref))
    a = jnp.dot(o_attn.astype(BF16), wa_ref[...], preferred_element_type=F32)
    g = jnp.dot(og_ref[...], wb_ref[...], preferred_element_type=F32)
    gate_a = jax.nn.sigmoid(gate_ref[:, :D_MODEL].astype(F32) + gbias_ref[:, :D_MODEL])
    gate_g = jax.nn.sigmoid(gate_ref[:, D_MODEL:].astype(F32) + gbias_ref[:, D_MODEL:])
    mixed = gate_a * a + gate_g * g
    x1 = x_ref[...] + jnp.dot(mixed.astype(BF16), wo_ref[...], preferred_element_type=F32)

    ms = jnp.mean(x1 * x1, axis=-1, keepdims=True)
    h2 = (x1 * lax.rsqrt(ms + EPS) * g2_ref[...]).astype(BF16)
    acc = x1
    for c in range(D_FF // FF_CHUNK):
        u = jnp.dot(h2, wup_ref[:, c * FF_CHUNK:(c + 1) * FF_CHUNK], preferred_element_type=F32)
        u = jnp.maximum(u, 0.0)
        u = (u * u).astype(BF16)
        acc = acc + jnp.dot(u, wdn_ref[c * FF_CHUNK:(c + 1) * FF_CHUNK, :],
                            preferred_element_type=F32)
    out_ref[...] = acc


def _out_call(x2d, o_groups, lse_groups, o_gla, proj2d, gbias, wa, wb, wo, g2, wup, wdn, batch, tm):
    t = x2d.shape[0]
    lanes = np.arange(LANES)
    cols = np.arange(GROUP_WIDTH)
    head_expand = jnp.asarray(lanes[:, None] == cols[None, :] // HEAD_DIM, BF16)

    tiles_per_seq = t // batch // tm

    def tok(width):
        return pl.BlockSpec((tm, width), lambda i: (i, 0))

    attn_o = pl.BlockSpec((1, N_PAIRS, tm, LANES),
                          lambda i: (i // tiles_per_seq, 0, i % tiles_per_seq, 0))
    attn_lse = pl.BlockSpec((1, tm, LANES), lambda i: (i // tiles_per_seq, i % tiles_per_seq, 0))

    def const(shape):
        return pl.BlockSpec(shape, lambda i: (0, 0), pipeline_mode=pl.Buffered(1))

    return pl.pallas_call(
        _out_kernel,
        grid=(t // tm,),
        in_specs=[
            tok(D_MODEL),
            attn_o, attn_o, attn_o,
            attn_lse, attn_lse, attn_lse,
            tok(GLA_DV),
            pl.BlockSpec((tm, 2 * D_MODEL), lambda i: (i, P_GATE // (2 * D_MODEL))),
            const((1, 2 * D_MODEL)),
            const((LANES, GROUP_WIDTH)),
            const((GROUP_WIDTH, D_MODEL)),
            const((GLA_DV, D_MODEL)),
            const((D_MODEL, D_MODEL)),
            const((1, D_MODEL)),
            const((D_MODEL, D_FF)),
            const((D_FF, D_MODEL)),
        ],
        out_specs=tok(D_MODEL),
        out_shape=jax.ShapeDtypeStruct((t, D_MODEL), F32),
        compiler_params=pltpu.CompilerParams(
            dimension_semantics=("parallel",),
            vmem_limit_bytes=VMEM_LIMIT_BYTES),
        name="out",
    )(x2d, *o_groups, *lse_groups, o_gla, proj2d, gbias, head_expand, wa, wb, wo, g2, wup, wdn)


def _layer(x2d, batch, norm1_g, w_in, gq, gk, gate_up, gate_bias, gla_norm_g, branch_bias,
           w_a, w_b, w_out, norm2_g, w_up, w_down, out_tm=512):
    def cols(off, size):
        return w_in[:, off:off + size].astype(BF16)

    g1 = norm1_g.reshape(1, D_MODEL)
    w_main = jnp.concatenate([
        cols(O_GATE, 2 * D_MODEL), cols(O_GV, GLA_DV), cols(O_GR, GLA_DV),
        cols(O_GQ, GLA_DK), cols(O_GK, GLA_DK)], axis=1)
    w_pa = jnp.pad(cols(O_PA, GLA_RANK), ((0, 0), (0, PA_PAD - GLA_RANK)))
    q_gain = jnp.tile(gq, HEADS_PER_GROUP) * (HEAD_DIM ** -0.5 * LOG2_E)
    k_gain = jnp.tile(gk, HEADS_PER_GROUP)
    qk_gain = jnp.stack([q_gain, k_gain]).reshape(2, 1, GROUP_WIDTH)
    idx = np.arange(MXU_DIM)
    bd = jnp.asarray((idx[:, None] // HEAD_DIM) == (idx[None, :] // HEAD_DIM), BF16)

    proj, pa = _main_proj_call(x2d, g1, w_main, w_pa, tm=1024, rows=512, tn=1024)

    w_qkv = [jnp.concatenate(
        [cols(O_ATTN + (kind * N_GROUPS + g) * GROUP_WIDTH, GROUP_WIDTH) for kind in range(N_KINDS)],
        axis=1) for g in range(N_GROUPS)]
    qkv_groups = _attn_proj_call(x2d, g1, w_qkv, qk_gain, bd, batch, tm=512)

    o_groups, lse_groups = [], []
    for qkv, (_, dilation) in zip(qkv_groups, ATTN_GROUPS):
        o, lse = _attn_call(qkv, dilation)
        o_groups.append(o)
        lse_groups.append(lse)

    u_pad = jnp.pad(gate_up, ((0, PA_PAD - GLA_RANK), (0, 0))).astype(BF16)
    o_gla = _gla_call(proj, pa, u_pad, gate_bias.reshape(1, GLA_DK),
                      gla_norm_g.reshape(1, GLA_HV), batch)

    return _out_call(x2d, o_groups, lse_groups, o_gla, proj,
                     branch_bias.reshape(1, 2 * D_MODEL),
                     w_a.astype(BF16), w_b.astype(BF16), w_out.astype(BF16),
                     norm2_g.reshape(1, D_MODEL), w_up.astype(BF16), w_down.astype(BF16), batch, out_tm)


def kernel(x, norm1_g, w_in, attn_q_norm_g, attn_k_norm_g, gla_gate_up, gla_gate_bias, gla_out_norm_g, branch_gate_bias, w_attn_branch, w_gla_branch, w_out, norm2_g, w_ff_up, w_ff_down):
    b, s, d = x.shape
    x2d = x.reshape(b * s, d)
    for l in range(norm1_g.shape[0]):
        x2d = _layer(x2d, b, norm1_g[l], w_in[l], attn_q_norm_g[l], attn_k_norm_g[l],
                     gla_gate_up[l], gla_gate_bias[l], gla_out_norm_g[l], branch_gate_bias[l],
                     w_attn_branch[l], w_gla_branch[l], w_out[l], norm2_g[l],
                     w_ff_up[l], w_ff_down[l])
    return x2d.reshape(b, s, d)
```

```python
import functools

import numpy as np
import jax
import jax.numpy as jnp
from jax import lax
from jax.experimental import pallas as pl
from jax.experimental.pallas import tpu as pltpu

F32 = jnp.float32
BF16 = jnp.bfloat16

D_MODEL = 1024
ATTN_GROUPS = ((128, 1), (512, 4), (2048, 16))
N_GROUPS = len(ATTN_GROUPS)
HEADS_PER_GROUP = 8
HEAD_DIM = 64
ATTN_BLOCK = 128
GROUP_WIDTH = HEADS_PER_GROUP * HEAD_DIM
ATTN_WIDTH = 3 * N_GROUPS * GROUP_WIDTH
N_PAIRS = GROUP_WIDTH // 128
ATTN_HEADS_PER_DOT = 4
ATTN_UNROLL = 4

GLA_HEADS = 4
GLA_DK = 512
GLA_DV = 1024
GLA_HK = GLA_DK // GLA_HEADS
GLA_HV = GLA_DV // GLA_HEADS
GLA_RANK = 16
GLA_TAU = 16.0
GLA_CHUNK = 64
GLA_SUB = 16
GLA_STEP = 256
GLA_SUBSTEPS = 4

D_FF = 4 * D_MODEL
EPS = 1e-6
LOG2_E = 1.4426950408889634
LN_2 = 0.6931471805599453

LANES = 128
MXU_DIM = 256
VMEM_LIMIT_BYTES = 56 * 1024 * 1024

_ORIG_SIZES = (ATTN_WIDTH, GLA_DK, GLA_DK, GLA_DV, GLA_DV, GLA_RANK, 2 * D_MODEL)
_ORIG_OFF = tuple(int(v) for v in np.cumsum((0,) + _ORIG_SIZES))
O_ATTN, O_GQ, O_GK, O_GV, O_GR, O_PA, O_GATE = _ORIG_OFF[:7]

P_GATE = 0
P_GV = P_GATE + 2 * D_MODEL
P_GR = P_GV + GLA_DV
P_GQ = P_GR + GLA_DV
P_GK = P_GQ + GLA_DK
P_MAIN = P_GK + GLA_DK
N_KINDS = 3
PA_PAD = LANES


def _rms_norm_rows(x, gain):
    ms = jnp.mean(x * x, axis=-1, keepdims=True)
    return x * lax.rsqrt(ms + EPS) * gain


def _main_proj_kernel(x_ref, g1_ref, w_ref, wpa_ref, o_ref, pa_ref, *, rows, tn):
    gain = g1_ref[...]
    for rc in range(x_ref.shape[0] // rows):
        rs = slice(rc * rows, (rc + 1) * rows)
        h = _rms_norm_rows(x_ref[rs, :], gain).astype(BF16)
        pa_ref[rs, :] = jnp.dot(h, wpa_ref[...], preferred_element_type=F32)
        for c in range(w_ref.shape[1] // tn):
            cols = slice(c * tn, (c + 1) * tn)
            o_ref[rs, cols] = jnp.dot(h, w_ref[:, cols], preferred_element_type=F32).astype(BF16)


def _main_proj_call(x2d, g1, w_main, w_pa, tm, rows, tn):
    t = x2d.shape[0]
    n = w_main.shape[1]

    def const(shape):
        return pl.BlockSpec(shape, lambda i: (0, 0), pipeline_mode=pl.Buffered(1))

    return pl.pallas_call(
        functools.partial(_main_proj_kernel, rows=rows, tn=tn),
        grid=(t // tm,),
        in_specs=[
            pl.BlockSpec((tm, D_MODEL), lambda i: (i, 0)),
            const((1, D_MODEL)),
            const((D_MODEL, n)),
            const((D_MODEL, PA_PAD)),
        ],
        out_specs=[
            pl.BlockSpec((tm, n), lambda i: (i, 0)),
            pl.BlockSpec((tm, PA_PAD), lambda i: (i, 0)),
        ],
        out_shape=[
            jax.ShapeDtypeStruct((t, n), BF16),
            jax.ShapeDtypeStruct((t, PA_PAD), F32),
        ],
        compiler_params=pltpu.CompilerParams(
            dimension_semantics=("parallel",),
            vmem_limit_bytes=VMEM_LIMIT_BYTES),
        name="proj_main",
    )(x2d, g1, w_main, w_pa)


def _qkv_project(h, w_ref, gain_ref, bd, o_ref):
    d = o_ref.shape[1]
    rows = o_ref.shape[2]
    for kind in range(N_KINDS):
        cols = slice(kind * GROUP_WIDTH, (kind + 1) * GROUP_WIDTH)
        acc = jnp.dot(h, w_ref[:, cols], preferred_element_type=F32)
        if kind < 2:
            sq = (acc * acc).astype(BF16)
            ss = jnp.concatenate(
                [jnp.dot(sq[:, c * MXU_DIM:(c + 1) * MXU_DIM], bd, preferred_element_type=F32)
                 for c in range(GROUP_WIDTH // MXU_DIM)], axis=1)
            acc = acc * lax.rsqrt(ss * (1.0 / HEAD_DIM) + EPS) * gain_ref[kind]
        y = acc.astype(BF16)
        for r in range(d):
            o_ref[0, r, :, cols] = y[r * rows:(r + 1) * rows, :]


def _attn_proj_kernel(x_ref, g1_ref, w0_ref, w1_ref, w2_ref, gain_ref, bd_ref,
                      o0_ref, o1_ref, o2_ref, col_scr, perm1_scr, perm2_scr):
    tm = x_ref.shape[0]
    n_col = D_MODEL // LANES
    bd = bd_ref[...]
    hf = _rms_norm_rows(x_ref[...], g1_ref[...])
    for c in range(n_col):
        col_scr[c] = hf[:, c * LANES:(c + 1) * LANES]
    _qkv_project(hf.astype(BF16), w0_ref, gain_ref, bd, o0_ref)
    for w_ref, o_ref, perm_scr in ((w1_ref, o1_ref, perm1_scr), (w2_ref, o2_ref, perm2_scr)):
        d = o_ref.shape[1]
        rows = tm // d
        for r in range(d):
            for c in range(n_col):
                perm_scr[r * rows:(r + 1) * rows, c * LANES:(c + 1) * LANES] = (
                    col_scr[c, pl.ds(r, rows, stride=d), :].astype(BF16))
        _qkv_project(perm_scr[...], w_ref, gain_ref, bd, o_ref)


def _attn_proj_call(x2d, g1, w_qkv, qk_gain, bd, batch, tm):
    t = x2d.shape[0]
    s = t // batch
    tiles_per_seq = s // tm
    width = N_KINDS * GROUP_WIDTH

    def const(shape):
        return pl.BlockSpec(shape, lambda i: (0,) * len(shape), pipeline_mode=pl.Buffered(1))

    return pl.pallas_call(
        _attn_proj_kernel,
        grid=(t // tm,),
        in_specs=[
            pl.BlockSpec((tm, D_MODEL), lambda i: (i, 0)),
            const((1, D_MODEL)),
            const((D_MODEL, width)), const((D_MODEL, width)), const((D_MODEL, width)),
            const((2, 1, GROUP_WIDTH)),
            const((MXU_DIM, MXU_DIM)),
        ],
        out_specs=[
            pl.BlockSpec((1, d, tm // d, width),
                         lambda i: (i // tiles_per_seq, 0, i % tiles_per_seq, 0))
            for _, d in ATTN_GROUPS],
        out_shape=[jax.ShapeDtypeStruct((batch, d, s // d, width), BF16) for _, d in ATTN_GROUPS],
        scratch_shapes=[
            pltpu.VMEM((D_MODEL // LANES, tm, LANES), F32),
            pltpu.VMEM((tm, D_MODEL), BF16),
            pltpu.VMEM((tm, D_MODEL), BF16),
        ],
        compiler_params=pltpu.CompilerParams(
            dimension_semantics=("parallel",),
            vmem_limit_bytes=VMEM_LIMIT_BYTES),
        name="proj_attn",
    )(x2d, g1, *w_qkv, qk_gain, bd)


def _attn_kernel(q_ref, kp_ref, kc_ref, vp_ref, vc_ref, o_ref, lse_ref):
    n = pl.program_id(1)
    d = q_ref.shape[1]
    blk = ATTN_BLOCK
    n_sub = q_ref.shape[2] // blk
    qi = lax.broadcasted_iota(jnp.int32, (blk, 2 * blk), 0)
    ki = lax.broadcasted_iota(jnp.int32, (blk, 2 * blk), 1)
    band = (ki >= qi) & (ki <= qi + blk)
    band_first = band & ((ki >= blk) | (n > 0))
    lane = lax.broadcasted_iota(jnp.int32, (blk, LANES), 1)
    first_head = lane < HEAD_DIM
    hb = ATTN_HEADS_PER_DOT
    width = hb * HEAD_DIM
    lane_w = lax.broadcasted_iota(jnp.int32, (blk, width), 1)
    head_lanes = [(lane_w >= h * HEAD_DIM) & (lane_w < (h + 1) * HEAD_DIM) for h in range(hb)]
    valid_by_sub = [jnp.concatenate([band_first if u == 0 else band] * hb, axis=0)
                    for u in range(min(n_sub, 2))]

    def unit(r, u):
        rows = slice(u * blk, (u + 1) * blk)
        valid_b = valid_by_sub[min(u, 1)]
        lse_tile = jnp.zeros((blk, LANES), F32)
        for g in range(HEADS_PER_GROUP // hb):
            sl = slice(g * width, (g + 1) * width)
            q = q_ref[0, r, rows, sl]
            if u == 0:
                k_prev, v_prev = kp_ref[0, r, :, sl], vp_ref[0, r, :, sl]
            else:
                prev_rows = slice((u - 1) * blk, u * blk)
                k_prev, v_prev = kc_ref[0, r, prev_rows, sl], vc_ref[0, r, prev_rows, sl]
            k = jnp.concatenate([k_prev, kc_ref[0, r, rows, sl]], axis=0)
            v = jnp.concatenate([v_prev, vc_ref[0, r, rows, sl]], axis=0)
            zero = jnp.zeros_like(q)
            q_rows = jnp.concatenate([jnp.where(head_lanes[h], q, zero) for h in range(hb)],
                                     axis=0)
            s = lax.dot_general(q_rows, k, (((1,), (1,)), ((), ())), preferred_element_type=F32)
            s = jnp.where(valid_b, s, -jnp.inf)
            m = jnp.max(s, axis=-1, keepdims=True)
            p = jnp.exp2(s - m)
            l = jnp.sum(p, axis=-1, keepdims=True)
            pv = jnp.dot(p.astype(BF16), v, preferred_element_type=F32)
            o_all = pv / l
            lse_all = m * LN_2 + jnp.log(l)
            tok = pl.ds(u * blk * d + r, blk, stride=d)
            for h in range(hb):
                head = g * hb + h
                lse_tile = jnp.where(lane == head, lse_all[h * blk:(h + 1) * blk], lse_tile)
            for pp in range(hb // 2):
                cols = slice(pp * LANES, (pp + 1) * LANES)
                lo = o_all[(2 * pp) * blk:(2 * pp + 1) * blk, cols]
                hi = o_all[(2 * pp + 1) * blk:(2 * pp + 2) * blk, cols]
                o_ref[0, g * (hb // 2) + pp, tok, :] = jnp.where(first_head, lo, hi)
        lse_ref[0, tok, :] = lse_tile

    if d * n_sub <= ATTN_UNROLL:
        for r in range(d):
            for u in range(n_sub):
                unit(r, u)
    else:
        def residue(r, carry):
            for u in range(n_sub):
                unit(r, u)
            return carry
        lax.fori_loop(0, d, residue, 0, unroll=ATTN_UNROLL // n_sub)


def _attn_call(qkv, dilation):
    b, d, sub_len, _ = qkv.shape
    assert d == dilation
    blk = ATTN_BLOCK
    n_sub = max(1, ATTN_UNROLL // d)
    rows = n_sub * blk
    steps = sub_len // rows
    s = sub_len * d

    def cur(kind):
        return lambda bi, n: (bi, 0, n, kind)

    def prev(kind):
        return lambda bi, n: (bi, 0, jnp.maximum(n * n_sub - 1, 0), kind)

    cur_shape = (1, d, rows, GROUP_WIDTH)
    prev_shape = (1, d, blk, GROUP_WIDTH)
    o, lse = pl.pallas_call(
        _attn_kernel,
        grid=(b, steps),
        in_specs=[
            pl.BlockSpec(cur_shape, cur(0)),
            pl.BlockSpec(prev_shape, prev(1)),
            pl.BlockSpec(cur_shape, cur(1)),
            pl.BlockSpec(prev_shape, prev(2)),
            pl.BlockSpec(cur_shape, cur(2)),
        ],
        out_specs=[
            pl.BlockSpec((1, N_PAIRS, d * rows, LANES), lambda bi, n: (bi, 0, n, 0)),
            pl.BlockSpec((1, d * rows, LANES), lambda bi, n: (bi, n, 0)),
        ],
        out_shape=[
            jax.ShapeDtypeStruct((b, N_PAIRS, s, LANES), F32),
            jax.ShapeDtypeStruct((b, s, LANES), F32),
        ],
        compiler_params=pltpu.CompilerParams(
            dimension_semantics=("parallel", "arbitrary"),
            vmem_limit_bytes=VMEM_LIMIT_BYTES),
        name=f"attn_d{d}",
    )(qkv, qkv, qkv, qkv, qkv)
    return o, lse


def _log_sigmoid(x):
    return jnp.minimum(x, 0.0) - jnp.log1p(jnp.exp(-jnp.abs(x)))


def _split3(x):
    hi = x.astype(BF16)
    r1 = x - hi.astype(F32)
    mid = r1.astype(BF16)
    lo = (r1 - mid.astype(F32)).astype(BF16)
    return hi, mid, lo


def _gla_pairwise_products(b_scr, qs_scr, kf_scr, pcat_scr):
    sub = GLA_SUB
    half = sub // 2
    tl = lax.broadcasted_iota(jnp.int32, (half, GLA_HK), 0)
    zeros_half = jnp.zeros((half, GLA_HK), F32)
    for blk_i in range(GLA_STEP // sub):
        r0 = blk_i * sub
        b_lo, b_hi = b_scr[r0:r0 + half, :], b_scr[r0 + half:r0 + sub, :]
        q_lo, q_hi = qs_scr[r0:r0 + half, :], qs_scr[r0 + half:r0 + sub, :]
        for s in range(sub):
            kb = kf_scr[r0 + s:r0 + s + 1, :]
            bb = b_scr[r0 + s:r0 + s + 1, :]
            d_hi = b_hi - bb
            if s > half:
                d_hi = jnp.where(tl + half >= s, d_hi, -jnp.inf)
            p_hi = q_hi * kb * jnp.exp2(d_hi)
            if s < half:
                d_lo = b_lo - bb
                if s > 0:
                    d_lo = jnp.where(tl >= s, d_lo, -jnp.inf)
                p_lo = q_lo * kb * jnp.exp2(d_lo)
            else:
                p_lo = zeros_half
            pcat_scr[r0:r0 + sub, s * GLA_HK:(s + 1) * GLA_HK] = (
                jnp.concatenate([p_lo, p_hi], axis=0).astype(BF16))


def _gla_reference_factors(b, qs, kf):
    c_len = GLA_CHUNK
    n_chunks = GLA_STEP // c_len

    def z(nrows):
        return jnp.zeros((nrows, GLA_HK), F32)

    qw, kw = [], []
    for c in range(n_chunks):
        bc, qc, kc = (x[c * c_len:(c + 1) * c_len] for x in (b, qs, kf))
        b15, b31, b47 = bc[15:16], bc[31:32], bc[47:48]
        q1 = jnp.concatenate([z(16), qc[16:32] * jnp.exp2(bc[16:32] - b15), z(32)], axis=0)
        q2 = jnp.concatenate([z(32), qc[32:64] * jnp.exp2(bc[32:64] - b31)], axis=0)
        q3 = jnp.concatenate([z(48), qc[48:64] * jnp.exp2(bc[48:64] - b47)], axis=0)
        k1 = jnp.concatenate([kc[0:16] * jnp.exp2(b15 - bc[0:16]), z(48)], axis=0)
        k2 = jnp.concatenate([kc[0:32] * jnp.exp2(b31 - bc[0:32]), z(32)], axis=0)
        k3 = jnp.concatenate([z(32), kc[32:48] * jnp.exp2(b47 - bc[32:48]), z(16)], axis=0)
        qw.append(jnp.concatenate([q1, q2, q3], axis=1))
        kw.append(jnp.concatenate([k1, k2, k3], axis=1))
    q_within = jnp.concatenate(qw, axis=0).astype(BF16)
    k_within = jnp.concatenate(kw, axis=0).astype(BF16)

    qx, kx = [], []
    for j in range(n_chunks - 1):
        lo, hi = j * c_len, (j + 1) * c_len
        b_ref = b[hi - 1:hi]
        qx.append(jnp.concatenate([z(hi), qs[hi:] * jnp.exp2(b[hi:] - b_ref)], axis=0))
        parts = [kf[lo:hi] * jnp.exp2(b_ref - b[lo:hi])]
        if lo:
            parts.insert(0, z(lo))
        parts.append(z(GLA_STEP - hi))
        kx.append(jnp.concatenate(parts, axis=0))
    q_cross = jnp.concatenate(qx, axis=1).astype(BF16)
    k_cross = jnp.concatenate(kx, axis=1).astype(BF16)
    return q_within, k_within, q_cross, k_cross


def _gla_kernel(q_ref, k_ref, v_ref, r_ref, pa_ref, u_ref, bias_ref, gn_ref, ltri_ref, e_ref,
                o_ref, st_ref, b_scr, qs_scr, kf_scr, pcat_scr):
    @pl.when(pl.program_id(1) == 0)
    def _():
        st_ref[...] = jnp.zeros_like(st_ref)

    n = GLA_STEP

    def substep(ss, carry):
        rows = pl.ds(pl.multiple_of(ss * n, n), n)
        _gla_substep(q_ref.at[rows], k_ref.at[rows], v_ref.at[rows], r_ref.at[rows], pa_ref.at[rows],
                     u_ref, bias_ref, gn_ref, ltri_ref, e_ref, o_ref.at[rows], st_ref,
                     b_scr, qs_scr, kf_scr, pcat_scr)
        return carry

    lax.fori_loop(0, q_ref.shape[0] // n, substep, 0)


def _gla_substep(q_ref, k_ref, v_ref, r_ref, pa_ref, u_ref, bias_ref, gn_ref, ltri_ref, e_ref,
                 o_ref, st_ref, b_scr, qs_scr, kf_scr, pcat_scr):
    n = GLA_STEP
    nt = (((1,), (1,)), ((), ()))

    logits = jnp.dot(pa_ref[...].astype(BF16), u_ref[...], preferred_element_type=F32) + bias_ref[...]
    la = _log_sigmoid(logits) * (1.0 / GLA_TAU)
    parts = jnp.dot(ltri_ref[...], jnp.concatenate(_split3(la), axis=1), preferred_element_type=F32)
    b_all = (parts[:, :GLA_DK] + parts[:, GLA_DK:2 * GLA_DK] + parts[:, 2 * GLA_DK:]) * LOG2_E

    heads = []
    for h in range(GLA_HEADS):
        kcols = slice(h * GLA_HK, (h + 1) * GLA_HK)
        b = b_all[:, kcols]
        qs = q_ref[:, kcols].astype(F32) * (GLA_HK ** -0.5)
        kf = k_ref[:, kcols].astype(F32)
        b_scr[h], qs_scr[h], kf_scr[h] = b, qs, kf
        _gla_pairwise_products(b_scr.at[h], qs_scr.at[h], kf_scr.at[h], pcat_scr.at[h * n:(h + 1) * n])
        heads.append((b, qs, kf))
    a_diag = jnp.dot(pcat_scr[...], e_ref[...], preferred_element_type=F32)

    ri = lax.broadcasted_iota(jnp.int32, (n, n), 0)
    ci = lax.broadcasted_iota(jnp.int32, (n, n), 1)
    same_sub = (ri // GLA_SUB) == (ci // GLA_SUB)
    same_chunk = (ri // GLA_CHUNK) == (ci // GLA_CHUNK)
    gn = gn_ref[...]

    for h, (b, qs, kf) in enumerate(heads):
        vcols = slice(h * GLA_HV, (h + 1) * GLA_HV)
        q_w, k_w, q_x, k_x = _gla_reference_factors(b, qs, kf)
        a_within = lax.dot_general(q_w, k_w, nt, preferred_element_type=F32)
        a_cross = lax.dot_general(q_x, k_x, nt, preferred_element_type=F32)
        a = (jnp.where(same_chunk, a_within, 0.0) + a_cross
             + jnp.where(same_sub, a_diag[h * n:(h + 1) * n], 0.0))
        v = v_ref[:, vcols]
        st = st_ref[h]
        b_last = b[n - 1:n]
        q_in = (qs * jnp.exp2(b)).astype(BF16)
        o = (jnp.dot(a.astype(BF16), v, preferred_element_type=F32)
             + lax.dot_general(q_in, st.astype(BF16), nt, preferred_element_type=F32))
        k_st = (kf * jnp.exp2(b_last - b)).astype(BF16)
        upd = lax.dot_general(v, k_st, (((0,), (0,)), ((), ())), preferred_element_type=F32)
        st_ref[h] = st * jnp.exp2(b_last) + upd

        ms = jnp.mean(o * o, axis=-1, keepdims=True)
        y = o * lax.rsqrt(ms + EPS) * gn
        r = r_ref[:, vcols].astype(F32)
        o_ref[:, vcols] = (y * (r * jax.nn.sigmoid(r))).astype(BF16)


def _gla_constants():
    n = GLA_STEP
    idx = np.arange(n)
    ltri = idx[:, None] >= idx[None, :]
    rows = np.arange(GLA_SUB * GLA_HK)
    cols = np.arange(n)
    e = (rows[:, None] // GLA_HK) == (cols[None, :] % GLA_SUB)
    return jnp.asarray(ltri, BF16), jnp.asarray(e, BF16)


def _gla_call(proj2d, pa, u_pad, bias, gn, batch):
    t = proj2d.shape[0]
    n = GLA_STEP
    rows = n * GLA_SUBSTEPS
    steps = t // batch // rows
    slots = GLA_HEADS
    ltri, e = _gla_constants()

    def tok(width, col_block):
        return pl.BlockSpec((rows, width), lambda bi, i: (bi * steps + i, col_block))

    def const(shape):
        return pl.BlockSpec(shape, lambda bi, i: (0, 0))

    return pl.pallas_call(
        _gla_kernel,
        grid=(batch, steps),
        in_specs=[
            tok(GLA_DK, P_GQ // GLA_DK),
            tok(GLA_DK, P_GK // GLA_DK),
            tok(GLA_DV, P_GV // GLA_DV),
            tok(GLA_DV, P_GR // GLA_DV),
            tok(PA_PAD, 0),
            const((PA_PAD, GLA_DK)),
            const((1, GLA_DK)),
            const((1, GLA_HV)),
            const((n, n)),
            const((GLA_SUB * GLA_HK, n)),
        ],
        out_specs=tok(GLA_DV, 0),
        out_shape=jax.ShapeDtypeStruct((t, GLA_DV), BF16),
        scratch_shapes=[
            pltpu.VMEM((GLA_HEADS, GLA_HV, GLA_HK), F32),
            pltpu.VMEM((slots, n, GLA_HK), F32),
            pltpu.VMEM((slots, n, GLA_HK), F32),
            pltpu.VMEM((slots, n, GLA_HK), F32),
            pltpu.VMEM((slots * n, GLA_SUB * GLA_HK), BF16),
        ],
        compiler_params=pltpu.CompilerParams(
            dimension_semantics=("parallel", "arbitrary"),
            vmem_limit_bytes=VMEM_LIMIT_BYTES),
        name="gla",
    )(proj2d, proj2d, proj2d, proj2d, pa, u_pad, bias, gn, ltri, e)


FF_CHUNK = 1024


def _out_kernel(x_ref, o0_ref, o1_ref, o2_ref, l0_ref, l1_ref, l2_ref, og_ref, gate_ref, gbias_ref,
                hx_ref, wa_ref, wb_ref, wo_ref, g2_ref, wup_ref, wdn_ref, out_ref):
    l0, l1, l2 = l0_ref[0], l1_ref[0], l2_ref[0]
    mx = jnp.maximum(jnp.maximum(l0, l1), l2)
    e0, e1, e2 = jnp.exp(l0 - mx), jnp.exp(l1 - mx), jnp.exp(l2 - mx)
    inv = 1.0 / (e0 + e1 + e2)
    hx = hx_ref[...]
    def pairs(ref):
        return jnp.concatenate([ref[0, p] for p in range(N_PAIRS)], axis=1)

    def expand(w):
        return jnp.dot(w.astype(BF16), hx, preferred_element_type=F32)

    o_attn = (expand(e0 * inv) * pairs(o0_ref) + expand(e1 * inv) * pairs(o1_ref)
              + expand(e2 * inv) * pairs(o2_ref))
    a = jnp.dot(o_attn.astype(BF16), wa_ref[...], preferred_element_type=F32)
    g = jnp.dot(og_ref[...], wb_ref[...], preferred_element_type=F32)
    gate_a = jax.nn.sigmoid(gate_ref[:, :D_MODEL].astype(F32) + gbias_ref[:, :D_MODEL])
    gate_g = jax.nn.sigmoid(gate_ref[:, D_MODEL:].astype(F32) + gbias_ref[:, D_MODEL:])
    mixed = gate_a * a + gate_g * g
    x1 = x_ref[...] + jnp.dot(mixed.astype(BF16), wo_ref[...], preferred_element_type=F32)

    ms = jnp.mean(x1 * x1, axis=-1, keepdims=True)
    h2 = (x1 * lax.rsqrt(ms + EPS) * g2_ref[...]).astype(BF16)
    acc = x1
    for c in range(D_FF // FF_CHUNK):
        u = jnp.dot(h2, wup_ref[:, c * FF_CHUNK:(c + 1) * FF_CHUNK], preferred_element_type=F32)
        u = jnp.maximum(u, 0.0)
        u = (u * u).astype(BF16)
        acc = acc + jnp.dot(u, wdn_ref[c * FF_CHUNK:(c + 1) * FF_CHUNK, :],
                            preferred_element_type=F32)
    out_ref[...] = acc


def _out_call(x2d, o_groups, lse_groups, o_gla, proj2d, gbias, wa, wb, wo, g2, wup, wdn, batch, tm):
    t = x2d.shape[0]
    lanes = np.arange(LANES)
    cols = np.arange(GROUP_WIDTH)
    head_expand = jnp.asarray(lanes[:, None] == cols[None, :] // HEAD_DIM, BF16)

    tiles_per_seq = t // batch // tm

    def tok(width):
        return pl.BlockSpec((tm, width), lambda i: (i, 0))

    attn_o = pl.BlockSpec((1, N_PAIRS, tm, LANES),
                          lambda i: (i // tiles_per_seq, 0, i % tiles_per_seq, 0))
    attn_lse = pl.BlockSpec((1, tm, LANES), lambda i: (i // tiles_per_seq, i % tiles_per_seq, 0))

    def const(shape):
        return pl.BlockSpec(shape, lambda i: (0, 0), pipeline_mode=pl.Buffered(1))

    return pl.pallas_call(
        _out_kernel,
        grid=(t // tm,),
        in_specs=[
            tok(D_MODEL),
            attn_o, attn_o, attn_o,
            attn_lse, attn_lse, attn_lse,
            tok(GLA_DV),
            pl.BlockSpec((tm, 2 * D_MODEL), lambda i: (i, P_GATE // (2 * D_MODEL))),
            const((1, 2 * D_MODEL)),
            const((LANES, GROUP_WIDTH)),
            const((GROUP_WIDTH, D_MODEL)),
            const((GLA_DV, D_MODEL)),
            const((D_MODEL, D_MODEL)),
            const((1, D_MODEL)),
            const((D_MODEL, D_FF)),
            const((D_FF, D_MODEL)),
        ],
        out_specs=tok(D_MODEL),
        out_shape=jax.ShapeDtypeStruct((t, D_MODEL), F32),
        compiler_params=pltpu.CompilerParams(
            dimension_semantics=("parallel",),
            vmem_limit_bytes=VMEM_LIMIT_BYTES),
        name="out",
    )(x2d, *o_groups, *lse_groups, o_gla, proj2d, gbias, head_expand, wa, wb, wo, g2, wup, wdn)


def _layer(x2d, batch, norm1_g, w_in, gq, gk, gate_up, gate_bias, gla_norm_g, branch_bias,
           w_a, w_b, w_out, norm2_g, w_up, w_down, out_tm=512):
    def cols(off, size):
        return w_in[:, off:off + size].astype(BF16)

    g1 = norm1_g.reshape(1, D_MODEL)
    w_main = jnp.concatenate([
        cols(O_GATE, 2 * D_MODEL), cols(O_GV, GLA_DV), cols(O_GR, GLA_DV),
        cols(O_GQ, GLA_DK), cols(O_GK, GLA_DK)], axis=1)
    w_pa = jnp.pad(cols(O_PA, GLA_RANK), ((0, 0), (0, PA_PAD - GLA_RANK)))
    q_gain = jnp.tile(gq, HEADS_PER_GROUP) * (HEAD_DIM ** -0.5 * LOG2_E)
    k_gain = jnp.tile(gk, HEADS_PER_GROUP)
    qk_gain = jnp.stack([q_gain, k_gain]).reshape(2, 1, GROUP_WIDTH)
    idx = np.arange(MXU_DIM)
    bd = jnp.asarray((idx[:, None] // HEAD_DIM) == (idx[None, :] // HEAD_DIM), BF16)

    proj, pa = _main_proj_call(x2d, g1, w_main, w_pa, tm=1024, rows=512, tn=1024)

    w_qkv = [jnp.concatenate(
        [cols(O_ATTN + (kind * N_GROUPS + g) * GROUP_WIDTH, GROUP_WIDTH) for kind in range(N_KINDS)],
        axis=1) for g in range(N_GROUPS)]
    qkv_groups = _attn_proj_call(x2d, g1, w_qkv, qk_gain, bd, batch, tm=512)

    o_groups, lse_groups = [], []
    for qkv, (_, dilation) in zip(qkv_groups, ATTN_GROUPS):
        o, lse = _attn_call(qkv, dilation)
        o_groups.append(o)
        lse_groups.append(lse)

    u_pad = jnp.pad(gate_up, ((0, PA_PAD - GLA_RANK), (0, 0))).astype(BF16)
    o_gla = _gla_call(proj, pa, u_pad, gate_bias.reshape(1, GLA_DK),
                      gla_norm_g.reshape(1, GLA_HV), batch)

    return _out_call(x2d, o_groups, lse_groups, o_gla, proj,
                     branch_bias.reshape(1, 2 * D_MODEL),
                     w_a.astype(BF16), w_b.astype(BF16), w_out.astype(BF16),
                     norm2_g.reshape(1, D_MODEL), w_up.astype(BF16), w_down.astype(BF16), batch, out_tm)


def kernel(x, norm1_g, w_in, attn_q_norm_g, attn_k_norm_g, gla_gate_up, gla_gate_bias, gla_out_norm_g, branch_gate_bias, w_attn_branch, w_gla_branch, w_out, norm2_g, w_ff_up, w_ff_down):
    b, s, d = x.shape
    x2d = x.reshape(b * s, d)
    for l in range(norm1_g.shape[0]):
        x2d = _layer(x2d, b, norm1_g[l], w_in[l], attn_q_norm_g[l], attn_k_norm_g[l],
                     gla_gate_up[l], gla_gate_bias[l], gla_out_norm_g[l], branch_gate_bias[l],
                     w_attn_branch[l], w_gla_branch[l], w_out[l], norm2_g[l],
                     w_ff_up[l], w_ff_down[l])
    return x2d.reshape(b, s, d)
```

```python
import functools

import numpy as np
import jax
import jax.numpy as jnp
from jax import lax
from jax.experimental import pallas as pl
from jax.experimental.pallas import tpu as pltpu

F32 = jnp.float32
BF16 = jnp.bfloat16

D_MODEL = 1024
ATTN_GROUPS = ((128, 1), (512, 4), (2048, 16))
N_GROUPS = len(ATTN_GROUPS)
HEADS_PER_GROUP = 8
HEAD_DIM = 64
ATTN_BLOCK = 128
GROUP_WIDTH = HEADS_PER_GROUP * HEAD_DIM
ATTN_WIDTH = 3 * N_GROUPS * GROUP_WIDTH
N_PAIRS = GROUP_WIDTH // 128
ATTN_HEADS_PER_DOT = 4
ATTN_UNROLL = 4

GLA_HEADS = 4
GLA_DK = 512
GLA_DV = 1024
GLA_HK = GLA_DK // GLA_HEADS
GLA_HV = GLA_DV // GLA_HEADS
GLA_RANK = 16
GLA_TAU = 16.0
GLA_CHUNK = 64
GLA_SUB = 8
GLA_STEP = 256
GLA_SUBSTEPS = 4

D_FF = 4 * D_MODEL
EPS = 1e-6
LOG2_E = 1.4426950408889634
LN_2 = 0.6931471805599453

LANES = 128
MXU_DIM = 256
VMEM_LIMIT_BYTES = 56 * 1024 * 1024

_ORIG_SIZES = (ATTN_WIDTH, GLA_DK, GLA_DK, GLA_DV, GLA_DV, GLA_RANK, 2 * D_MODEL)
_ORIG_OFF = tuple(int(v) for v in np.cumsum((0,) + _ORIG_SIZES))
O_ATTN, O_GQ, O_GK, O_GV, O_GR, O_PA, O_GATE = _ORIG_OFF[:7]

P_GATE = 0
P_GV = P_GATE + 2 * D_MODEL
P_GR = P_GV + GLA_DV
P_GQ = P_GR + GLA_DV
P_GK = P_GQ + GLA_DK
P_MAIN = P_GK + GLA_DK
N_KINDS = 3
PA_PAD = LANES


def _rms_norm_rows(x, gain):
    ms = jnp.mean(x * x, axis=-1, keepdims=True)
    return x * lax.rsqrt(ms + EPS) * gain


def _main_proj_kernel(x_ref, g1_ref, w_ref, wpa_ref, o_ref, pa_ref, *, rows, tn):
    gain = g1_ref[...]
    for rc in range(x_ref.shape[0] // rows):
        rs = slice(rc * rows, (rc + 1) * rows)
        h = _rms_norm_rows(x_ref[rs, :], gain).astype(BF16)
        pa_ref[rs, :] = jnp.dot(h, wpa_ref[...], preferred_element_type=F32)
        for c in range(w_ref.shape[1] // tn):
            cols = slice(c * tn, (c + 1) * tn)
            o_ref[rs, cols] = jnp.dot(h, w_ref[:, cols], preferred_element_type=F32).astype(BF16)


def _main_proj_call(x2d, g1, w_main, w_pa, tm, rows, tn):
    t = x2d.shape[0]
    n = w_main.shape[1]

    def const(shape):
        return pl.BlockSpec(shape, lambda i: (0, 0), pipeline_mode=pl.Buffered(1))

    return pl.pallas_call(
        functools.partial(_main_proj_kernel, rows=rows, tn=tn),
        grid=(t // tm,),
        in_specs=[
            pl.BlockSpec((tm, D_MODEL), lambda i: (i, 0)),
            const((1, D_MODEL)),
            const((D_MODEL, n)),
            const((D_MODEL, PA_PAD)),
        ],
        out_specs=[
            pl.BlockSpec((tm, n), lambda i: (i, 0)),
            pl.BlockSpec((tm, PA_PAD), lambda i: (i, 0)),
        ],
        out_shape=[
            jax.ShapeDtypeStruct((t, n), BF16),
            jax.ShapeDtypeStruct((t, PA_PAD), F32),
        ],
        compiler_params=pltpu.CompilerParams(
            dimension_semantics=("parallel",),
            vmem_limit_bytes=VMEM_LIMIT_BYTES),
        name="proj_main",
    )(x2d, g1, w_main, w_pa)


def _qkv_project(h, w_ref, gain_ref, bd, o_ref):
    d = o_ref.shape[1]
    rows = o_ref.shape[2]
    for kind in range(N_KINDS):
        cols = slice(kind * GROUP_WIDTH, (kind + 1) * GROUP_WIDTH)
        acc = jnp.dot(h, w_ref[:, cols], preferred_element_type=F32)
        if kind < 2:
            sq = (acc * acc).astype(BF16)
            ss = jnp.concatenate(
                [jnp.dot(sq[:, c * MXU_DIM:(c + 1) * MXU_DIM], bd, preferred_element_type=F32)
                 for c in range(GROUP_WIDTH // MXU_DIM)], axis=1)
            acc = acc * lax.rsqrt(ss * (1.0 / HEAD_DIM) + EPS) * gain_ref[kind]
        y = acc.astype(BF16)
        for r in range(d):
            o_ref[0, r, :, cols] = y[r * rows:(r + 1) * rows, :]


def _attn_proj_kernel(x_ref, g1_ref, w0_ref, w1_ref, w2_ref, gain_ref, bd_ref,
                      o0_ref, o1_ref, o2_ref, col_scr, perm1_scr, perm2_scr):
    tm = x_ref.shape[0]
    n_col = D_MODEL // LANES
    bd = bd_ref[...]
    hf = _rms_norm_rows(x_ref[...], g1_ref[...])
    for c in range(n_col):
        col_scr[c] = hf[:, c * LANES:(c + 1) * LANES]
    _qkv_project(hf.astype(BF16), w0_ref, gain_ref, bd, o0_ref)
    for w_ref, o_ref, perm_scr in ((w1_ref, o1_ref, perm1_scr), (w2_ref, o2_ref, perm2_scr)):
        d = o_ref.shape[1]
        rows = tm // d
        for r in range(d):
            for c in range(n_col):
                perm_scr[r * rows:(r + 1) * rows, c * LANES:(c + 1) * LANES] = (
                    col_scr[c, pl.ds(r, rows, stride=d), :].astype(BF16))
        _qkv_project(perm_scr[...], w_ref, gain_ref, bd, o_ref)


def _attn_proj_call(x2d, g1, w_qkv, qk_gain, bd, batch, tm):
    t = x2d.shape[0]
    s = t // batch
    tiles_per_seq = s // tm
    width = N_KINDS * GROUP_WIDTH

    def const(shape):
        return pl.BlockSpec(shape, lambda i: (0,) * len(shape), pipeline_mode=pl.Buffered(1))

    return pl.pallas_call(
        _attn_proj_kernel,
        grid=(t // tm,),
        in_specs=[
            pl.BlockSpec((tm, D_MODEL), lambda i: (i, 0)),
            const((1, D_MODEL)),
            const((D_MODEL, width)), const((D_MODEL, width)), const((D_MODEL, width)),
            const((2, 1, GROUP_WIDTH)),
            const((MXU_DIM, MXU_DIM)),
        ],
        out_specs=[
            pl.BlockSpec((1, d, tm // d, width),
                         lambda i: (i // tiles_per_seq, 0, i % tiles_per_seq, 0))
            for _, d in ATTN_GROUPS],
        out_shape=[jax.ShapeDtypeStruct((batch, d, s // d, width), BF16) for _, d in ATTN_GROUPS],
        scratch_shapes=[
            pltpu.VMEM((D_MODEL // LANES, tm, LANES), F32),
            pltpu.VMEM((tm, D_MODEL), BF16),
            pltpu.VMEM((tm, D_MODEL), BF16),
        ],
        compiler_params=pltpu.CompilerParams(
            dimension_semantics=("parallel",),
            vmem_limit_bytes=VMEM_LIMIT_BYTES),
        name="proj_attn",
    )(x2d, g1, *w_qkv, qk_gain, bd)


def _attn_kernel(q_ref, kp_ref, kc_ref, vp_ref, vc_ref, o_ref, lse_ref):
    n = pl.program_id(1)
    d = q_ref.shape[1]
    blk = ATTN_BLOCK
    n_sub = q_ref.shape[2] // blk
    qi = lax.broadcasted_iota(jnp.int32, (blk, 2 * blk), 0)
    ki = lax.broadcasted_iota(jnp.int32, (blk, 2 * blk), 1)
    band = (ki >= qi) & (ki <= qi + blk)
    band_first = band & ((ki >= blk) | (n > 0))
    lane = lax.broadcasted_iota(jnp.int32, (blk, LANES), 1)
    first_head = lane < HEAD_DIM
    hb = ATTN_HEADS_PER_DOT
    width = hb * HEAD_DIM
    lane_w = lax.broadcasted_iota(jnp.int32, (blk, width), 1)
    head_lanes = [(lane_w >= h * HEAD_DIM) & (lane_w < (h + 1) * HEAD_DIM) for h in range(hb)]
    valid_by_sub = [jnp.concatenate([band_first if u == 0 else band] * hb, axis=0)
                    for u in range(min(n_sub, 2))]

    def unit(r, u):
        rows = slice(u * blk, (u + 1) * blk)
        valid_b = valid_by_sub[min(u, 1)]
        lse_tile = jnp.zeros((blk, LANES), F32)
        for g in range(HEADS_PER_GROUP // hb):
            sl = slice(g * width, (g + 1) * width)
            q = q_ref[0, r, rows, sl]
            if u == 0:
                k_prev, v_prev = kp_ref[0, r, :, sl], vp_ref[0, r, :, sl]
            else:
                prev_rows = slice((u - 1) * blk, u * blk)
                k_prev, v_prev = kc_ref[0, r, prev_rows, sl], vc_ref[0, r, prev_rows, sl]
            k = jnp.concatenate([k_prev, kc_ref[0, r, rows, sl]], axis=0)
            v = jnp.concatenate([v_prev, vc_ref[0, r, rows, sl]], axis=0)
            zero = jnp.zeros_like(q)
            q_rows = jnp.concatenate([jnp.where(head_lanes[h], q, zero) for h in range(hb)],
                                     axis=0)
            s = lax.dot_general(q_rows, k, (((1,), (1,)), ((), ())), preferred_element_type=F32)
            s = jnp.where(valid_b, s, -jnp.inf)
            m = jnp.max(s, axis=-1, keepdims=True)
            p = jnp.exp2(s - m)
            l = jnp.sum(p, axis=-1, keepdims=True)
            pv = jnp.dot(p.astype(BF16), v, preferred_element_type=F32)
            o_all = pv / l
            lse_all = m * LN_2 + jnp.log(l)
            tok = pl.ds(u * blk * d + r, blk, stride=d)
            for h in range(hb):
                head = g * hb + h
                lse_tile = jnp.where(lane == head, lse_all[h * blk:(h + 1) * blk], lse_tile)
            for pp in range(hb // 2):
                cols = slice(pp * LANES, (pp + 1) * LANES)
                lo = o_all[(2 * pp) * blk:(2 * pp + 1) * blk, cols]
                hi = o_all[(2 * pp + 1) * blk:(2 * pp + 2) * blk, cols]
                o_ref[0, g * (hb // 2) + pp, tok, :] = jnp.where(first_head, lo, hi)
        lse_ref[0, tok, :] = lse_tile

    if d * n_sub <= ATTN_UNROLL:
        for r in range(d):
            for u in range(n_sub):
                unit(r, u)
    else:
        def residue(r, carry):
            for u in range(n_sub):
                unit(r, u)
            return carry
        lax.fori_loop(0, d, residue, 0, unroll=ATTN_UNROLL // n_sub)


def _attn_call(qkv, dilation):
    b, d, sub_len, _ = qkv.shape
    assert d == dilation
    blk = ATTN_BLOCK
    n_sub = max(1, ATTN_UNROLL // d)
    rows = n_sub * blk
    steps = sub_len // rows
    s = sub_len * d

    def cur(kind):
        return lambda bi, n: (bi, 0, n, kind)

    def prev(kind):
        return lambda bi, n: (bi, 0, jnp.maximum(n * n_sub - 1, 0), kind)

    cur_shape = (1, d, rows, GROUP_WIDTH)
    prev_shape = (1, d, blk, GROUP_WIDTH)
    o, lse = pl.pallas_call(
        _attn_kernel,
        grid=(b, steps),
        in_specs=[
            pl.BlockSpec(cur_shape, cur(0)),
            pl.BlockSpec(prev_shape, prev(1)),
            pl.BlockSpec(cur_shape, cur(1)),
            pl.BlockSpec(prev_shape, prev(2)),
            pl.BlockSpec(cur_shape, cur(2)),
        ],
        out_specs=[
            pl.BlockSpec((1, N_PAIRS, d * rows, LANES), lambda bi, n: (bi, 0, n, 0)),
            pl.BlockSpec((1, d * rows, LANES), lambda bi, n: (bi, n, 0)),
        ],
        out_shape=[
            jax.ShapeDtypeStruct((b, N_PAIRS, s, LANES), F32),
            jax.ShapeDtypeStruct((b, s, LANES), F32),
        ],
        compiler_params=pltpu.CompilerParams(
            dimension_semantics=("parallel", "arbitrary"),
            vmem_limit_bytes=VMEM_LIMIT_BYTES),
        name=f"attn_d{d}",
    )(qkv, qkv, qkv, qkv, qkv)
    return o, lse


def _log_sigmoid(x):
    return jnp.minimum(x, 0.0) - jnp.log(1.0 + jnp.exp(-jnp.abs(x)))


def _split3(x):
    hi = x.astype(BF16)
    r1 = x - hi.astype(F32)
    mid = r1.astype(BF16)
    lo = (r1 - mid.astype(F32)).astype(BF16)
    return hi, mid, lo


def _gla_pairwise_products(b_scr, qs_scr, kf_scr, pcat_scr):
    sub = GLA_SUB
    tl = lax.broadcasted_iota(jnp.int32, (sub, GLA_HK), 0)
    for pair_i in range(GLA_STEP // (2 * sub)):
        tiles = []
        for r0 in (2 * pair_i * sub, (2 * pair_i + 1) * sub):
            b_blk, q_blk = b_scr[r0:r0 + sub, :], qs_scr[r0:r0 + sub, :]
            row = []
            for s in range(sub):
                d = b_blk - b_scr[r0 + s:r0 + s + 1, :]
                if s > 0:
                    d = jnp.where(tl >= s, d, -jnp.inf)
                row.append(q_blk * kf_scr[r0 + s:r0 + s + 1, :] * jnp.exp2(d))
            tiles.append(row)
        r0 = 2 * pair_i * sub
        for s in range(sub):
            pcat_scr[r0:r0 + 2 * sub, s * GLA_HK:(s + 1) * GLA_HK] = (
                jnp.concatenate([tiles[0][s], tiles[1][s]], axis=0).astype(BF16))


def _gla_reference_factors(b, qs, kf):
    c_len = GLA_CHUNK
    n_chunks = GLA_STEP // c_len
    sub = GLA_SUB

    def z(nrows):
        return jnp.zeros((nrows, GLA_HK), F32)

    qb, kb = [], []
    for r0 in range(0, GLA_STEP, 2 * sub):
        mid = r0 + sub
        b_ref = b[mid - 1:mid]
        qb += [z(sub), qs[mid:mid + sub] * jnp.exp2(b[mid:mid + sub] - b_ref)]
        kb += [kf[r0:mid] * jnp.exp2(b_ref - b[r0:mid]), z(sub)]
    q_block = jnp.concatenate(qb, axis=0).astype(BF16)
    k_block = jnp.concatenate(kb, axis=0).astype(BF16)

    qw, kw = [], []
    for c in range(n_chunks):
        bc, qc, kc = (x[c * c_len:(c + 1) * c_len] for x in (b, qs, kf))
        b15, b31, b47 = bc[15:16], bc[31:32], bc[47:48]
        q1 = jnp.concatenate([z(16), qc[16:32] * jnp.exp2(bc[16:32] - b15), z(32)], axis=0)
        q2 = jnp.concatenate([z(32), qc[32:64] * jnp.exp2(bc[32:64] - b31)], axis=0)
        q3 = jnp.concatenate([z(48), qc[48:64] * jnp.exp2(bc[48:64] - b47)], axis=0)
        k1 = jnp.concatenate([kc[0:16] * jnp.exp2(b15 - bc[0:16]), z(48)], axis=0)
        k2 = jnp.concatenate([kc[0:32] * jnp.exp2(b31 - bc[0:32]), z(32)], axis=0)
        k3 = jnp.concatenate([z(32), kc[32:48] * jnp.exp2(b47 - bc[32:48]), z(16)], axis=0)
        qw.append(jnp.concatenate([q1, q2, q3], axis=1))
        kw.append(jnp.concatenate([k1, k2, k3], axis=1))
    q_within = jnp.concatenate(qw, axis=0).astype(BF16)
    k_within = jnp.concatenate(kw, axis=0).astype(BF16)

    qx, kx = [], []
    for j in range(n_chunks - 1):
        lo, hi = j * c_len, (j + 1) * c_len
        b_ref = b[hi - 1:hi]
        qx.append(jnp.concatenate([z(hi), qs[hi:] * jnp.exp2(b[hi:] - b_ref)], axis=0))
        parts = [kf[lo:hi] * jnp.exp2(b_ref - b[lo:hi])]
        if lo:
            parts.insert(0, z(lo))
        parts.append(z(GLA_STEP - hi))
        kx.append(jnp.concatenate(parts, axis=0))
    q_cross = jnp.concatenate(qx, axis=1).astype(BF16)
    k_cross = jnp.concatenate(kx, axis=1).astype(BF16)
    return (q_block, k_block), (q_within, k_within), (q_cross, k_cross)


def _gla_kernel(q_ref, k_ref, v_ref, r_ref, pa_ref, u_ref, bias_ref, gn_ref, ltri_ref, e_ref,
                o_ref, st_ref, b_scr, qs_scr, kf_scr, pcat_scr):
    @pl.when(pl.program_id(1) == 0)
    def _():
        st_ref[...] = jnp.zeros_like(st_ref)

    n = GLA_STEP

    def substep(ss, carry):
        rows = pl.ds(pl.multiple_of(ss * n, n), n)
        _gla_substep(q_ref.at[rows], k_ref.at[rows], v_ref.at[rows], r_ref.at[rows], pa_ref.at[rows],
                     u_ref, bias_ref, gn_ref, ltri_ref, e_ref, o_ref.at[rows], st_ref,
                     b_scr, qs_scr, kf_scr, pcat_scr)
        return carry

    lax.fori_loop(0, q_ref.shape[0] // n, substep, 0)


def _gla_substep(q_ref, k_ref, v_ref, r_ref, pa_ref, u_ref, bias_ref, gn_ref, ltri_ref, e_ref,
                 o_ref, st_ref, b_scr, qs_scr, kf_scr, pcat_scr):
    n = GLA_STEP
    nt = (((1,), (1,)), ((), ()))

    logits = jnp.dot(pa_ref[...].astype(BF16), u_ref[...], preferred_element_type=F32) + bias_ref[...]
    la = _log_sigmoid(logits) * (1.0 / GLA_TAU)
    parts = jnp.dot(ltri_ref[...], jnp.concatenate(_split3(la), axis=1), preferred_element_type=F32)
    b_all = (parts[:, :GLA_DK] + parts[:, GLA_DK:2 * GLA_DK] + parts[:, 2 * GLA_DK:]) * LOG2_E

    heads = []
    for h in range(GLA_HEADS):
        kcols = slice(h * GLA_HK, (h + 1) * GLA_HK)
        b = b_all[:, kcols]
        qs = q_ref[:, kcols].astype(F32) * (GLA_HK ** -0.5)
        kf = k_ref[:, kcols].astype(F32)
        b_scr[h], qs_scr[h], kf_scr[h] = b, qs, kf
        _gla_pairwise_products(b_scr.at[h], qs_scr.at[h], kf_scr.at[h], pcat_scr.at[h * n:(h + 1) * n])
        heads.append((b, qs, kf))
    a_diag = jnp.dot(pcat_scr[...], e_ref[...], preferred_element_type=F32)

    ri = lax.broadcasted_iota(jnp.int32, (n, n), 0)
    ci = lax.broadcasted_iota(jnp.int32, (n, n), 1)
    same_sub = (ri // GLA_SUB) == (ci // GLA_SUB)
    same_pair = (ri // (2 * GLA_SUB)) == (ci // (2 * GLA_SUB))
    same_chunk = (ri // GLA_CHUNK) == (ci // GLA_CHUNK)
    gn = gn_ref[...]

    for h, (b, qs, kf) in enumerate(heads):
        vcols = slice(h * GLA_HV, (h + 1) * GLA_HV)
        a_block, a_within, a_cross = (
            lax.dot_general(qf, kf_, nt, preferred_element_type=F32)
            for qf, kf_ in _gla_reference_factors(b, qs, kf))
        a = (jnp.where(same_sub, a_diag[h * n:(h + 1) * n], 0.0) + jnp.where(same_pair, a_block, 0.0)
             + jnp.where(same_chunk, a_within, 0.0) + a_cross)
        v = v_ref[:, vcols]
        st = st_ref[h]
        b_last = b[n - 1:n]
        q_in = (qs * jnp.exp2(b)).astype(BF16)
        o = (jnp.dot(a.astype(BF16), v, preferred_element_type=F32)
             + lax.dot_general(q_in, st.astype(BF16), nt, preferred_element_type=F32))
        k_st = (kf * jnp.exp2(b_last - b)).astype(BF16)
        upd = lax.dot_general(v, k_st, (((0,), (0,)), ((), ())), preferred_element_type=F32)
        st_ref[h] = st * jnp.exp2(b_last) + upd

        ms = jnp.mean(o * o, axis=-1, keepdims=True)
        y = o * lax.rsqrt(ms + EPS) * gn
        r = r_ref[:, vcols].astype(F32)
        o_ref[:, vcols] = (y * (r * jax.nn.sigmoid(r))).astype(BF16)


def _gla_constants():
    n = GLA_STEP
    idx = np.arange(n)
    ltri = idx[:, None] >= idx[None, :]
    rows = np.arange(GLA_SUB * GLA_HK)
    cols = np.arange(n)
    e = (rows[:, None] // GLA_HK) == (cols[None, :] % GLA_SUB)
    return jnp.asarray(ltri, BF16), jnp.asarray(e, BF16)


def _gla_call(proj2d, pa, u_pad, bias, gn, batch):
    t = proj2d.shape[0]
    n = GLA_STEP
    rows = n * GLA_SUBSTEPS
    steps = t // batch // rows
    slots = GLA_HEADS
    ltri, e = _gla_constants()

    def tok(width, col_block):
        return pl.BlockSpec((rows, width), lambda bi, i: (bi * steps + i, col_block))

    def const(shape):
        return pl.BlockSpec(shape, lambda bi, i: (0, 0))

    return pl.pallas_call(
        _gla_kernel,
        grid=(batch, steps),
        in_specs=[
            tok(GLA_DK, P_GQ // GLA_DK),
            tok(GLA_DK, P_GK // GLA_DK),
            tok(GLA_DV, P_GV // GLA_DV),
            tok(GLA_DV, P_GR // GLA_DV),
            tok(PA_PAD, 0),
            const((PA_PAD, GLA_DK)),
            const((1, GLA_DK)),
            const((1, GLA_HV)),
            const((n, n)),
            const((GLA_SUB * GLA_HK, n)),
        ],
        out_specs=tok(GLA_DV, 0),
        out_shape=jax.ShapeDtypeStruct((t, GLA_DV), BF16),
        scratch_shapes=[
            pltpu.VMEM((GLA_HEADS, GLA_HV, GLA_HK), F32),
            pltpu.VMEM((slots, n, GLA_HK), F32),
            pltpu.VMEM((slots, n, GLA_HK), F32),
            pltpu.VMEM((slots, n, GLA_HK), F32),
            pltpu.VMEM((slots * n, GLA_SUB * GLA_HK), BF16),
        ],
        compiler_params=pltpu.CompilerParams(
            dimension_semantics=("parallel", "arbitrary"),
            vmem_limit_bytes=VMEM_LIMIT_BYTES),
        name="gla",
    )(proj2d, proj2d, proj2d, proj2d, pa, u_pad, bias, gn, ltri, e)


FF_CHUNK = 1024


def _out_kernel(x_ref, o0_ref, o1_ref, o2_ref, l0_ref, l1_ref, l2_ref, og_ref, gate_ref, gbias_ref,
                hx_ref, wa_ref, wb_ref, wo_ref, g2_ref, wup_ref, wdn_ref, out_ref):
    l0, l1, l2 = l0_ref[0], l1_ref[0], l2_ref[0]
    mx = jnp.maximum(jnp.maximum(l0, l1), l2)
    e0, e1, e2 = jnp.exp(l0 - mx), jnp.exp(l1 - mx), jnp.exp(l2 - mx)
    inv = 1.0 / (e0 + e1 + e2)
    hx = hx_ref[...]
    def pairs(ref):
        return jnp.concatenate([ref[0, p] for p in range(N_PAIRS)], axis=1)

    def expand(w):
        return jnp.dot(w.astype(BF16), hx, preferred_element_type=F32)

    o_attn = (expand(e0 * inv) * pairs(o0_ref) + expand(e1 * inv) * pairs(o1_ref)
              + expand(e2 * inv) * pairs(o2_ref))
    a = jnp.dot(o_attn.astype(BF16), wa_ref[...], preferred_element_type=F32)
    g = jnp.dot(og_ref[...], wb_ref[...], preferred_element_type=F32)
    gate_a = jax.nn.sigmoid(gate_ref[:, :D_MODEL].astype(F32) + gbias_ref[:, :D_MODEL])
    gate_g = jax.nn.sigmoid(gate_ref[:, D_MODEL:].astype(F32) + gbias_ref[:, D_MODEL:])
    mixed = gate_a * a + gate_g * g
    x1 = x_ref[...] + jnp.dot(mixed.astype(BF16), wo_ref[...], preferred_element_type=F32)

    ms = jnp.mean(x1 * x1, axis=-1, keepdims=True)
    h2 = (x1 * lax.rsqrt(ms + EPS) * g2_ref[...]).astype(BF16)
    acc = x1
    for c in range(D_FF // FF_CHUNK):
        u = jnp.dot(h2, wup_ref[:, c * FF_CHUNK:(c + 1) * FF_CHUNK], preferred_element_type=F32)
        u = jnp.maximum(u, 0.0)
        u = (u * u).astype(BF16)
        acc = acc + jnp.dot(u, wdn_ref[c * FF_CHUNK:(c + 1) * FF_CHUNK, :],
                            preferred_element_type=F32)
    out_ref[...] = acc


def _out_call(x2d, o_groups, lse_groups, o_gla, proj2d, gbias, wa, wb, wo, g2, wup, wdn, batch, tm):
    t = x2d.shape[0]
    lanes = np.arange(LANES)
    cols = np.arange(GROUP_WIDTH)
    head_expand = jnp.asarray(lanes[:, None] == cols[None, :] // HEAD_DIM, BF16)

    tiles_per_seq = t // batch // tm

    def tok(width):
        return pl.BlockSpec((tm, width), lambda i: (i, 0))

    attn_o = pl.BlockSpec((1, N_PAIRS, tm, LANES),
                          lambda i: (i // tiles_per_seq, 0, i % tiles_per_seq, 0))
    attn_lse = pl.BlockSpec((1, tm, LANES), lambda i: (i // tiles_per_seq, i % tiles_per_seq, 0))

    def const(shape):
        return pl.BlockSpec(shape, lambda i: (0, 0), pipeline_mode=pl.Buffered(1))

    return pl.pallas_call(
        _out_kernel,
        grid=(t // tm,),
        in_specs=[
            tok(D_MODEL),
            attn_o, attn_o, attn_o,
            attn_lse, attn_lse, attn_lse,
            tok(GLA_DV),
            pl.BlockSpec((tm, 2 * D_MODEL), lambda i: (i, P_GATE // (2 * D_MODEL))),
            const((1, 2 * D_MODEL)),
            const((LANES, GROUP_WIDTH)),
            const((GROUP_WIDTH, D_MODEL)),
            const((GLA_DV, D_MODEL)),
            const((D_MODEL, D_MODEL)),
            const((1, D_MODEL)),
            const((D_MODEL, D_FF)),
            const((D_FF, D_MODEL)),
        ],
        out_specs=tok(D_MODEL),
        out_shape=jax.ShapeDtypeStruct((t, D_MODEL), F32),
        compiler_params=pltpu.CompilerParams(
            dimension_semantics=("parallel",),
            vmem_limit_bytes=VMEM_LIMIT_BYTES),
        name="out",
    )(x2d, *o_groups, *lse_groups, o_gla, proj2d, gbias, head_expand, wa, wb, wo, g2, wup, wdn)


def _layer(x2d, batch, norm1_g, w_in, gq, gk, gate_up, gate_bias, gla_norm_g, branch_bias,
           w_a, w_b, w_out, norm2_g, w_up, w_down, out_tm=512):
    def cols(off, size):
        return w_in[:, off:off + size].astype(BF16)

    g1 = norm1_g.reshape(1, D_MODEL)
    w_main = jnp.concatenate([
        cols(O_GATE, 2 * D_MODEL), cols(O_GV, GLA_DV), cols(O_GR, GLA_DV),
        cols(O_GQ, GLA_DK), cols(O_GK, GLA_DK)], axis=1)
    w_pa = jnp.pad(cols(O_PA, GLA_RANK), ((0, 0), (0, PA_PAD - GLA_RANK)))
    q_gain = jnp.tile(gq, HEADS_PER_GROUP) * (HEAD_DIM ** -0.5 * LOG2_E)
    k_gain = jnp.tile(gk, HEADS_PER_GROUP)
    qk_gain = jnp.stack([q_gain, k_gain]).reshape(2, 1, GROUP_WIDTH)
    idx = np.arange(MXU_DIM)
    bd = jnp.asarray((idx[:, None] // HEAD_DIM) == (idx[None, :] // HEAD_DIM), BF16)

    proj, pa = _main_proj_call(x2d, g1, w_main, w_pa, tm=1024, rows=512, tn=1024)

    w_qkv = [jnp.concatenate(
        [cols(O_ATTN + (kind * N_GROUPS + g) * GROUP_WIDTH, GROUP_WIDTH) for kind in range(N_KINDS)],
        axis=1) for g in range(N_GROUPS)]
    qkv_groups = _attn_proj_call(x2d, g1, w_qkv, qk_gain, bd, batch, tm=512)

    o_groups, lse_groups = [], []
    for qkv, (_, dilation) in zip(qkv_groups, ATTN_GROUPS):
        o, lse = _attn_call(qkv, dilation)
        o_groups.append(o)
        lse_groups.append(lse)

    u_pad = jnp.pad(gate_up, ((0, PA_PAD - GLA_RANK), (0, 0))).astype(BF16)
    o_gla = _gla_call(proj, pa, u_pad, gate_bias.reshape(1, GLA_DK),
                      gla_norm_g.reshape(1, GLA_HV), batch)

    return _out_call(x2d, o_groups, lse_groups, o_gla, proj,
                     branch_bias.reshape(1, 2 * D_MODEL),
                     w_a.astype(BF16), w_b.astype(BF16), w_out.astype(BF16),
                     norm2_g.reshape(1, D_MODEL), w_up.astype(BF16), w_down.astype(BF16), batch, out_tm)


def kernel(x, norm1_g, w_in, attn_q_norm_g, attn_k_norm_g, gla_gate_up, gla_gate_bias, gla_out_norm_g, branch_gate_bias, w_attn_branch, w_gla_branch, w_out, norm2_g, w_ff_up, w_ff_down):
    b, s, d = x.shape
    x2d = x.reshape(b * s, d)
    for l in range(norm1_g.shape[0]):
        x2d = _layer(x2d, b, norm1_g[l], w_in[l], attn_q_norm_g[l], attn_k_norm_g[l],
                     gla_gate_up[l], gla_gate_bias[l], gla_out_norm_g[l], branch_gate_bias[l],
                     w_attn_branch[l], w_gla_branch[l], w_out[l], norm2_g[l],
                     w_ff_up[l], w_ff_down[l])
    return x2d.reshape(b, s, d)
```

```python
import functools

import numpy as np
import jax
import jax.numpy as jnp
from jax import lax
from jax.experimental import pallas as pl
from jax.experimental.pallas import tpu as pltpu

F32 = jnp.float32
BF16 = jnp.bfloat16

D_MODEL = 1024
ATTN_GROUPS = ((128, 1), (512, 4), (2048, 16))
N_GROUPS = len(ATTN_GROUPS)
HEADS_PER_GROUP = 8
HEAD_DIM = 64
ATTN_BLOCK = 128
GROUP_WIDTH = HEADS_PER_GROUP * HEAD_DIM
ATTN_WIDTH = 3 * N_GROUPS * GROUP_WIDTH
N_PAIRS = GROUP_WIDTH // 128
ATTN_HEADS_PER_DOT = 4
ATTN_UNROLL = 8

GLA_HEADS = 4
GLA_DK = 512
GLA_DV = 1024
GLA_HK = GLA_DK // GLA_HEADS
GLA_HV = GLA_DV // GLA_HEADS
GLA_RANK = 16
GLA_TAU = 16.0
GLA_CHUNK = 64
GLA_SUB = 8
GLA_STEP = 256
GLA_SUBSTEPS = 4

D_FF = 4 * D_MODEL
EPS = 1e-6
LOG2_E = 1.4426950408889634
LN_2 = 0.6931471805599453

LANES = 128
MXU_DIM = 256
VMEM_LIMIT_BYTES = 56 * 1024 * 1024

_ORIG_SIZES = (ATTN_WIDTH, GLA_DK, GLA_DK, GLA_DV, GLA_DV, GLA_RANK, 2 * D_MODEL)
_ORIG_OFF = tuple(int(v) for v in np.cumsum((0,) + _ORIG_SIZES))
O_ATTN, O_GQ, O_GK, O_GV, O_GR, O_PA, O_GATE = _ORIG_OFF[:7]

P_GQ = 0
P_GK = P_GQ + GLA_DK
P_GV = P_GK + GLA_DK
P_GR = P_GV + GLA_DV
P_GATE = P_GR + GLA_DV
P_MAIN = P_GATE + 2 * D_MODEL
GLA_W_BLOCK = (O_GATE - GLA_RANK - O_GQ) // 2
N_KINDS = 3
PA_PAD = LANES


def _rms_norm_rows(x, gain):
    ms = jnp.mean(x * x, axis=-1, keepdims=True)
    return x * lax.rsqrt(ms + EPS) * gain


def _main_proj_kernel(x_ref, g1_ref, wa_ref, wb_ref, wgate_ref, wpa_ref, o_ref, pa_ref, *, rows):
    gain = g1_ref[...]
    for rc in range(x_ref.shape[0] // rows):
        rs = slice(rc * rows, (rc + 1) * rows)
        h = _rms_norm_rows(x_ref[rs, :], gain).astype(BF16)
        pa_ref[rs, :] = jnp.dot(h, wpa_ref[...], preferred_element_type=F32)
        col = 0
        for w_ref in (wa_ref, wb_ref, wgate_ref):
            width = w_ref.shape[1]
            o_ref[rs, col:col + width] = jnp.dot(
                h, w_ref[...], preferred_element_type=F32).astype(BF16)
            col += width


def _main_proj_call(x2d, g1, w_in_bf16, w_gate, w_pa, tm, rows):
    t = x2d.shape[0]
    first_block = O_GQ // GLA_W_BLOCK

    def const(shape, col_block=0):
        return pl.BlockSpec(shape, lambda i: (0, col_block), pipeline_mode=pl.Buffered(1))

    return pl.pallas_call(
        functools.partial(_main_proj_kernel, rows=rows),
        grid=(t // tm,),
        in_specs=[
            pl.BlockSpec((tm, D_MODEL), lambda i: (i, 0)),
            const((1, D_MODEL)),
            const((D_MODEL, GLA_W_BLOCK), first_block),
            const((D_MODEL, GLA_W_BLOCK), first_block + 1),
            const((D_MODEL, 2 * D_MODEL)),
            const((D_MODEL, PA_PAD)),
        ],
        out_specs=[
            pl.BlockSpec((tm, P_MAIN), lambda i: (i, 0)),
            pl.BlockSpec((tm, PA_PAD), lambda i: (i, 0)),
        ],
        out_shape=[
            jax.ShapeDtypeStruct((t, P_MAIN), BF16),
            jax.ShapeDtypeStruct((t, PA_PAD), F32),
        ],
        compiler_params=pltpu.CompilerParams(
            dimension_semantics=("parallel",),
            vmem_limit_bytes=VMEM_LIMIT_BYTES),
        name="proj_main",
    )(x2d, g1, w_in_bf16, w_in_bf16, w_gate, w_pa)


def _qkv_project(h, w_refs, gain_ref, bd, o_ref):
    d = o_ref.shape[1]
    rows = o_ref.shape[2]
    for kind in range(N_KINDS):
        cols = slice(kind * GROUP_WIDTH, (kind + 1) * GROUP_WIDTH)
        acc = jnp.dot(h, w_refs[kind][...], preferred_element_type=F32)
        if kind < 2:
            sq = (acc * acc).astype(BF16)
            ss = jnp.concatenate(
                [jnp.dot(sq[:, c * MXU_DIM:(c + 1) * MXU_DIM], bd, preferred_element_type=F32)
                 for c in range(GROUP_WIDTH // MXU_DIM)], axis=1)
            acc = acc * lax.rsqrt(ss * (1.0 / HEAD_DIM) + EPS) * gain_ref[kind]
        y = acc.astype(BF16)
        for r in range(d):
            o_ref[0, r, :, cols] = y[r * rows:(r + 1) * rows, :]


def _attn_proj_kernel(x_ref, g1_ref, *rest):
    n_w = N_KINDS * N_GROUPS
    w_all = rest[:n_w]
    gain_ref, bd_ref, o0_ref, o1_ref, o2_ref, col_scr, perm1_scr, perm2_scr = rest[n_w:]
    w_group = [[w_all[kind * N_GROUPS + g] for kind in range(N_KINDS)] for g in range(N_GROUPS)]
    tm = x_ref.shape[0]
    n_col = D_MODEL // LANES
    bd = bd_ref[...]
    hf = _rms_norm_rows(x_ref[...], g1_ref[...])
    for c in range(n_col):
        col_scr[c] = hf[:, c * LANES:(c + 1) * LANES]
    _qkv_project(hf.astype(BF16), w_group[0], gain_ref, bd, o0_ref)
    for w_ref, o_ref, perm_scr in ((w_group[1], o1_ref, perm1_scr), (w_group[2], o2_ref, perm2_scr)):
        d = o_ref.shape[1]
        rows = tm // d
        for r in range(d):
            for c in range(n_col):
                perm_scr[r * rows:(r + 1) * rows, c * LANES:(c + 1) * LANES] = (
                    col_scr[c, pl.ds(r, rows, stride=d), :].astype(BF16))
        _qkv_project(perm_scr[...], w_ref, gain_ref, bd, o_ref)


def _attn_proj_call(x2d, g1, w_in_bf16, qk_gain, bd, batch, tm):
    t = x2d.shape[0]
    s = t // batch
    tiles_per_seq = s // tm
    width = N_KINDS * GROUP_WIDTH
    n_w = N_KINDS * N_GROUPS

    def const(shape):
        return pl.BlockSpec(shape, lambda i: (0,) * len(shape), pipeline_mode=pl.Buffered(1))

    def w_block(j):
        return pl.BlockSpec((D_MODEL, GROUP_WIDTH), lambda i: (0, O_ATTN // GROUP_WIDTH + j),
                            pipeline_mode=pl.Buffered(1))

    return pl.pallas_call(
        _attn_proj_kernel,
        grid=(t // tm,),
        in_specs=[
            pl.BlockSpec((tm, D_MODEL), lambda i: (i, 0)),
            const((1, D_MODEL)),
            *[w_block(j) for j in range(n_w)],
            const((2, 1, GROUP_WIDTH)),
            const((MXU_DIM, MXU_DIM)),
        ],
        out_specs=[
            pl.BlockSpec((1, d, tm // d, width),
                         lambda i: (i // tiles_per_seq, 0, i % tiles_per_seq, 0))
            for _, d in ATTN_GROUPS],
        out_shape=[jax.ShapeDtypeStruct((batch, d, s // d, width), BF16) for _, d in ATTN_GROUPS],
        scratch_shapes=[
            pltpu.VMEM((D_MODEL // LANES, tm, LANES), F32),
            pltpu.VMEM((tm, D_MODEL), BF16),
            pltpu.VMEM((tm, D_MODEL), BF16),
        ],
        compiler_params=pltpu.CompilerParams(
            dimension_semantics=("parallel",),
            vmem_limit_bytes=VMEM_LIMIT_BYTES),
        name="proj_attn",
    )(x2d, g1, *([w_in_bf16] * n_w), qk_gain, bd)


def _attn_kernel(q_ref, kp_ref, kc_ref, vp_ref, vc_ref, o_ref, lse_ref):
    n = pl.program_id(1)
    d = q_ref.shape[1]
    blk = ATTN_BLOCK
    n_sub = q_ref.shape[2] // blk
    qi = lax.broadcasted_iota(jnp.int32, (blk, 2 * blk), 0)
    ki = lax.broadcasted_iota(jnp.int32, (blk, 2 * blk), 1)
    band = (ki >= qi) & (ki <= qi + blk)
    band_first = band & ((ki >= blk) | (n > 0))
    lane = lax.broadcasted_iota(jnp.int32, (blk, LANES), 1)
    first_head = lane < HEAD_DIM
    hb = ATTN_HEADS_PER_DOT
    width = hb * HEAD_DIM
    lane_w = lax.broadcasted_iota(jnp.int32, (blk, width), 1)
    head_lanes = [(lane_w >= h * HEAD_DIM) & (lane_w < (h + 1) * HEAD_DIM) for h in range(hb)]
    valid_by_sub = [jnp.concatenate([band_first if u == 0 else band] * hb, axis=0)
                    for u in range(min(n_sub, 2))]

    def unit(r, u):
        rows = slice(u * blk, (u + 1) * blk)
        valid_b = valid_by_sub[min(u, 1)]
        lse_tile = jnp.zeros((blk, LANES), F32)
        for g in range(HEADS_PER_GROUP // hb):
            sl = slice(g * width, (g + 1) * width)
            q = q_ref[0, r, rows, sl]
            if u == 0:
                k_prev, v_prev = kp_ref[0, r, :, sl], vp_ref[0, r, :, sl]
            else:
                prev_rows = slice((u - 1) * blk, u * blk)
                k_prev, v_prev = kc_ref[0, r, prev_rows, sl], vc_ref[0, r, prev_rows, sl]
            k = jnp.concatenate([k_prev, kc_ref[0, r, rows, sl]], axis=0)
            v = jnp.concatenate([v_prev, vc_ref[0, r, rows, sl]], axis=0)
            zero = jnp.zeros_like(q)
            q_rows = jnp.concatenate([jnp.where(head_lanes[h], q, zero) for h in range(hb)],
                                     axis=0)
            s = lax.dot_general(q_rows, k, (((1,), (1,)), ((), ())), preferred_element_type=F32)
            s = jnp.where(valid_b, s, -jnp.inf)
            m = jnp.max(s, axis=-1, keepdims=True)
            p = jnp.exp2(s - m)
            l = jnp.sum(p, axis=-1, keepdims=True)
            pv = jnp.dot(p.astype(BF16), v, preferred_element_type=F32)
            o_all = pv / l
            lse_all = m * LN_2 + jnp.log(l)
            tok = pl.ds(u * blk * d + r, blk, stride=d)
            for h in range(hb):
                head = g * hb + h
                lse_tile = jnp.where(lane == head, lse_all[h * blk:(h + 1) * blk], lse_tile)
            for pp in range(hb // 2):
                cols = slice(pp * LANES, (pp + 1) * LANES)
                lo = o_all[(2 * pp) * blk:(2 * pp + 1) * blk, cols]
                hi = o_all[(2 * pp + 1) * blk:(2 * pp + 2) * blk, cols]
                o_ref[0, g * (hb // 2) + pp, tok, :] = jnp.where(first_head, lo, hi)
        lse_ref[0, tok, :] = lse_tile

    if d * n_sub <= ATTN_UNROLL:
        for r in range(d):
            for u in range(n_sub):
                unit(r, u)
    else:
        def residue(r, carry):
            for u in range(n_sub):
                unit(r, u)
            return carry
        lax.fori_loop(0, d, residue, 0, unroll=ATTN_UNROLL // n_sub)


def _attn_call(qkv, dilation):
    b, d, sub_len, _ = qkv.shape
    assert d == dilation
    blk = ATTN_BLOCK
    n_sub = max(1, ATTN_UNROLL // d)
    rows = n_sub * blk
    steps = sub_len // rows
    s = sub_len * d

    def cur(kind):
        return lambda bi, n: (bi, 0, n, kind)

    def prev(kind):
        return lambda bi, n: (bi, 0, jnp.maximum(n * n_sub - 1, 0), kind)

    cur_shape = (1, d, rows, GROUP_WIDTH)
    prev_shape = (1, d, blk, GROUP_WIDTH)
    o, lse = pl.pallas_call(
        _attn_kernel,
        grid=(b, steps),
        in_specs=[
            pl.BlockSpec(cur_shape, cur(0)),
            pl.BlockSpec(prev_shape, prev(1)),
            pl.BlockSpec(cur_shape, cur(1)),
            pl.BlockSpec(prev_shape, prev(2)),
            pl.BlockSpec(cur_shape, cur(2)),
        ],
        out_specs=[
            pl.BlockSpec((1, N_PAIRS, d * rows, LANES), lambda bi, n: (bi, 0, n, 0)),
            pl.BlockSpec((1, d * rows, LANES), lambda bi, n: (bi, n, 0)),
        ],
        out_shape=[
            jax.ShapeDtypeStruct((b, N_PAIRS, s, LANES), F32),
            jax.ShapeDtypeStruct((b, s, LANES), F32),
        ],
        compiler_params=pltpu.CompilerParams(
            dimension_semantics=("parallel", "arbitrary"),
            vmem_limit_bytes=VMEM_LIMIT_BYTES),
        name=f"attn_d{d}",
    )(qkv, qkv, qkv, qkv, qkv)
    return o, lse


def _log_sigmoid(x):
    return jnp.minimum(x, 0.0) - jnp.log(1.0 + jnp.exp(-jnp.abs(x)))


def _split3(x):
    hi = x.astype(BF16)
    r1 = x - hi.astype(F32)
    mid = r1.astype(BF16)
    lo = (r1 - mid.astype(F32)).astype(BF16)
    return hi, mid, lo


def _gla_pairwise_products(b_scr, qs_scr, kf_scr, pcat_scr):
    sub = GLA_SUB
    tl = lax.broadcasted_iota(jnp.int32, (sub, GLA_HK), 0)
    for pair_i in range(GLA_STEP // (2 * sub)):
        tiles = []
        for r0 in (2 * pair_i * sub, (2 * pair_i + 1) * sub):
            b_blk, q_blk = b_scr[r0:r0 + sub, :], qs_scr[r0:r0 + sub, :]
            row = []
            for s in range(sub):
                d = b_blk - b_scr[r0 + s:r0 + s + 1, :]
                if s > 0:
                    d = jnp.where(tl >= s, d, -jnp.inf)
                row.append(q_blk * kf_scr[r0 + s:r0 + s + 1, :] * jnp.exp2(d))
            tiles.append(row)
        r0 = 2 * pair_i * sub
        for s in range(sub):
            pcat_scr[r0:r0 + 2 * sub, s * GLA_HK:(s + 1) * GLA_HK] = (
                jnp.concatenate([tiles[0][s], tiles[1][s]], axis=0).astype(BF16))


def _gla_reference_factors(b, qs, kf):
    c_len = GLA_CHUNK
    n_chunks = GLA_STEP // c_len
    sub = GLA_SUB

    def z(nrows):
        return jnp.zeros((nrows, GLA_HK), F32)

    qb, kb = [], []
    for r0 in range(0, GLA_STEP, 2 * sub):
        mid = r0 + sub
        b_ref = b[mid - 1:mid]
        qb += [z(sub), qs[mid:mid + sub] * jnp.exp2(b[mid:mid + sub] - b_ref)]
        kb += [kf[r0:mid] * jnp.exp2(b_ref - b[r0:mid]), z(sub)]
    q_block = jnp.concatenate(qb, axis=0).astype(BF16)
    k_block = jnp.concatenate(kb, axis=0).astype(BF16)

    qw, kw = [], []
    for c in range(n_chunks):
        bc, qc, kc = (x[c * c_len:(c + 1) * c_len] for x in (b, qs, kf))
        b15, b31, b47 = bc[15:16], bc[31:32], bc[47:48]
        q1 = jnp.concatenate([z(16), qc[16:32] * jnp.exp2(bc[16:32] - b15), z(32)], axis=0)
        q2 = jnp.concatenate([z(32), qc[32:64] * jnp.exp2(bc[32:64] - b31)], axis=0)
        q3 = jnp.concatenate([z(48), qc[48:64] * jnp.exp2(bc[48:64] - b47)], axis=0)
        k1 = jnp.concatenate([kc[0:16] * jnp.exp2(b15 - bc[0:16]), z(48)], axis=0)
        k2 = jnp.concatenate([kc[0:32] * jnp.exp2(b31 - bc[0:32]), z(32)], axis=0)
        k3 = jnp.concatenate([z(32), kc[32:48] * jnp.exp2(b47 - bc[32:48]), z(16)], axis=0)
        qw.append(jnp.concatenate([q1, q2, q3], axis=1))
        kw.append(jnp.concatenate([k1, k2, k3], axis=1))
    q_within = jnp.concatenate(qw, axis=0).astype(BF16)
    k_within = jnp.concatenate(kw, axis=0).astype(BF16)

    qx, kx = [], []
    for j in range(n_chunks - 1):
        lo, hi = j * c_len, (j + 1) * c_len
        b_ref = b[hi - 1:hi]
        qx.append(jnp.concatenate([z(hi), qs[hi:] * jnp.exp2(b[hi:] - b_ref)], axis=0))
        parts = [kf[lo:hi] * jnp.exp2(b_ref - b[lo:hi])]
        if lo:
            parts.insert(0, z(lo))
        parts.append(z(GLA_STEP - hi))
        kx.append(jnp.concatenate(parts, axis=0))
    q_cross = jnp.concatenate(qx, axis=1).astype(BF16)
    k_cross = jnp.concatenate(kx, axis=1).astype(BF16)
    return (q_block, k_block), (q_within, k_within), (q_cross, k_cross)


def _gla_kernel(q_ref, k_ref, v_ref, r_ref, pa_ref, u_ref, bias_ref, gn_ref, ltri_ref, e_ref,
                o_ref, st_ref, b_scr, qs_scr, kf_scr, pcat_scr):
    @pl.when(pl.program_id(1) == 0)
    def _():
        st_ref[...] = jnp.zeros_like(st_ref)

    n = GLA_STEP

    def substep(ss, carry):
        rows = pl.ds(pl.multiple_of(ss * n, n), n)
        _gla_substep(q_ref.at[rows], k_ref.at[rows], v_ref.at[rows], r_ref.at[rows], pa_ref.at[rows],
                     u_ref, bias_ref, gn_ref, ltri_ref, e_ref, o_ref.at[rows], st_ref,
                     b_scr, qs_scr, kf_scr, pcat_scr)
        return carry

    lax.fori_loop(0, q_ref.shape[0] // n, substep, 0)


def _gla_substep(q_ref, k_ref, v_ref, r_ref, pa_ref, u_ref, bias_ref, gn_ref, ltri_ref, e_ref,
                 o_ref, st_ref, b_scr, qs_scr, kf_scr, pcat_scr):
    n = GLA_STEP
    nt = (((1,), (1,)), ((), ()))

    logits = jnp.dot(pa_ref[...].astype(BF16), u_ref[...], preferred_element_type=F32) + bias_ref[...]
    la = _log_sigmoid(logits) * (1.0 / GLA_TAU)
    parts = jnp.dot(ltri_ref[...], jnp.concatenate(_split3(la), axis=1), preferred_element_type=F32)
    b_all = (parts[:, :GLA_DK] + parts[:, GLA_DK:2 * GLA_DK] + parts[:, 2 * GLA_DK:]) * LOG2_E

    heads = []
    for h in range(GLA_HEADS):
        kcols = slice(h * GLA_HK, (h + 1) * GLA_HK)
        b = b_all[:, kcols]
        qs = q_ref[:, kcols].astype(F32) * (GLA_HK ** -0.5)
        kf = k_ref[:, kcols].astype(F32)
        b_scr[h], qs_scr[h], kf_scr[h] = b, qs, kf
        _gla_pairwise_products(b_scr.at[h], qs_scr.at[h], kf_scr.at[h], pcat_scr.at[h * n:(h + 1) * n])
        heads.append((b, qs, kf))
    a_diag = jnp.dot(pcat_scr[...], e_ref[...], preferred_element_type=F32)

    ri = lax.broadcasted_iota(jnp.int32, (n, n), 0)
    ci = lax.broadcasted_iota(jnp.int32, (n, n), 1)
    same_sub = (ri // GLA_SUB) == (ci // GLA_SUB)
    same_pair = (ri // (2 * GLA_SUB)) == (ci // (2 * GLA_SUB))
    same_chunk = (ri // GLA_CHUNK) == (ci // GLA_CHUNK)
    gn = gn_ref[...]

    for h, (b, qs, kf) in enumerate(heads):
        vcols = slice(h * GLA_HV, (h + 1) * GLA_HV)
        a_block, a_within, a_cross = (
            lax.dot_general(qf, kf_, nt, preferred_element_type=F32)
            for qf, kf_ in _gla_reference_factors(b, qs, kf))
        a = (jnp.where(same_sub, a_diag[h * n:(h + 1) * n], 0.0) + jnp.where(same_pair, a_block, 0.0)
             + jnp.where(same_chunk, a_within, 0.0) + a_cross)
        v = v_ref[:, vcols]
        st = st_ref[h]
        b_last = b[n - 1:n]
        q_in = (qs * jnp.exp2(b)).astype(BF16)
        o = (jnp.dot(a.astype(BF16), v, preferred_element_type=F32)
             + lax.dot_general(q_in, st.astype(BF16), nt, preferred_element_type=F32))
        k_st = (kf * jnp.exp2(b_last - b)).astype(BF16)
        upd = lax.dot_general(v, k_st, (((0,), (0,)), ((), ())), preferred_element_type=F32)
        st_ref[h] = st * jnp.exp2(b_last) + upd

        ms = jnp.mean(o * o, axis=-1, keepdims=True)
        y = o * lax.rsqrt(ms + EPS) * gn
        r = r_ref[:, vcols].astype(F32)
        o_ref[:, vcols] = (y * (r * jax.nn.sigmoid(r))).astype(BF16)


def _gla_constants():
    n = GLA_STEP
    idx = np.arange(n)
    ltri = idx[:, None] >= idx[None, :]
    rows = np.arange(GLA_SUB * GLA_HK)
    cols = np.arange(n)
    e = (rows[:, None] // GLA_HK) == (cols[None, :] % GLA_SUB)
    return jnp.asarray(ltri, BF16), jnp.asarray(e, BF16)


def _gla_call(proj2d, pa, u_pad, bias, gn, batch):
    t = proj2d.shape[0]
    n = GLA_STEP
    rows = n * GLA_SUBSTEPS
    steps = t // batch // rows
    slots = GLA_HEADS
    ltri, e = _gla_constants()

    def tok(width, col_block):
        return pl.BlockSpec((rows, width), lambda bi, i: (bi * steps + i, col_block))

    def const(shape):
        return pl.BlockSpec(shape, lambda bi, i: (0, 0))

    return pl.pallas_call(
        _gla_kernel,
        grid=(batch, steps),
        in_specs=[
            tok(GLA_DK, P_GQ // GLA_DK),
            tok(GLA_DK, P_GK // GLA_DK),
            tok(GLA_DV, P_GV // GLA_DV),
            tok(GLA_DV, P_GR // GLA_DV),
            tok(PA_PAD, 0),
            const((PA_PAD, GLA_DK)),
            const((1, GLA_DK)),
            const((1, GLA_HV)),
            const((n, n)),
            const((GLA_SUB * GLA_HK, n)),
        ],
        out_specs=tok(GLA_DV, 0),
        out_shape=jax.ShapeDtypeStruct((t, GLA_DV), BF16),
        scratch_shapes=[
            pltpu.VMEM((GLA_HEADS, GLA_HV, GLA_HK), F32),
            pltpu.VMEM((slots, n, GLA_HK), F32),
            pltpu.VMEM((slots, n, GLA_HK), F32),
            pltpu.VMEM((slots, n, GLA_HK), F32),
            pltpu.VMEM((slots * n, GLA_SUB * GLA_HK), BF16),
        ],
        compiler_params=pltpu.CompilerParams(
            dimension_semantics=("parallel", "arbitrary"),
            vmem_limit_bytes=VMEM_LIMIT_BYTES),
        name="gla",
    )(proj2d, proj2d, proj2d, proj2d, pa, u_pad, bias, gn, ltri, e)


FF_CHUNK = 1024


def _out_kernel(x_ref, o0_ref, o1_ref, o2_ref, l0_ref, l1_ref, l2_ref, og_ref, gate_a_ref, gate_g_ref, gbias_ref,
                hx_ref, wa_ref, wb_ref, wo_ref, g2_ref, wup_ref, wdn_ref, out_ref):
    l0, l1, l2 = l0_ref[0], l1_ref[0], l2_ref[0]
    mx = jnp.maximum(jnp.maximum(l0, l1), l2)
    e0, e1, e2 = jnp.exp(l0 - mx), jnp.exp(l1 - mx), jnp.exp(l2 - mx)
    inv = 1.0 / (e0 + e1 + e2)
    hx = hx_ref[...]
    def pairs(ref):
        return jnp.concatenate([ref[0, p] for p in range(N_PAIRS)], axis=1)

    def expand(w):
        return jnp.dot(w.astype(BF16), hx, preferred_element_type=F32)

    o_attn = (expand(e0 * inv) * pairs(o0_ref) + expand(e1 * inv) * pairs(o1_ref)
              + expand(e2 * inv) * pairs(o2_ref))
    a = jnp.dot(o_attn.astype(BF16), wa_ref[...], preferred_element_type=F32)
    g = jnp.dot(og_ref[...], wb_ref[...], preferred_element_type=F32)
    gate_a = jax.nn.sigmoid(gate_a_ref[...].astype(F32) + gbias_ref[:, :D_MODEL])
    gate_g = jax.nn.sigmoid(gate_g_ref[...].astype(F32) + gbias_ref[:, D_MODEL:])
    mixed = gate_a * a + gate_g * g
    x1 = x_ref[...] + jnp.dot(mixed.astype(BF16), wo_ref[...], preferred_element_type=F32)

    ms = jnp.mean(x1 * x1, axis=-1, keepdims=True)
    h2 = (x1 * lax.rsqrt(ms + EPS) * g2_ref[...]).astype(BF16)
    acc = x1
    for c in range(D_FF // FF_CHUNK):
        u = jnp.dot(h2, wup_ref[:, c * FF_CHUNK:(c + 1) * FF_CHUNK], preferred_element_type=F32)
        u = jnp.maximum(u, 0.0)
        u = (u * u).astype(BF16)
        acc = acc + jnp.dot(u, wdn_ref[c * FF_CHUNK:(c + 1) * FF_CHUNK, :],
                            preferred_element_type=F32)
    out_ref[...] = acc


def _out_call(x2d, o_groups, lse_groups, o_gla, proj2d, gbias, wa, wb, wo, g2, wup, wdn, batch, tm):
    t = x2d.shape[0]
    lanes = np.arange(LANES)
    cols = np.arange(GROUP_WIDTH)
    head_expand = jnp.asarray(lanes[:, None] == cols[None, :] // HEAD_DIM, BF16)

    tiles_per_seq = t // batch // tm

    def tok(width):
        return pl.BlockSpec((tm, width), lambda i: (i, 0))

    attn_o = pl.BlockSpec((1, N_PAIRS, tm, LANES),
                          lambda i: (i // tiles_per_seq, 0, i % tiles_per_seq, 0))
    attn_lse = pl.BlockSpec((1, tm, LANES), lambda i: (i // tiles_per_seq, i % tiles_per_seq, 0))

    def const(shape):
        return pl.BlockSpec(shape, lambda i: (0, 0), pipeline_mode=pl.Buffered(1))

    return pl.pallas_call(
        _out_kernel,
        grid=(t // tm,),
        in_specs=[
            tok(D_MODEL),
            attn_o, attn_o, attn_o,
            attn_lse, attn_lse, attn_lse,
            tok(GLA_DV),
            pl.BlockSpec((tm, D_MODEL), lambda i: (i, P_GATE // D_MODEL)),
            pl.BlockSpec((tm, D_MODEL), lambda i: (i, P_GATE // D_MODEL + 1)),
            const((1, 2 * D_MODEL)),
            const((LANES, GROUP_WIDTH)),
            const((GROUP_WIDTH, D_MODEL)),
            const((GLA_DV, D_MODEL)),
            const((D_MODEL, D_MODEL)),
            const((1, D_MODEL)),
            const((D_MODEL, D_FF)),
            const((D_FF, D_MODEL)),
        ],
        out_specs=tok(D_MODEL),
        out_shape=jax.ShapeDtypeStruct((t, D_MODEL), F32),
        compiler_params=pltpu.CompilerParams(
            dimension_semantics=("parallel",),
            vmem_limit_bytes=VMEM_LIMIT_BYTES),
        name="out",
    )(x2d, *o_groups, *lse_groups, o_gla, proj2d, proj2d, gbias, head_expand, wa, wb, wo, g2, wup, wdn)


def _layer(x2d, batch, norm1_g, w_in, gq, gk, gate_up, gate_bias, gla_norm_g, branch_bias,
           w_a, w_b, w_out, norm2_g, w_up, w_down, out_tm=512):
    g1 = norm1_g.reshape(1, D_MODEL)
    w_in_bf16 = w_in.astype(BF16)
    w_gate = w_in_bf16[:, O_GATE:O_GATE + 2 * D_MODEL]
    w_pa = jnp.pad(w_in_bf16[:, O_PA:O_PA + GLA_RANK], ((0, 0), (0, PA_PAD - GLA_RANK)))
    q_gain = jnp.tile(gq, HEADS_PER_GROUP) * (HEAD_DIM ** -0.5 * LOG2_E)
    k_gain = jnp.tile(gk, HEADS_PER_GROUP)
    qk_gain = jnp.stack([q_gain, k_gain]).reshape(2, 1, GROUP_WIDTH)
    idx = np.arange(MXU_DIM)
    bd = jnp.asarray((idx[:, None] // HEAD_DIM) == (idx[None, :] // HEAD_DIM), BF16)

    proj, pa = _main_proj_call(x2d, g1, w_in_bf16, w_gate, w_pa, tm=1024, rows=512)
    qkv_groups = _attn_proj_call(x2d, g1, w_in_bf16, qk_gain, bd, batch, tm=512)

    o_groups, lse_groups = [], []
    for qkv, (_, dilation) in zip(qkv_groups, ATTN_GROUPS):
        o, lse = _attn_call(qkv, dilation)
        o_groups.append(o)
        lse_groups.append(lse)

    u_pad = jnp.pad(gate_up, ((0, PA_PAD - GLA_RANK), (0, 0))).astype(BF16)
    o_gla = _gla_call(proj, pa, u_pad, gate_bias.reshape(1, GLA_DK),
                      gla_norm_g.reshape(1, GLA_HV), batch)

    return _out_call(x2d, o_groups, lse_groups, o_gla, proj,
                     branch_bias.reshape(1, 2 * D_MODEL),
                     w_a.astype(BF16), w_b.astype(BF16), w_out.astype(BF16),
                     norm2_g.reshape(1, D_MODEL), w_up.astype(BF16), w_down.astype(BF16), batch, out_tm)


def kernel(x, norm1_g, w_in, attn_q_norm_g, attn_k_norm_g, gla_gate_up, gla_gate_bias, gla_out_norm_g, branch_gate_bias, w_attn_branch, w_gla_branch, w_out, norm2_g, w_ff_up, w_ff_down):
    b, s, d = x.shape
    x2d = x.reshape(b * s, d)
    for l in range(norm1_g.shape[0]):
        x2d = _layer(x2d, b, norm1_g[l], w_in[l], attn_q_norm_g[l], attn_k_norm_g[l],
                     gla_gate_up[l], gla_gate_bias[l], gla_out_norm_g[l], branch_gate_bias[l],
                     w_attn_branch[l], w_gla_branch[l], w_out[l], norm2_g[l],
                     w_ff_up[l], w_ff_down[l])
    return x2d.reshape(b, s, d)
```

```python
import functools

import numpy as np
import jax
import jax.numpy as jnp
from jax import lax
from jax.experimental import pallas as pl
from jax.experimental.pallas import tpu as pltpu

F32 = jnp.float32
BF16 = jnp.bfloat16

D_MODEL = 1024
ATTN_GROUPS = ((128, 1), (512, 4), (2048, 16))
N_GROUPS = len(ATTN_GROUPS)
HEADS_PER_GROUP = 8
HEAD_DIM = 64
ATTN_BLOCK = 128
GROUP_WIDTH = HEADS_PER_GROUP * HEAD_DIM
ATTN_WIDTH = 3 * N_GROUPS * GROUP_WIDTH
N_PAIRS = GROUP_WIDTH // 128
ATTN_HEADS_PER_DOT = 4
ATTN_UNROLL = 8

GLA_HEADS = 4
GLA_DK = 512
GLA_DV = 1024
GLA_HK = GLA_DK // GLA_HEADS
GLA_HV = GLA_DV // GLA_HEADS
GLA_RANK = 16
GLA_TAU = 16.0
GLA_CHUNK = 64
GLA_SUB = 8
GLA_STEP = 256
GLA_SUBSTEPS = 4

D_FF = 4 * D_MODEL
EPS = 1e-6
LOG2_E = 1.4426950408889634
LN_2 = 0.6931471805599453

LANES = 128
MXU_DIM = 256
VMEM_LIMIT_BYTES = 56 * 1024 * 1024

_ORIG_SIZES = (ATTN_WIDTH, GLA_DK, GLA_DK, GLA_DV, GLA_DV, GLA_RANK, 2 * D_MODEL)
_ORIG_OFF = tuple(int(v) for v in np.cumsum((0,) + _ORIG_SIZES))
O_ATTN, O_GQ, O_GK, O_GV, O_GR, O_PA, O_GATE = _ORIG_OFF[:7]

P_GQ = 0
P_GK = P_GQ + GLA_DK
P_GV = P_GK + GLA_DK
P_GR = P_GV + GLA_DV
P_GATE = P_GR + GLA_DV
P_MAIN = P_GATE + 2 * D_MODEL
GLA_W_BLOCK = (O_GATE - GLA_RANK - O_GQ) // 2
N_KINDS = 3
PA_PAD = LANES


def _rms_norm_rows(x, gain):
    ms = jnp.mean(x * x, axis=-1, keepdims=True)
    return x * lax.rsqrt(ms + EPS) * gain


def _main_proj_kernel(x_ref, g1_ref, wa_ref, wb_ref, wgate_ref, wpa_ref, o_ref, pa_ref, *, rows):
    gain = g1_ref[...]
    for rc in range(x_ref.shape[0] // rows):
        rs = slice(rc * rows, (rc + 1) * rows)
        h = _rms_norm_rows(x_ref[rs, :], gain).astype(BF16)
        pa_ref[rs, :] = jnp.dot(h, wpa_ref[...], preferred_element_type=F32)
        col = 0
        for w_ref in (wa_ref, wb_ref, wgate_ref):
            width = w_ref.shape[1]
            o_ref[rs, col:col + width] = jnp.dot(
                h, w_ref[...], preferred_element_type=F32).astype(BF16)
            col += width


def _main_proj_call(x2d, g1, w_in_bf16, w_gate, w_pa, tm, rows):
    t = x2d.shape[0]
    first_block = O_GQ // GLA_W_BLOCK

    def const(shape, col_block=0):
        return pl.BlockSpec(shape, lambda i: (0, col_block), pipeline_mode=pl.Buffered(1))

    return pl.pallas_call(
        functools.partial(_main_proj_kernel, rows=rows),
        grid=(t // tm,),
        in_specs=[
            pl.BlockSpec((tm, D_MODEL), lambda i: (i, 0)),
            const((1, D_MODEL)),
            const((D_MODEL, GLA_W_BLOCK), first_block),
            const((D_MODEL, GLA_W_BLOCK), first_block + 1),
            const((D_MODEL, 2 * D_MODEL)),
            const((D_MODEL, PA_PAD)),
        ],
        out_specs=[
            pl.BlockSpec((tm, P_MAIN), lambda i: (i, 0)),
            pl.BlockSpec((tm, PA_PAD), lambda i: (i, 0)),
        ],
        out_shape=[
            jax.ShapeDtypeStruct((t, P_MAIN), BF16),
            jax.ShapeDtypeStruct((t, PA_PAD), F32),
        ],
        compiler_params=pltpu.CompilerParams(
            dimension_semantics=("parallel",),
            vmem_limit_bytes=VMEM_LIMIT_BYTES),
        name="proj_main",
    )(x2d, g1, w_in_bf16, w_in_bf16, w_gate, w_pa)


def _qkv_project(h, w_refs, gain_ref, bd, o_ref, chunk):
    d = o_ref.shape[1]
    rows = h.shape[0] // d
    for kind in range(N_KINDS):
        cols = slice(kind * GROUP_WIDTH, (kind + 1) * GROUP_WIDTH)
        acc = jnp.dot(h, w_refs[kind][...], preferred_element_type=F32)
        if kind < 2:
            sq = (acc * acc).astype(BF16)
            ss = jnp.concatenate(
                [jnp.dot(sq[:, c * MXU_DIM:(c + 1) * MXU_DIM], bd, preferred_element_type=F32)
                 for c in range(GROUP_WIDTH // MXU_DIM)], axis=1)
            acc = acc * lax.rsqrt(ss * (1.0 / HEAD_DIM) + EPS) * gain_ref[kind]
        y = acc.astype(BF16)
        for r in range(d):
            o_ref[0, r, chunk * rows:(chunk + 1) * rows, cols] = y[r * rows:(r + 1) * rows, :]


def _attn_proj_kernel(x_ref, g1_ref, *rest):
    n_w = N_KINDS * N_GROUPS
    w_all = rest[:n_w]
    gain_ref, bd_ref, o0_ref, o1_ref, o2_ref, col_scr, perm1_scr, perm2_scr = rest[n_w:]
    w_group = [[w_all[kind * N_GROUPS + g] for kind in range(N_KINDS)] for g in range(N_GROUPS)]
    n_chunks, _, tc, _ = col_scr.shape
    n_col = D_MODEL // LANES
    bd = bd_ref[...]
    gain = g1_ref[...]
    for ch in range(n_chunks):
        hf = _rms_norm_rows(x_ref[ch * tc:(ch + 1) * tc, :], gain)
        for c in range(n_col):
            col_scr[ch, c] = hf[:, c * LANES:(c + 1) * LANES]
        _qkv_project(hf.astype(BF16), w_group[0], gain_ref, bd, o0_ref, ch)
        for w_ref, o_ref, perm_scr in ((w_group[1], o1_ref, perm1_scr), (w_group[2], o2_ref, perm2_scr)):
            d = o_ref.shape[1]
            rows = tc // d
            for r in range(d):
                for c in range(n_col):
                    perm_scr[ch, r * rows:(r + 1) * rows, c * LANES:(c + 1) * LANES] = (
                        col_scr[ch, c, pl.ds(r, rows, stride=d), :].astype(BF16))
            _qkv_project(perm_scr[ch], w_ref, gain_ref, bd, o_ref, ch)


def _attn_proj_call(x2d, g1, w_in_bf16, qk_gain, bd, batch, tm, rows):
    t = x2d.shape[0]
    s = t // batch
    tiles_per_seq = s // tm
    width = N_KINDS * GROUP_WIDTH
    n_w = N_KINDS * N_GROUPS

    def const(shape):
        return pl.BlockSpec(shape, lambda i: (0,) * len(shape), pipeline_mode=pl.Buffered(1))

    def w_block(j):
        return pl.BlockSpec((D_MODEL, GROUP_WIDTH), lambda i: (0, O_ATTN // GROUP_WIDTH + j),
                            pipeline_mode=pl.Buffered(1))

    return pl.pallas_call(
        _attn_proj_kernel,
        grid=(t // tm,),
        in_specs=[
            pl.BlockSpec((tm, D_MODEL), lambda i: (i, 0)),
            const((1, D_MODEL)),
            *[w_block(j) for j in range(n_w)],
            const((2, 1, GROUP_WIDTH)),
            const((MXU_DIM, MXU_DIM)),
        ],
        out_specs=[
            pl.BlockSpec((1, d, tm // d, width),
                         lambda i: (i // tiles_per_seq, 0, i % tiles_per_seq, 0))
            for _, d in ATTN_GROUPS],
        out_shape=[jax.ShapeDtypeStruct((batch, d, s // d, width), BF16) for _, d in ATTN_GROUPS],
        scratch_shapes=[
            pltpu.VMEM((tm // rows, D_MODEL // LANES, rows, LANES), F32),
            pltpu.VMEM((tm // rows, rows, D_MODEL), BF16),
            pltpu.VMEM((tm // rows, rows, D_MODEL), BF16),
        ],
        compiler_params=pltpu.CompilerParams(
            dimension_semantics=("parallel",),
            vmem_limit_bytes=VMEM_LIMIT_BYTES),
        name="proj_attn",
    )(x2d, g1, *([w_in_bf16] * n_w), qk_gain, bd)


def _attn_kernel(q_ref, kp_ref, kc_ref, vp_ref, vc_ref, o_ref, lse_ref):
    n = pl.program_id(1)
    d = q_ref.shape[1]
    blk = ATTN_BLOCK
    n_sub = q_ref.shape[2] // blk
    qi = lax.broadcasted_iota(jnp.int32, (blk, 2 * blk), 0)
    ki = lax.broadcasted_iota(jnp.int32, (blk, 2 * blk), 1)
    band = (ki >= qi) & (ki <= qi + blk)
    band_first = band & ((ki >= blk) | (n > 0))
    lane = lax.broadcasted_iota(jnp.int32, (blk, LANES), 1)
    first_head = lane < HEAD_DIM
    hb = ATTN_HEADS_PER_DOT
    width = hb * HEAD_DIM
    lane_w = lax.broadcasted_iota(jnp.int32, (blk, width), 1)
    head_lanes = [(lane_w >= h * HEAD_DIM) & (lane_w < (h + 1) * HEAD_DIM) for h in range(hb)]
    valid_by_sub = [jnp.concatenate([band_first if u == 0 else band] * hb, axis=0)
                    for u in range(min(n_sub, 2))]

    def unit(r, u):
        rows = slice(u * blk, (u + 1) * blk)
        valid_b = valid_by_sub[min(u, 1)]
        lse_tile = jnp.zeros((blk, LANES), F32)
        for g in range(HEADS_PER_GROUP // hb):
            sl = slice(g * width, (g + 1) * width)
            q = q_ref[0, r, rows, sl]
            if u == 0:
                k_prev, v_prev = kp_ref[0, r, :, sl], vp_ref[0, r, :, sl]
            else:
                prev_rows = slice((u - 1) * blk, u * blk)
                k_prev, v_prev = kc_ref[0, r, prev_rows, sl], vc_ref[0, r, prev_rows, sl]
            k = jnp.concatenate([k_prev, kc_ref[0, r, rows, sl]], axis=0)
            v = jnp.concatenate([v_prev, vc_ref[0, r, rows, sl]], axis=0)
            zero = jnp.zeros_like(q)
            q_rows = jnp.concatenate([jnp.where(head_lanes[h], q, zero) for h in range(hb)],
                                     axis=0)
            s = lax.dot_general(q_rows, k, (((1,), (1,)), ((), ())), preferred_element_type=F32)
            s = jnp.where(valid_b, s, -jnp.inf)
            m = jnp.max(s, axis=-1, keepdims=True)
            p = jnp.exp2(s - m)
            l = jnp.sum(p, axis=-1, keepdims=True)
            pv = jnp.dot(p.astype(BF16), v, preferred_element_type=F32)
            o_all = pv / l
            lse_all = m * LN_2 + jnp.log(l)
            tok = pl.ds(u * blk * d + r, blk, stride=d)
            for h in range(hb):
                head = g * hb + h
                lse_tile = jnp.where(lane == head, lse_all[h * blk:(h + 1) * blk], lse_tile)
            for pp in range(hb // 2):
                cols = slice(pp * LANES, (pp + 1) * LANES)
                lo = o_all[(2 * pp) * blk:(2 * pp + 1) * blk, cols]
                hi = o_all[(2 * pp + 1) * blk:(2 * pp + 2) * blk, cols]
                o_ref[0, g * (hb // 2) + pp, tok, :] = jnp.where(first_head, lo, hi)
        lse_ref[0, tok, :] = lse_tile

    if d * n_sub <= ATTN_UNROLL:
        for r in range(d):
            for u in range(n_sub):
                unit(r, u)
    else:
        def residue(r, carry):
            for u in range(n_sub):
                unit(r, u)
            return carry
        lax.fori_loop(0, d, residue, 0, unroll=ATTN_UNROLL // n_sub)


def _attn_call(qkv, dilation):
    b, d, sub_len, _ = qkv.shape
    assert d == dilation
    blk = ATTN_BLOCK
    n_sub = max(1, ATTN_UNROLL // d)
    rows = n_sub * blk
    steps = sub_len // rows
    s = sub_len * d

    def cur(kind):
        return lambda bi, n: (bi, 0, n, kind)

    def prev(kind):
        return lambda bi, n: (bi, 0, jnp.maximum(n * n_sub - 1, 0), kind)

    cur_shape = (1, d, rows, GROUP_WIDTH)
    prev_shape = (1, d, blk, GROUP_WIDTH)
    o, lse = pl.pallas_call(
        _attn_kernel,
        grid=(b, steps),
        in_specs=[
            pl.BlockSpec(cur_shape, cur(0)),
            pl.BlockSpec(prev_shape, prev(1)),
            pl.BlockSpec(cur_shape, cur(1)),
            pl.BlockSpec(prev_shape, prev(2)),
            pl.BlockSpec(cur_shape, cur(2)),
        ],
        out_specs=[
            pl.BlockSpec((1, N_PAIRS, d * rows, LANES), lambda bi, n: (bi, 0, n, 0)),
            pl.BlockSpec((1, d * rows, LANES), lambda bi, n: (bi, n, 0)),
        ],
        out_shape=[
            jax.ShapeDtypeStruct((b, N_PAIRS, s, LANES), F32),
            jax.ShapeDtypeStruct((b, s, LANES), F32),
        ],
        compiler_params=pltpu.CompilerParams(
            dimension_semantics=("parallel", "arbitrary"),
            vmem_limit_bytes=VMEM_LIMIT_BYTES),
        name=f"attn_d{d}",
    )(qkv, qkv, qkv, qkv, qkv)
    return o, lse


def _log_sigmoid(x):
    return jnp.minimum(x, 0.0) - jnp.log(1.0 + jnp.exp(-jnp.abs(x)))


def _split3(x):
    hi = x.astype(BF16)
    r1 = x - hi.astype(F32)
    mid = r1.astype(BF16)
    lo = (r1 - mid.astype(F32)).astype(BF16)
    return hi, mid, lo


def _gla_pairwise_products(b_scr, qs_scr, kf_scr, pcat_scr):
    sub = GLA_SUB
    tl = lax.broadcasted_iota(jnp.int32, (sub, GLA_HK), 0)
    for pair_i in range(GLA_STEP // (2 * sub)):
        tiles = []
        for r0 in (2 * pair_i * sub, (2 * pair_i + 1) * sub):
            b_blk, q_blk = b_scr[r0:r0 + sub, :], qs_scr[r0:r0 + sub, :]
            row = []
            for s in range(sub):
                d = b_blk - b_scr[r0 + s:r0 + s + 1, :]
                if s > 0:
                    d = jnp.where(tl >= s, d, -jnp.inf)
                row.append(q_blk * kf_scr[r0 + s:r0 + s + 1, :] * jnp.exp2(d))
            tiles.append(row)
        r0 = 2 * pair_i * sub
        for s in range(sub):
            pcat_scr[r0:r0 + 2 * sub, s * GLA_HK:(s + 1) * GLA_HK] = (
                jnp.concatenate([tiles[0][s], tiles[1][s]], axis=0).astype(BF16))


def _gla_reference_factors(b, qs, kf):
    c_len = GLA_CHUNK
    n_chunks = GLA_STEP // c_len
    sub = GLA_SUB

    def z(nrows):
        return jnp.zeros((nrows, GLA_HK), F32)

    qb, kb = [], []
    for r0 in range(0, GLA_STEP, 2 * sub):
        mid = r0 + sub
        b_ref = b[mid - 1:mid]
        qb += [z(sub), qs[mid:mid + sub] * jnp.exp2(b[mid:mid + sub] - b_ref)]
        kb += [kf[r0:mid] * jnp.exp2(b_ref - b[r0:mid]), z(sub)]
    q_block = jnp.concatenate(qb, axis=0).astype(BF16)
    k_block = jnp.concatenate(kb, axis=0).astype(BF16)

    qw, kw = [], []
    for c in range(n_chunks):
        bc, qc, kc = (x[c * c_len:(c + 1) * c_len] for x in (b, qs, kf))
        b15, b31, b47 = bc[15:16], bc[31:32], bc[47:48]
        q1 = jnp.concatenate([z(16), qc[16:32] * jnp.exp2(bc[16:32] - b15), z(32)], axis=0)
        q2 = jnp.concatenate([z(32), qc[32:64] * jnp.exp2(bc[32:64] - b31)], axis=0)
        q3 = jnp.concatenate([z(48), qc[48:64] * jnp.exp2(bc[48:64] - b47)], axis=0)
        k1 = jnp.concatenate([kc[0:16] * jnp.exp2(b15 - bc[0:16]), z(48)], axis=0)
        k2 = jnp.concatenate([kc[0:32] * jnp.exp2(b31 - bc[0:32]), z(32)], axis=0)
        k3 = jnp.concatenate([z(32), kc[32:48] * jnp.exp2(b47 - bc[32:48]), z(16)], axis=0)
        qw.append(jnp.concatenate([q1, q2, q3], axis=1))
        kw.append(jnp.concatenate([k1, k2, k3], axis=1))
    q_within = jnp.concatenate(qw, axis=0).astype(BF16)
    k_within = jnp.concatenate(kw, axis=0).astype(BF16)

    qx, kx = [], []
    for j in range(n_chunks - 1):
        lo, hi = j * c_len, (j + 1) * c_len
        b_ref = b[hi - 1:hi]
        qx.append(jnp.concatenate([z(hi), qs[hi:] * jnp.exp2(b[hi:] - b_ref)], axis=0))
        parts = [kf[lo:hi] * jnp.exp2(b_ref - b[lo:hi])]
        if lo:
            parts.insert(0, z(lo))
        parts.append(z(GLA_STEP - hi))
        kx.append(jnp.concatenate(parts, axis=0))
    q_cross = jnp.concatenate(qx, axis=1).astype(BF16)
    k_cross = jnp.concatenate(kx, axis=1).astype(BF16)
    return (q_block, k_block), (q_within, k_within), (q_cross, k_cross)


def _gla_kernel(q_ref, k_ref, v_ref, r_ref, pa_ref, u_ref, bias_ref, gn_ref, ltri_ref, e_ref,
                o_ref, st_ref, b_scr, qs_scr, kf_scr, pcat_scr):
    @pl.when(pl.program_id(1) == 0)
    def _():
        st_ref[...] = jnp.zeros_like(st_ref)

    n = GLA_STEP

    def substep(ss, carry):
        rows = pl.ds(pl.multiple_of(ss * n, n), n)
        _gla_substep(q_ref.at[rows], k_ref.at[rows], v_ref.at[rows], r_ref.at[rows], pa_ref.at[rows],
                     u_ref, bias_ref, gn_ref, ltri_ref, e_ref, o_ref.at[rows], st_ref,
                     b_scr, qs_scr, kf_scr, pcat_scr)
        return carry

    lax.fori_loop(0, q_ref.shape[0] // n, substep, 0)


def _gla_substep(q_ref, k_ref, v_ref, r_ref, pa_ref, u_ref, bias_ref, gn_ref, ltri_ref, e_ref,
                 o_ref, st_ref, b_scr, qs_scr, kf_scr, pcat_scr):
    n = GLA_STEP
    nt = (((1,), (1,)), ((), ()))

    logits = jnp.dot(pa_ref[...].astype(BF16), u_ref[...], preferred_element_type=F32) + bias_ref[...]
    la = _log_sigmoid(logits) * (1.0 / GLA_TAU)
    parts = jnp.dot(ltri_ref[...], jnp.concatenate(_split3(la), axis=1), preferred_element_type=F32)
    b_all = (parts[:, :GLA_DK] + parts[:, GLA_DK:2 * GLA_DK] + parts[:, 2 * GLA_DK:]) * LOG2_E

    heads = []
    for h in range(GLA_HEADS):
        kcols = slice(h * GLA_HK, (h + 1) * GLA_HK)
        b = b_all[:, kcols]
        qs = q_ref[:, kcols].astype(F32) * (GLA_HK ** -0.5)
        kf = k_ref[:, kcols].astype(F32)
        b_scr[h], qs_scr[h], kf_scr[h] = b, qs, kf
        _gla_pairwise_products(b_scr.at[h], qs_scr.at[h], kf_scr.at[h], pcat_scr.at[h * n:(h + 1) * n])
        heads.append((b, qs, kf))
    a_diag = jnp.dot(pcat_scr[...], e_ref[...], preferred_element_type=F32)

    ri = lax.broadcasted_iota(jnp.int32, (n, n), 0)
    ci = lax.broadcasted_iota(jnp.int32, (n, n), 1)
    same_sub = (ri // GLA_SUB) == (ci // GLA_SUB)
    same_pair = (ri // (2 * GLA_SUB)) == (ci // (2 * GLA_SUB))
    same_chunk = (ri // GLA_CHUNK) == (ci // GLA_CHUNK)
    gn = gn_ref[...]

    for h, (b, qs, kf) in enumerate(heads):
        vcols = slice(h * GLA_HV, (h + 1) * GLA_HV)
        a_block, a_within, a_cross = (
            lax.dot_general(qf, kf_, nt, preferred_element_type=F32)
            for qf, kf_ in _gla_reference_factors(b, qs, kf))
        a = (jnp.where(same_sub, a_diag[h * n:(h + 1) * n], 0.0) + jnp.where(same_pair, a_block, 0.0)
             + jnp.where(same_chunk, a_within, 0.0) + a_cross)
        v = v_ref[:, vcols]
        st = st_ref[h]
        b_last = b[n - 1:n]
        q_in = (qs * jnp.exp2(b)).astype(BF16)
        o = (jnp.dot(a.astype(BF16), v, preferred_element_type=F32)
             + lax.dot_general(q_in, st.astype(BF16), nt, preferred_element_type=F32))
        k_st = (kf * jnp.exp2(b_last - b)).astype(BF16)
        upd = lax.dot_general(v, k_st, (((0,), (0,)), ((), ())), preferred_element_type=F32)
        st_ref[h] = st * jnp.exp2(b_last) + upd

        ms = jnp.mean(o * o, axis=-1, keepdims=True)
        y = o * lax.rsqrt(ms + EPS) * gn
        r = r_ref[:, vcols].astype(F32)
        o_ref[:, vcols] = (y * (r * jax.nn.sigmoid(r))).astype(BF16)


def _gla_constants():
    n = GLA_STEP
    idx = np.arange(n)
    ltri = idx[:, None] >= idx[None, :]
    rows = np.arange(GLA_SUB * GLA_HK)
    cols = np.arange(n)
    e = (rows[:, None] // GLA_HK) == (cols[None, :] % GLA_SUB)
    return jnp.asarray(ltri, BF16), jnp.asarray(e, BF16)


def _gla_call(proj2d, pa, u_pad, bias, gn, batch):
    t = proj2d.shape[0]
    n = GLA_STEP
    rows = n * GLA_SUBSTEPS
    steps = t // batch // rows
    slots = GLA_HEADS
    ltri, e = _gla_constants()

    def tok(width, col_block):
        return pl.BlockSpec((rows, width), lambda bi, i: (bi * steps + i, col_block))

    def const(shape):
        return pl.BlockSpec(shape, lambda bi, i: (0, 0))

    return pl.pallas_call(
        _gla_kernel,
        grid=(batch, steps),
        in_specs=[
            tok(GLA_DK, P_GQ // GLA_DK),
            tok(GLA_DK, P_GK // GLA_DK),
            tok(GLA_DV, P_GV // GLA_DV),
            tok(GLA_DV, P_GR // GLA_DV),
            tok(PA_PAD, 0),
            const((PA_PAD, GLA_DK)),
            const((1, GLA_DK)),
            const((1, GLA_HV)),
            const((n, n)),
            const((GLA_SUB * GLA_HK, n)),
        ],
        out_specs=tok(GLA_DV, 0),
        out_shape=jax.ShapeDtypeStruct((t, GLA_DV), BF16),
        scratch_shapes=[
            pltpu.VMEM((GLA_HEADS, GLA_HV, GLA_HK), F32),
            pltpu.VMEM((slots, n, GLA_HK), F32),
            pltpu.VMEM((slots, n, GLA_HK), F32),
            pltpu.VMEM((slots, n, GLA_HK), F32),
            pltpu.VMEM((slots * n, GLA_SUB * GLA_HK), BF16),
        ],
        compiler_params=pltpu.CompilerParams(
            dimension_semantics=("parallel", "arbitrary"),
            vmem_limit_bytes=VMEM_LIMIT_BYTES),
        name="gla",
    )(proj2d, proj2d, proj2d, proj2d, pa, u_pad, bias, gn, ltri, e)


FF_CHUNK = 1024


def _out_kernel(x_ref, o0_ref, o1_ref, o2_ref, l0_ref, l1_ref, l2_ref, og_ref, gate_a_ref, gate_g_ref, gbias_ref,
                hx_ref, wa_ref, wb_ref, wo_ref, g2_ref, wup_ref, wdn_ref, out_ref):
    l0, l1, l2 = l0_ref[0], l1_ref[0], l2_ref[0]
    mx = jnp.maximum(jnp.maximum(l0, l1), l2)
    e0, e1, e2 = jnp.exp(l0 - mx), jnp.exp(l1 - mx), jnp.exp(l2 - mx)
    inv = 1.0 / (e0 + e1 + e2)
    hx = hx_ref[...]
    def pairs(ref):
        return jnp.concatenate([ref[0, p] for p in range(N_PAIRS)], axis=1)

    def expand(w):
        return jnp.dot(w.astype(BF16), hx, preferred_element_type=F32)

    o_attn = (expand(e0 * inv) * pairs(o0_ref) + expand(e1 * inv) * pairs(o1_ref)
              + expand(e2 * inv) * pairs(o2_ref))
    a = jnp.dot(o_attn.astype(BF16), wa_ref[...], preferred_element_type=F32)
    g = jnp.dot(og_ref[...], wb_ref[...], preferred_element_type=F32)
    gate_a = jax.nn.sigmoid(gate_a_ref[...].astype(F32) + gbias_ref[:, :D_MODEL])
    gate_g = jax.nn.sigmoid(gate_g_ref[...].astype(F32) + gbias_ref[:, D_MODEL:])
    mixed = gate_a * a + gate_g * g
    x1 = x_ref[...] + jnp.dot(mixed.astype(BF16), wo_ref[...], preferred_element_type=F32)

    ms = jnp.mean(x1 * x1, axis=-1, keepdims=True)
    h2 = (x1 * lax.rsqrt(ms + EPS) * g2_ref[...]).astype(BF16)
    acc = x1
    for c in range(D_FF // FF_CHUNK):
        u = jnp.dot(h2, wup_ref[:, c * FF_CHUNK:(c + 1) * FF_CHUNK], preferred_element_type=F32)
        u = jnp.maximum(u, 0.0)
        u = (u * u).astype(BF16)
        acc = acc + jnp.dot(u, wdn_ref[c * FF_CHUNK:(c + 1) * FF_CHUNK, :],
                            preferred_element_type=F32)
    out_ref[...] = acc


def _out_call(x2d, o_groups, lse_groups, o_gla, proj2d, gbias, wa, wb, wo, g2, wup, wdn, batch, tm):
    t = x2d.shape[0]
    lanes = np.arange(LANES)
    cols = np.arange(GROUP_WIDTH)
    head_expand = jnp.asarray(lanes[:, None] == cols[None, :] // HEAD_DIM, BF16)

    tiles_per_seq = t // batch // tm

    def tok(width):
        return pl.BlockSpec((tm, width), lambda i: (i, 0))

    attn_o = pl.BlockSpec((1, N_PAIRS, tm, LANES),
                          lambda i: (i // tiles_per_seq, 0, i % tiles_per_seq, 0))
    attn_lse = pl.BlockSpec((1, tm, LANES), lambda i: (i // tiles_per_seq, i % tiles_per_seq, 0))

    def const(shape):
        return pl.BlockSpec(shape, lambda i: (0, 0), pipeline_mode=pl.Buffered(1))

    return pl.pallas_call(
        _out_kernel,
        grid=(t // tm,),
        in_specs=[
            tok(D_MODEL),
            attn_o, attn_o, attn_o,
            attn_lse, attn_lse, attn_lse,
            tok(GLA_DV),
            pl.BlockSpec((tm, D_MODEL), lambda i: (i, P_GATE // D_MODEL)),
            pl.BlockSpec((tm, D_MODEL), lambda i: (i, P_GATE // D_MODEL + 1)),
            const((1, 2 * D_MODEL)),
            const((LANES, GROUP_WIDTH)),
            const((GROUP_WIDTH, D_MODEL)),
            const((GLA_DV, D_MODEL)),
            const((D_MODEL, D_MODEL)),
            const((1, D_MODEL)),
            const((D_MODEL, D_FF)),
            const((D_FF, D_MODEL)),
        ],
        out_specs=tok(D_MODEL),
        out_shape=jax.ShapeDtypeStruct((t, D_MODEL), F32),
        compiler_params=pltpu.CompilerParams(
            dimension_semantics=("parallel",),
            vmem_limit_bytes=VMEM_LIMIT_BYTES),
        name="out",
    )(x2d, *o_groups, *lse_groups, o_gla, proj2d, proj2d, gbias, head_expand, wa, wb, wo, g2, wup, wdn)


def _layer(x2d, batch, norm1_g, w_in, gq, gk, gate_up, gate_bias, gla_norm_g, branch_bias,
           w_a, w_b, w_out, norm2_g, w_up, w_down, out_tm=512):
    g1 = norm1_g.reshape(1, D_MODEL)
    w_in_bf16 = w_in.astype(BF16)
    w_gate = w_in_bf16[:, O_GATE:O_GATE + 2 * D_MODEL]
    w_pa = jnp.pad(w_in_bf16[:, O_PA:O_PA + GLA_RANK], ((0, 0), (0, PA_PAD - GLA_RANK)))
    q_gain = jnp.tile(gq, HEADS_PER_GROUP) * (HEAD_DIM ** -0.5 * LOG2_E)
    k_gain = jnp.tile(gk, HEADS_PER_GROUP)
    qk_gain = jnp.stack([q_gain, k_gain]).reshape(2, 1, GROUP_WIDTH)
    idx = np.arange(MXU_DIM)
    bd = jnp.asarray((idx[:, None] // HEAD_DIM) == (idx[None, :] // HEAD_DIM), BF16)

    proj, pa = _main_proj_call(x2d, g1, w_in_bf16, w_gate, w_pa, tm=1024, rows=512)
    qkv_groups = _attn_proj_call(x2d, g1, w_in_bf16, qk_gain, bd, batch, tm=1024, rows=512)

    o_groups, lse_groups = [], []
    for qkv, (_, dilation) in zip(qkv_groups, ATTN_GROUPS):
        o, lse = _attn_call(qkv, dilation)
        o_groups.append(o)
        lse_groups.append(lse)

    u_pad = jnp.pad(gate_up, ((0, PA_PAD - GLA_RANK), (0, 0))).astype(BF16)
    o_gla = _gla_call(proj, pa, u_pad, gate_bias.reshape(1, GLA_DK),
                      gla_norm_g.reshape(1, GLA_HV), batch)

    return _out_call(x2d, o_groups, lse_groups, o_gla, proj,
                     branch_bias.reshape(1, 2 * D_MODEL),
                     w_a.astype(BF16), w_b.astype(BF16), w_out.astype(BF16),
                     norm2_g.reshape(1, D_MODEL), w_up.astype(BF16), w_down.astype(BF16), batch, out_tm)


def kernel(x, norm1_g, w_in, attn_q_norm_g, attn_k_norm_g, gla_gate_up, gla_gate_bias, gla_out_norm_g, branch_gate_bias, w_attn_branch, w_gla_branch, w_out, norm2_g, w_ff_up, w_ff_down):
    b, s, d = x.shape
    x2d = x.reshape(b * s, d)
    for l in range(norm1_g.shape[0]):
        x2d = _layer(x2d, b, norm1_g[l], w_in[l], attn_q_norm_g[l], attn_k_norm_g[l],
                     gla_gate_up[l], gla_gate_bias[l], gla_out_norm_g[l], branch_gate_bias[l],
                     w_attn_branch[l], w_gla_branch[l], w_out[l], norm2_g[l],
                     w_ff_up[l], w_ff_down[l])
    return x2d.reshape(b, s, d)
```

```python
import functools

import numpy as np
import jax
import jax.numpy as jnp
from jax import lax
from jax.experimental import pallas as pl
from jax.experimental.pallas import tpu as pltpu

F32 = jnp.float32
BF16 = jnp.bfloat16

D_MODEL = 1024
ATTN_GROUPS = ((128, 1), (512, 4), (2048, 16))
N_GROUPS = len(ATTN_GROUPS)
HEADS_PER_GROUP = 8
HEAD_DIM = 64
ATTN_BLOCK = 128
GROUP_WIDTH = HEADS_PER_GROUP * HEAD_DIM
ATTN_WIDTH = 3 * N_GROUPS * GROUP_WIDTH
N_PAIRS = GROUP_WIDTH // 128
ATTN_HEADS_PER_DOT = 4
ATTN_UNROLL = 16

GLA_HEADS = 4
GLA_DK = 512
GLA_DV = 1024
GLA_HK = GLA_DK // GLA_HEADS
GLA_HV = GLA_DV // GLA_HEADS
GLA_RANK = 16
GLA_TAU = 16.0
GLA_CHUNK = 64
GLA_SUB = 8
GLA_STEP = 256
GLA_SUBSTEPS = 4

D_FF = 4 * D_MODEL
EPS = 1e-6
LOG2_E = 1.4426950408889634
LN_2 = 0.6931471805599453

LANES = 128
MXU_DIM = 256
VMEM_LIMIT_BYTES = 56 * 1024 * 1024

_ORIG_SIZES = (ATTN_WIDTH, GLA_DK, GLA_DK, GLA_DV, GLA_DV, GLA_RANK, 2 * D_MODEL)
_ORIG_OFF = tuple(int(v) for v in np.cumsum((0,) + _ORIG_SIZES))
O_ATTN, O_GQ, O_GK, O_GV, O_GR, O_PA, O_GATE = _ORIG_OFF[:7]

P_GQ = 0
P_GK = P_GQ + GLA_DK
P_GV = P_GK + GLA_DK
P_GR = P_GV + GLA_DV
P_GATE = P_GR + GLA_DV
P_MAIN = P_GATE + 2 * D_MODEL
GLA_W_BLOCK = (O_GATE - GLA_RANK - O_GQ) // 2
N_KINDS = 3
PA_PAD = LANES


def _rms_norm_rows(x, gain):
    ms = jnp.mean(x * x, axis=-1, keepdims=True)
    return x * lax.rsqrt(ms + EPS) * gain


def _main_proj_kernel(x_ref, g1_ref, wa_ref, wb_ref, wgate_ref, wpa_ref, o_ref, pa_ref, *, rows):
    gain = g1_ref[...]
    for rc in range(x_ref.shape[0] // rows):
        rs = slice(rc * rows, (rc + 1) * rows)
        h = _rms_norm_rows(x_ref[rs, :], gain).astype(BF16)
        pa_ref[rs, :] = jnp.dot(h, wpa_ref[...], preferred_element_type=F32)
        col = 0
        for w_ref in (wa_ref, wb_ref, wgate_ref):
            width = w_ref.shape[1]
            o_ref[rs, col:col + width] = jnp.dot(
                h, w_ref[...], preferred_element_type=F32).astype(BF16)
            col += width


def _main_proj_call(x2d, g1, w_in_bf16, w_gate, w_pa, tm, rows):
    t = x2d.shape[0]
    first_block = O_GQ // GLA_W_BLOCK

    def const(shape, col_block=0):
        return pl.BlockSpec(shape, lambda i: (0, col_block), pipeline_mode=pl.Buffered(1))

    return pl.pallas_call(
        functools.partial(_main_proj_kernel, rows=rows),
        grid=(t // tm,),
        in_specs=[
            pl.BlockSpec((tm, D_MODEL), lambda i: (i, 0)),
            const((1, D_MODEL)),
            const((D_MODEL, GLA_W_BLOCK), first_block),
            const((D_MODEL, GLA_W_BLOCK), first_block + 1),
            const((D_MODEL, 2 * D_MODEL)),
            const((D_MODEL, PA_PAD)),
        ],
        out_specs=[
            pl.BlockSpec((tm, P_MAIN), lambda i: (i, 0)),
            pl.BlockSpec((tm, PA_PAD), lambda i: (i, 0)),
        ],
        out_shape=[
            jax.ShapeDtypeStruct((t, P_MAIN), BF16),
            jax.ShapeDtypeStruct((t, PA_PAD), F32),
        ],
        compiler_params=pltpu.CompilerParams(
            dimension_semantics=("parallel",),
            vmem_limit_bytes=VMEM_LIMIT_BYTES),
        name="proj_main",
    )(x2d, g1, w_in_bf16, w_in_bf16, w_gate, w_pa)


def _qkv_project(h, w_refs, gain_ref, bd, o_ref, chunk):
    d = o_ref.shape[1]
    rows = h.shape[0] // d
    for kind in range(N_KINDS):
        cols = slice(kind * GROUP_WIDTH, (kind + 1) * GROUP_WIDTH)
        acc = jnp.dot(h, w_refs[kind][...], preferred_element_type=F32)
        if kind < 2:
            sq = (acc * acc).astype(BF16)
            ss = jnp.concatenate(
                [jnp.dot(sq[:, c * MXU_DIM:(c + 1) * MXU_DIM], bd, preferred_element_type=F32)
                 for c in range(GROUP_WIDTH // MXU_DIM)], axis=1)
            acc = acc * lax.rsqrt(ss * (1.0 / HEAD_DIM) + EPS) * gain_ref[kind]
        y = acc.astype(BF16)
        for r in range(d):
            o_ref[0, r, chunk * rows:(chunk + 1) * rows, cols] = y[r * rows:(r + 1) * rows, :]


def _attn_proj_kernel(x_ref, g1_ref, *rest):
    n_w = N_KINDS * N_GROUPS
    w_all = rest[:n_w]
    gain_ref, bd_ref, o0_ref, o1_ref, o2_ref, col_scr, perm1_scr, perm2_scr = rest[n_w:]
    w_group = [[w_all[kind * N_GROUPS + g] for kind in range(N_KINDS)] for g in range(N_GROUPS)]
    n_chunks, _, tc, _ = col_scr.shape
    n_col = D_MODEL // LANES
    bd = bd_ref[...]
    gain = g1_ref[...]
    for ch in range(n_chunks):
        hf = _rms_norm_rows(x_ref[ch * tc:(ch + 1) * tc, :], gain)
        for c in range(n_col):
            col_scr[ch, c] = hf[:, c * LANES:(c + 1) * LANES]
        _qkv_project(hf.astype(BF16), w_group[0], gain_ref, bd, o0_ref, ch)
        for w_ref, o_ref, perm_scr in ((w_group[1], o1_ref, perm1_scr), (w_group[2], o2_ref, perm2_scr)):
            d = o_ref.shape[1]
            rows = tc // d
            for r in range(d):
                for c in range(n_col):
                    perm_scr[ch, r * rows:(r + 1) * rows, c * LANES:(c + 1) * LANES] = (
                        col_scr[ch, c, pl.ds(r, rows, stride=d), :].astype(BF16))
            _qkv_project(perm_scr[ch], w_ref, gain_ref, bd, o_ref, ch)


def _attn_proj_call(x2d, g1, w_in_bf16, qk_gain, bd, batch, tm, rows):
    t = x2d.shape[0]
    s = t // batch
    tiles_per_seq = s // tm
    width = N_KINDS * GROUP_WIDTH
    n_w = N_KINDS * N_GROUPS

    def const(shape):
        return pl.BlockSpec(shape, lambda i: (0,) * len(shape), pipeline_mode=pl.Buffered(1))

    def w_block(j):
        return pl.BlockSpec((D_MODEL, GROUP_WIDTH), lambda i: (0, O_ATTN // GROUP_WIDTH + j),
                            pipeline_mode=pl.Buffered(1))

    return pl.pallas_call(
        _attn_proj_kernel,
        grid=(t // tm,),
        in_specs=[
            pl.BlockSpec((tm, D_MODEL), lambda i: (i, 0)),
            const((1, D_MODEL)),
            *[w_block(j) for j in range(n_w)],
            const((2, 1, GROUP_WIDTH)),
            const((MXU_DIM, MXU_DIM)),
        ],
        out_specs=[
            pl.BlockSpec((1, d, tm // d, width),
                         lambda i: (i // tiles_per_seq, 0, i % tiles_per_seq, 0))
            for _, d in ATTN_GROUPS],
        out_shape=[jax.ShapeDtypeStruct((batch, d, s // d, width), BF16) for _, d in ATTN_GROUPS],
        scratch_shapes=[
            pltpu.VMEM((tm // rows, D_MODEL // LANES, rows, LANES), F32),
            pltpu.VMEM((tm // rows, rows, D_MODEL), BF16),
            pltpu.VMEM((tm // rows, rows, D_MODEL), BF16),
        ],
        compiler_params=pltpu.CompilerParams(
            dimension_semantics=("parallel",),
            vmem_limit_bytes=VMEM_LIMIT_BYTES),
        name="proj_attn",
    )(x2d, g1, *([w_in_bf16] * n_w), qk_gain, bd)


def _attn_kernel(q_ref, kp_ref, kc_ref, vp_ref, vc_ref, o_ref, lse_ref):
    n = pl.program_id(1)
    d = q_ref.shape[1]
    blk = ATTN_BLOCK
    n_sub = q_ref.shape[2] // blk
    qi = lax.broadcasted_iota(jnp.int32, (blk, 2 * blk), 0)
    ki = lax.broadcasted_iota(jnp.int32, (blk, 2 * blk), 1)
    band = (ki >= qi) & (ki <= qi + blk)
    band_first = band & ((ki >= blk) | (n > 0))
    lane = lax.broadcasted_iota(jnp.int32, (blk, LANES), 1)
    first_head = lane < HEAD_DIM
    hb = ATTN_HEADS_PER_DOT
    width = hb * HEAD_DIM
    lane_w = lax.broadcasted_iota(jnp.int32, (blk, width), 1)
    head_lanes = [(lane_w >= h * HEAD_DIM) & (lane_w < (h + 1) * HEAD_DIM) for h in range(hb)]
    valid_by_sub = [jnp.concatenate([band_first if u == 0 else band] * hb, axis=0)
                    for u in range(min(n_sub, 2))]

    def unit(r, u):
        rows = slice(u * blk, (u + 1) * blk)
        valid_b = valid_by_sub[min(u, 1)]
        lse_tile = jnp.zeros((blk, LANES), F32)
        for g in range(HEADS_PER_GROUP // hb):
            sl = slice(g * width, (g + 1) * width)
            q = q_ref[0, r, rows, sl]
            if u == 0:
                k_prev, v_prev = kp_ref[0, r, :, sl], vp_ref[0, r, :, sl]
            else:
                prev_rows = slice((u - 1) * blk, u * blk)
                k_prev, v_prev = kc_ref[0, r, prev_rows, sl], vc_ref[0, r, prev_rows, sl]
            k = jnp.concatenate([k_prev, kc_ref[0, r, rows, sl]], axis=0)
            v = jnp.concatenate([v_prev, vc_ref[0, r, rows, sl]], axis=0)
            zero = jnp.zeros_like(q)
            q_rows = jnp.concatenate([jnp.where(head_lanes[h], q, zero) for h in range(hb)],
                                     axis=0)
            s = lax.dot_general(q_rows, k, (((1,), (1,)), ((), ())), preferred_element_type=F32)
            s = jnp.where(valid_b, s, -jnp.inf)
            m = jnp.max(s, axis=-1, keepdims=True)
            p = jnp.exp2(s - m)
            l = jnp.sum(p, axis=-1, keepdims=True)
            pv = jnp.dot(p.astype(BF16), v, preferred_element_type=F32)
            o_all = pv / l
            lse_all = m * LN_2 + jnp.log(l)
            tok = pl.ds(u * blk * d + r, blk, stride=d)
            for h in range(hb):
                head = g * hb + h
                lse_tile = jnp.where(lane == head, lse_all[h * blk:(h + 1) * blk], lse_tile)
            for pp in range(hb // 2):
                cols = slice(pp * LANES, (pp + 1) * LANES)
                lo = o_all[(2 * pp) * blk:(2 * pp + 1) * blk, cols]
                hi = o_all[(2 * pp + 1) * blk:(2 * pp + 2) * blk, cols]
                o_ref[0, g * (hb // 2) + pp, tok, :] = jnp.where(first_head, lo, hi)
        lse_ref[0, tok, :] = lse_tile

    if d * n_sub <= ATTN_UNROLL:
        for r in range(d):
            for u in range(n_sub):
                unit(r, u)
    else:
        def residue(r, carry):
            for u in range(n_sub):
                unit(r, u)
            return carry
        lax.fori_loop(0, d, residue, 0, unroll=ATTN_UNROLL // n_sub)


def _attn_call(qkv, dilation):
    b, d, sub_len, _ = qkv.shape
    assert d == dilation
    blk = ATTN_BLOCK
    n_sub = max(1, ATTN_UNROLL // d)
    rows = n_sub * blk
    steps = sub_len // rows
    s = sub_len * d

    def cur(kind):
        return lambda bi, n: (bi, 0, n, kind)

    def prev(kind):
        return lambda bi, n: (bi, 0, jnp.maximum(n * n_sub - 1, 0), kind)

    cur_shape = (1, d, rows, GROUP_WIDTH)
    prev_shape = (1, d, blk, GROUP_WIDTH)
    o, lse = pl.pallas_call(
        _attn_kernel,
        grid=(b, steps),
        in_specs=[
            pl.BlockSpec(cur_shape, cur(0)),
            pl.BlockSpec(prev_shape, prev(1)),
            pl.BlockSpec(cur_shape, cur(1)),
            pl.BlockSpec(prev_shape, prev(2)),
            pl.BlockSpec(cur_shape, cur(2)),
        ],
        out_specs=[
            pl.BlockSpec((1, N_PAIRS, d * rows, LANES), lambda bi, n: (bi, 0, n, 0)),
            pl.BlockSpec((1, d * rows, LANES), lambda bi, n: (bi, n, 0)),
        ],
        out_shape=[
            jax.ShapeDtypeStruct((b, N_PAIRS, s, LANES), F32),
            jax.ShapeDtypeStruct((b, s, LANES), F32),
        ],
        compiler_params=pltpu.CompilerParams(
            dimension_semantics=("parallel", "arbitrary"),
            vmem_limit_bytes=VMEM_LIMIT_BYTES),
        name=f"attn_d{d}",
    )(qkv, qkv, qkv, qkv, qkv)
    return o, lse


def _log_sigmoid(x):
    return jnp.minimum(x, 0.0) - jnp.log(1.0 + jnp.exp(-jnp.abs(x)))


def _split3(x):
    hi = x.astype(BF16)
    r1 = x - hi.astype(F32)
    mid = r1.astype(BF16)
    lo = (r1 - mid.astype(F32)).astype(BF16)
    return hi, mid, lo


def _gla_pairwise_products(b_scr, qs_scr, kf_scr, pcat_scr):
    sub = GLA_SUB
    tl = lax.broadcasted_iota(jnp.int32, (sub, GLA_HK), 0)
    for pair_i in range(GLA_STEP // (2 * sub)):
        tiles = []
        for r0 in (2 * pair_i * sub, (2 * pair_i + 1) * sub):
            b_blk, q_blk = b_scr[r0:r0 + sub, :], qs_scr[r0:r0 + sub, :]
            row = []
            for s in range(sub):
                d = b_blk - b_scr[r0 + s:r0 + s + 1, :]
                if s > 0:
                    d = jnp.where(tl >= s, d, -jnp.inf)
                row.append(q_blk * kf_scr[r0 + s:r0 + s + 1, :] * jnp.exp2(d))
            tiles.append(row)
        r0 = 2 * pair_i * sub
        for s in range(sub):
            pcat_scr[r0:r0 + 2 * sub, s * GLA_HK:(s + 1) * GLA_HK] = (
                jnp.concatenate([tiles[0][s], tiles[1][s]], axis=0).astype(BF16))


def _gla_reference_factors(b, qs, kf):
    c_len = GLA_CHUNK
    n_chunks = GLA_STEP // c_len
    sub = GLA_SUB

    def z(nrows):
        return jnp.zeros((nrows, GLA_HK), F32)

    qb, kb = [], []
    for r0 in range(0, GLA_STEP, 2 * sub):
        mid = r0 + sub
        b_ref = b[mid - 1:mid]
        qb += [z(sub), qs[mid:mid + sub] * jnp.exp2(b[mid:mid + sub] - b_ref)]
        kb += [kf[r0:mid] * jnp.exp2(b_ref - b[r0:mid]), z(sub)]
    q_block = jnp.concatenate(qb, axis=0).astype(BF16)
    k_block = jnp.concatenate(kb, axis=0).astype(BF16)

    qw, kw = [], []
    for c in range(n_chunks):
        bc, qc, kc = (x[c * c_len:(c + 1) * c_len] for x in (b, qs, kf))
        b15, b31, b47 = bc[15:16], bc[31:32], bc[47:48]
        q1 = jnp.concatenate([z(16), qc[16:32] * jnp.exp2(bc[16:32] - b15), z(32)], axis=0)
        q2 = jnp.concatenate([z(32), qc[32:64] * jnp.exp2(bc[32:64] - b31)], axis=0)
        q3 = jnp.concatenate([z(48), qc[48:64] * jnp.exp2(bc[48:64] - b47)], axis=0)
        k1 = jnp.concatenate([kc[0:16] * jnp.exp2(b15 - bc[0:16]), z(48)], axis=0)
        k2 = jnp.concatenate([kc[0:32] * jnp.exp2(b31 - bc[0:32]), z(32)], axis=0)
        k3 = jnp.concatenate([z(32), kc[32:48] * jnp.exp2(b47 - bc[32:48]), z(16)], axis=0)
        qw.append(jnp.concatenate([q1, q2, q3], axis=1))
        kw.append(jnp.concatenate([k1, k2, k3], axis=1))
    q_within = jnp.concatenate(qw, axis=0).astype(BF16)
    k_within = jnp.concatenate(kw, axis=0).astype(BF16)

    qx, kx = [], []
    for j in range(n_chunks - 1):
        lo, hi = j * c_len, (j + 1) * c_len
        b_ref = b[hi - 1:hi]
        qx.append(jnp.concatenate([z(hi), qs[hi:] * jnp.exp2(b[hi:] - b_ref)], axis=0))
        parts = [kf[lo:hi] * jnp.exp2(b_ref - b[lo:hi])]
        if lo:
            parts.insert(0, z(lo))
        parts.append(z(GLA_STEP - hi))
        kx.append(jnp.concatenate(parts, axis=0))
    q_cross = jnp.concatenate(qx, axis=1).astype(BF16)
    k_cross = jnp.concatenate(kx, axis=1).astype(BF16)
    return (q_block, k_block), (q_within, k_within), (q_cross, k_cross)


def _gla_kernel(q_ref, k_ref, v_ref, r_ref, pa_ref, u_ref, bias_ref, gn_ref, ltri_ref, e_ref,
                o_ref, st_ref, b_scr, qs_scr, kf_scr, pcat_scr):
    @pl.when(pl.program_id(1) == 0)
    def _():
        st_ref[...] = jnp.zeros_like(st_ref)

    n = GLA_STEP

    def substep(ss, carry):
        rows = pl.ds(pl.multiple_of(ss * n, n), n)
        _gla_substep(q_ref.at[rows], k_ref.at[rows], v_ref.at[rows], r_ref.at[rows], pa_ref.at[rows],
                     u_ref, bias_ref, gn_ref, ltri_ref, e_ref, o_ref.at[rows], st_ref,
                     b_scr, qs_scr, kf_scr, pcat_scr)
        return carry

    lax.fori_loop(0, q_ref.shape[0] // n, substep, 0)


def _gla_substep(q_ref, k_ref, v_ref, r_ref, pa_ref, u_ref, bias_ref, gn_ref, ltri_ref, e_ref,
                 o_ref, st_ref, b_scr, qs_scr, kf_scr, pcat_scr):
    n = GLA_STEP
    nt = (((1,), (1,)), ((), ()))

    logits = jnp.dot(pa_ref[...].astype(BF16), u_ref[...], preferred_element_type=F32) + bias_ref[...]
    la = _log_sigmoid(logits) * (1.0 / GLA_TAU)
    parts = jnp.dot(ltri_ref[...], jnp.concatenate(_split3(la), axis=1), preferred_element_type=F32)
    b_all = (parts[:, :GLA_DK] + parts[:, GLA_DK:2 * GLA_DK] + parts[:, 2 * GLA_DK:]) * LOG2_E

    heads = []
    for h in range(GLA_HEADS):
        kcols = slice(h * GLA_HK, (h + 1) * GLA_HK)
        b = b_all[:, kcols]
        qs = q_ref[:, kcols].astype(F32) * (GLA_HK ** -0.5)
        kf = k_ref[:, kcols].astype(F32)
        b_scr[h], qs_scr[h], kf_scr[h] = b, qs, kf
        _gla_pairwise_products(b_scr.at[h], qs_scr.at[h], kf_scr.at[h], pcat_scr.at[h * n:(h + 1) * n])
        heads.append((b, qs, kf))
    a_diag = jnp.dot(pcat_scr[...], e_ref[...], preferred_element_type=F32)

    ri = lax.broadcasted_iota(jnp.int32, (n, n), 0)
    ci = lax.broadcasted_iota(jnp.int32, (n, n), 1)
    same_sub = (ri // GLA_SUB) == (ci // GLA_SUB)
    same_pair = (ri // (2 * GLA_SUB)) == (ci // (2 * GLA_SUB))
    same_chunk = (ri // GLA_CHUNK) == (ci // GLA_CHUNK)
    gn = gn_ref[...]

    for h, (b, qs, kf) in enumerate(heads):
        vcols = slice(h * GLA_HV, (h + 1) * GLA_HV)
        a_block, a_within, a_cross = (
            lax.dot_general(qf, kf_, nt, preferred_element_type=F32)
            for qf, kf_ in _gla_reference_factors(b, qs, kf))
        a = (jnp.where(same_sub, a_diag[h * n:(h + 1) * n], 0.0) + jnp.where(same_pair, a_block, 0.0)
             + jnp.where(same_chunk, a_within, 0.0) + a_cross)
        v = v_ref[:, vcols]
        st = st_ref[h]
        b_last = b[n - 1:n]
        q_in = (qs * jnp.exp2(b)).astype(BF16)
        o = (jnp.dot(a.astype(BF16), v, preferred_element_type=F32)
             + lax.dot_general(q_in, st.astype(BF16), nt, preferred_element_type=F32))
        k_st = (kf * jnp.exp2(b_last - b)).astype(BF16)
        upd = lax.dot_general(v, k_st, (((0,), (0,)), ((), ())), preferred_element_type=F32)
        st_ref[h] = st * jnp.exp2(b_last) + upd

        ms = jnp.mean(o * o, axis=-1, keepdims=True)
        y = o * lax.rsqrt(ms + EPS) * gn
        r = r_ref[:, vcols].astype(F32)
        o_ref[:, vcols] = (y * (r * jax.nn.sigmoid(r))).astype(BF16)


def _gla_constants():
    n = GLA_STEP
    idx = np.arange(n)
    ltri = idx[:, None] >= idx[None, :]
    rows = np.arange(GLA_SUB * GLA_HK)
    cols = np.arange(n)
    e = (rows[:, None] // GLA_HK) == (cols[None, :] % GLA_SUB)
    return jnp.asarray(ltri, BF16), jnp.asarray(e, BF16)


def _gla_call(proj2d, pa, u_pad, bias, gn, batch):
    t = proj2d.shape[0]
    n = GLA_STEP
    rows = n * GLA_SUBSTEPS
    steps = t // batch // rows
    slots = GLA_HEADS
    ltri, e = _gla_constants()

    def tok(width, col_block):
        return pl.BlockSpec((rows, width), lambda bi, i: (bi * steps + i, col_block))

    def const(shape):
        return pl.BlockSpec(shape, lambda bi, i: (0, 0))

    return pl.pallas_call(
        _gla_kernel,
        grid=(batch, steps),
        in_specs=[
            tok(GLA_DK, P_GQ // GLA_DK),
            tok(GLA_DK, P_GK // GLA_DK),
            tok(GLA_DV, P_GV // GLA_DV),
            tok(GLA_DV, P_GR // GLA_DV),
            tok(PA_PAD, 0),
            const((PA_PAD, GLA_DK)),
            const((1, GLA_DK)),
            const((1, GLA_HV)),
            const((n, n)),
            const((GLA_SUB * GLA_HK, n)),
        ],
        out_specs=tok(GLA_DV, 0),
        out_shape=jax.ShapeDtypeStruct((t, GLA_DV), BF16),
        scratch_shapes=[
            pltpu.VMEM((GLA_HEADS, GLA_HV, GLA_HK), F32),
            pltpu.VMEM((slots, n, GLA_HK), F32),
            pltpu.VMEM((slots, n, GLA_HK), F32),
            pltpu.VMEM((slots, n, GLA_HK), F32),
            pltpu.VMEM((slots * n, GLA_SUB * GLA_HK), BF16),
        ],
        compiler_params=pltpu.CompilerParams(
            dimension_semantics=("parallel", "arbitrary"),
            vmem_limit_bytes=VMEM_LIMIT_BYTES),
        name="gla",
    )(proj2d, proj2d, proj2d, proj2d, pa, u_pad, bias, gn, ltri, e)


FF_CHUNK = 1024


def _out_kernel(x_ref, o0_ref, o1_ref, o2_ref, l0_ref, l1_ref, l2_ref, og_ref, gate_a_ref, gate_g_ref, gbias_ref,
                hx_ref, wa_ref, wb_ref, wo_ref, g2_ref, wup_ref, wdn_ref, out_ref):
    l0, l1, l2 = l0_ref[0], l1_ref[0], l2_ref[0]
    mx = jnp.maximum(jnp.maximum(l0, l1), l2)
    e0, e1, e2 = jnp.exp(l0 - mx), jnp.exp(l1 - mx), jnp.exp(l2 - mx)
    inv = 1.0 / (e0 + e1 + e2)
    hx = hx_ref[...]
    def pairs(ref):
        return jnp.concatenate([ref[0, p] for p in range(N_PAIRS)], axis=1)

    def expand(w):
        return jnp.dot(w.astype(BF16), hx, preferred_element_type=F32)

    o_attn = (expand(e0 * inv) * pairs(o0_ref) + expand(e1 * inv) * pairs(o1_ref)
              + expand(e2 * inv) * pairs(o2_ref))
    a = jnp.dot(o_attn.astype(BF16), wa_ref[...], preferred_element_type=F32)
    g = jnp.dot(og_ref[...], wb_ref[...], preferred_element_type=F32)
    gate_a = jax.nn.sigmoid(gate_a_ref[...].astype(F32) + gbias_ref[:, :D_MODEL])
    gate_g = jax.nn.sigmoid(gate_g_ref[...].astype(F32) + gbias_ref[:, D_MODEL:])
    mixed = gate_a * a + gate_g * g
    x1 = x_ref[...] + jnp.dot(mixed.astype(BF16), wo_ref[...], preferred_element_type=F32)

    ms = jnp.mean(x1 * x1, axis=-1, keepdims=True)
    h2 = (x1 * lax.rsqrt(ms + EPS) * g2_ref[...]).astype(BF16)
    hidden = []
    for c in range(D_FF // FF_CHUNK):
        u = jnp.dot(h2, wup_ref[:, c * FF_CHUNK:(c + 1) * FF_CHUNK], preferred_element_type=F32)
        u = jnp.maximum(u, 0.0)
        hidden.append((u * u).astype(BF16))
    out_ref[...] = x1 + jnp.dot(jnp.concatenate(hidden, axis=1), wdn_ref[...],
                                preferred_element_type=F32)


def _out_call(x2d, o_groups, lse_groups, o_gla, proj2d, gbias, wa, wb, wo, g2, wup, wdn, batch, tm):
    t = x2d.shape[0]
    lanes = np.arange(LANES)
    cols = np.arange(GROUP_WIDTH)
    head_expand = jnp.asarray(lanes[:, None] == cols[None, :] // HEAD_DIM, BF16)

    tiles_per_seq = t // batch // tm

    def tok(width):
        return pl.BlockSpec((tm, width), lambda i: (i, 0))

    attn_o = pl.BlockSpec((1, N_PAIRS, tm, LANES),
                          lambda i: (i // tiles_per_seq, 0, i % tiles_per_seq, 0))
    attn_lse = pl.BlockSpec((1, tm, LANES), lambda i: (i // tiles_per_seq, i % tiles_per_seq, 0))

    def const(shape):
        return pl.BlockSpec(shape, lambda i: (0, 0), pipeline_mode=pl.Buffered(1))

    return pl.pallas_call(
        _out_kernel,
        grid=(t // tm,),
        in_specs=[
            tok(D_MODEL),
            attn_o, attn_o, attn_o,
            attn_lse, attn_lse, attn_lse,
            tok(GLA_DV),
            pl.BlockSpec((tm, D_MODEL), lambda i: (i, P_GATE // D_MODEL)),
            pl.BlockSpec((tm, D_MODEL), lambda i: (i, P_GATE // D_MODEL + 1)),
            const((1, 2 * D_MODEL)),
            const((LANES, GROUP_WIDTH)),
            const((GROUP_WIDTH, D_MODEL)),
            const((GLA_DV, D_MODEL)),
            const((D_MODEL, D_MODEL)),
            const((1, D_MODEL)),
            const((D_MODEL, D_FF)),
            const((D_FF, D_MODEL)),
        ],
        out_specs=tok(D_MODEL),
        out_shape=jax.ShapeDtypeStruct((t, D_MODEL), F32),
        compiler_params=pltpu.CompilerParams(
            dimension_semantics=("parallel",),
            vmem_limit_bytes=VMEM_LIMIT_BYTES),
        name="out",
    )(x2d, *o_groups, *lse_groups, o_gla, proj2d, proj2d, gbias, head_expand, wa, wb, wo, g2, wup, wdn)


def _layer(x2d, batch, norm1_g, w_in, gq, gk, gate_up, gate_bias, gla_norm_g, branch_bias,
           w_a, w_b, w_out, norm2_g, w_up, w_down, out_tm=512):
    g1 = norm1_g.reshape(1, D_MODEL)
    w_in_bf16 = w_in.astype(BF16)
    w_gate = w_in_bf16[:, O_GATE:O_GATE + 2 * D_MODEL]
    w_pa = jnp.pad(w_in_bf16[:, O_PA:O_PA + GLA_RANK], ((0, 0), (0, PA_PAD - GLA_RANK)))
    q_gain = jnp.tile(gq, HEADS_PER_GROUP) * (HEAD_DIM ** -0.5 * LOG2_E)
    k_gain = jnp.tile(gk, HEADS_PER_GROUP)
    qk_gain = jnp.stack([q_gain, k_gain]).reshape(2, 1, GROUP_WIDTH)
    idx = np.arange(MXU_DIM)
    bd = jnp.asarray((idx[:, None] // HEAD_DIM) == (idx[None, :] // HEAD_DIM), BF16)

    proj, pa = _main_proj_call(x2d, g1, w_in_bf16, w_gate, w_pa, tm=1024, rows=512)
    qkv_groups = _attn_proj_call(x2d, g1, w_in_bf16, qk_gain, bd, batch, tm=1024, rows=512)

    o_groups, lse_groups = [], []
    for qkv, (_, dilation) in zip(qkv_groups, ATTN_GROUPS):
        o, lse = _attn_call(qkv, dilation)
        o_groups.append(o)
        lse_groups.append(lse)

    u_pad = jnp.pad(gate_up, ((0, PA_PAD - GLA_RANK), (0, 0))).astype(BF16)
    o_gla = _gla_call(proj, pa, u_pad, gate_bias.reshape(1, GLA_DK),
                      gla_norm_g.reshape(1, GLA_HV), batch)

    return _out_call(x2d, o_groups, lse_groups, o_gla, proj,
                     branch_bias.reshape(1, 2 * D_MODEL),
                     w_a.astype(BF16), w_b.astype(BF16), w_out.astype(BF16),
                     norm2_g.reshape(1, D_MODEL), w_up.astype(BF16), w_down.astype(BF16), batch, out_tm)


def kernel(x, norm1_g, w_in, attn_q_norm_g, attn_k_norm_g, gla_gate_up, gla_gate_bias, gla_out_norm_g, branch_gate_bias, w_attn_branch, w_gla_branch, w_out, norm2_g, w_ff_up, w_ff_down):
    b, s, d = x.shape
    x2d = x.reshape(b * s, d)
    for l in range(norm1_g.shape[0]):
        x2d = _layer(x2d, b, norm1_g[l], w_in[l], attn_q_norm_g[l], attn_k_norm_g[l],
                     gla_gate_up[l], gla_gate_bias[l], gla_out_norm_g[l], branch_gate_bias[l],
                     w_attn_branch[l], w_gla_branch[l], w_out[l], norm2_g[l],
                     w_ff_up[l], w_ff_down[l])
    return x2d.reshape(b, s, d)
```

```python
import functools

import numpy as np
import jax
import jax.numpy as jnp
from jax import lax
from jax.experimental import pallas as pl
from jax.experimental.pallas import tpu as pltpu

F32 = jnp.float32
BF16 = jnp.bfloat16

D_MODEL = 1024
ATTN_GROUPS = ((128, 1), (512, 4), (2048, 16))
N_GROUPS = len(ATTN_GROUPS)
HEADS_PER_GROUP = 8
HEAD_DIM = 64
ATTN_BLOCK = 128
GROUP_WIDTH = HEADS_PER_GROUP * HEAD_DIM
ATTN_WIDTH = 3 * N_GROUPS * GROUP_WIDTH
N_PAIRS = GROUP_WIDTH // 128
ATTN_HEADS_PER_DOT = 4
ATTN_UNROLL = 16

GLA_HEADS = 4
GLA_DK = 512
GLA_DV = 1024
GLA_HK = GLA_DK // GLA_HEADS
GLA_HV = GLA_DV // GLA_HEADS
GLA_RANK = 16
GLA_TAU = 16.0
GLA_CHUNK = 64
GLA_SUB = 8
GLA_STEP = 256
GLA_SUBSTEPS = 4

D_FF = 4 * D_MODEL
EPS = 1e-6
LOG2_E = 1.4426950408889634
LN_2 = 0.6931471805599453

LANES = 128
MXU_DIM = 256
VMEM_LIMIT_BYTES = 56 * 1024 * 1024

_ORIG_SIZES = (ATTN_WIDTH, GLA_DK, GLA_DK, GLA_DV, GLA_DV, GLA_RANK, 2 * D_MODEL)
_ORIG_OFF = tuple(int(v) for v in np.cumsum((0,) + _ORIG_SIZES))
O_ATTN, O_GQ, O_GK, O_GV, O_GR, O_PA, O_GATE = _ORIG_OFF[:7]

P_GQ = 0
P_GK = P_GQ + GLA_DK
P_GV = P_GK + GLA_DK
P_GR = P_GV + GLA_DV
P_GATE = P_GR + GLA_DV
P_MAIN = P_GATE + 2 * D_MODEL
GLA_W_BLOCK = (O_GATE - GLA_RANK - O_GQ) // 2
N_KINDS = 3
PA_PAD = LANES


def _rms_norm_rows(x, gain):
    ms = jnp.mean(x * x, axis=-1, keepdims=True)
    return x * lax.rsqrt(ms + EPS) * gain


def _main_proj_kernel(x_ref, g1_ref, wa_ref, wb_ref, wgate_ref, wpa_ref, o_ref, pa_ref, *, rows):
    gain = g1_ref[...]
    for rc in range(x_ref.shape[0] // rows):
        rs = slice(rc * rows, (rc + 1) * rows)
        h = _rms_norm_rows(x_ref[rs, :], gain).astype(BF16)
        pa_ref[rs, :] = jnp.dot(h, wpa_ref[...], preferred_element_type=F32)
        col = 0
        for w_ref in (wa_ref, wb_ref, wgate_ref):
            width = w_ref.shape[1]
            o_ref[rs, col:col + width] = jnp.dot(
                h, w_ref[...], preferred_element_type=F32).astype(BF16)
            col += width


def _main_proj_call(x2d, g1, w_in_bf16, w_gate, w_pa, tm, rows):
    t = x2d.shape[0]
    first_block = O_GQ // GLA_W_BLOCK

    def const(shape, col_block=0):
        return pl.BlockSpec(shape, lambda i: (0, col_block), pipeline_mode=pl.Buffered(1))

    return pl.pallas_call(
        functools.partial(_main_proj_kernel, rows=rows),
        grid=(t // tm,),
        in_specs=[
            pl.BlockSpec((tm, D_MODEL), lambda i: (i, 0)),
            const((1, D_MODEL)),
            const((D_MODEL, GLA_W_BLOCK), first_block),
            const((D_MODEL, GLA_W_BLOCK), first_block + 1),
            const((D_MODEL, 2 * D_MODEL)),
            const((D_MODEL, PA_PAD)),
        ],
        out_specs=[
            pl.BlockSpec((tm, P_MAIN), lambda i: (i, 0)),
            pl.BlockSpec((tm, PA_PAD), lambda i: (i, 0)),
        ],
        out_shape=[
            jax.ShapeDtypeStruct((t, P_MAIN), BF16),
            jax.ShapeDtypeStruct((t, PA_PAD), F32),
        ],
        compiler_params=pltpu.CompilerParams(
            dimension_semantics=("parallel",),
            vmem_limit_bytes=VMEM_LIMIT_BYTES),
        name="proj_main",
    )(x2d, g1, w_in_bf16, w_in_bf16, w_gate, w_pa)


def _qkv_project(h, w_refs, gain_ref, bd, o_ref, chunk, residue_of_block=None):
    d = o_ref.shape[1]
    rows = h.shape[0] // d
    residue_of_block = residue_of_block or list(range(d))
    for kind in range(N_KINDS):
        cols = slice(kind * GROUP_WIDTH, (kind + 1) * GROUP_WIDTH)
        acc = jnp.dot(h, w_refs[kind][...], preferred_element_type=F32)
        if kind < 2:
            sq = (acc * acc).astype(BF16)
            ss = jnp.concatenate(
                [jnp.dot(sq[:, c * MXU_DIM:(c + 1) * MXU_DIM], bd, preferred_element_type=F32)
                 for c in range(GROUP_WIDTH // MXU_DIM)], axis=1)
            acc = acc * lax.rsqrt(ss * (1.0 / HEAD_DIM) + EPS) * gain_ref[kind]
        y = acc.astype(BF16)
        for j, r in enumerate(residue_of_block):
            o_ref[0, r, chunk * rows:(chunk + 1) * rows, cols] = y[j * rows:(j + 1) * rows, :]


def _attn_proj_kernel(x_ref, g1_ref, *rest):
    n_w = N_KINDS * N_GROUPS
    w_all = rest[:n_w]
    gain_ref, bd_ref, o0_ref, o1_ref, o2_ref, col_scr, col2_scr, perm1_scr, perm2_scr = rest[n_w:]
    w_group = [[w_all[kind * N_GROUPS + g] for kind in range(N_KINDS)] for g in range(N_GROUPS)]
    n_chunks, _, tc, _ = col_scr.shape
    n_col = D_MODEL // LANES
    bd = bd_ref[...]
    gain = g1_ref[...]
    for ch in range(n_chunks):
        hf = _rms_norm_rows(x_ref[ch * tc:(ch + 1) * tc, :], gain)
        for c in range(n_col):
            col_scr[ch, c] = hf[:, c * LANES:(c + 1) * LANES]
        _qkv_project(hf.astype(BF16), w_group[0], gain_ref, bd, o0_ref, ch)
        f = o1_ref.shape[1]
        assert o2_ref.shape[1] == f * f
        rows1, rows2 = tc // f, tc // (f * f)
        for a in range(f):
            for c in range(n_col):
                t = col_scr[ch, c, pl.ds(a, rows1, stride=f), :]
                col2_scr[ch, c, a * rows1:(a + 1) * rows1, :] = t
                perm1_scr[ch, a * rows1:(a + 1) * rows1, c * LANES:(c + 1) * LANES] = t.astype(BF16)
        _qkv_project(perm1_scr[ch], w_group[1], gain_ref, bd, o1_ref, ch)
        for a in range(f):
            for b in range(f):
                blk = a * f + b
                for c in range(n_col):
                    perm2_scr[ch, blk * rows2:(blk + 1) * rows2, c * LANES:(c + 1) * LANES] = (
                        col2_scr[ch, c, pl.ds(a * rows1 + b, rows2, stride=f), :].astype(BF16))
        _qkv_project(perm2_scr[ch], w_group[2], gain_ref, bd, o2_ref, ch,
                     residue_of_block=[b * f + a for a in range(f) for b in range(f)])


def _attn_proj_call(x2d, g1, w_in_bf16, qk_gain, bd, batch, tm, rows):
    t = x2d.shape[0]
    s = t // batch
    tiles_per_seq = s // tm
    width = N_KINDS * GROUP_WIDTH
    n_w = N_KINDS * N_GROUPS

    def const(shape):
        return pl.BlockSpec(shape, lambda i: (0,) * len(shape), pipeline_mode=pl.Buffered(1))

    def w_block(j):
        return pl.BlockSpec((D_MODEL, GROUP_WIDTH), lambda i: (0, O_ATTN // GROUP_WIDTH + j),
                            pipeline_mode=pl.Buffered(1))

    return pl.pallas_call(
        _attn_proj_kernel,
        grid=(t // tm,),
        in_specs=[
            pl.BlockSpec((tm, D_MODEL), lambda i: (i, 0)),
            const((1, D_MODEL)),
            *[w_block(j) for j in range(n_w)],
            const((2, 1, GROUP_WIDTH)),
            const((MXU_DIM, MXU_DIM)),
        ],
        out_specs=[
            pl.BlockSpec((1, d, tm // d, width),
                         lambda i: (i // tiles_per_seq, 0, i % tiles_per_seq, 0))
            for _, d in ATTN_GROUPS],
        out_shape=[jax.ShapeDtypeStruct((batch, d, s // d, width), BF16) for _, d in ATTN_GROUPS],
        scratch_shapes=[
            pltpu.VMEM((tm // rows, D_MODEL // LANES, rows, LANES), F32),
            pltpu.VMEM((tm // rows, D_MODEL // LANES, rows, LANES), F32),
            pltpu.VMEM((tm // rows, rows, D_MODEL), BF16),
            pltpu.VMEM((tm // rows, rows, D_MODEL), BF16),
        ],
        compiler_params=pltpu.CompilerParams(
            dimension_semantics=("parallel",),
            vmem_limit_bytes=VMEM_LIMIT_BYTES),
        name="proj_attn",
    )(x2d, g1, *([w_in_bf16] * n_w), qk_gain, bd)


def _attn_kernel(q_ref, kp_ref, kc_ref, vp_ref, vc_ref, o_ref, lse_ref):
    n = pl.program_id(1)
    d = q_ref.shape[1]
    blk = ATTN_BLOCK
    n_sub = q_ref.shape[2] // blk
    qi = lax.broadcasted_iota(jnp.int32, (blk, 2 * blk), 0)
    ki = lax.broadcasted_iota(jnp.int32, (blk, 2 * blk), 1)
    band = (ki >= qi) & (ki <= qi + blk)
    band_first = band & ((ki >= blk) | (n > 0))
    lane = lax.broadcasted_iota(jnp.int32, (blk, LANES), 1)
    first_head = lane < HEAD_DIM
    hb = ATTN_HEADS_PER_DOT
    width = hb * HEAD_DIM
    lane_w = lax.broadcasted_iota(jnp.int32, (blk, width), 1)
    head_lanes = [(lane_w >= h * HEAD_DIM) & (lane_w < (h + 1) * HEAD_DIM) for h in range(hb)]
    valid_by_sub = [jnp.concatenate([band_first if u == 0 else band] * hb, axis=0)
                    for u in range(min(n_sub, 2))]

    def unit(r, u):
        rows = slice(u * blk, (u + 1) * blk)
        valid_b = valid_by_sub[min(u, 1)]
        lse_tile = jnp.zeros((blk, LANES), F32)
        for g in range(HEADS_PER_GROUP // hb):
            sl = slice(g * width, (g + 1) * width)
            q = q_ref[0, r, rows, sl]
            if u == 0:
                k_prev, v_prev = kp_ref[0, r, :, sl], vp_ref[0, r, :, sl]
            else:
                prev_rows = slice((u - 1) * blk, u * blk)
                k_prev, v_prev = kc_ref[0, r, prev_rows, sl], vc_ref[0, r, prev_rows, sl]
            k = jnp.concatenate([k_prev, kc_ref[0, r, rows, sl]], axis=0)
            v = jnp.concatenate([v_prev, vc_ref[0, r, rows, sl]], axis=0)
            zero = jnp.zeros_like(q)
            q_rows = jnp.concatenate([jnp.where(head_lanes[h], q, zero) for h in range(hb)],
                                     axis=0)
            s = lax.dot_general(q_rows, k, (((1,), (1,)), ((), ())), preferred_element_type=F32)
            s = jnp.where(valid_b, s, -jnp.inf)
            m = jnp.max(s, axis=-1, keepdims=True)
            p = jnp.exp2(s - m)
            l = jnp.sum(p, axis=-1, keepdims=True)
            pv = jnp.dot(p.astype(BF16), v, preferred_element_type=F32)
            o_all = pv / l
            lse_all = m * LN_2 + jnp.log(l)
            tok = pl.ds(u * blk * d + r, blk, stride=d)
            for h in range(hb):
                head = g * hb + h
                lse_tile = jnp.where(lane == head, lse_all[h * blk:(h + 1) * blk], lse_tile)
            for pp in range(hb // 2):
                cols = slice(pp * LANES, (pp + 1) * LANES)
                lo = o_all[(2 * pp) * blk:(2 * pp + 1) * blk, cols]
                hi = o_all[(2 * pp + 1) * blk:(2 * pp + 2) * blk, cols]
                o_ref[0, g * (hb // 2) + pp, tok, :] = jnp.where(first_head, lo, hi)
        lse_ref[0, tok, :] = lse_tile

    if d * n_sub <= ATTN_UNROLL:
        for r in range(d):
            for u in range(n_sub):
                unit(r, u)
    else:
        def residue(r, carry):
            for u in range(n_sub):
                unit(r, u)
            return carry
        lax.fori_loop(0, d, residue, 0, unroll=ATTN_UNROLL // n_sub)


def _attn_call(qkv, dilation):
    b, d, sub_len, _ = qkv.shape
    assert d == dilation
    blk = ATTN_BLOCK
    n_sub = max(1, ATTN_UNROLL // d)
    rows = n_sub * blk
    steps = sub_len // rows
    s = sub_len * d

    def cur(kind):
        return lambda bi, n: (bi, 0, n, kind)

    def prev(kind):
        return lambda bi, n: (bi, 0, jnp.maximum(n * n_sub - 1, 0), kind)

    cur_shape = (1, d, rows, GROUP_WIDTH)
    prev_shape = (1, d, blk, GROUP_WIDTH)
    o, lse = pl.pallas_call(
        _attn_kernel,
        grid=(b, steps),
        in_specs=[
            pl.BlockSpec(cur_shape, cur(0)),
            pl.BlockSpec(prev_shape, prev(1)),
            pl.BlockSpec(cur_shape, cur(1)),
            pl.BlockSpec(prev_shape, prev(2)),
            pl.BlockSpec(cur_shape, cur(2)),
        ],
        out_specs=[
            pl.BlockSpec((1, N_PAIRS, d * rows, LANES), lambda bi, n: (bi, 0, n, 0)),
            pl.BlockSpec((1, d * rows, LANES), lambda bi, n: (bi, n, 0)),
        ],
        out_shape=[
            jax.ShapeDtypeStruct((b, N_PAIRS, s, LANES), F32),
            jax.ShapeDtypeStruct((b, s, LANES), F32),
        ],
        compiler_params=pltpu.CompilerParams(
            dimension_semantics=("parallel", "arbitrary"),
            vmem_limit_bytes=VMEM_LIMIT_BYTES),
        name=f"attn_d{d}",
    )(qkv, qkv, qkv, qkv, qkv)
    return o, lse


def _log_sigmoid(x):
    return jnp.minimum(x, 0.0) - jnp.log(1.0 + jnp.exp(-jnp.abs(x)))


def _split3(x):
    hi = x.astype(BF16)
    r1 = x - hi.astype(F32)
    mid = r1.astype(BF16)
    lo = (r1 - mid.astype(F32)).astype(BF16)
    return hi, mid, lo


def _gla_pairwise_products(b_scr, qs_scr, kf_scr, pcat_scr):
    sub = GLA_SUB
    tl = lax.broadcasted_iota(jnp.int32, (sub, GLA_HK), 0)
    for pair_i in range(GLA_STEP // (2 * sub)):
        tiles = []
        for r0 in (2 * pair_i * sub, (2 * pair_i + 1) * sub):
            b_blk, q_blk = b_scr[r0:r0 + sub, :], qs_scr[r0:r0 + sub, :]
            row = []
            for s in range(sub):
                d = b_blk - b_scr[r0 + s:r0 + s + 1, :]
                if s > 0:
                    d = jnp.where(tl >= s, d, -jnp.inf)
                row.append(q_blk * kf_scr[r0 + s:r0 + s + 1, :] * jnp.exp2(d))
            tiles.append(row)
        r0 = 2 * pair_i * sub
        for s in range(sub):
            pcat_scr[r0:r0 + 2 * sub, s * GLA_HK:(s + 1) * GLA_HK] = (
                jnp.concatenate([tiles[0][s], tiles[1][s]], axis=0).astype(BF16))


def _gla_reference_factors(b, qs, kf):
    c_len = GLA_CHUNK
    n_chunks = GLA_STEP // c_len
    sub = GLA_SUB

    def z(nrows):
        return jnp.zeros((nrows, GLA_HK), F32)

    qb, kb = [], []
    for r0 in range(0, GLA_STEP, 2 * sub):
        mid = r0 + sub
        b_ref = b[mid - 1:mid]
        qb += [z(sub), qs[mid:mid + sub] * jnp.exp2(b[mid:mid + sub] - b_ref)]
        kb += [kf[r0:mid] * jnp.exp2(b_ref - b[r0:mid]), z(sub)]
    q_block = jnp.concatenate(qb, axis=0).astype(BF16)
    k_block = jnp.concatenate(kb, axis=0).astype(BF16)

    qw, kw = [], []
    for c in range(n_chunks):
        bc, qc, kc = (x[c * c_len:(c + 1) * c_len] for x in (b, qs, kf))
        b15, b31, b47 = bc[15:16], bc[31:32], bc[47:48]
        q1 = jnp.concatenate([z(16), qc[16:32] * jnp.exp2(bc[16:32] - b15), z(32)], axis=0)
        q2 = jnp.concatenate([z(32), qc[32:64] * jnp.exp2(bc[32:64] - b31)], axis=0)
        q3 = jnp.concatenate([z(48), qc[48:64] * jnp.exp2(bc[48:64] - b47)], axis=0)
        k1 = jnp.concatenate([kc[0:16] * jnp.exp2(b15 - bc[0:16]), z(48)], axis=0)
        k2 = jnp.concatenate([kc[0:32] * jnp.exp2(b31 - bc[0:32]), z(32)], axis=0)
        k3 = jnp.concatenate([z(32), kc[32:48] * jnp.exp2(b47 - bc[32:48]), z(16)], axis=0)
        qw.append(jnp.concatenate([q1, q2, q3], axis=1))
        kw.append(jnp.concatenate([k1, k2, k3], axis=1))
    q_within = jnp.concatenate(qw, axis=0).astype(BF16)
    k_within = jnp.concatenate(kw, axis=0).astype(BF16)

    qx, kx = [], []
    for j in range(n_chunks - 1):
        lo, hi = j * c_len, (j + 1) * c_len
        b_ref = b[hi - 1:hi]
        qx.append(jnp.concatenate([z(hi), qs[hi:] * jnp.exp2(b[hi:] - b_ref)], axis=0))
        parts = [kf[lo:hi] * jnp.exp2(b_ref - b[lo:hi])]
        if lo:
            parts.insert(0, z(lo))
        parts.append(z(GLA_STEP - hi))
        kx.append(jnp.concatenate(parts, axis=0))
    q_cross = jnp.concatenate(qx, axis=1).astype(BF16)
    k_cross = jnp.concatenate(kx, axis=1).astype(BF16)
    return (q_block, k_block), (q_within, k_within), (q_cross, k_cross)


def _gla_kernel(q_ref, k_ref, v_ref, r_ref, pa_ref, u_ref, bias_ref, gn_ref, ltri_ref, e_ref,
                o_ref, st_ref, b_scr, qs_scr, kf_scr, pcat_scr):
    @pl.when(pl.program_id(1) == 0)
    def _():
        st_ref[...] = jnp.zeros_like(st_ref)

    n = GLA_STEP

    def substep(ss, carry):
        rows = pl.ds(pl.multiple_of(ss * n, n), n)
        _gla_substep(q_ref.at[rows], k_ref.at[rows], v_ref.at[rows], r_ref.at[rows], pa_ref.at[rows],
                     u_ref, bias_ref, gn_ref, ltri_ref, e_ref, o_ref.at[rows], st_ref,
                     b_scr, qs_scr, kf_scr, pcat_scr)
        return carry

    lax.fori_loop(0, q_ref.shape[0] // n, substep, 0)


def _gla_substep(q_ref, k_ref, v_ref, r_ref, pa_ref, u_ref, bias_ref, gn_ref, ltri_ref, e_ref,
                 o_ref, st_ref, b_scr, qs_scr, kf_scr, pcat_scr):
    n = GLA_STEP
    nt = (((1,), (1,)), ((), ()))

    logits = jnp.dot(pa_ref[...].astype(BF16), u_ref[...], preferred_element_type=F32) + bias_ref[...]
    la = _log_sigmoid(logits) * (1.0 / GLA_TAU)
    parts = jnp.dot(ltri_ref[...], jnp.concatenate(_split3(la), axis=1), preferred_element_type=F32)
    b_all = (parts[:, :GLA_DK] + parts[:, GLA_DK:2 * GLA_DK] + parts[:, 2 * GLA_DK:]) * LOG2_E

    heads = []
    for h in range(GLA_HEADS):
        kcols = slice(h * GLA_HK, (h + 1) * GLA_HK)
        b = b_all[:, kcols]
        qs = q_ref[:, kcols].astype(F32) * (GLA_HK ** -0.5)
        kf = k_ref[:, kcols].astype(F32)
        b_scr[h], qs_scr[h], kf_scr[h] = b, qs, kf
        _gla_pairwise_products(b_scr.at[h], qs_scr.at[h], kf_scr.at[h], pcat_scr.at[h * n:(h + 1) * n])
        heads.append((b, qs, kf))
    a_diag = jnp.dot(pcat_scr[...], e_ref[...], preferred_element_type=F32)

    ri = lax.broadcasted_iota(jnp.int32, (n, n), 0)
    ci = lax.broadcasted_iota(jnp.int32, (n, n), 1)
    same_sub = (ri // GLA_SUB) == (ci // GLA_SUB)
    same_pair = (ri // (2 * GLA_SUB)) == (ci // (2 * GLA_SUB))
    same_chunk = (ri // GLA_CHUNK) == (ci // GLA_CHUNK)
    gn = gn_ref[...]

    for h, (b, qs, kf) in enumerate(heads):
        vcols = slice(h * GLA_HV, (h + 1) * GLA_HV)
        a_block, a_within, a_cross = (
            lax.dot_general(qf, kf_, nt, preferred_element_type=F32)
            for qf, kf_ in _gla_reference_factors(b, qs, kf))
        a = (jnp.where(same_sub, a_diag[h * n:(h + 1) * n], 0.0) + jnp.where(same_pair, a_block, 0.0)
             + jnp.where(same_chunk, a_within, 0.0) + a_cross)
        v = v_ref[:, vcols]
        st = st_ref[h]
        b_last = b[n - 1:n]
        q_in = (qs * jnp.exp2(b)).astype(BF16)
        o = (jnp.dot(a.astype(BF16), v, preferred_element_type=F32)
             + lax.dot_general(q_in, st.astype(BF16), nt, preferred_element_type=F32))
        k_st = (kf * jnp.exp2(b_last - b)).astype(BF16)
        upd = lax.dot_general(v, k_st, (((0,), (0,)), ((), ())), preferred_element_type=F32)
        st_ref[h] = st * jnp.exp2(b_last) + upd

        ms = jnp.mean(o * o, axis=-1, keepdims=True)
        y = o * lax.rsqrt(ms + EPS) * gn
        r = r_ref[:, vcols].astype(F32)
        o_ref[:, vcols] = (y * (r * jax.nn.sigmoid(r))).astype(BF16)


def _gla_constants():
    n = GLA_STEP
    idx = np.arange(n)
    ltri = idx[:, None] >= idx[None, :]
    rows = np.arange(GLA_SUB * GLA_HK)
    cols = np.arange(n)
    e = (rows[:, None] // GLA_HK) == (cols[None, :] % GLA_SUB)
    return jnp.asarray(ltri, BF16), jnp.asarray(e, BF16)


def _gla_call(proj2d, pa, u_pad, bias, gn, batch):
    t = proj2d.shape[0]
    n = GLA_STEP
    rows = n * GLA_SUBSTEPS
    steps = t // batch // rows
    slots = GLA_HEADS
    ltri, e = _gla_constants()

    def tok(width, col_block):
        return pl.BlockSpec((rows, width), lambda bi, i: (bi * steps + i, col_block))

    def const(shape):
        return pl.BlockSpec(shape, lambda bi, i: (0, 0))

    return pl.pallas_call(
        _gla_kernel,
        grid=(batch, steps),
        in_specs=[
            tok(GLA_DK, P_GQ // GLA_DK),
            tok(GLA_DK, P_GK // GLA_DK),
            tok(GLA_DV, P_GV // GLA_DV),
            tok(GLA_DV, P_GR // GLA_DV),
            tok(PA_PAD, 0),
            const((PA_PAD, GLA_DK)),
            const((1, GLA_DK)),
            const((1, GLA_HV)),
            const((n, n)),
            const((GLA_SUB * GLA_HK, n)),
        ],
        out_specs=tok(GLA_DV, 0),
        out_shape=jax.ShapeDtypeStruct((t, GLA_DV), BF16),
        scratch_shapes=[
            pltpu.VMEM((GLA_HEADS, GLA_HV, GLA_HK), F32),
            pltpu.VMEM((slots, n, GLA_HK), F32),
            pltpu.VMEM((slots, n, GLA_HK), F32),
            pltpu.VMEM((slots, n, GLA_HK), F32),
            pltpu.VMEM((slots * n, GLA_SUB * GLA_HK), BF16),
        ],
        compiler_params=pltpu.CompilerParams(
            dimension_semantics=("parallel", "arbitrary"),
            vmem_limit_bytes=VMEM_LIMIT_BYTES),
        name="gla",
    )(proj2d, proj2d, proj2d, proj2d, pa, u_pad, bias, gn, ltri, e)


FF_CHUNK = 1024


def _out_kernel(x_ref, o0_ref, o1_ref, o2_ref, l0_ref, l1_ref, l2_ref, og_ref, gate_a_ref, gate_g_ref, gbias_ref,
                hx_ref, wa_ref, wb_ref, wo_ref, g2_ref, wup_ref, wdn_ref, out_ref):
    l0, l1, l2 = l0_ref[0], l1_ref[0], l2_ref[0]
    mx = jnp.maximum(jnp.maximum(l0, l1), l2)
    e0, e1, e2 = jnp.exp(l0 - mx), jnp.exp(l1 - mx), jnp.exp(l2 - mx)
    inv = 1.0 / (e0 + e1 + e2)
    hx = hx_ref[...]
    def pairs(ref):
        return jnp.concatenate([ref[0, p] for p in range(N_PAIRS)], axis=1)

    def expand(w):
        return jnp.dot(w.astype(BF16), hx, preferred_element_type=F32)

    o_attn = (expand(e0 * inv) * pairs(o0_ref) + expand(e1 * inv) * pairs(o1_ref)
              + expand(e2 * inv) * pairs(o2_ref))
    a = jnp.dot(o_attn.astype(BF16), wa_ref[...], preferred_element_type=F32)
    g = jnp.dot(og_ref[...], wb_ref[...], preferred_element_type=F32)
    gate_a = jax.nn.sigmoid(gate_a_ref[...].astype(F32) + gbias_ref[:, :D_MODEL])
    gate_g = jax.nn.sigmoid(gate_g_ref[...].astype(F32) + gbias_ref[:, D_MODEL:])
    mixed = gate_a * a + gate_g * g
    x1 = x_ref[...] + jnp.dot(mixed.astype(BF16), wo_ref[...], preferred_element_type=F32)

    ms = jnp.mean(x1 * x1, axis=-1, keepdims=True)
    h2 = (x1 * lax.rsqrt(ms + EPS) * g2_ref[...]).astype(BF16)
    hidden = []
    for c in range(D_FF // FF_CHUNK):
        u = jnp.dot(h2, wup_ref[:, c * FF_CHUNK:(c + 1) * FF_CHUNK], preferred_element_type=F32)
        u = jnp.maximum(u, 0.0)
        hidden.append((u * u).astype(BF16))
    out_ref[...] = x1 + jnp.dot(jnp.concatenate(hidden, axis=1), wdn_ref[...],
                                preferred_element_type=F32)


def _out_call(x2d, o_groups, lse_groups, o_gla, proj2d, gbias, wa, wb, wo, g2, wup, wdn, batch, tm):
    t = x2d.shape[0]
    lanes = np.arange(LANES)
    cols = np.arange(GROUP_WIDTH)
    head_expand = jnp.asarray(lanes[:, None] == cols[None, :] // HEAD_DIM, BF16)

    tiles_per_seq = t // batch // tm

    def tok(width):
        return pl.BlockSpec((tm, width), lambda i: (i, 0))

    attn_o = pl.BlockSpec((1, N_PAIRS, tm, LANES),
                          lambda i: (i // tiles_per_seq, 0, i % tiles_per_seq, 0))
    attn_lse = pl.BlockSpec((1, tm, LANES), lambda i: (i // tiles_per_seq, i % tiles_per_seq, 0))

    def const(shape):
        return pl.BlockSpec(shape, lambda i: (0, 0), pipeline_mode=pl.Buffered(1))

    return pl.pallas_call(
        _out_kernel,
        grid=(t // tm,),
        in_specs=[
            tok(D_MODEL),
            attn_o, attn_o, attn_o,
            attn_lse, attn_lse, attn_lse,
            tok(GLA_DV),
            pl.BlockSpec((tm, D_MODEL), lambda i: (i, P_GATE // D_MODEL)),
            pl.BlockSpec((tm, D_MODEL), lambda i: (i, P_GATE // D_MODEL + 1)),
            const((1, 2 * D_MODEL)),
            const((LANES, GROUP_WIDTH)),
            const((GROUP_WIDTH, D_MODEL)),
            const((GLA_DV, D_MODEL)),
            const((D_MODEL, D_MODEL)),
            const((1, D_MODEL)),
            const((D_MODEL, D_FF)),
            const((D_FF, D_MODEL)),
        ],
        out_specs=tok(D_MODEL),
        out_shape=jax.ShapeDtypeStruct((t, D_MODEL), F32),
        compiler_params=pltpu.CompilerParams(
            dimension_semantics=("parallel",),
            vmem_limit_bytes=VMEM_LIMIT_BYTES),
        name="out",
    )(x2d, *o_groups, *lse_groups, o_gla, proj2d, proj2d, gbias, head_expand, wa, wb, wo, g2, wup, wdn)


def _layer(x2d, batch, norm1_g, w_in, gq, gk, gate_up, gate_bias, gla_norm_g, branch_bias,
           w_a, w_b, w_out, norm2_g, w_up, w_down, out_tm=512):
    g1 = norm1_g.reshape(1, D_MODEL)
    w_in_bf16 = w_in.astype(BF16)
    w_gate = w_in_bf16[:, O_GATE:O_GATE + 2 * D_MODEL]
    w_pa = jnp.pad(w_in_bf16[:, O_PA:O_PA + GLA_RANK], ((0, 0), (0, PA_PAD - GLA_RANK)))
    q_gain = jnp.tile(gq, HEADS_PER_GROUP) * (HEAD_DIM ** -0.5 * LOG2_E)
    k_gain = jnp.tile(gk, HEADS_PER_GROUP)
    qk_gain = jnp.stack([q_gain, k_gain]).reshape(2, 1, GROUP_WIDTH)
    idx = np.arange(MXU_DIM)
    bd = jnp.asarray((idx[:, None] // HEAD_DIM) == (idx[None, :] // HEAD_DIM), BF16)

    proj, pa = _main_proj_call(x2d, g1, w_in_bf16, w_gate, w_pa, tm=1024, rows=512)
    qkv_groups = _attn_proj_call(x2d, g1, w_in_bf16, qk_gain, bd, batch, tm=1024, rows=512)

    o_groups, lse_groups = [], []
    for qkv, (_, dilation) in zip(qkv_groups, ATTN_GROUPS):
        o, lse = _attn_call(qkv, dilation)
        o_groups.append(o)
        lse_groups.append(lse)

    u_pad = jnp.pad(gate_up, ((0, PA_PAD - GLA_RANK), (0, 0))).astype(BF16)
    o_gla = _gla_call(proj, pa, u_pad, gate_bias.reshape(1, GLA_DK),
                      gla_norm_g.reshape(1, GLA_HV), batch)

    return _out_call(x2d, o_groups, lse_groups, o_gla, proj,
                     branch_bias.reshape(1, 2 * D_MODEL),
                     w_a.astype(BF16), w_b.astype(BF16), w_out.astype(BF16),
                     norm2_g.reshape(1, D_MODEL), w_up.astype(BF16), w_down.astype(BF16), batch, out_tm)


def kernel(x, norm1_g, w_in, attn_q_norm_g, attn_k_norm_g, gla_gate_up, gla_gate_bias, gla_out_norm_g, branch_gate_bias, w_attn_branch, w_gla_branch, w_out, norm2_g, w_ff_up, w_ff_down):
    b, s, d = x.shape
    x2d = x.reshape(b * s, d)
    for l in range(norm1_g.shape[0]):
        x2d = _layer(x2d, b, norm1_g[l], w_in[l], attn_q_norm_g[l], attn_k_norm_g[l],
                     gla_gate_up[l], gla_gate_bias[l], gla_out_norm_g[l], branch_gate_bias[l],
                     w_attn_branch[l], w_gla_branch[l], w_out[l], norm2_g[l],
                     w_ff_up[l], w_ff_down[l])
    return x2d.reshape(b, s, d)
```

```python
import functools

import numpy as np
import jax
import jax.numpy as jnp
from jax import lax
from jax.experimental import pallas as pl
from jax.experimental.pallas import tpu as pltpu

F32 = jnp.float32
BF16 = jnp.bfloat16

D_MODEL = 1024
ATTN_GROUPS = ((128, 1), (512, 4), (2048, 16))
N_GROUPS = len(ATTN_GROUPS)
HEADS_PER_GROUP = 8
HEAD_DIM = 64
ATTN_BLOCK = 128
GROUP_WIDTH = HEADS_PER_GROUP * HEAD_DIM
ATTN_WIDTH = 3 * N_GROUPS * GROUP_WIDTH
N_PAIRS = GROUP_WIDTH // 128
ATTN_HEADS_PER_DOT = 4
ATTN_UNROLL = 16

GLA_HEADS = 4
GLA_DK = 512
GLA_DV = 1024
GLA_HK = GLA_DK // GLA_HEADS
GLA_HV = GLA_DV // GLA_HEADS
GLA_RANK = 16
GLA_TAU = 16.0
GLA_CHUNK = 64
GLA_SUB = 8
GLA_STEP = 256
GLA_SUBSTEPS = 4

D_FF = 4 * D_MODEL
EPS = 1e-6
LOG2_E = 1.4426950408889634
LN_2 = 0.6931471805599453

FAST_STRIDE = 4
LANES = 128
MXU_DIM = 256
VMEM_LIMIT_BYTES = 56 * 1024 * 1024

_ORIG_SIZES = (ATTN_WIDTH, GLA_DK, GLA_DK, GLA_DV, GLA_DV, GLA_RANK, 2 * D_MODEL)
_ORIG_OFF = tuple(int(v) for v in np.cumsum((0,) + _ORIG_SIZES))
O_ATTN, O_GQ, O_GK, O_GV, O_GR, O_PA, O_GATE = _ORIG_OFF[:7]

P_GQ = 0
P_GK = P_GQ + GLA_DK
P_GV = P_GK + GLA_DK
P_GR = P_GV + GLA_DV
P_GATE = P_GR + GLA_DV
P_MAIN = P_GATE + 2 * D_MODEL
GLA_W_BLOCK = (O_GATE - GLA_RANK - O_GQ) // 2
N_KINDS = 3
PA_PAD = LANES


def _rms_norm_rows(x, gain):
    ms = jnp.mean(x * x, axis=-1, keepdims=True)
    return x * lax.rsqrt(ms + EPS) * gain


def _main_proj_kernel(x_ref, g1_ref, wa_ref, wb_ref, wgate_ref, wpa_ref, o_ref, pa_ref, *, rows):
    gain = g1_ref[...]
    for rc in range(x_ref.shape[0] // rows):
        rs = slice(rc * rows, (rc + 1) * rows)
        h = _rms_norm_rows(x_ref[rs, :], gain).astype(BF16)
        pa_ref[rs, :] = jnp.dot(h, wpa_ref[...], preferred_element_type=F32)
        col = 0
        for w_ref in (wa_ref, wb_ref, wgate_ref):
            width = w_ref.shape[1]
            o_ref[rs, col:col + width] = jnp.dot(
                h, w_ref[...], preferred_element_type=F32).astype(BF16)
            col += width


def _main_proj_call(x2d, g1, w_in_bf16, w_gate, w_pa, tm, rows):
    t = x2d.shape[0]
    first_block = O_GQ // GLA_W_BLOCK

    def const(shape, col_block=0):
        return pl.BlockSpec(shape, lambda i: (0, col_block), pipeline_mode=pl.Buffered(1))

    return pl.pallas_call(
        functools.partial(_main_proj_kernel, rows=rows),
        grid=(t // tm,),
        in_specs=[
            pl.BlockSpec((tm, D_MODEL), lambda i: (i, 0)),
            const((1, D_MODEL)),
            const((D_MODEL, GLA_W_BLOCK), first_block),
            const((D_MODEL, GLA_W_BLOCK), first_block + 1),
            const((D_MODEL, 2 * D_MODEL)),
            const((D_MODEL, PA_PAD)),
        ],
        out_specs=[
            pl.BlockSpec((tm, P_MAIN), lambda i: (i, 0)),
            pl.BlockSpec((tm, PA_PAD), lambda i: (i, 0)),
        ],
        out_shape=[
            jax.ShapeDtypeStruct((t, P_MAIN), BF16),
            jax.ShapeDtypeStruct((t, PA_PAD), F32),
        ],
        compiler_params=pltpu.CompilerParams(
            dimension_semantics=("parallel",),
            vmem_limit_bytes=VMEM_LIMIT_BYTES),
        name="proj_main",
    )(x2d, g1, w_in_bf16, w_in_bf16, w_gate, w_pa)


def _qkv_project(h, w_refs, gain_ref, bd, o_ref, chunk, residue_of_block=None):
    d = o_ref.shape[1]
    rows = h.shape[0] // d
    residue_of_block = residue_of_block or list(range(d))
    for kind in range(N_KINDS):
        cols = slice(kind * GROUP_WIDTH, (kind + 1) * GROUP_WIDTH)
        acc = jnp.dot(h, w_refs[kind][...], preferred_element_type=F32)
        if kind < 2:
            sq = (acc * acc).astype(BF16)
            ss = jnp.concatenate(
                [jnp.dot(sq[:, c * MXU_DIM:(c + 1) * MXU_DIM], bd, preferred_element_type=F32)
                 for c in range(GROUP_WIDTH // MXU_DIM)], axis=1)
            acc = acc * lax.rsqrt(ss * (1.0 / HEAD_DIM) + EPS) * gain_ref[kind]
        y = acc.astype(BF16)
        for j, r in enumerate(residue_of_block):
            o_ref[0, r, chunk * rows:(chunk + 1) * rows, cols] = y[j * rows:(j + 1) * rows, :]


def _attn_proj_kernel(x_ref, g1_ref, *rest):
    n_w = N_KINDS * N_GROUPS
    w_all = rest[:n_w]
    gain_ref, bd_ref, o0_ref, o1_ref, o2_ref, col_scr, col2_scr, perm1_scr, perm2_scr = rest[n_w:]
    w_group = [[w_all[kind * N_GROUPS + g] for kind in range(N_KINDS)] for g in range(N_GROUPS)]
    n_chunks, _, tc, _ = col_scr.shape
    n_col = D_MODEL // LANES
    bd = bd_ref[...]
    gain = g1_ref[...]
    for ch in range(n_chunks):
        hf = _rms_norm_rows(x_ref[ch * tc:(ch + 1) * tc, :], gain)
        for c in range(n_col):
            col_scr[ch, c] = hf[:, c * LANES:(c + 1) * LANES]
        _qkv_project(hf.astype(BF16), w_group[0], gain_ref, bd, o0_ref, ch)
        f = o1_ref.shape[1]
        assert o2_ref.shape[1] == f * f
        rows1, rows2 = tc // f, tc // (f * f)
        for a in range(f):
            for c in range(n_col):
                t = col_scr[ch, c, pl.ds(a, rows1, stride=f), :]
                col2_scr[ch, c, a * rows1:(a + 1) * rows1, :] = t
                perm1_scr[ch, a * rows1:(a + 1) * rows1, c * LANES:(c + 1) * LANES] = t.astype(BF16)
        _qkv_project(perm1_scr[ch], w_group[1], gain_ref, bd, o1_ref, ch)
        for a in range(f):
            for b in range(f):
                blk = a * f + b
                for c in range(n_col):
                    perm2_scr[ch, blk * rows2:(blk + 1) * rows2, c * LANES:(c + 1) * LANES] = (
                        col2_scr[ch, c, pl.ds(a * rows1 + b, rows2, stride=f), :].astype(BF16))
        _qkv_project(perm2_scr[ch], w_group[2], gain_ref, bd, o2_ref, ch,
                     residue_of_block=[b * f + a for a in range(f) for b in range(f)])


def _attn_proj_call(x2d, g1, w_in_bf16, qk_gain, bd, batch, tm, rows):
    t = x2d.shape[0]
    s = t // batch
    tiles_per_seq = s // tm
    width = N_KINDS * GROUP_WIDTH
    n_w = N_KINDS * N_GROUPS

    def const(shape):
        return pl.BlockSpec(shape, lambda i: (0,) * len(shape), pipeline_mode=pl.Buffered(1))

    def w_block(j):
        return pl.BlockSpec((D_MODEL, GROUP_WIDTH), lambda i: (0, O_ATTN // GROUP_WIDTH + j),
                            pipeline_mode=pl.Buffered(1))

    return pl.pallas_call(
        _attn_proj_kernel,
        grid=(t // tm,),
        in_specs=[
            pl.BlockSpec((tm, D_MODEL), lambda i: (i, 0)),
            const((1, D_MODEL)),
            *[w_block(j) for j in range(n_w)],
            const((2, 1, GROUP_WIDTH)),
            const((MXU_DIM, MXU_DIM)),
        ],
        out_specs=[
            pl.BlockSpec((1, d, tm // d, width),
                         lambda i: (i // tiles_per_seq, 0, i % tiles_per_seq, 0))
            for _, d in ATTN_GROUPS],
        out_shape=[jax.ShapeDtypeStruct((batch, d, s // d, width), BF16) for _, d in ATTN_GROUPS],
        scratch_shapes=[
            pltpu.VMEM((tm // rows, D_MODEL // LANES, rows, LANES), F32),
            pltpu.VMEM((tm // rows, D_MODEL // LANES, rows, LANES), F32),
            pltpu.VMEM((tm // rows, rows, D_MODEL), BF16),
            pltpu.VMEM((tm // rows, rows, D_MODEL), BF16),
        ],
        compiler_params=pltpu.CompilerParams(
            dimension_semantics=("parallel",),
            vmem_limit_bytes=VMEM_LIMIT_BYTES),
        name="proj_attn",
    )(x2d, g1, *([w_in_bf16] * n_w), qk_gain, bd)


def _attn_kernel(q_ref, kp_ref, kc_ref, vp_ref, vc_ref, o_ref, lse_ref, *stage):
    n = pl.program_id(1)
    d = q_ref.shape[1]
    blk = ATTN_BLOCK
    n_sub = q_ref.shape[2] // blk
    stage_scr = stage[0] if stage else None
    f = FAST_STRIDE

    def put(idx, r, u, tile):
        if stage_scr is None:
            tok = pl.ds(u * blk * d + r, blk, stride=d)
            if idx == N_PAIRS:
                lse_ref[0, tok, :] = tile
            else:
                o_ref[0, idx, tok, :] = tile
        else:
            a, b = r % f, r // f
            stage_scr[idx, a, pl.ds(u * blk * (d // f) + b, blk, stride=d // f), :] = tile
    qi = lax.broadcasted_iota(jnp.int32, (blk, 2 * blk), 0)
    ki = lax.broadcasted_iota(jnp.int32, (blk, 2 * blk), 1)
    band = (ki >= qi) & (ki <= qi + blk)
    band_first = band & ((ki >= blk) | (n > 0))
    lane = lax.broadcasted_iota(jnp.int32, (blk, LANES), 1)
    first_head = lane < HEAD_DIM
    hb = ATTN_HEADS_PER_DOT
    width = hb * HEAD_DIM
    lane_w = lax.broadcasted_iota(jnp.int32, (blk, width), 1)
    head_lanes = [(lane_w >= h * HEAD_DIM) & (lane_w < (h + 1) * HEAD_DIM) for h in range(hb)]
    valid_by_sub = [jnp.concatenate([band_first if u == 0 else band] * hb, axis=0)
                    for u in range(min(n_sub, 2))]

    def unit(r, u):
        rows = slice(u * blk, (u + 1) * blk)
        valid_b = valid_by_sub[min(u, 1)]
        lse_tile = jnp.zeros((blk, LANES), F32)
        for g in range(HEADS_PER_GROUP // hb):
            sl = slice(g * width, (g + 1) * width)
            q = q_ref[0, r, rows, sl]
            if u == 0:
                k_prev, v_prev = kp_ref[0, r, :, sl], vp_ref[0, r, :, sl]
            else:
                prev_rows = slice((u - 1) * blk, u * blk)
                k_prev, v_prev = kc_ref[0, r, prev_rows, sl], vc_ref[0, r, prev_rows, sl]
            k = jnp.concatenate([k_prev, kc_ref[0, r, rows, sl]], axis=0)
            v = jnp.concatenate([v_prev, vc_ref[0, r, rows, sl]], axis=0)
            zero = jnp.zeros_like(q)
            q_rows = jnp.concatenate([jnp.where(head_lanes[h], q, zero) for h in range(hb)],
                                     axis=0)
            s = lax.dot_general(q_rows, k, (((1,), (1,)), ((), ())), preferred_element_type=F32)
            s = jnp.where(valid_b, s, -jnp.inf)
            m = jnp.max(s, axis=-1, keepdims=True)
            p = jnp.exp2(s - m)
            l = jnp.sum(p, axis=-1, keepdims=True)
            pv = jnp.dot(p.astype(BF16), v, preferred_element_type=F32)
            o_all = pv / l
            lse_all = m * LN_2 + jnp.log(l)
            for h in range(hb):
                head = g * hb + h
                lse_tile = jnp.where(lane == head, lse_all[h * blk:(h + 1) * blk], lse_tile)
            for pp in range(hb // 2):
                cols = slice(pp * LANES, (pp + 1) * LANES)
                lo = o_all[(2 * pp) * blk:(2 * pp + 1) * blk, cols]
                hi = o_all[(2 * pp + 1) * blk:(2 * pp + 2) * blk, cols]
                put(g * (hb // 2) + pp, r, u, jnp.where(first_head, lo, hi))
        put(N_PAIRS, r, u, lse_tile)

    if d * n_sub <= ATTN_UNROLL:
        for r in range(d):
            for u in range(n_sub):
                unit(r, u)
        if stage_scr is not None:
            for a in range(f):
                merged = pl.ds(a, stage_scr.shape[2], stride=f)
                for idx in range(N_PAIRS):
                    o_ref[0, idx, merged, :] = stage_scr[idx, a]
                lse_ref[0, merged, :] = stage_scr[N_PAIRS, a]
    else:
        assert stage_scr is None
        def residue(r, carry):
            for u in range(n_sub):
                unit(r, u)
            return carry
        lax.fori_loop(0, d, residue, 0, unroll=ATTN_UNROLL // n_sub)


def _attn_call(qkv, dilation):
    b, d, sub_len, _ = qkv.shape
    assert d == dilation
    blk = ATTN_BLOCK
    n_sub = max(1, ATTN_UNROLL // d)
    rows = n_sub * blk
    steps = sub_len // rows
    s = sub_len * d

    def cur(kind):
        return lambda bi, n: (bi, 0, n, kind)

    def prev(kind):
        return lambda bi, n: (bi, 0, jnp.maximum(n * n_sub - 1, 0), kind)

    cur_shape = (1, d, rows, GROUP_WIDTH)
    prev_shape = (1, d, blk, GROUP_WIDTH)
    o, lse = pl.pallas_call(
        _attn_kernel,
        grid=(b, steps),
        in_specs=[
            pl.BlockSpec(cur_shape, cur(0)),
            pl.BlockSpec(prev_shape, prev(1)),
            pl.BlockSpec(cur_shape, cur(1)),
            pl.BlockSpec(prev_shape, prev(2)),
            pl.BlockSpec(cur_shape, cur(2)),
        ],
        out_specs=[
            pl.BlockSpec((1, N_PAIRS, d * rows, LANES), lambda bi, n: (bi, 0, n, 0)),
            pl.BlockSpec((1, d * rows, LANES), lambda bi, n: (bi, n, 0)),
        ],
        out_shape=[
            jax.ShapeDtypeStruct((b, N_PAIRS, s, LANES), F32),
            jax.ShapeDtypeStruct((b, s, LANES), F32),
        ],
        scratch_shapes=([pltpu.VMEM((N_PAIRS + 1, FAST_STRIDE, d * rows // FAST_STRIDE, LANES), F32)]
                        if d > FAST_STRIDE else []),
        compiler_params=pltpu.CompilerParams(
            dimension_semantics=("parallel", "arbitrary"),
            vmem_limit_bytes=VMEM_LIMIT_BYTES),
        name=f"attn_d{d}",
    )(qkv, qkv, qkv, qkv, qkv)
    return o, lse


def _log_sigmoid(x):
    return jnp.minimum(x, 0.0) - jnp.log(1.0 + jnp.exp(-jnp.abs(x)))


def _split3(x):
    hi = x.astype(BF16)
    r1 = x - hi.astype(F32)
    mid = r1.astype(BF16)
    lo = (r1 - mid.astype(F32)).astype(BF16)
    return hi, mid, lo


def _gla_pairwise_products(b_scr, qs_scr, kf_scr, pcat_scr):
    sub = GLA_SUB
    tl = lax.broadcasted_iota(jnp.int32, (sub, GLA_HK), 0)
    for pair_i in range(GLA_STEP // (2 * sub)):
        tiles = []
        for r0 in (2 * pair_i * sub, (2 * pair_i + 1) * sub):
            b_blk, q_blk = b_scr[r0:r0 + sub, :], qs_scr[r0:r0 + sub, :]
            row = []
            for s in range(sub):
                d = b_blk - b_scr[r0 + s:r0 + s + 1, :]
                if s > 0:
                    d = jnp.where(tl >= s, d, -jnp.inf)
                row.append(q_blk * kf_scr[r0 + s:r0 + s + 1, :] * jnp.exp2(d))
            tiles.append(row)
        r0 = 2 * pair_i * sub
        for s in range(sub):
            pcat_scr[r0:r0 + 2 * sub, s * GLA_HK:(s + 1) * GLA_HK] = (
                jnp.concatenate([tiles[0][s], tiles[1][s]], axis=0).astype(BF16))


def _gla_reference_factors(b, qs, kf):
    c_len = GLA_CHUNK
    n_chunks = GLA_STEP // c_len
    sub = GLA_SUB

    def z(nrows):
        return jnp.zeros((nrows, GLA_HK), F32)

    qb, kb = [], []
    for r0 in range(0, GLA_STEP, 2 * sub):
        mid = r0 + sub
        b_ref = b[mid - 1:mid]
        qb += [z(sub), qs[mid:mid + sub] * jnp.exp2(b[mid:mid + sub] - b_ref)]
        kb += [kf[r0:mid] * jnp.exp2(b_ref - b[r0:mid]), z(sub)]
    q_block = jnp.concatenate(qb, axis=0).astype(BF16)
    k_block = jnp.concatenate(kb, axis=0).astype(BF16)

    qw, kw = [], []
    for c in range(n_chunks):
        bc, qc, kc = (x[c * c_len:(c + 1) * c_len] for x in (b, qs, kf))
        b15, b31, b47 = bc[15:16], bc[31:32], bc[47:48]
        q1 = jnp.concatenate([z(16), qc[16:32] * jnp.exp2(bc[16:32] - b15), z(32)], axis=0)
        q2 = jnp.concatenate([z(32), qc[32:64] * jnp.exp2(bc[32:64] - b31)], axis=0)
        q3 = jnp.concatenate([z(48), qc[48:64] * jnp.exp2(bc[48:64] - b47)], axis=0)
        k1 = jnp.concatenate([kc[0:16] * jnp.exp2(b15 - bc[0:16]), z(48)], axis=0)
        k2 = jnp.concatenate([kc[0:32] * jnp.exp2(b31 - bc[0:32]), z(32)], axis=0)
        k3 = jnp.concatenate([z(32), kc[32:48] * jnp.exp2(b47 - bc[32:48]), z(16)], axis=0)
        qw.append(jnp.concatenate([q1, q2, q3], axis=1))
        kw.append(jnp.concatenate([k1, k2, k3], axis=1))
    q_within = jnp.concatenate(qw, axis=0).astype(BF16)
    k_within = jnp.concatenate(kw, axis=0).astype(BF16)

    qx, kx = [], []
    for j in range(n_chunks - 1):
        lo, hi = j * c_len, (j + 1) * c_len
        b_ref = b[hi - 1:hi]
        qx.append(jnp.concatenate([z(hi), qs[hi:] * jnp.exp2(b[hi:] - b_ref)], axis=0))
        parts = [kf[lo:hi] * jnp.exp2(b_ref - b[lo:hi])]
        if lo:
            parts.insert(0, z(lo))
        parts.append(z(GLA_STEP - hi))
        kx.append(jnp.concatenate(parts, axis=0))
    q_cross = jnp.concatenate(qx, axis=1).astype(BF16)
    k_cross = jnp.concatenate(kx, axis=1).astype(BF16)
    return (q_block, k_block), (q_within, k_within), (q_cross, k_cross)


def _gla_kernel(q_ref, k_ref, v_ref, r_ref, pa_ref, u_ref, bias_ref, gn_ref, ltri_ref, e_ref,
                o_ref, st_ref, b_scr, qs_scr, kf_scr, pcat_scr):
    @pl.when(pl.program_id(1) == 0)
    def _():
        st_ref[...] = jnp.zeros_like(st_ref)

    n = GLA_STEP

    def substep(ss, carry):
        rows = pl.ds(pl.multiple_of(ss * n, n), n)
        _gla_substep(q_ref.at[rows], k_ref.at[rows], v_ref.at[rows], r_ref.at[rows], pa_ref.at[rows],
                     u_ref, bias_ref, gn_ref, ltri_ref, e_ref, o_ref.at[rows], st_ref,
                     b_scr, qs_scr, kf_scr, pcat_scr)
        return carry

    lax.fori_loop(0, q_ref.shape[0] // n, substep, 0)


def _gla_substep(q_ref, k_ref, v_ref, r_ref, pa_ref, u_ref, bias_ref, gn_ref, ltri_ref, e_ref,
                 o_ref, st_ref, b_scr, qs_scr, kf_scr, pcat_scr):
    n = GLA_STEP
    nt = (((1,), (1,)), ((), ()))

    logits = jnp.dot(pa_ref[...].astype(BF16), u_ref[...], preferred_element_type=F32) + bias_ref[...]
    la = _log_sigmoid(logits) * (1.0 / GLA_TAU)
    parts = jnp.dot(ltri_ref[...], jnp.concatenate(_split3(la), axis=1), preferred_element_type=F32)
    b_all = (parts[:, :GLA_DK] + parts[:, GLA_DK:2 * GLA_DK] + parts[:, 2 * GLA_DK:]) * LOG2_E

    heads = []
    for h in range(GLA_HEADS):
        kcols = slice(h * GLA_HK, (h + 1) * GLA_HK)
        b = b_all[:, kcols]
        qs = q_ref[:, kcols].astype(F32) * (GLA_HK ** -0.5)
        kf = k_ref[:, kcols].astype(F32)
        b_scr[h], qs_scr[h], kf_scr[h] = b, qs, kf
        _gla_pairwise_products(b_scr.at[h], qs_scr.at[h], kf_scr.at[h], pcat_scr.at[h * n:(h + 1) * n])
        heads.append((b, qs, kf))
    a_diag = jnp.dot(pcat_scr[...], e_ref[...], preferred_element_type=F32)

    ri = lax.broadcasted_iota(jnp.int32, (n, n), 0)
    ci = lax.broadcasted_iota(jnp.int32, (n, n), 1)
    same_sub = (ri // GLA_SUB) == (ci // GLA_SUB)
    same_pair = (ri // (2 * GLA_SUB)) == (ci // (2 * GLA_SUB))
    same_chunk = (ri // GLA_CHUNK) == (ci // GLA_CHUNK)
    gn = gn_ref[...]

    for h, (b, qs, kf) in enumerate(heads):
        vcols = slice(h * GLA_HV, (h + 1) * GLA_HV)
        a_block, a_within, a_cross = (
            lax.dot_general(qf, kf_, nt, preferred_element_type=F32)
            for qf, kf_ in _gla_reference_factors(b, qs, kf))
        a = (jnp.where(same_sub, a_diag[h * n:(h + 1) * n], 0.0) + jnp.where(same_pair, a_block, 0.0)
             + jnp.where(same_chunk, a_within, 0.0) + a_cross)
        v = v_ref[:, vcols]
        st = st_ref[h]
        b_last = b[n - 1:n]
        q_in = (qs * jnp.exp2(b)).astype(BF16)
        o = (jnp.dot(a.astype(BF16), v, preferred_element_type=F32)
             + lax.dot_general(q_in, st.astype(BF16), nt, preferred_element_type=F32))
        k_st = (kf * jnp.exp2(b_last - b)).astype(BF16)
        upd = lax.dot_general(v, k_st, (((0,), (0,)), ((), ())), preferred_element_type=F32)
        st_ref[h] = st * jnp.exp2(b_last) + upd

        ms = jnp.mean(o * o, axis=-1, keepdims=True)
        y = o * lax.rsqrt(ms + EPS) * gn
        r = r_ref[:, vcols].astype(F32)
        o_ref[:, vcols] = (y * (r * jax.nn.sigmoid(r))).astype(BF16)


def _gla_constants():
    n = GLA_STEP
    idx = np.arange(n)
    ltri = idx[:, None] >= idx[None, :]
    rows = np.arange(GLA_SUB * GLA_HK)
    cols = np.arange(n)
    e = (rows[:, None] // GLA_HK) == (cols[None, :] % GLA_SUB)
    return jnp.asarray(ltri, BF16), jnp.asarray(e, BF16)


def _gla_call(proj2d, pa, u_pad, bias, gn, batch):
    t = proj2d.shape[0]
    n = GLA_STEP
    rows = n * GLA_SUBSTEPS
    steps = t // batch // rows
    slots = GLA_HEADS
    ltri, e = _gla_constants()

    def tok(width, col_block):
        return pl.BlockSpec((rows, width), lambda bi, i: (bi * steps + i, col_block))

    def const(shape):
        return pl.BlockSpec(shape, lambda bi, i: (0, 0))

    return pl.pallas_call(
        _gla_kernel,
        grid=(batch, steps),
        in_specs=[
            tok(GLA_DK, P_GQ // GLA_DK),
            tok(GLA_DK, P_GK // GLA_DK),
            tok(GLA_DV, P_GV // GLA_DV),
            tok(GLA_DV, P_GR // GLA_DV),
            tok(PA_PAD, 0),
            const((PA_PAD, GLA_DK)),
            const((1, GLA_DK)),
            const((1, GLA_HV)),
            const((n, n)),
            const((GLA_SUB * GLA_HK, n)),
        ],
        out_specs=tok(GLA_DV, 0),
        out_shape=jax.ShapeDtypeStruct((t, GLA_DV), BF16),
        scratch_shapes=[
            pltpu.VMEM((GLA_HEADS, GLA_HV, GLA_HK), F32),
            pltpu.VMEM((slots, n, GLA_HK), F32),
            pltpu.VMEM((slots, n, GLA_HK), F32),
            pltpu.VMEM((slots, n, GLA_HK), F32),
            pltpu.VMEM((slots * n, GLA_SUB * GLA_HK), BF16),
        ],
        compiler_params=pltpu.CompilerParams(
            dimension_semantics=("parallel", "arbitrary"),
            vmem_limit_bytes=VMEM_LIMIT_BYTES),
        name="gla",
    )(proj2d, proj2d, proj2d, proj2d, pa, u_pad, bias, gn, ltri, e)


FF_CHUNK = 1024


def _out_kernel(x_ref, o0_ref, o1_ref, o2_ref, l0_ref, l1_ref, l2_ref, og_ref, gate_a_ref, gate_g_ref, gbias_ref,
                hx_ref, wa_ref, wb_ref, wo_ref, g2_ref, wup_ref, wdn_ref, out_ref):
    l0, l1, l2 = l0_ref[0], l1_ref[0], l2_ref[0]
    mx = jnp.maximum(jnp.maximum(l0, l1), l2)
    e0, e1, e2 = jnp.exp(l0 - mx), jnp.exp(l1 - mx), jnp.exp(l2 - mx)
    inv = 1.0 / (e0 + e1 + e2)
    hx = hx_ref[...]
    def pairs(ref):
        return jnp.concatenate([ref[0, p] for p in range(N_PAIRS)], axis=1)

    def expand(w):
        return jnp.dot(w.astype(BF16), hx, preferred_element_type=F32)

    o_attn = (expand(e0 * inv) * pairs(o0_ref) + expand(e1 * inv) * pairs(o1_ref)
              + expand(e2 * inv) * pairs(o2_ref))
    a = jnp.dot(o_attn.astype(BF16), wa_ref[...], preferred_element_type=F32)
    g = jnp.dot(og_ref[...], wb_ref[...], preferred_element_type=F32)
    gate_a = jax.nn.sigmoid(gate_a_ref[...].astype(F32) + gbias_ref[:, :D_MODEL])
    gate_g = jax.nn.sigmoid(gate_g_ref[...].astype(F32) + gbias_ref[:, D_MODEL:])
    mixed = gate_a * a + gate_g * g
    x1 = x_ref[...] + jnp.dot(mixed.astype(BF16), wo_ref[...], preferred_element_type=F32)

    ms = jnp.mean(x1 * x1, axis=-1, keepdims=True)
    h2 = (x1 * lax.rsqrt(ms + EPS) * g2_ref[...]).astype(BF16)
    hidden = []
    for c in range(D_FF // FF_CHUNK):
        u = jnp.dot(h2, wup_ref[:, c * FF_CHUNK:(c + 1) * FF_CHUNK], preferred_element_type=F32)
        u = jnp.maximum(u, 0.0)
        hidden.append((u * u).astype(BF16))
    out_ref[...] = x1 + jnp.dot(jnp.concatenate(hidden, axis=1), wdn_ref[...],
                                preferred_element_type=F32)


def _out_call(x2d, o_groups, lse_groups, o_gla, proj2d, gbias, wa, wb, wo, g2, wup, wdn, batch, tm):
    t = x2d.shape[0]
    lanes = np.arange(LANES)
    cols = np.arange(GROUP_WIDTH)
    head_expand = jnp.asarray(lanes[:, None] == cols[None, :] // HEAD_DIM, BF16)

    tiles_per_seq = t // batch // tm

    def tok(width):
        return pl.BlockSpec((tm, width), lambda i: (i, 0))

    attn_o = pl.BlockSpec((1, N_PAIRS, tm, LANES),
                          lambda i: (i // tiles_per_seq, 0, i % tiles_per_seq, 0))
    attn_lse = pl.BlockSpec((1, tm, LANES), lambda i: (i // tiles_per_seq, i % tiles_per_seq, 0))

    def const(shape):
        return pl.BlockSpec(shape, lambda i: (0, 0), pipeline_mode=pl.Buffered(1))

    return pl.pallas_call(
        _out_kernel,
        grid=(t // tm,),
        in_specs=[
            tok(D_MODEL),
            attn_o, attn_o, attn_o,
            attn_lse, attn_lse, attn_lse,
            tok(GLA_DV),
            pl.BlockSpec((tm, D_MODEL), lambda i: (i, P_GATE // D_MODEL)),
            pl.BlockSpec((tm, D_MODEL), lambda i: (i, P_GATE // D_MODEL + 1)),
            const((1, 2 * D_MODEL)),
            const((LANES, GROUP_WIDTH)),
            const((GROUP_WIDTH, D_MODEL)),
            const((GLA_DV, D_MODEL)),
            const((D_MODEL, D_MODEL)),
            const((1, D_MODEL)),
            const((D_MODEL, D_FF)),
            const((D_FF, D_MODEL)),
        ],
        out_specs=tok(D_MODEL),
        out_shape=jax.ShapeDtypeStruct((t, D_MODEL), F32),
        compiler_params=pltpu.CompilerParams(
            dimension_semantics=("parallel",),
            vmem_limit_bytes=VMEM_LIMIT_BYTES),
        name="out",
    )(x2d, *o_groups, *lse_groups, o_gla, proj2d, proj2d, gbias, head_expand, wa, wb, wo, g2, wup, wdn)


def _layer(x2d, batch, norm1_g, w_in, gq, gk, gate_up, gate_bias, gla_norm_g, branch_bias,
           w_a, w_b, w_out, norm2_g, w_up, w_down, out_tm=512):
    g1 = norm1_g.reshape(1, D_MODEL)
    w_in_bf16 = w_in.astype(BF16)
    w_gate = w_in_bf16[:, O_GATE:O_GATE + 2 * D_MODEL]
    w_pa = jnp.pad(w_in_bf16[:, O_PA:O_PA + GLA_RANK], ((0, 0), (0, PA_PAD - GLA_RANK)))
    q_gain = jnp.tile(gq, HEADS_PER_GROUP) * (HEAD_DIM ** -0.5 * LOG2_E)
    k_gain = jnp.tile(gk, HEADS_PER_GROUP)
    qk_gain = jnp.stack([q_gain, k_gain]).reshape(2, 1, GROUP_WIDTH)
    idx = np.arange(MXU_DIM)
    bd = jnp.asarray((idx[:, None] // HEAD_DIM) == (idx[None, :] // HEAD_DIM), BF16)

    proj, pa = _main_proj_call(x2d, g1, w_in_bf16, w_gate, w_pa, tm=1024, rows=512)
    qkv_groups = _attn_proj_call(x2d, g1, w_in_bf16, qk_gain, bd, batch, tm=1024, rows=512)

    o_groups, lse_groups = [], []
    for qkv, (_, dilation) in zip(qkv_groups, ATTN_GROUPS):
        o, lse = _attn_call(qkv, dilation)
        o_groups.append(o)
        lse_groups.append(lse)

    u_pad = jnp.pad(gate_up, ((0, PA_PAD - GLA_RANK), (0, 0))).astype(BF16)
    o_gla = _gla_call(proj, pa, u_pad, gate_bias.reshape(1, GLA_DK),
                      gla_norm_g.reshape(1, GLA_HV), batch)

    return _out_call(x2d, o_groups, lse_groups, o_gla, proj,
                     branch_bias.reshape(1, 2 * D_MODEL),
                     w_a.astype(BF16), w_b.astype(BF16), w_out.astype(BF16),
                     norm2_g.reshape(1, D_MODEL), w_up.astype(BF16), w_down.astype(BF16), batch, out_tm)


def kernel(x, norm1_g, w_in, attn_q_norm_g, attn_k_norm_g, gla_gate_up, gla_gate_bias, gla_out_norm_g, branch_gate_bias, w_attn_branch, w_gla_branch, w_out, norm2_g, w_ff_up, w_ff_down):
    b, s, d = x.shape
    x2d = x.reshape(b * s, d)
    for l in range(norm1_g.shape[0]):
        x2d = _layer(x2d, b, norm1_g[l], w_in[l], attn_q_norm_g[l], attn_k_norm_g[l],
                     gla_gate_up[l], gla_gate_bias[l], gla_out_norm_g[l], branch_gate_bias[l],
                     w_attn_branch[l], w_gla_branch[l], w_out[l], norm2_g[l],
                     w_ff_up[l], w_ff_down[l])
    return x2d.reshape(b, s, d)
```

```python
import functools

import numpy as np
import jax
import jax.numpy as jnp
from jax import lax
from jax.experimental import pallas as pl
from jax.experimental.pallas import tpu as pltpu

F32 = jnp.float32
BF16 = jnp.bfloat16

D_MODEL = 1024
ATTN_GROUPS = ((128, 1), (512, 4), (2048, 16))
N_GROUPS = len(ATTN_GROUPS)
HEADS_PER_GROUP = 8
HEAD_DIM = 64
ATTN_BLOCK = 128
GROUP_WIDTH = HEADS_PER_GROUP * HEAD_DIM
ATTN_WIDTH = 3 * N_GROUPS * GROUP_WIDTH
N_PAIRS = GROUP_WIDTH // 128
ATTN_HEADS_PER_DOT = 4
ATTN_UNROLL = 16

GLA_HEADS = 4
GLA_DK = 512
GLA_DV = 1024
GLA_HK = GLA_DK // GLA_HEADS
GLA_HV = GLA_DV // GLA_HEADS
GLA_RANK = 16
GLA_TAU = 16.0
GLA_CHUNK = 64
GLA_SUB = 8
GLA_STEP = 256
GLA_SUBSTEPS = 4

D_FF = 4 * D_MODEL
EPS = 1e-6
LOG2_E = 1.4426950408889634
LN_2 = 0.6931471805599453

FAST_STRIDE = 4
LANES = 128
MXU_DIM = 256
VMEM_LIMIT_BYTES = 56 * 1024 * 1024

_ORIG_SIZES = (ATTN_WIDTH, GLA_DK, GLA_DK, GLA_DV, GLA_DV, GLA_RANK, 2 * D_MODEL)
_ORIG_OFF = tuple(int(v) for v in np.cumsum((0,) + _ORIG_SIZES))
O_ATTN, O_GQ, O_GK, O_GV, O_GR, O_PA, O_GATE = _ORIG_OFF[:7]

P_GQ = 0
P_GK = P_GQ + GLA_DK
P_GV = P_GK + GLA_DK
P_GR = P_GV + GLA_DV
P_GATE = P_GR + GLA_DV
P_MAIN = P_GATE + 2 * D_MODEL
GLA_W_BLOCK = (O_GATE - GLA_RANK - O_GQ) // 2
N_KINDS = 3
PA_PAD = LANES


def _rms_norm_rows(x, gain):
    ms = jnp.mean(x * x, axis=-1, keepdims=True)
    return x * lax.rsqrt(ms + EPS) * gain


def _main_proj_kernel(x_ref, g1_ref, wa_ref, wb_ref, wgate_ref, wpa_ref, o_ref, pa_ref, *, rows):
    gain = g1_ref[...]
    for rc in range(x_ref.shape[0] // rows):
        rs = slice(rc * rows, (rc + 1) * rows)
        h = _rms_norm_rows(x_ref[rs, :], gain).astype(BF16)
        pa_ref[rs, :] = jnp.dot(h, wpa_ref[...], preferred_element_type=F32)
        col = 0
        for w_ref in (wa_ref, wb_ref, wgate_ref):
            width = w_ref.shape[1]
            o_ref[rs, col:col + width] = jnp.dot(
                h, w_ref[...], preferred_element_type=F32).astype(BF16)
            col += width


def _main_proj_call(x2d, g1, w_in_bf16, w_gate, w_pa, tm, rows):
    t = x2d.shape[0]
    first_block = O_GQ // GLA_W_BLOCK

    def const(shape, col_block=0):
        return pl.BlockSpec(shape, lambda i: (0, col_block), pipeline_mode=pl.Buffered(1))

    return pl.pallas_call(
        functools.partial(_main_proj_kernel, rows=rows),
        grid=(t // tm,),
        in_specs=[
            pl.BlockSpec((tm, D_MODEL), lambda i: (i, 0)),
            const((1, D_MODEL)),
            const((D_MODEL, GLA_W_BLOCK), first_block),
            const((D_MODEL, GLA_W_BLOCK), first_block + 1),
            const((D_MODEL, 2 * D_MODEL)),
            const((D_MODEL, PA_PAD)),
        ],
        out_specs=[
            pl.BlockSpec((tm, P_MAIN), lambda i: (i, 0)),
            pl.BlockSpec((tm, PA_PAD), lambda i: (i, 0)),
        ],
        out_shape=[
            jax.ShapeDtypeStruct((t, P_MAIN), BF16),
            jax.ShapeDtypeStruct((t, PA_PAD), F32),
        ],
        compiler_params=pltpu.CompilerParams(
            dimension_semantics=("parallel",),
            vmem_limit_bytes=VMEM_LIMIT_BYTES),
        name="proj_main",
    )(x2d, g1, w_in_bf16, w_in_bf16, w_gate, w_pa)


def _qkv_project(h, w_refs, gain_ref, bd, o_ref, chunk, residue_of_block=None):
    d = o_ref.shape[1]
    rows = h.shape[0] // d
    residue_of_block = residue_of_block or list(range(d))
    for kind in range(N_KINDS):
        cols = slice(kind * GROUP_WIDTH, (kind + 1) * GROUP_WIDTH)
        acc = jnp.dot(h, w_refs[kind][...], preferred_element_type=F32)
        if kind < 2:
            sq = (acc * acc).astype(BF16)
            ss = jnp.concatenate(
                [jnp.dot(sq[:, c * MXU_DIM:(c + 1) * MXU_DIM], bd, preferred_element_type=F32)
                 for c in range(GROUP_WIDTH // MXU_DIM)], axis=1)
            acc = acc * lax.rsqrt(ss * (1.0 / HEAD_DIM) + EPS) * gain_ref[kind]
        y = acc.astype(BF16)
        for j, r in enumerate(residue_of_block):
            o_ref[0, r, chunk * rows:(chunk + 1) * rows, cols] = y[j * rows:(j + 1) * rows, :]


def _attn_proj_kernel(x_ref, g1_ref, *rest):
    n_w = N_KINDS * N_GROUPS
    w_all = rest[:n_w]
    gain_ref, bd_ref, o0_ref, o1_ref, o2_ref, col_scr, col2_scr, perm1_scr, perm2_scr = rest[n_w:]
    w_group = [[w_all[kind * N_GROUPS + g] for kind in range(N_KINDS)] for g in range(N_GROUPS)]
    n_chunks, _, tc, _ = col_scr.shape
    n_col = D_MODEL // LANES
    bd = bd_ref[...]
    gain = g1_ref[...]
    for ch in range(n_chunks):
        hf = _rms_norm_rows(x_ref[ch * tc:(ch + 1) * tc, :], gain)
        for c in range(n_col):
            col_scr[ch, c] = hf[:, c * LANES:(c + 1) * LANES]
        _qkv_project(hf.astype(BF16), w_group[0], gain_ref, bd, o0_ref, ch)
        f = o1_ref.shape[1]
        assert o2_ref.shape[1] == f * f
        rows1, rows2 = tc // f, tc // (f * f)
        for a in range(f):
            for c in range(n_col):
                t = col_scr[ch, c, pl.ds(a, rows1, stride=f), :]
                col2_scr[ch, c, a * rows1:(a + 1) * rows1, :] = t
                perm1_scr[ch, a * rows1:(a + 1) * rows1, c * LANES:(c + 1) * LANES] = t.astype(BF16)
        _qkv_project(perm1_scr[ch], w_group[1], gain_ref, bd, o1_ref, ch)
        for a in range(f):
            for b in range(f):
                blk = a * f + b
                for c in range(n_col):
                    perm2_scr[ch, blk * rows2:(blk + 1) * rows2, c * LANES:(c + 1) * LANES] = (
                        col2_scr[ch, c, pl.ds(a * rows1 + b, rows2, stride=f), :].astype(BF16))
        _qkv_project(perm2_scr[ch], w_group[2], gain_ref, bd, o2_ref, ch,
                     residue_of_block=[b * f + a for a in range(f) for b in range(f)])


def _attn_proj_call(x2d, g1, w_in_bf16, qk_gain, bd, batch, tm, rows):
    t = x2d.shape[0]
    s = t // batch
    tiles_per_seq = s // tm
    width = N_KINDS * GROUP_WIDTH
    n_w = N_KINDS * N_GROUPS

    def const(shape):
        return pl.BlockSpec(shape, lambda i: (0,) * len(shape), pipeline_mode=pl.Buffered(1))

    def w_block(j):
        return pl.BlockSpec((D_MODEL, GROUP_WIDTH), lambda i: (0, O_ATTN // GROUP_WIDTH + j),
                            pipeline_mode=pl.Buffered(1))

    return pl.pallas_call(
        _attn_proj_kernel,
        grid=(t // tm,),
        in_specs=[
            pl.BlockSpec((tm, D_MODEL), lambda i: (i, 0)),
            const((1, D_MODEL)),
            *[w_block(j) for j in range(n_w)],
            const((2, 1, GROUP_WIDTH)),
            const((MXU_DIM, MXU_DIM)),
        ],
        out_specs=[
            pl.BlockSpec((1, d, tm // d, width),
                         lambda i: (i // tiles_per_seq, 0, i % tiles_per_seq, 0))
            for _, d in ATTN_GROUPS],
        out_shape=[jax.ShapeDtypeStruct((batch, d, s // d, width), BF16) for _, d in ATTN_GROUPS],
        scratch_shapes=[
            pltpu.VMEM((tm // rows, D_MODEL // LANES, rows, LANES), F32),
            pltpu.VMEM((tm // rows, D_MODEL // LANES, rows, LANES), F32),
            pltpu.VMEM((tm // rows, rows, D_MODEL), BF16),
            pltpu.VMEM((tm // rows, rows, D_MODEL), BF16),
        ],
        compiler_params=pltpu.CompilerParams(
            dimension_semantics=("parallel",),
            vmem_limit_bytes=VMEM_LIMIT_BYTES),
        name="proj_attn",
    )(x2d, g1, *([w_in_bf16] * n_w), qk_gain, bd)


def _attn_kernel(q_ref, kp_ref, kc_ref, vp_ref, vc_ref, o_ref, lse_ref, *stage):
    n = pl.program_id(1)
    d = q_ref.shape[1]
    blk = ATTN_BLOCK
    n_sub = q_ref.shape[2] // blk
    stage_scr = stage[0] if stage else None
    f = FAST_STRIDE

    def put(idx, r, u, tile):
        if stage_scr is None:
            tok = pl.ds(u * blk * d + r, blk, stride=d)
            if idx == N_PAIRS:
                lse_ref[0, tok, :] = tile
            else:
                o_ref[0, idx, tok, :] = tile
        else:
            a, b = r % f, r // f
            stage_scr[idx, a, pl.ds(u * blk * (d // f) + b, blk, stride=d // f), :] = tile
    qi = lax.broadcasted_iota(jnp.int32, (blk, 2 * blk), 0)
    ki = lax.broadcasted_iota(jnp.int32, (blk, 2 * blk), 1)
    band = (ki >= qi) & (ki <= qi + blk)
    band_first = band & ((ki >= blk) | (n > 0))
    lane = lax.broadcasted_iota(jnp.int32, (blk, LANES), 1)
    first_head = lane < HEAD_DIM
    hb = ATTN_HEADS_PER_DOT
    width = hb * HEAD_DIM
    lane_w = lax.broadcasted_iota(jnp.int32, (blk, width), 1)
    head_lanes = [(lane_w >= h * HEAD_DIM) & (lane_w < (h + 1) * HEAD_DIM) for h in range(hb)]
    valid_by_sub = [jnp.concatenate([band_first if u == 0 else band] * hb, axis=0)
                    for u in range(min(n_sub, 2))]

    def unit(r, u):
        rows = slice(u * blk, (u + 1) * blk)
        valid_b = valid_by_sub[min(u, 1)]
        m_tile = jnp.zeros((blk, LANES), F32)
        l_tile = jnp.ones((blk, LANES), F32)
        for g in range(HEADS_PER_GROUP // hb):
            sl = slice(g * width, (g + 1) * width)
            q = q_ref[0, r, rows, sl]
            if u == 0:
                k_prev, v_prev = kp_ref[0, r, :, sl], vp_ref[0, r, :, sl]
            else:
                prev_rows = slice((u - 1) * blk, u * blk)
                k_prev, v_prev = kc_ref[0, r, prev_rows, sl], vc_ref[0, r, prev_rows, sl]
            k = jnp.concatenate([k_prev, kc_ref[0, r, rows, sl]], axis=0)
            v = jnp.concatenate([v_prev, vc_ref[0, r, rows, sl]], axis=0)
            zero = jnp.zeros_like(q)
            q_rows = jnp.concatenate([jnp.where(head_lanes[h], q, zero) for h in range(hb)],
                                     axis=0)
            s = lax.dot_general(q_rows, k, (((1,), (1,)), ((), ())), preferred_element_type=F32)
            s = jnp.where(valid_b, s, -jnp.inf)
            m = jnp.max(s, axis=-1, keepdims=True)
            p = jnp.exp2(s - m)
            l = jnp.sum(p, axis=-1, keepdims=True)
            p = p.astype(BF16)
            inv_l = 1.0 / l
            for h in range(hb):
                head = g * hb + h
                m_tile = jnp.where(lane == head, m[h * blk:(h + 1) * blk], m_tile)
                l_tile = jnp.where(lane == head, l[h * blk:(h + 1) * blk], l_tile)
            for pp in range(hb // 2):
                rows2 = slice(2 * pp * blk, (2 * pp + 2) * blk)
                pv = jnp.dot(p[rows2], v[:, pp * LANES:(pp + 1) * LANES],
                             preferred_element_type=F32) * inv_l[rows2]
                put(g * (hb // 2) + pp, r, u, jnp.where(first_head, pv[:blk], pv[blk:]))
        put(N_PAIRS, r, u, m_tile * LN_2 + jnp.log(l_tile))

    if d * n_sub <= ATTN_UNROLL:
        for r in range(d):
            for u in range(n_sub):
                unit(r, u)
        if stage_scr is not None:
            for a in range(f):
                merged = pl.ds(a, stage_scr.shape[2], stride=f)
                for idx in range(N_PAIRS):
                    o_ref[0, idx, merged, :] = stage_scr[idx, a]
                lse_ref[0, merged, :] = stage_scr[N_PAIRS, a]
    else:
        assert stage_scr is None
        def residue(r, carry):
            for u in range(n_sub):
                unit(r, u)
            return carry
        lax.fori_loop(0, d, residue, 0, unroll=ATTN_UNROLL // n_sub)


def _attn_call(qkv, dilation):
    b, d, sub_len, _ = qkv.shape
    assert d == dilation
    blk = ATTN_BLOCK
    n_sub = max(1, ATTN_UNROLL // d)
    rows = n_sub * blk
    steps = sub_len // rows
    s = sub_len * d

    def cur(kind):
        return lambda bi, n: (bi, 0, n, kind)

    def prev(kind):
        return lambda bi, n: (bi, 0, jnp.maximum(n * n_sub - 1, 0), kind)

    cur_shape = (1, d, rows, GROUP_WIDTH)
    prev_shape = (1, d, blk, GROUP_WIDTH)
    o, lse = pl.pallas_call(
        _attn_kernel,
        grid=(b, steps),
        in_specs=[
            pl.BlockSpec(cur_shape, cur(0)),
            pl.BlockSpec(prev_shape, prev(1)),
            pl.BlockSpec(cur_shape, cur(1)),
            pl.BlockSpec(prev_shape, prev(2)),
            pl.BlockSpec(cur_shape, cur(2)),
        ],
        out_specs=[
            pl.BlockSpec((1, N_PAIRS, d * rows, LANES), lambda bi, n: (bi, 0, n, 0)),
            pl.BlockSpec((1, d * rows, LANES), lambda bi, n: (bi, n, 0)),
        ],
        out_shape=[
            jax.ShapeDtypeStruct((b, N_PAIRS, s, LANES), F32),
            jax.ShapeDtypeStruct((b, s, LANES), F32),
        ],
        scratch_shapes=([pltpu.VMEM((N_PAIRS + 1, FAST_STRIDE, d * rows // FAST_STRIDE, LANES), F32)]
                        if d > FAST_STRIDE else []),
        compiler_params=pltpu.CompilerParams(
            dimension_semantics=("parallel", "arbitrary"),
            vmem_limit_bytes=VMEM_LIMIT_BYTES),
        name=f"attn_d{d}",
    )(qkv, qkv, qkv, qkv, qkv)
    return o, lse


def _log_sigmoid(x):
    return jnp.minimum(x, 0.0) - jnp.log(1.0 + jnp.exp(-jnp.abs(x)))


def _split3(x):
    hi = x.astype(BF16)
    r1 = x - hi.astype(F32)
    mid = r1.astype(BF16)
    lo = (r1 - mid.astype(F32)).astype(BF16)
    return hi, mid, lo


def _gla_pairwise_products(b_scr, qs_scr, kf_scr, pcat_scr):
    sub = GLA_SUB
    tl = lax.broadcasted_iota(jnp.int32, (sub, GLA_HK), 0)
    for pair_i in range(GLA_STEP // (2 * sub)):
        tiles = []
        for r0 in (2 * pair_i * sub, (2 * pair_i + 1) * sub):
            b_blk, q_blk = b_scr[r0:r0 + sub, :], qs_scr[r0:r0 + sub, :]
            row = []
            for s in range(sub):
                d = b_blk - b_scr[r0 + s:r0 + s + 1, :]
                if s > 0:
                    d = jnp.where(tl >= s, d, -jnp.inf)
                row.append(q_blk * kf_scr[r0 + s:r0 + s + 1, :] * jnp.exp2(d))
            tiles.append(row)
        r0 = 2 * pair_i * sub
        for s in range(sub):
            pcat_scr[r0:r0 + 2 * sub, s * GLA_HK:(s + 1) * GLA_HK] = (
                jnp.concatenate([tiles[0][s], tiles[1][s]], axis=0).astype(BF16))


def _gla_reference_factors(b, qs, kf):
    c_len = GLA_CHUNK
    n_chunks = GLA_STEP // c_len
    sub = GLA_SUB

    def z(nrows):
        return jnp.zeros((nrows, GLA_HK), F32)

    qb, kb = [], []
    for r0 in range(0, GLA_STEP, 2 * sub):
        mid = r0 + sub
        b_ref = b[mid - 1:mid]
        qb += [z(sub), qs[mid:mid + sub] * jnp.exp2(b[mid:mid + sub] - b_ref)]
        kb += [kf[r0:mid] * jnp.exp2(b_ref - b[r0:mid]), z(sub)]
    q_block = jnp.concatenate(qb, axis=0).astype(BF16)
    k_block = jnp.concatenate(kb, axis=0).astype(BF16)

    qw, kw = [], []
    for c in range(n_chunks):
        bc, qc, kc = (x[c * c_len:(c + 1) * c_len] for x in (b, qs, kf))
        b15, b31, b47 = bc[15:16], bc[31:32], bc[47:48]
        q1 = jnp.concatenate([z(16), qc[16:32] * jnp.exp2(bc[16:32] - b15), z(32)], axis=0)
        q2 = jnp.concatenate([z(32), qc[32:64] * jnp.exp2(bc[32:64] - b31)], axis=0)
        q3 = jnp.concatenate([z(48), qc[48:64] * jnp.exp2(bc[48:64] - b47)], axis=0)
        k1 = jnp.concatenate([kc[0:16] * jnp.exp2(b15 - bc[0:16]), z(48)], axis=0)
        k2 = jnp.concatenate([kc[0:32] * jnp.exp2(b31 - bc[0:32]), z(32)], axis=0)
        k3 = jnp.concatenate([z(32), kc[32:48] * jnp.exp2(b47 - bc[32:48]), z(16)], axis=0)
        qw.append(jnp.concatenate([q1, q2, q3], axis=1))
        kw.append(jnp.concatenate([k1, k2, k3], axis=1))
    q_within = jnp.concatenate(qw, axis=0).astype(BF16)
    k_within = jnp.concatenate(kw, axis=0).astype(BF16)

    qx, kx = [], []
    for j in range(n_chunks - 1):
        lo, hi = j * c_len, (j + 1) * c_len
        b_ref = b[hi - 1:hi]
        qx.append(jnp.concatenate([z(hi), qs[hi:] * jnp.exp2(b[hi:] - b_ref)], axis=0))
        parts = [kf[lo:hi] * jnp.exp2(b_ref - b[lo:hi])]
        if lo:
            parts.insert(0, z(lo))
        parts.append(z(GLA_STEP - hi))
        kx.append(jnp.concatenate(parts, axis=0))
    q_cross = jnp.concatenate(qx, axis=1).astype(BF16)
    k_cross = jnp.concatenate(kx, axis=1).astype(BF16)
    return (q_block, k_block), (q_within, k_within), (q_cross, k_cross)


def _gla_kernel(q_ref, k_ref, v_ref, r_ref, pa_ref, u_ref, bias_ref, gn_ref, ltri_ref, e_ref,
                o_ref, st_ref, b_scr, qs_scr, kf_scr, pcat_scr):
    @pl.when(pl.program_id(1) == 0)
    def _():
        st_ref[...] = jnp.zeros_like(st_ref)

    n = GLA_STEP

    def substep(ss, carry):
        rows = pl.ds(pl.multiple_of(ss * n, n), n)
        _gla_substep(q_ref.at[rows], k_ref.at[rows], v_ref.at[rows], r_ref.at[rows], pa_ref.at[rows],
                     u_ref, bias_ref, gn_ref, ltri_ref, e_ref, o_ref.at[rows], st_ref,
                     b_scr, qs_scr, kf_scr, pcat_scr)
        return carry

    lax.fori_loop(0, q_ref.shape[0] // n, substep, 0)


def _gla_substep(q_ref, k_ref, v_ref, r_ref, pa_ref, u_ref, bias_ref, gn_ref, ltri_ref, e_ref,
                 o_ref, st_ref, b_scr, qs_scr, kf_scr, pcat_scr):
    n = GLA_STEP
    nt = (((1,), (1,)), ((), ()))

    logits = jnp.dot(pa_ref[...].astype(BF16), u_ref[...], preferred_element_type=F32) + bias_ref[...]
    la = _log_sigmoid(logits) * (1.0 / GLA_TAU)
    parts = jnp.dot(ltri_ref[...], jnp.concatenate(_split3(la), axis=1), preferred_element_type=F32)
    b_all = (parts[:, :GLA_DK] + parts[:, GLA_DK:2 * GLA_DK] + parts[:, 2 * GLA_DK:]) * LOG2_E

    heads = []
    for h in range(GLA_HEADS):
        kcols = slice(h * GLA_HK, (h + 1) * GLA_HK)
        b = b_all[:, kcols]
        qs = q_ref[:, kcols].astype(F32) * (GLA_HK ** -0.5)
        kf = k_ref[:, kcols].astype(F32)
        b_scr[h], qs_scr[h], kf_scr[h] = b, qs, kf
        _gla_pairwise_products(b_scr.at[h], qs_scr.at[h], kf_scr.at[h], pcat_scr.at[h * n:(h + 1) * n])
        heads.append((b, qs, kf))
    a_diag = jnp.dot(pcat_scr[...], e_ref[...], preferred_element_type=F32)

    ri = lax.broadcasted_iota(jnp.int32, (n, n), 0)
    ci = lax.broadcasted_iota(jnp.int32, (n, n), 1)
    same_sub = (ri // GLA_SUB) == (ci // GLA_SUB)
    same_pair = (ri // (2 * GLA_SUB)) == (ci // (2 * GLA_SUB))
    same_chunk = (ri // GLA_CHUNK) == (ci // GLA_CHUNK)
    gn = gn_ref[...]

    for h, (b, qs, kf) in enumerate(heads):
        vcols = slice(h * GLA_HV, (h + 1) * GLA_HV)
        a_block, a_within, a_cross = (
            lax.dot_general(qf, kf_, nt, preferred_element_type=F32)
            for qf, kf_ in _gla_reference_factors(b, qs, kf))
        a = (jnp.where(same_sub, a_diag[h * n:(h + 1) * n], 0.0) + jnp.where(same_pair, a_block, 0.0)
             + jnp.where(same_chunk, a_within, 0.0) + a_cross)
        v = v_ref[:, vcols]
        st = st_ref[h]
        b_last = b[n - 1:n]
        q_in = (qs * jnp.exp2(b)).astype(BF16)
        o = (jnp.dot(a.astype(BF16), v, preferred_element_type=F32)
             + lax.dot_general(q_in, st.astype(BF16), nt, preferred_element_type=F32))
        k_st = (kf * jnp.exp2(b_last - b)).astype(BF16)
        upd = lax.dot_general(v, k_st, (((0,), (0,)), ((), ())), preferred_element_type=F32)
        st_ref[h] = st * jnp.exp2(b_last) + upd

        ms = jnp.mean(o * o, axis=-1, keepdims=True)
        y = o * lax.rsqrt(ms + EPS) * gn
        r = r_ref[:, vcols].astype(F32)
        o_ref[:, vcols] = (y * (r * jax.nn.sigmoid(r))).astype(BF16)


def _gla_constants():
    n = GLA_STEP
    idx = np.arange(n)
    ltri = idx[:, None] >= idx[None, :]
    rows = np.arange(GLA_SUB * GLA_HK)
    cols = np.arange(n)
    e = (rows[:, None] // GLA_HK) == (cols[None, :] % GLA_SUB)
    return jnp.asarray(ltri, BF16), jnp.asarray(e, BF16)


def _gla_call(proj2d, pa, u_pad, bias, gn, batch):
    t = proj2d.shape[0]
    n = GLA_STEP
    rows = n * GLA_SUBSTEPS
    steps = t // batch // rows
    slots = GLA_HEADS
    ltri, e = _gla_constants()

    def tok(width, col_block):
        return pl.BlockSpec((rows, width), lambda bi, i: (bi * steps + i, col_block))

    def const(shape):
        return pl.BlockSpec(shape, lambda bi, i: (0, 0))

    return pl.pallas_call(
        _gla_kernel,
        grid=(batch, steps),
        in_specs=[
            tok(GLA_DK, P_GQ // GLA_DK),
            tok(GLA_DK, P_GK // GLA_DK),
            tok(GLA_DV, P_GV // GLA_DV),
            tok(GLA_DV, P_GR // GLA_DV),
            tok(PA_PAD, 0),
            const((PA_PAD, GLA_DK)),
            const((1, GLA_DK)),
            const((1, GLA_HV)),
            const((n, n)),
            const((GLA_SUB * GLA_HK, n)),
        ],
        out_specs=tok(GLA_DV, 0),
        out_shape=jax.ShapeDtypeStruct((t, GLA_DV), BF16),
        scratch_shapes=[
            pltpu.VMEM((GLA_HEADS, GLA_HV, GLA_HK), F32),
            pltpu.VMEM((slots, n, GLA_HK), F32),
            pltpu.VMEM((slots, n, GLA_HK), F32),
            pltpu.VMEM((slots, n, GLA_HK), F32),
            pltpu.VMEM((slots * n, GLA_SUB * GLA_HK), BF16),
        ],
        compiler_params=pltpu.CompilerParams(
            dimension_semantics=("parallel", "arbitrary"),
            vmem_limit_bytes=VMEM_LIMIT_BYTES),
        name="gla",
    )(proj2d, proj2d, proj2d, proj2d, pa, u_pad, bias, gn, ltri, e)


FF_CHUNK = 1024


def _out_kernel(x_ref, o0_ref, o1_ref, o2_ref, l0_ref, l1_ref, l2_ref, og_ref, gate_a_ref, gate_g_ref, gbias_ref,
                hx_ref, wa_ref, wb_ref, wo_ref, g2_ref, wup_ref, wdn_ref, out_ref):
    l0, l1, l2 = l0_ref[0], l1_ref[0], l2_ref[0]
    mx = jnp.maximum(jnp.maximum(l0, l1), l2)
    e0, e1, e2 = jnp.exp(l0 - mx), jnp.exp(l1 - mx), jnp.exp(l2 - mx)
    inv = 1.0 / (e0 + e1 + e2)
    hx = hx_ref[...]
    def pairs(ref):
        return jnp.concatenate([ref[0, p] for p in range(N_PAIRS)], axis=1)

    def expand(w):
        return jnp.dot(w.astype(BF16), hx, preferred_element_type=F32)

    o_attn = (expand(e0 * inv) * pairs(o0_ref) + expand(e1 * inv) * pairs(o1_ref)
              + expand(e2 * inv) * pairs(o2_ref))
    a = jnp.dot(o_attn.astype(BF16), wa_ref[...], preferred_element_type=F32)
    g = jnp.dot(og_ref[...], wb_ref[...], preferred_element_type=F32)
    gate_a = jax.nn.sigmoid(gate_a_ref[...].astype(F32) + gbias_ref[:, :D_MODEL])
    gate_g = jax.nn.sigmoid(gate_g_ref[...].astype(F32) + gbias_ref[:, D_MODEL:])
    mixed = gate_a * a + gate_g * g
    x1 = x_ref[...] + jnp.dot(mixed.astype(BF16), wo_ref[...], preferred_element_type=F32)

    ms = jnp.mean(x1 * x1, axis=-1, keepdims=True)
    h2 = (x1 * lax.rsqrt(ms + EPS) * g2_ref[...]).astype(BF16)
    hidden = []
    for c in range(D_FF // FF_CHUNK):
        u = jnp.dot(h2, wup_ref[:, c * FF_CHUNK:(c + 1) * FF_CHUNK], preferred_element_type=F32)
        u = jnp.maximum(u, 0.0)
        hidden.append((u * u).astype(BF16))
    out_ref[...] = x1 + jnp.dot(jnp.concatenate(hidden, axis=1), wdn_ref[...],
                                preferred_element_type=F32)


def _out_call(x2d, o_groups, lse_groups, o_gla, proj2d, gbias, wa, wb, wo, g2, wup, wdn, batch, tm):
    t = x2d.shape[0]
    lanes = np.arange(LANES)
    cols = np.arange(GROUP_WIDTH)
    head_expand = jnp.asarray(lanes[:, None] == cols[None, :] // HEAD_DIM, BF16)

    tiles_per_seq = t // batch // tm

    def tok(width):
        return pl.BlockSpec((tm, width), lambda i: (i, 0))

    attn_o = pl.BlockSpec((1, N_PAIRS, tm, LANES),
                          lambda i: (i // tiles_per_seq, 0, i % tiles_per_seq, 0))
    attn_lse = pl.BlockSpec((1, tm, LANES), lambda i: (i // tiles_per_seq, i % tiles_per_seq, 0))

    def const(shape):
        return pl.BlockSpec(shape, lambda i: (0, 0), pipeline_mode=pl.Buffered(1))

    return pl.pallas_call(
        _out_kernel,
        grid=(t // tm,),
        in_specs=[
            tok(D_MODEL),
            attn_o, attn_o, attn_o,
            attn_lse, attn_lse, attn_lse,
            tok(GLA_DV),
            pl.BlockSpec((tm, D_MODEL), lambda i: (i, P_GATE // D_MODEL)),
            pl.BlockSpec((tm, D_MODEL), lambda i: (i, P_GATE // D_MODEL + 1)),
            const((1, 2 * D_MODEL)),
            const((LANES, GROUP_WIDTH)),
            const((GROUP_WIDTH, D_MODEL)),
            const((GLA_DV, D_MODEL)),
            const((D_MODEL, D_MODEL)),
            const((1, D_MODEL)),
            const((D_MODEL, D_FF)),
            const((D_FF, D_MODEL)),
        ],
        out_specs=tok(D_MODEL),
        out_shape=jax.ShapeDtypeStruct((t, D_MODEL), F32),
        compiler_params=pltpu.CompilerParams(
            dimension_semantics=("parallel",),
            vmem_limit_bytes=VMEM_LIMIT_BYTES),
        name="out",
    )(x2d, *o_groups, *lse_groups, o_gla, proj2d, proj2d, gbias, head_expand, wa, wb, wo, g2, wup, wdn)


def _layer(x2d, batch, norm1_g, w_in, gq, gk, gate_up, gate_bias, gla_norm_g, branch_bias,
           w_a, w_b, w_out, norm2_g, w_up, w_down, out_tm=512):
    g1 = norm1_g.reshape(1, D_MODEL)
    w_in_bf16 = w_in.astype(BF16)
    w_gate = w_in_bf16[:, O_GATE:O_GATE + 2 * D_MODEL]
    w_pa = jnp.pad(w_in_bf16[:, O_PA:O_PA + GLA_RANK], ((0, 0), (0, PA_PAD - GLA_RANK)))
    q_gain = jnp.tile(gq, HEADS_PER_GROUP) * (HEAD_DIM ** -0.5 * LOG2_E)
    k_gain = jnp.tile(gk, HEADS_PER_GROUP)
    qk_gain = jnp.stack([q_gain, k_gain]).reshape(2, 1, GROUP_WIDTH)
    idx = np.arange(MXU_DIM)
    bd = jnp.asarray((idx[:, None] // HEAD_DIM) == (idx[None, :] // HEAD_DIM), BF16)

    proj, pa = _main_proj_call(x2d, g1, w_in_bf16, w_gate, w_pa, tm=1024, rows=512)
    qkv_groups = _attn_proj_call(x2d, g1, w_in_bf16, qk_gain, bd, batch, tm=1024, rows=512)

    o_groups, lse_groups = [], []
    for qkv, (_, dilation) in zip(qkv_groups, ATTN_GROUPS):
        o, lse = _attn_call(qkv, dilation)
        o_groups.append(o)
        lse_groups.append(lse)

    u_pad = jnp.pad(gate_up, ((0, PA_PAD - GLA_RANK), (0, 0))).astype(BF16)
    o_gla = _gla_call(proj, pa, u_pad, gate_bias.reshape(1, GLA_DK),
                      gla_norm_g.reshape(1, GLA_HV), batch)

    return _out_call(x2d, o_groups, lse_groups, o_gla, proj,
                     branch_bias.reshape(1, 2 * D_MODEL),
                     w_a.astype(BF16), w_b.astype(BF16), w_out.astype(BF16),
                     norm2_g.reshape(1, D_MODEL), w_up.astype(BF16), w_down.astype(BF16), batch, out_tm)


def kernel(x, norm1_g, w_in, attn_q_norm_g, attn_k_norm_g, gla_gate_up, gla_gate_bias, gla_out_norm_g, branch_gate_bias, w_attn_branch, w_gla_branch, w_out, norm2_g, w_ff_up, w_ff_down):
    b, s, d = x.shape
    x2d = x.reshape(b * s, d)
    for l in range(norm1_g.shape[0]):
        x2d = _layer(x2d, b, norm1_g[l], w_in[l], attn_q_norm_g[l], attn_k_norm_g[l],
                     gla_gate_up[l], gla_gate_bias[l], gla_out_norm_g[l], branch_gate_bias[l],
                     w_attn_branch[l], w_gla_branch[l], w_out[l], norm2_g[l],
                     w_ff_up[l], w_ff_down[l])
    return x2d.reshape(b, s, d)
```

```python
import functools

import numpy as np
import jax
import jax.numpy as jnp
from jax import lax
from jax.experimental import pallas as pl
from jax.experimental.pallas import tpu as pltpu

F32 = jnp.float32
BF16 = jnp.bfloat16

D_MODEL = 1024
ATTN_GROUPS = ((128, 1), (512, 4), (2048, 16))
N_GROUPS = len(ATTN_GROUPS)
HEADS_PER_GROUP = 8
HEAD_DIM = 64
ATTN_BLOCK = 128
GROUP_WIDTH = HEADS_PER_GROUP * HEAD_DIM
ATTN_WIDTH = 3 * N_GROUPS * GROUP_WIDTH
N_PAIRS = GROUP_WIDTH // 128
ATTN_HEADS_PER_DOT = 4
ATTN_UNROLL = 16

GLA_HEADS = 4
GLA_DK = 512
GLA_DV = 1024
GLA_HK = GLA_DK // GLA_HEADS
GLA_HV = GLA_DV // GLA_HEADS
GLA_RANK = 16
GLA_TAU = 16.0
GLA_CHUNK = 64
GLA_SUB = 8
GLA_STEP = 256
GLA_SUBSTEPS = 4

D_FF = 4 * D_MODEL
EPS = 1e-6
LOG2_E = 1.4426950408889634
LN_2 = 0.6931471805599453

FAST_STRIDE = 4
LANES = 128
MXU_DIM = 256
VMEM_LIMIT_BYTES = 56 * 1024 * 1024

_ORIG_SIZES = (ATTN_WIDTH, GLA_DK, GLA_DK, GLA_DV, GLA_DV, GLA_RANK, 2 * D_MODEL)
_ORIG_OFF = tuple(int(v) for v in np.cumsum((0,) + _ORIG_SIZES))
O_ATTN, O_GQ, O_GK, O_GV, O_GR, O_PA, O_GATE = _ORIG_OFF[:7]

P_GQ = 0
P_GK = P_GQ + GLA_DK
P_GV = P_GK + GLA_DK
P_GR = P_GV + GLA_DV
P_GATE = P_GR + GLA_DV
P_MAIN = P_GATE + 2 * D_MODEL
GLA_W_BLOCK = (O_GATE - GLA_RANK - O_GQ) // 2
N_KINDS = 3
PA_PAD = LANES


def _rms_norm_rows(x, gain):
    ms = jnp.mean(x * x, axis=-1, keepdims=True)
    return x * lax.rsqrt(ms + EPS) * gain


def _main_proj_kernel(x_ref, g1_ref, wa_ref, wb_ref, wgate_ref, wpa_ref, o_ref, pa_ref, *, rows):
    gain = g1_ref[...]
    for rc in range(x_ref.shape[0] // rows):
        rs = slice(rc * rows, (rc + 1) * rows)
        h = _rms_norm_rows(x_ref[rs, :], gain).astype(BF16)
        pa_ref[rs, :] = jnp.dot(h, wpa_ref[...], preferred_element_type=F32)
        col = 0
        for w_ref in (wa_ref, wb_ref, wgate_ref):
            width = w_ref.shape[1]
            o_ref[rs, col:col + width] = jnp.dot(
                h, w_ref[...], preferred_element_type=F32).astype(BF16)
            col += width


def _main_proj_call(x2d, g1, w_in_bf16, w_gate, w_pa, tm, rows):
    t = x2d.shape[0]
    first_block = O_GQ // GLA_W_BLOCK

    def const(shape, col_block=0):
        return pl.BlockSpec(shape, lambda i: (0, col_block), pipeline_mode=pl.Buffered(1))

    return pl.pallas_call(
        functools.partial(_main_proj_kernel, rows=rows),
        grid=(t // tm,),
        in_specs=[
            pl.BlockSpec((tm, D_MODEL), lambda i: (i, 0)),
            const((1, D_MODEL)),
            const((D_MODEL, GLA_W_BLOCK), first_block),
            const((D_MODEL, GLA_W_BLOCK), first_block + 1),
            const((D_MODEL, 2 * D_MODEL)),
            const((D_MODEL, PA_PAD)),
        ],
        out_specs=[
            pl.BlockSpec((tm, P_MAIN), lambda i: (i, 0)),
            pl.BlockSpec((tm, PA_PAD), lambda i: (i, 0)),
        ],
        out_shape=[
            jax.ShapeDtypeStruct((t, P_MAIN), BF16),
            jax.ShapeDtypeStruct((t, PA_PAD), F32),
        ],
        compiler_params=pltpu.CompilerParams(
            dimension_semantics=("parallel",),
            vmem_limit_bytes=VMEM_LIMIT_BYTES),
        name="proj_main",
    )(x2d, g1, w_in_bf16, w_in_bf16, w_gate, w_pa)


def _qkv_project(h, w_refs, gain_ref, bd, o_ref, chunk, residue_of_block=None):
    d = o_ref.shape[1]
    rows = h.shape[0] // d
    residue_of_block = residue_of_block or list(range(d))
    for kind in range(N_KINDS):
        cols = slice(kind * GROUP_WIDTH, (kind + 1) * GROUP_WIDTH)
        acc = jnp.dot(h, w_refs[kind][...], preferred_element_type=F32)
        if kind < 2:
            sq = acc * acc
            first_head = lax.broadcasted_iota(jnp.int32, (1, LANES), 1) < HEAD_DIM
            scales = []
            for pp in range(N_PAIRS):
                t = sq[:, pp * LANES:(pp + 1) * LANES]
                s_lo = jnp.sum(jnp.where(first_head, t, 0.0), axis=-1, keepdims=True)
                s_hi = jnp.sum(jnp.where(first_head, 0.0, t), axis=-1, keepdims=True)
                scales.append(jnp.where(first_head,
                                        lax.rsqrt(s_lo * (1.0 / HEAD_DIM) + EPS),
                                        lax.rsqrt(s_hi * (1.0 / HEAD_DIM) + EPS)))
            acc = acc * jnp.concatenate(scales, axis=1) * gain_ref[kind]
        y = acc.astype(BF16)
        for j, r in enumerate(residue_of_block):
            o_ref[0, r, chunk * rows:(chunk + 1) * rows, cols] = y[j * rows:(j + 1) * rows, :]


def _attn_proj_kernel(x_ref, g1_ref, *rest):
    n_w = N_KINDS * N_GROUPS
    w_all = rest[:n_w]
    gain_ref, bd_ref, o0_ref, o1_ref, o2_ref, col_scr, col2_scr, perm1_scr, perm2_scr = rest[n_w:]
    w_group = [[w_all[kind * N_GROUPS + g] for kind in range(N_KINDS)] for g in range(N_GROUPS)]
    n_chunks, _, tc, _ = col_scr.shape
    n_col = D_MODEL // LANES
    bd = bd_ref[...]
    gain = g1_ref[...]
    for ch in range(n_chunks):
        hf = _rms_norm_rows(x_ref[ch * tc:(ch + 1) * tc, :], gain)
        for c in range(n_col):
            col_scr[ch, c] = hf[:, c * LANES:(c + 1) * LANES]
        _qkv_project(hf.astype(BF16), w_group[0], gain_ref, bd, o0_ref, ch)
        f = o1_ref.shape[1]
        assert o2_ref.shape[1] == f * f
        rows1, rows2 = tc // f, tc // (f * f)
        for a in range(f):
            for c in range(n_col):
                t = col_scr[ch, c, pl.ds(a, rows1, stride=f), :]
                col2_scr[ch, c, a * rows1:(a + 1) * rows1, :] = t
                perm1_scr[ch, a * rows1:(a + 1) * rows1, c * LANES:(c + 1) * LANES] = t.astype(BF16)
        _qkv_project(perm1_scr[ch], w_group[1], gain_ref, bd, o1_ref, ch)
        for a in range(f):
            for b in range(f):
                blk = a * f + b
                for c in range(n_col):
                    perm2_scr[ch, blk * rows2:(blk + 1) * rows2, c * LANES:(c + 1) * LANES] = (
                        col2_scr[ch, c, pl.ds(a * rows1 + b, rows2, stride=f), :].astype(BF16))
        _qkv_project(perm2_scr[ch], w_group[2], gain_ref, bd, o2_ref, ch,
                     residue_of_block=[b * f + a for a in range(f) for b in range(f)])


def _attn_proj_call(x2d, g1, w_in_bf16, qk_gain, bd, batch, tm, rows):
    t = x2d.shape[0]
    s = t // batch
    tiles_per_seq = s // tm
    width = N_KINDS * GROUP_WIDTH
    n_w = N_KINDS * N_GROUPS

    def const(shape):
        return pl.BlockSpec(shape, lambda i: (0,) * len(shape), pipeline_mode=pl.Buffered(1))

    def w_block(j):
        return pl.BlockSpec((D_MODEL, GROUP_WIDTH), lambda i: (0, O_ATTN // GROUP_WIDTH + j),
                            pipeline_mode=pl.Buffered(1))

    return pl.pallas_call(
        _attn_proj_kernel,
        grid=(t // tm,),
        in_specs=[
            pl.BlockSpec((tm, D_MODEL), lambda i: (i, 0)),
            const((1, D_MODEL)),
            *[w_block(j) for j in range(n_w)],
            const((2, 1, GROUP_WIDTH)),
            const((MXU_DIM, MXU_DIM)),
        ],
        out_specs=[
            pl.BlockSpec((1, d, tm // d, width),
                         lambda i: (i // tiles_per_seq, 0, i % tiles_per_seq, 0))
            for _, d in ATTN_GROUPS],
        out_shape=[jax.ShapeDtypeStruct((batch, d, s // d, width), BF16) for _, d in ATTN_GROUPS],
        scratch_shapes=[
            pltpu.VMEM((tm // rows, D_MODEL // LANES, rows, LANES), F32),
            pltpu.VMEM((tm // rows, D_MODEL // LANES, rows, LANES), F32),
            pltpu.VMEM((tm // rows, rows, D_MODEL), BF16),
            pltpu.VMEM((tm // rows, rows, D_MODEL), BF16),
        ],
        compiler_params=pltpu.CompilerParams(
            dimension_semantics=("parallel",),
            vmem_limit_bytes=VMEM_LIMIT_BYTES),
        name="proj_attn",
    )(x2d, g1, *([w_in_bf16] * n_w), qk_gain, bd)


def _attn_kernel(q_ref, kp_ref, kc_ref, vp_ref, vc_ref, o_ref, lse_ref, *stage):
    n = pl.program_id(1)
    d = q_ref.shape[1]
    blk = ATTN_BLOCK
    n_sub = q_ref.shape[2] // blk
    stage_scr = stage[0] if stage else None
    f = FAST_STRIDE

    def put(idx, r, u, tile):
        if stage_scr is None:
            tok = pl.ds(u * blk * d + r, blk, stride=d)
            if idx == N_PAIRS:
                lse_ref[0, tok, :] = tile
            else:
                o_ref[0, idx, tok, :] = tile
        else:
            a, b = r % f, r // f
            stage_scr[idx, a, pl.ds(u * blk * (d // f) + b, blk, stride=d // f), :] = tile
    qi = lax.broadcasted_iota(jnp.int32, (blk, 2 * blk), 0)
    ki = lax.broadcasted_iota(jnp.int32, (blk, 2 * blk), 1)
    band = (ki >= qi) & (ki <= qi + blk)
    band_first = band & ((ki >= blk) | (n > 0))
    lane = lax.broadcasted_iota(jnp.int32, (blk, LANES), 1)
    first_head = lane < HEAD_DIM
    hb = ATTN_HEADS_PER_DOT
    width = hb * HEAD_DIM
    lane_w = lax.broadcasted_iota(jnp.int32, (blk, width), 1)
    head_lanes = [(lane_w >= h * HEAD_DIM) & (lane_w < (h + 1) * HEAD_DIM) for h in range(hb)]
    valid_by_sub = [jnp.concatenate([band_first if u == 0 else band] * hb, axis=0)
                    for u in range(min(n_sub, 2))]

    def unit(r, u):
        rows = slice(u * blk, (u + 1) * blk)
        valid_b = valid_by_sub[min(u, 1)]
        m_tile = jnp.zeros((blk, LANES), F32)
        l_tile = jnp.ones((blk, LANES), F32)
        for g in range(HEADS_PER_GROUP // hb):
            sl = slice(g * width, (g + 1) * width)
            q = q_ref[0, r, rows, sl]
            if u == 0:
                k_prev, v_prev = kp_ref[0, r, :, sl], vp_ref[0, r, :, sl]
            else:
                prev_rows = slice((u - 1) * blk, u * blk)
                k_prev, v_prev = kc_ref[0, r, prev_rows, sl], vc_ref[0, r, prev_rows, sl]
            k = jnp.concatenate([k_prev, kc_ref[0, r, rows, sl]], axis=0)
            v = jnp.concatenate([v_prev, vc_ref[0, r, rows, sl]], axis=0)
            zero = jnp.zeros_like(q)
            q_rows = jnp.concatenate([jnp.where(head_lanes[h], q, zero) for h in range(hb)],
                                     axis=0)
            s = lax.dot_general(q_rows, k, (((1,), (1,)), ((), ())), preferred_element_type=F32)
            s = jnp.where(valid_b, s, -jnp.inf)
            m = jnp.max(s, axis=-1, keepdims=True)
            p = jnp.exp2(s - m)
            l = jnp.sum(p, axis=-1, keepdims=True)
            p = p.astype(BF16)
            inv_l = 1.0 / l
            for h in range(hb):
                head = g * hb + h
                m_tile = jnp.where(lane == head, m[h * blk:(h + 1) * blk], m_tile)
                l_tile = jnp.where(lane == head, l[h * blk:(h + 1) * blk], l_tile)
            for pp in range(hb // 2):
                rows2 = slice(2 * pp * blk, (2 * pp + 2) * blk)
                pv = jnp.dot(p[rows2], v[:, pp * LANES:(pp + 1) * LANES],
                             preferred_element_type=F32) * inv_l[rows2]
                put(g * (hb // 2) + pp, r, u, jnp.where(first_head, pv[:blk], pv[blk:]))
        put(N_PAIRS, r, u, m_tile * LN_2 + jnp.log(l_tile))

    if d * n_sub <= ATTN_UNROLL:
        for r in range(d):
            for u in range(n_sub):
                unit(r, u)
        if stage_scr is not None:
            for a in range(f):
                merged = pl.ds(a, stage_scr.shape[2], stride=f)
                for idx in range(N_PAIRS):
                    o_ref[0, idx, merged, :] = stage_scr[idx, a]
                lse_ref[0, merged, :] = stage_scr[N_PAIRS, a]
    else:
        assert stage_scr is None
        def residue(r, carry):
            for u in range(n_sub):
                unit(r, u)
            return carry
        lax.fori_loop(0, d, residue, 0, unroll=ATTN_UNROLL // n_sub)


def _attn_call(qkv, dilation):
    b, d, sub_len, _ = qkv.shape
    assert d == dilation
    blk = ATTN_BLOCK
    n_sub = max(1, ATTN_UNROLL // d)
    rows = n_sub * blk
    steps = sub_len // rows
    s = sub_len * d

    def cur(kind):
        return lambda bi, n: (bi, 0, n, kind)

    def prev(kind):
        return lambda bi, n: (bi, 0, jnp.maximum(n * n_sub - 1, 0), kind)

    cur_shape = (1, d, rows, GROUP_WIDTH)
    prev_shape = (1, d, blk, GROUP_WIDTH)
    o, lse = pl.pallas_call(
        _attn_kernel,
        grid=(b, steps),
        in_specs=[
            pl.BlockSpec(cur_shape, cur(0)),
            pl.BlockSpec(prev_shape, prev(1)),
            pl.BlockSpec(cur_shape, cur(1)),
            pl.BlockSpec(prev_shape, prev(2)),
            pl.BlockSpec(cur_shape, cur(2)),
        ],
        out_specs=[
            pl.BlockSpec((1, N_PAIRS, d * rows, LANES), lambda bi, n: (bi, 0, n, 0)),
            pl.BlockSpec((1, d * rows, LANES), lambda bi, n: (bi, n, 0)),
        ],
        out_shape=[
            jax.ShapeDtypeStruct((b, N_PAIRS, s, LANES), F32),
            jax.ShapeDtypeStruct((b, s, LANES), F32),
        ],
        scratch_shapes=([pltpu.VMEM((N_PAIRS + 1, FAST_STRIDE, d * rows // FAST_STRIDE, LANES), F32)]
                        if d > FAST_STRIDE else []),
        compiler_params=pltpu.CompilerParams(
            dimension_semantics=("parallel", "arbitrary"),
            vmem_limit_bytes=VMEM_LIMIT_BYTES),
        name=f"attn_d{d}",
    )(qkv, qkv, qkv, qkv, qkv)
    return o, lse


def _log_sigmoid(x):
    return jnp.minimum(x, 0.0) - jnp.log(1.0 + jnp.exp(-jnp.abs(x)))


def _split3(x):
    hi = x.astype(BF16)
    r1 = x - hi.astype(F32)
    mid = r1.astype(BF16)
    lo = (r1 - mid.astype(F32)).astype(BF16)
    return hi, mid, lo


def _gla_pairwise_products(b_scr, qs_scr, kf_scr, pcat_scr):
    sub = GLA_SUB
    tl = lax.broadcasted_iota(jnp.int32, (sub, GLA_HK), 0)
    for pair_i in range(GLA_STEP // (2 * sub)):
        tiles = []
        for r0 in (2 * pair_i * sub, (2 * pair_i + 1) * sub):
            b_blk, q_blk = b_scr[r0:r0 + sub, :], qs_scr[r0:r0 + sub, :]
            row = []
            for s in range(sub):
                d = b_blk - b_scr[r0 + s:r0 + s + 1, :]
                if s > 0:
                    d = jnp.where(tl >= s, d, -jnp.inf)
                row.append(q_blk * kf_scr[r0 + s:r0 + s + 1, :] * jnp.exp2(d))
            tiles.append(row)
        r0 = 2 * pair_i * sub
        for s in range(sub):
            pcat_scr[r0:r0 + 2 * sub, s * GLA_HK:(s + 1) * GLA_HK] = (
                jnp.concatenate([tiles[0][s], tiles[1][s]], axis=0).astype(BF16))


def _gla_reference_factors(b, qs, kf):
    c_len = GLA_CHUNK
    n_chunks = GLA_STEP // c_len
    sub = GLA_SUB

    def z(nrows):
        return jnp.zeros((nrows, GLA_HK), F32)

    qb, kb = [], []
    for r0 in range(0, GLA_STEP, 2 * sub):
        mid = r0 + sub
        b_ref = b[mid - 1:mid]
        qb += [z(sub), qs[mid:mid + sub] * jnp.exp2(b[mid:mid + sub] - b_ref)]
        kb += [kf[r0:mid] * jnp.exp2(b_ref - b[r0:mid]), z(sub)]
    q_block = jnp.concatenate(qb, axis=0).astype(BF16)
    k_block = jnp.concatenate(kb, axis=0).astype(BF16)

    qw, kw = [], []
    for c in range(n_chunks):
        bc, qc, kc = (x[c * c_len:(c + 1) * c_len] for x in (b, qs, kf))
        b15, b31, b47 = bc[15:16], bc[31:32], bc[47:48]
        q1 = jnp.concatenate([z(16), qc[16:32] * jnp.exp2(bc[16:32] - b15), z(32)], axis=0)
        q2 = jnp.concatenate([z(32), qc[32:64] * jnp.exp2(bc[32:64] - b31)], axis=0)
        q3 = jnp.concatenate([z(48), qc[48:64] * jnp.exp2(bc[48:64] - b47)], axis=0)
        k1 = jnp.concatenate([kc[0:16] * jnp.exp2(b15 - bc[0:16]), z(48)], axis=0)
        k2 = jnp.concatenate([kc[0:32] * jnp.exp2(b31 - bc[0:32]), z(32)], axis=0)
        k3 = jnp.concatenate([z(32), kc[32:48] * jnp.exp2(b47 - bc[32:48]), z(16)], axis=0)
        qw.append(jnp.concatenate([q1, q2, q3], axis=1))
        kw.append(jnp.concatenate([k1, k2, k3], axis=1))
    q_within = jnp.concatenate(qw, axis=0).astype(BF16)
    k_within = jnp.concatenate(kw, axis=0).astype(BF16)

    qx, kx = [], []
    for j in range(n_chunks - 1):
        lo, hi = j * c_len, (j + 1) * c_len
        b_ref = b[hi - 1:hi]
        qx.append(jnp.concatenate([z(hi), qs[hi:] * jnp.exp2(b[hi:] - b_ref)], axis=0))
        parts = [kf[lo:hi] * jnp.exp2(b_ref - b[lo:hi])]
        if lo:
            parts.insert(0, z(lo))
        parts.append(z(GLA_STEP - hi))
        kx.append(jnp.concatenate(parts, axis=0))
    q_cross = jnp.concatenate(qx, axis=1).astype(BF16)
    k_cross = jnp.concatenate(kx, axis=1).astype(BF16)
    return (q_block, k_block), (q_within, k_within), (q_cross, k_cross)


def _gla_kernel(q_ref, k_ref, v_ref, r_ref, pa_ref, u_ref, bias_ref, gn_ref, ltri_ref, e_ref,
                o_ref, st_ref, b_scr, qs_scr, kf_scr, pcat_scr):
    @pl.when(pl.program_id(1) == 0)
    def _():
        st_ref[...] = jnp.zeros_like(st_ref)

    n = GLA_STEP

    def substep(ss, carry):
        rows = pl.ds(pl.multiple_of(ss * n, n), n)
        _gla_substep(q_ref.at[rows], k_ref.at[rows], v_ref.at[rows], r_ref.at[rows], pa_ref.at[rows],
                     u_ref, bias_ref, gn_ref, ltri_ref, e_ref, o_ref.at[rows], st_ref,
                     b_scr, qs_scr, kf_scr, pcat_scr)
        return carry

    lax.fori_loop(0, q_ref.shape[0] // n, substep, 0)


def _gla_substep(q_ref, k_ref, v_ref, r_ref, pa_ref, u_ref, bias_ref, gn_ref, ltri_ref, e_ref,
                 o_ref, st_ref, b_scr, qs_scr, kf_scr, pcat_scr):
    n = GLA_STEP
    nt = (((1,), (1,)), ((), ()))

    logits = jnp.dot(pa_ref[...].astype(BF16), u_ref[...], preferred_element_type=F32) + bias_ref[...]
    la = _log_sigmoid(logits) * (1.0 / GLA_TAU)
    parts = jnp.dot(ltri_ref[...], jnp.concatenate(_split3(la), axis=1), preferred_element_type=F32)
    b_all = (parts[:, :GLA_DK] + parts[:, GLA_DK:2 * GLA_DK] + parts[:, 2 * GLA_DK:]) * LOG2_E

    heads = []
    for h in range(GLA_HEADS):
        kcols = slice(h * GLA_HK, (h + 1) * GLA_HK)
        b = b_all[:, kcols]
        qs = q_ref[:, kcols].astype(F32) * (GLA_HK ** -0.5)
        kf = k_ref[:, kcols].astype(F32)
        b_scr[h], qs_scr[h], kf_scr[h] = b, qs, kf
        _gla_pairwise_products(b_scr.at[h], qs_scr.at[h], kf_scr.at[h], pcat_scr.at[h * n:(h + 1) * n])
        heads.append((b, qs, kf))
    a_diag = jnp.dot(pcat_scr[...], e_ref[...], preferred_element_type=F32)

    ri = lax.broadcasted_iota(jnp.int32, (n, n), 0)
    ci = lax.broadcasted_iota(jnp.int32, (n, n), 1)
    same_sub = (ri // GLA_SUB) == (ci // GLA_SUB)
    same_pair = (ri // (2 * GLA_SUB)) == (ci // (2 * GLA_SUB))
    same_chunk = (ri // GLA_CHUNK) == (ci // GLA_CHUNK)
    gn = gn_ref[...]

    for h, (b, qs, kf) in enumerate(heads):
        vcols = slice(h * GLA_HV, (h + 1) * GLA_HV)
        a_block, a_within, a_cross = (
            lax.dot_general(qf, kf_, nt, preferred_element_type=F32)
            for qf, kf_ in _gla_reference_factors(b, qs, kf))
        a = jnp.where(same_sub, a_diag[h * n:(h + 1) * n],
                      jnp.where(same_pair, a_block, jnp.where(same_chunk, a_within, a_cross)))
        v = v_ref[:, vcols]
        st = st_ref[h]
        b_last = b[n - 1:n]
        q_in = (qs * jnp.exp2(b)).astype(BF16)
        o = (jnp.dot(a.astype(BF16), v, preferred_element_type=F32)
             + lax.dot_general(q_in, st.astype(BF16), nt, preferred_element_type=F32))
        k_st = (kf * jnp.exp2(b_last - b)).astype(BF16)
        upd = lax.dot_general(v, k_st, (((0,), (0,)), ((), ())), preferred_element_type=F32)
        st_ref[h] = st * jnp.exp2(b_last) + upd

        ms = jnp.mean(o * o, axis=-1, keepdims=True)
        y = o * lax.rsqrt(ms + EPS) * gn
        r = r_ref[:, vcols].astype(F32)
        o_ref[:, vcols] = (y * (r * jax.nn.sigmoid(r))).astype(BF16)


def _gla_constants():
    n = GLA_STEP
    idx = np.arange(n)
    ltri = idx[:, None] >= idx[None, :]
    rows = np.arange(GLA_SUB * GLA_HK)
    cols = np.arange(n)
    e = (rows[:, None] // GLA_HK) == (cols[None, :] % GLA_SUB)
    return jnp.asarray(ltri, BF16), jnp.asarray(e, BF16)


def _gla_call(proj2d, pa, u_pad, bias, gn, batch):
    t = proj2d.shape[0]
    n = GLA_STEP
    rows = n * GLA_SUBSTEPS
    steps = t // batch // rows
    slots = GLA_HEADS
    ltri, e = _gla_constants()

    def tok(width, col_block):
        return pl.BlockSpec((rows, width), lambda bi, i: (bi * steps + i, col_block))

    def const(shape):
        return pl.BlockSpec(shape, lambda bi, i: (0, 0))

    return pl.pallas_call(
        _gla_kernel,
        grid=(batch, steps),
        in_specs=[
            tok(GLA_DK, P_GQ // GLA_DK),
            tok(GLA_DK, P_GK // GLA_DK),
            tok(GLA_DV, P_GV // GLA_DV),
            tok(GLA_DV, P_GR // GLA_DV),
            tok(PA_PAD, 0),
            const((PA_PAD, GLA_DK)),
            const((1, GLA_DK)),
            const((1, GLA_HV)),
            const((n, n)),
            const((GLA_SUB * GLA_HK, n)),
        ],
        out_specs=tok(GLA_DV, 0),
        out_shape=jax.ShapeDtypeStruct((t, GLA_DV), BF16),
        scratch_shapes=[
            pltpu.VMEM((GLA_HEADS, GLA_HV, GLA_HK), F32),
            pltpu.VMEM((slots, n, GLA_HK), F32),
            pltpu.VMEM((slots, n, GLA_HK), F32),
            pltpu.VMEM((slots, n, GLA_HK), F32),
            pltpu.VMEM((slots * n, GLA_SUB * GLA_HK), BF16),
        ],
        compiler_params=pltpu.CompilerParams(
            dimension_semantics=("parallel", "arbitrary"),
            vmem_limit_bytes=VMEM_LIMIT_BYTES),
        name="gla",
    )(proj2d, proj2d, proj2d, proj2d, pa, u_pad, bias, gn, ltri, e)


FF_CHUNK = 1024


def _out_kernel(x_ref, o0_ref, o1_ref, o2_ref, l0_ref, l1_ref, l2_ref, og_ref, gate_a_ref, gate_g_ref, gbias_ref,
                hx_ref, wa_ref, wb_ref, wo_ref, g2_ref, wup_ref, wdn_ref, out_ref):
    l0, l1, l2 = l0_ref[0], l1_ref[0], l2_ref[0]
    mx = jnp.maximum(jnp.maximum(l0, l1), l2)
    e0, e1, e2 = jnp.exp(l0 - mx), jnp.exp(l1 - mx), jnp.exp(l2 - mx)
    inv = 1.0 / (e0 + e1 + e2)
    hx = hx_ref[...]
    def pairs(ref):
        return jnp.concatenate([ref[0, p] for p in range(N_PAIRS)], axis=1)

    def expand(w):
        return jnp.dot(w.astype(BF16), hx, preferred_element_type=F32)

    o_attn = (expand(e0 * inv) * pairs(o0_ref) + expand(e1 * inv) * pairs(o1_ref)
              + expand(e2 * inv) * pairs(o2_ref))
    a = jnp.dot(o_attn.astype(BF16), wa_ref[...], preferred_element_type=F32)
    g = jnp.dot(og_ref[...], wb_ref[...], preferred_element_type=F32)
    gate_a = jax.nn.sigmoid(gate_a_ref[...].astype(F32) + gbias_ref[:, :D_MODEL])
    gate_g = jax.nn.sigmoid(gate_g_ref[...].astype(F32) + gbias_ref[:, D_MODEL:])
    mixed = gate_a * a + gate_g * g
    x1 = x_ref[...] + jnp.dot(mixed.astype(BF16), wo_ref[...], preferred_element_type=F32)

    ms = jnp.mean(x1 * x1, axis=-1, keepdims=True)
    h2 = (x1 * lax.rsqrt(ms + EPS) * g2_ref[...]).astype(BF16)
    hidden = []
    for c in range(D_FF // FF_CHUNK):
        u = jnp.dot(h2, wup_ref[:, c * FF_CHUNK:(c + 1) * FF_CHUNK], preferred_element_type=F32)
        u = jnp.maximum(u, 0.0)
        hidden.append((u * u).astype(BF16))
    out_ref[...] = x1 + jnp.dot(jnp.concatenate(hidden, axis=1), wdn_ref[...],
                                preferred_element_type=F32)


def _out_call(x2d, o_groups, lse_groups, o_gla, proj2d, gbias, wa, wb, wo, g2, wup, wdn, batch, tm):
    t = x2d.shape[0]
    lanes = np.arange(LANES)
    cols = np.arange(GROUP_WIDTH)
    head_expand = jnp.asarray(lanes[:, None] == cols[None, :] // HEAD_DIM, BF16)

    tiles_per_seq = t // batch // tm

    def tok(width):
        return pl.BlockSpec((tm, width), lambda i: (i, 0))

    attn_o = pl.BlockSpec((1, N_PAIRS, tm, LANES),
                          lambda i: (i // tiles_per_seq, 0, i % tiles_per_seq, 0))
    attn_lse = pl.BlockSpec((1, tm, LANES), lambda i: (i // tiles_per_seq, i % tiles_per_seq, 0))

    def const(shape):
        return pl.BlockSpec(shape, lambda i: (0, 0), pipeline_mode=pl.Buffered(1))

    return pl.pallas_call(
        _out_kernel,
        grid=(t // tm,),
        in_specs=[
            tok(D_MODEL),
            attn_o, attn_o, attn_o,
            attn_lse, attn_lse, attn_lse,
            tok(GLA_DV),
            pl.BlockSpec((tm, D_MODEL), lambda i: (i, P_GATE // D_MODEL)),
            pl.BlockSpec((tm, D_MODEL), lambda i: (i, P_GATE // D_MODEL + 1)),
            const((1, 2 * D_MODEL)),
            const((LANES, GROUP_WIDTH)),
            const((GROUP_WIDTH, D_MODEL)),
            const((GLA_DV, D_MODEL)),
            const((D_MODEL, D_MODEL)),
            const((1, D_MODEL)),
            const((D_MODEL, D_FF)),
            const((D_FF, D_MODEL)),
        ],
        out_specs=tok(D_MODEL),
        out_shape=jax.ShapeDtypeStruct((t, D_MODEL), F32),
        compiler_params=pltpu.CompilerParams(
            dimension_semantics=("parallel",),
            vmem_limit_bytes=VMEM_LIMIT_BYTES),
        name="out",
    )(x2d, *o_groups, *lse_groups, o_gla, proj2d, proj2d, gbias, head_expand, wa, wb, wo, g2, wup, wdn)


def _layer(x2d, batch, norm1_g, w_in, gq, gk, gate_up, gate_bias, gla_norm_g, branch_bias,
           w_a, w_b, w_out, norm2_g, w_up, w_down, out_tm=512):
    g1 = norm1_g.reshape(1, D_MODEL)
    w_in_bf16 = w_in.astype(BF16)
    w_gate = w_in_bf16[:, O_GATE:O_GATE + 2 * D_MODEL]
    w_pa = jnp.pad(w_in_bf16[:, O_PA:O_PA + GLA_RANK], ((0, 0), (0, PA_PAD - GLA_RANK)))
    q_gain = jnp.tile(gq, HEADS_PER_GROUP) * (HEAD_DIM ** -0.5 * LOG2_E)
    k_gain = jnp.tile(gk, HEADS_PER_GROUP)
    qk_gain = jnp.stack([q_gain, k_gain]).reshape(2, 1, GROUP_WIDTH)
    idx = np.arange(MXU_DIM)
    bd = jnp.asarray((idx[:, None] // HEAD_DIM) == (idx[None, :] // HEAD_DIM), BF16)

    proj, pa = _main_proj_call(x2d, g1, w_in_bf16, w_gate, w_pa, tm=1024, rows=512)
    qkv_groups = _attn_proj_call(x2d, g1, w_in_bf16, qk_gain, bd, batch, tm=1024, rows=512)

    o_groups, lse_groups = [], []
    for qkv, (_, dilation) in zip(qkv_groups, ATTN_GROUPS):
        o, lse = _attn_call(qkv, dilation)
        o_groups.append(o)
        lse_groups.append(lse)

    u_pad = jnp.pad(gate_up, ((0, PA_PAD - GLA_RANK), (0, 0))).astype(BF16)
    o_gla = _gla_call(proj, pa, u_pad, gate_bias.reshape(1, GLA_DK),
                      gla_norm_g.reshape(1, GLA_HV), batch)

    return _out_call(x2d, o_groups, lse_groups, o_gla, proj,
                     branch_bias.reshape(1, 2 * D_MODEL),
                     w_a.astype(BF16), w_b.astype(BF16), w_out.astype(BF16),
                     norm2_g.reshape(1, D_MODEL), w_up.astype(BF16), w_down.astype(BF16), batch, out_tm)


def kernel(x, norm1_g, w_in, attn_q_norm_g, attn_k_norm_g, gla_gate_up, gla_gate_bias, gla_out_norm_g, branch_gate_bias, w_attn_branch, w_gla_branch, w_out, norm2_g, w_ff_up, w_ff_down):
    b, s, d = x.shape
    x2d = x.reshape(b * s, d)
    for l in range(norm1_g.shape[0]):
        x2d = _layer(x2d, b, norm1_g[l], w_in[l], attn_q_norm_g[l], attn_k_norm_g[l],
                     gla_gate_up[l], gla_gate_bias[l], gla_out_norm_g[l], branch_gate_bias[l],
                     w_attn_branch[l], w_gla_branch[l], w_out[l], norm2_g[l],
                     w_ff_up[l], w_ff_down[l])
    return x2d.reshape(b, s, d)
```

```python
import functools

import numpy as np
import jax
import jax.numpy as jnp
from jax import lax
from jax.experimental import pallas as pl
from jax.experimental.pallas import tpu as pltpu

F32 = jnp.float32
BF16 = jnp.bfloat16

D_MODEL = 1024
ATTN_GROUPS = ((128, 1), (512, 4), (2048, 16))
N_GROUPS = len(ATTN_GROUPS)
HEADS_PER_GROUP = 8
HEAD_DIM = 64
ATTN_BLOCK = 128
GROUP_WIDTH = HEADS_PER_GROUP * HEAD_DIM
ATTN_WIDTH = 3 * N_GROUPS * GROUP_WIDTH
N_PAIRS = GROUP_WIDTH // 128
ATTN_HEADS_PER_DOT = 4
ATTN_UNROLL = 16

GLA_HEADS = 4
GLA_DK = 512
GLA_DV = 1024
GLA_HK = GLA_DK // GLA_HEADS
GLA_HV = GLA_DV // GLA_HEADS
GLA_RANK = 16
GLA_TAU = 16.0
GLA_CHUNK = 64
GLA_SUB = 8
GLA_STEP = 256
GLA_SUBSTEPS = 4

D_FF = 4 * D_MODEL
EPS = 1e-6
LOG2_E = 1.4426950408889634
LN_2 = 0.6931471805599453

FAST_STRIDE = 4
LANES = 128
VMEM_LIMIT_BYTES = 56 * 1024 * 1024

_ORIG_SIZES = (ATTN_WIDTH, GLA_DK, GLA_DK, GLA_DV, GLA_DV, GLA_RANK, 2 * D_MODEL)
_ORIG_OFF = tuple(int(v) for v in np.cumsum((0,) + _ORIG_SIZES))
O_ATTN, O_GQ, O_GK, O_GV, O_GR, O_PA, O_GATE = _ORIG_OFF[:7]

P_GQ = 0
P_GK = P_GQ + GLA_DK
P_GV = P_GK + GLA_DK
P_GR = P_GV + GLA_DV
P_GATE = P_GR + GLA_DV
P_MAIN = P_GATE + 2 * D_MODEL
GLA_W_BLOCK = (O_GATE - GLA_RANK - O_GQ) // 2
N_KINDS = 3
PA_PAD = LANES


def _rms_norm_rows(x, gain):
    ms = jnp.mean(x * x, axis=-1, keepdims=True)
    return x * lax.rsqrt(ms + EPS) * gain


def _main_proj_kernel(x_ref, g1_ref, wa_ref, wb_ref, wgate_ref, wpa_ref, o_ref, pa_ref, *, rows):
    gain = g1_ref[...]
    for rc in range(x_ref.shape[0] // rows):
        rs = slice(rc * rows, (rc + 1) * rows)
        h = _rms_norm_rows(x_ref[rs, :], gain).astype(BF16)
        pa_ref[rs, :] = jnp.dot(h, wpa_ref[...], preferred_element_type=F32)
        col = 0
        for w_ref in (wa_ref, wb_ref, wgate_ref):
            width = w_ref.shape[1]
            o_ref[rs, col:col + width] = jnp.dot(
                h, w_ref[...], preferred_element_type=F32).astype(BF16)
            col += width


def _main_proj_call(x2d, g1, w_in_bf16, w_gate, w_pa, tm, rows):
    t = x2d.shape[0]
    first_block = O_GQ // GLA_W_BLOCK

    def const(shape, col_block=0):
        return pl.BlockSpec(shape, lambda i: (0, col_block), pipeline_mode=pl.Buffered(1))

    return pl.pallas_call(
        functools.partial(_main_proj_kernel, rows=rows),
        grid=(t // tm,),
        in_specs=[
            pl.BlockSpec((tm, D_MODEL), lambda i: (i, 0)),
            const((1, D_MODEL)),
            const((D_MODEL, GLA_W_BLOCK), first_block),
            const((D_MODEL, GLA_W_BLOCK), first_block + 1),
            const((D_MODEL, 2 * D_MODEL)),
            const((D_MODEL, PA_PAD)),
        ],
        out_specs=[
            pl.BlockSpec((tm, P_MAIN), lambda i: (i, 0)),
            pl.BlockSpec((tm, PA_PAD), lambda i: (i, 0)),
        ],
        out_shape=[
            jax.ShapeDtypeStruct((t, P_MAIN), BF16),
            jax.ShapeDtypeStruct((t, PA_PAD), F32),
        ],
        compiler_params=pltpu.CompilerParams(
            dimension_semantics=("parallel",),
            vmem_limit_bytes=VMEM_LIMIT_BYTES),
        name="proj_main",
    )(x2d, g1, w_in_bf16, w_in_bf16, w_gate, w_pa)


def _qkv_project(h, w_refs, gain_ref, o_ref, chunk, residue_of_block=None):
    d = o_ref.shape[1]
    rows = h.shape[0] // d
    residue_of_block = residue_of_block or list(range(d))
    for kind in range(N_KINDS):
        cols = slice(kind * GROUP_WIDTH, (kind + 1) * GROUP_WIDTH)
        acc = jnp.dot(h, w_refs[kind][...], preferred_element_type=F32)
        if kind < 2:
            sq = acc * acc
            first_head = lax.broadcasted_iota(jnp.int32, (1, LANES), 1) < HEAD_DIM
            scales = []
            for pp in range(N_PAIRS):
                t = sq[:, pp * LANES:(pp + 1) * LANES]
                s_lo = jnp.sum(jnp.where(first_head, t, 0.0), axis=-1, keepdims=True)
                s_hi = jnp.sum(jnp.where(first_head, 0.0, t), axis=-1, keepdims=True)
                scales.append(jnp.where(first_head,
                                        lax.rsqrt(s_lo * (1.0 / HEAD_DIM) + EPS),
                                        lax.rsqrt(s_hi * (1.0 / HEAD_DIM) + EPS)))
            acc = acc * jnp.concatenate(scales, axis=1) * gain_ref[kind]
        y = acc.astype(BF16)
        for j, r in enumerate(residue_of_block):
            o_ref[0, r, chunk * rows:(chunk + 1) * rows, cols] = y[j * rows:(j + 1) * rows, :]


def _attn_proj_kernel(x_ref, g1_ref, *rest):
    n_w = N_KINDS * N_GROUPS
    w_all = rest[:n_w]
    gain_ref, o0_ref, o1_ref, o2_ref, col_scr, col2_scr, perm1_scr, perm2_scr = rest[n_w:]
    w_group = [[w_all[kind * N_GROUPS + g] for kind in range(N_KINDS)] for g in range(N_GROUPS)]
    n_chunks, _, tc, _ = col_scr.shape
    n_col = D_MODEL // LANES
    gain = g1_ref[...]
    for ch in range(n_chunks):
        hf = _rms_norm_rows(x_ref[ch * tc:(ch + 1) * tc, :], gain)
        for c in range(n_col):
            col_scr[ch, c] = hf[:, c * LANES:(c + 1) * LANES]
        _qkv_project(hf.astype(BF16), w_group[0], gain_ref, o0_ref, ch)
        f = o1_ref.shape[1]
        assert o2_ref.shape[1] == f * f
        rows1, rows2 = tc // f, tc // (f * f)
        for a in range(f):
            for c in range(n_col):
                t = col_scr[ch, c, pl.ds(a, rows1, stride=f), :]
                col2_scr[ch, c, a * rows1:(a + 1) * rows1, :] = t
                perm1_scr[ch, a * rows1:(a + 1) * rows1, c * LANES:(c + 1) * LANES] = t.astype(BF16)
        _qkv_project(perm1_scr[ch], w_group[1], gain_ref, o1_ref, ch)
        for a in range(f):
            for b in range(f):
                blk = a * f + b
                for c in range(n_col):
                    perm2_scr[ch, blk * rows2:(blk + 1) * rows2, c * LANES:(c + 1) * LANES] = (
                        col2_scr[ch, c, pl.ds(a * rows1 + b, rows2, stride=f), :].astype(BF16))
        _qkv_project(perm2_scr[ch], w_group[2], gain_ref, o2_ref, ch,
                     residue_of_block=[b * f + a for a in range(f) for b in range(f)])


def _attn_proj_call(x2d, g1, w_in_bf16, qk_gain, batch, tm, rows):
    t = x2d.shape[0]
    s = t // batch
    tiles_per_seq = s // tm
    width = N_KINDS * GROUP_WIDTH
    n_w = N_KINDS * N_GROUPS

    def const(shape):
        return pl.BlockSpec(shape, lambda i: (0,) * len(shape), pipeline_mode=pl.Buffered(1))

    def w_block(j):
        return pl.BlockSpec((D_MODEL, GROUP_WIDTH), lambda i: (0, O_ATTN // GROUP_WIDTH + j),
                            pipeline_mode=pl.Buffered(1))

    return pl.pallas_call(
        _attn_proj_kernel,
        grid=(t // tm,),
        in_specs=[
            pl.BlockSpec((tm, D_MODEL), lambda i: (i, 0)),
            const((1, D_MODEL)),
            *[w_block(j) for j in range(n_w)],
            const((2, 1, GROUP_WIDTH)),
        ],
        out_specs=[
            pl.BlockSpec((1, d, tm // d, width),
                         lambda i: (i // tiles_per_seq, 0, i % tiles_per_seq, 0))
            for _, d in ATTN_GROUPS],
        out_shape=[jax.ShapeDtypeStruct((batch, d, s // d, width), BF16) for _, d in ATTN_GROUPS],
        scratch_shapes=[
            pltpu.VMEM((tm // rows, D_MODEL // LANES, rows, LANES), F32),
            pltpu.VMEM((tm // rows, D_MODEL // LANES, rows, LANES), F32),
            pltpu.VMEM((tm // rows, rows, D_MODEL), BF16),
            pltpu.VMEM((tm // rows, rows, D_MODEL), BF16),
        ],
        compiler_params=pltpu.CompilerParams(
            dimension_semantics=("parallel",),
            vmem_limit_bytes=VMEM_LIMIT_BYTES),
        name="proj_attn",
    )(x2d, g1, *([w_in_bf16] * n_w), qk_gain)


def _attn_kernel(q_ref, kp_ref, kc_ref, vp_ref, vc_ref, o_ref, lse_ref, *stage):
    n = pl.program_id(1)
    d = q_ref.shape[1]
    blk = ATTN_BLOCK
    n_sub = q_ref.shape[2] // blk
    stage_scr = stage[0] if stage else None
    f = FAST_STRIDE

    def put(idx, r, u, tile):
        if stage_scr is None:
            tok = pl.ds(u * blk * d + r, blk, stride=d)
            if idx == N_PAIRS:
                lse_ref[0, tok, :] = tile
            else:
                o_ref[0, idx, tok, :] = tile
        else:
            a, b = r % f, r // f
            stage_scr[idx, a, pl.ds(u * blk * (d // f) + b, blk, stride=d // f), :] = tile
    qi = lax.broadcasted_iota(jnp.int32, (blk, 2 * blk), 0)
    ki = lax.broadcasted_iota(jnp.int32, (blk, 2 * blk), 1)
    band = (ki >= qi) & (ki <= qi + blk)
    band_first = band & ((ki >= blk) | (n > 0))
    lane = lax.broadcasted_iota(jnp.int32, (blk, LANES), 1)
    first_head = lane < HEAD_DIM
    hb = ATTN_HEADS_PER_DOT
    width = hb * HEAD_DIM
    lane_w = lax.broadcasted_iota(jnp.int32, (blk, width), 1)
    head_lanes = [(lane_w >= h * HEAD_DIM) & (lane_w < (h + 1) * HEAD_DIM) for h in range(hb)]
    valid_by_sub = [jnp.concatenate([band_first if u == 0 else band] * hb, axis=0)
                    for u in range(min(n_sub, 2))]

    def unit(r, u):
        rows = slice(u * blk, (u + 1) * blk)
        valid_b = valid_by_sub[min(u, 1)]
        m_tile = jnp.zeros((blk, LANES), F32)
        l_tile = jnp.ones((blk, LANES), F32)
        for g in range(HEADS_PER_GROUP // hb):
            sl = slice(g * width, (g + 1) * width)
            q = q_ref[0, r, rows, sl]
            if u == 0:
                k_prev, v_prev = kp_ref[0, r, :, sl], vp_ref[0, r, :, sl]
            else:
                prev_rows = slice((u - 1) * blk, u * blk)
                k_prev, v_prev = kc_ref[0, r, prev_rows, sl], vc_ref[0, r, prev_rows, sl]
            k = jnp.concatenate([k_prev, kc_ref[0, r, rows, sl]], axis=0)
            v = jnp.concatenate([v_prev, vc_ref[0, r, rows, sl]], axis=0)
            zero = jnp.zeros_like(q)
            q_rows = jnp.concatenate([jnp.where(head_lanes[h], q, zero) for h in range(hb)],
                                     axis=0)
            s = lax.dot_general(q_rows, k, (((1,), (1,)), ((), ())), preferred_element_type=F32)
            s = jnp.where(valid_b, s, -jnp.inf)
            m = jnp.max(s, axis=-1, keepdims=True)
            p = jnp.exp2(s - m)
            l = jnp.sum(p, axis=-1, keepdims=True)
            p = p.astype(BF16)
            inv_l = 1.0 / l
            for h in range(hb):
                head = g * hb + h
                m_tile = jnp.where(lane == head, m[h * blk:(h + 1) * blk], m_tile)
                l_tile = jnp.where(lane == head, l[h * blk:(h + 1) * blk], l_tile)
            for pp in range(hb // 2):
                rows2 = slice(2 * pp * blk, (2 * pp + 2) * blk)
                pv = jnp.dot(p[rows2], v[:, pp * LANES:(pp + 1) * LANES],
                             preferred_element_type=F32) * inv_l[rows2]
                put(g * (hb // 2) + pp, r, u, jnp.where(first_head, pv[:blk], pv[blk:]))
        put(N_PAIRS, r, u, m_tile * LN_2 + jnp.log(l_tile))

    if d * n_sub <= ATTN_UNROLL:
        for r in range(d):
            for u in range(n_sub):
                unit(r, u)
        if stage_scr is not None:
            for a in range(f):
                merged = pl.ds(a, stage_scr.shape[2], stride=f)
                for idx in range(N_PAIRS):
                    o_ref[0, idx, merged, :] = stage_scr[idx, a]
                lse_ref[0, merged, :] = stage_scr[N_PAIRS, a]
    else:
        assert stage_scr is None
        def residue(r, carry):
            for u in range(n_sub):
                unit(r, u)
            return carry
        lax.fori_loop(0, d, residue, 0, unroll=ATTN_UNROLL // n_sub)


def _attn_call(qkv, dilation):
    b, d, sub_len, _ = qkv.shape
    assert d == dilation
    blk = ATTN_BLOCK
    n_sub = max(1, ATTN_UNROLL // d)
    rows = n_sub * blk
    steps = sub_len // rows
    s = sub_len * d

    def cur(kind):
        return lambda bi, n: (bi, 0, n, kind)

    def prev(kind):
        return lambda bi, n: (bi, 0, jnp.maximum(n * n_sub - 1, 0), kind)

    cur_shape = (1, d, rows, GROUP_WIDTH)
    prev_shape = (1, d, blk, GROUP_WIDTH)
    o, lse = pl.pallas_call(
        _attn_kernel,
        grid=(b, steps),
        in_specs=[
            pl.BlockSpec(cur_shape, cur(0)),
            pl.BlockSpec(prev_shape, prev(1)),
            pl.BlockSpec(cur_shape, cur(1)),
            pl.BlockSpec(prev_shape, prev(2)),
            pl.BlockSpec(cur_shape, cur(2)),
        ],
        out_specs=[
            pl.BlockSpec((1, N_PAIRS, d * rows, LANES), lambda bi, n: (bi, 0, n, 0)),
            pl.BlockSpec((1, d * rows, LANES), lambda bi, n: (bi, n, 0)),
        ],
        out_shape=[
            jax.ShapeDtypeStruct((b, N_PAIRS, s, LANES), F32),
            jax.ShapeDtypeStruct((b, s, LANES), F32),
        ],
        scratch_shapes=([pltpu.VMEM((N_PAIRS + 1, FAST_STRIDE, d * rows // FAST_STRIDE, LANES), F32)]
                        if d > FAST_STRIDE else []),
        compiler_params=pltpu.CompilerParams(
            dimension_semantics=("parallel", "arbitrary"),
            vmem_limit_bytes=VMEM_LIMIT_BYTES),
        name=f"attn_d{d}",
    )(qkv, qkv, qkv, qkv, qkv)
    return o, lse


def _log_sigmoid(x):
    return jnp.minimum(x, 0.0) - jnp.log(1.0 + jnp.exp(-jnp.abs(x)))


def _split3(x):
    hi = x.astype(BF16)
    r1 = x - hi.astype(F32)
    mid = r1.astype(BF16)
    lo = (r1 - mid.astype(F32)).astype(BF16)
    return hi, mid, lo


def _gla_pairwise_products(b_scr, qs_scr, kf_scr, pcat_scr):
    sub = GLA_SUB
    tl = lax.broadcasted_iota(jnp.int32, (sub, GLA_HK), 0)
    for pair_i in range(GLA_STEP // (2 * sub)):
        tiles = []
        for r0 in (2 * pair_i * sub, (2 * pair_i + 1) * sub):
            b_blk, q_blk = b_scr[r0:r0 + sub, :], qs_scr[r0:r0 + sub, :]
            row = []
            for s in range(sub):
                d = b_blk - b_scr[r0 + s:r0 + s + 1, :]
                if s > 0:
                    d = jnp.where(tl >= s, d, -jnp.inf)
                row.append(q_blk * kf_scr[r0 + s:r0 + s + 1, :] * jnp.exp2(d))
            tiles.append(row)
        r0 = 2 * pair_i * sub
        for s in range(sub):
            pcat_scr[r0:r0 + 2 * sub, s * GLA_HK:(s + 1) * GLA_HK] = (
                jnp.concatenate([tiles[0][s], tiles[1][s]], axis=0).astype(BF16))


def _gla_reference_factors(b, qs, kf):
    c_len = GLA_CHUNK
    n_chunks = GLA_STEP // c_len
    sub = GLA_SUB

    def z(nrows):
        return jnp.zeros((nrows, GLA_HK), F32)

    qb, kb = [], []
    for r0 in range(0, GLA_STEP, 2 * sub):
        mid = r0 + sub
        b_ref = b[mid - 1:mid]
        qb += [z(sub), qs[mid:mid + sub] * jnp.exp2(b[mid:mid + sub] - b_ref)]
        kb += [kf[r0:mid] * jnp.exp2(b_ref - b[r0:mid]), z(sub)]
    q_block = jnp.concatenate(qb, axis=0).astype(BF16)
    k_block = jnp.concatenate(kb, axis=0).astype(BF16)

    qw, kw = [], []
    for c in range(n_chunks):
        bc, qc, kc = (x[c * c_len:(c + 1) * c_len] for x in (b, qs, kf))
        b15, b31, b47 = bc[15:16], bc[31:32], bc[47:48]
        q1 = jnp.concatenate([z(16), qc[16:32] * jnp.exp2(bc[16:32] - b15), z(32)], axis=0)
        q2 = jnp.concatenate([z(32), qc[32:64] * jnp.exp2(bc[32:64] - b31)], axis=0)
        q3 = jnp.concatenate([z(48), qc[48:64] * jnp.exp2(bc[48:64] - b47)], axis=0)
        k1 = jnp.concatenate([kc[0:16] * jnp.exp2(b15 - bc[0:16]), z(48)], axis=0)
        k2 = jnp.concatenate([kc[0:32] * jnp.exp2(b31 - bc[0:32]), z(32)], axis=0)
        k3 = jnp.concatenate([z(32), kc[32:48] * jnp.exp2(b47 - bc[32:48]), z(16)], axis=0)
        qw.append(jnp.concatenate([q1, q2, q3], axis=1))
        kw.append(jnp.concatenate([k1, k2, k3], axis=1))
    q_within = jnp.concatenate(qw, axis=0).astype(BF16)
    k_within = jnp.concatenate(kw, axis=0).astype(BF16)

    qx, kx = [], []
    for j in range(n_chunks - 1):
        lo, hi = j * c_len, (j + 1) * c_len
        b_ref = b[hi - 1:hi]
        qx.append(jnp.concatenate([z(hi), qs[hi:] * jnp.exp2(b[hi:] - b_ref)], axis=0))
        parts = [kf[lo:hi] * jnp.exp2(b_ref - b[lo:hi])]
        if lo:
            parts.insert(0, z(lo))
        parts.append(z(GLA_STEP - hi))
        kx.append(jnp.concatenate(parts, axis=0))
    q_cross = jnp.concatenate(qx, axis=1).astype(BF16)
    k_cross = jnp.concatenate(kx, axis=1).astype(BF16)
    return (q_block, k_block), (q_within, k_within), (q_cross, k_cross)


def _gla_kernel(q_ref, k_ref, v_ref, r_ref, pa_ref, u_ref, bias_ref, gn_ref, ltri_ref, e_ref,
                o_ref, st_ref, b_scr, qs_scr, kf_scr, pcat_scr):
    @pl.when(pl.program_id(1) == 0)
    def _():
        st_ref[...] = jnp.zeros_like(st_ref)

    n = GLA_STEP

    def substep(ss, carry):
        rows = pl.ds(pl.multiple_of(ss * n, n), n)
        _gla_substep(q_ref.at[rows], k_ref.at[rows], v_ref.at[rows], r_ref.at[rows], pa_ref.at[rows],
                     u_ref, bias_ref, gn_ref, ltri_ref, e_ref, o_ref.at[rows], st_ref,
                     b_scr, qs_scr, kf_scr, pcat_scr)
        return carry

    lax.fori_loop(0, q_ref.shape[0] // n, substep, 0)


def _gla_substep(q_ref, k_ref, v_ref, r_ref, pa_ref, u_ref, bias_ref, gn_ref, ltri_ref, e_ref,
                 o_ref, st_ref, b_scr, qs_scr, kf_scr, pcat_scr):
    n = GLA_STEP
    nt = (((1,), (1,)), ((), ()))

    logits = jnp.dot(pa_ref[...].astype(BF16), u_ref[...], preferred_element_type=F32) + bias_ref[...]
    la = _log_sigmoid(logits) * (1.0 / GLA_TAU)
    parts = jnp.dot(ltri_ref[...], jnp.concatenate(_split3(la), axis=1), preferred_element_type=F32)
    b_all = (parts[:, :GLA_DK] + parts[:, GLA_DK:2 * GLA_DK] + parts[:, 2 * GLA_DK:]) * LOG2_E

    heads = []
    for h in range(GLA_HEADS):
        kcols = slice(h * GLA_HK, (h + 1) * GLA_HK)
        b = b_all[:, kcols]
        qs = q_ref[:, kcols].astype(F32) * (GLA_HK ** -0.5)
        kf = k_ref[:, kcols].astype(F32)
        b_scr[h], qs_scr[h], kf_scr[h] = b, qs, kf
        _gla_pairwise_products(b_scr.at[h], qs_scr.at[h], kf_scr.at[h], pcat_scr.at[h * n:(h + 1) * n])
        heads.append((b, qs, kf))
    a_diag = jnp.dot(pcat_scr[...], e_ref[...], preferred_element_type=F32)

    ri = lax.broadcasted_iota(jnp.int32, (n, n), 0)
    ci = lax.broadcasted_iota(jnp.int32, (n, n), 1)
    same_sub = (ri // GLA_SUB) == (ci // GLA_SUB)
    same_pair = (ri // (2 * GLA_SUB)) == (ci // (2 * GLA_SUB))
    same_chunk = (ri // GLA_CHUNK) == (ci // GLA_CHUNK)
    gn = gn_ref[...]

    for h, (b, qs, kf) in enumerate(heads):
        vcols = slice(h * GLA_HV, (h + 1) * GLA_HV)
        a_block, a_within, a_cross = (
            lax.dot_general(qf, kf_, nt, preferred_element_type=F32)
            for qf, kf_ in _gla_reference_factors(b, qs, kf))
        a = jnp.where(same_sub, a_diag[h * n:(h + 1) * n],
                      jnp.where(same_pair, a_block, jnp.where(same_chunk, a_within, a_cross)))
        v = v_ref[:, vcols]
        st = st_ref[h]
        b_last = b[n - 1:n]
        q_in = (qs * jnp.exp2(b)).astype(BF16)
        o = (jnp.dot(a.astype(BF16), v, preferred_element_type=F32)
             + lax.dot_general(q_in, st.astype(BF16), nt, preferred_element_type=F32))
        k_st = (kf * jnp.exp2(b_last - b)).astype(BF16)
        upd = lax.dot_general(v, k_st, (((0,), (0,)), ((), ())), preferred_element_type=F32)
        st_ref[h] = st * jnp.exp2(b_last) + upd

        ms = jnp.mean(o * o, axis=-1, keepdims=True)
        y = o * lax.rsqrt(ms + EPS) * gn
        r = r_ref[:, vcols].astype(F32)
        o_ref[:, vcols] = (y * (r * jax.nn.sigmoid(r))).astype(BF16)


def _gla_constants():
    n = GLA_STEP
    idx = np.arange(n)
    ltri = idx[:, None] >= idx[None, :]
    rows = np.arange(GLA_SUB * GLA_HK)
    cols = np.arange(n)
    e = (rows[:, None] // GLA_HK) == (cols[None, :] % GLA_SUB)
    return jnp.asarray(ltri, BF16), jnp.asarray(e, BF16)


def _gla_call(proj2d, pa, u_pad, bias, gn, batch):
    t = proj2d.shape[0]
    n = GLA_STEP
    rows = n * GLA_SUBSTEPS
    steps = t // batch // rows
    slots = GLA_HEADS
    ltri, e = _gla_constants()

    def tok(width, col_block):
        return pl.BlockSpec((rows, width), lambda bi, i: (bi * steps + i, col_block))

    def const(shape):
        return pl.BlockSpec(shape, lambda bi, i: (0, 0))

    return pl.pallas_call(
        _gla_kernel,
        grid=(batch, steps),
        in_specs=[
            tok(GLA_DK, P_GQ // GLA_DK),
            tok(GLA_DK, P_GK // GLA_DK),
            tok(GLA_DV, P_GV // GLA_DV),
            tok(GLA_DV, P_GR // GLA_DV),
            tok(PA_PAD, 0),
            const((PA_PAD, GLA_DK)),
            const((1, GLA_DK)),
            const((1, GLA_HV)),
            const((n, n)),
            const((GLA_SUB * GLA_HK, n)),
        ],
        out_specs=tok(GLA_DV, 0),
        out_shape=jax.ShapeDtypeStruct((t, GLA_DV), BF16),
        scratch_shapes=[
            pltpu.VMEM((GLA_HEADS, GLA_HV, GLA_HK), F32),
            pltpu.VMEM((slots, n, GLA_HK), F32),
            pltpu.VMEM((slots, n, GLA_HK), F32),
            pltpu.VMEM((slots, n, GLA_HK), F32),
            pltpu.VMEM((slots * n, GLA_SUB * GLA_HK), BF16),
        ],
        compiler_params=pltpu.CompilerParams(
            dimension_semantics=("parallel", "arbitrary"),
            vmem_limit_bytes=VMEM_LIMIT_BYTES),
        name="gla",
    )(proj2d, proj2d, proj2d, proj2d, pa, u_pad, bias, gn, ltri, e)


FF_CHUNK = 1024


def _out_kernel(x_ref, o0_ref, o1_ref, o2_ref, l0_ref, l1_ref, l2_ref, og_ref, gate_a_ref, gate_g_ref, gbias_ref,
                hx_ref, wa_ref, wb_ref, wo_ref, g2_ref, wup_ref, wdn_ref, out_ref):
    l0, l1, l2 = l0_ref[0], l1_ref[0], l2_ref[0]
    mx = jnp.maximum(jnp.maximum(l0, l1), l2)
    e0, e1, e2 = jnp.exp(l0 - mx), jnp.exp(l1 - mx), jnp.exp(l2 - mx)
    inv = 1.0 / (e0 + e1 + e2)
    hx = hx_ref[...]
    def pairs(ref):
        return jnp.concatenate([ref[0, p] for p in range(N_PAIRS)], axis=1)

    def expand(w):
        return jnp.dot(w.astype(BF16), hx, preferred_element_type=F32)

    o_attn = (expand(e0 * inv) * pairs(o0_ref) + expand(e1 * inv) * pairs(o1_ref)
              + expand(e2 * inv) * pairs(o2_ref))
    a = jnp.dot(o_attn.astype(BF16), wa_ref[...], preferred_element_type=F32)
    g = jnp.dot(og_ref[...], wb_ref[...], preferred_element_type=F32)
    gate_a = jax.nn.sigmoid(gate_a_ref[...].astype(F32) + gbias_ref[:, :D_MODEL])
    gate_g = jax.nn.sigmoid(gate_g_ref[...].astype(F32) + gbias_ref[:, D_MODEL:])
    mixed = gate_a * a + gate_g * g
    x1 = x_ref[...] + jnp.dot(mixed.astype(BF16), wo_ref[...], preferred_element_type=F32)

    ms = jnp.mean(x1 * x1, axis=-1, keepdims=True)
    h2 = (x1 * lax.rsqrt(ms + EPS) * g2_ref[...]).astype(BF16)
    hidden = []
    for c in range(D_FF // FF_CHUNK):
        u = jnp.dot(h2, wup_ref[:, c * FF_CHUNK:(c + 1) * FF_CHUNK], preferred_element_type=F32)
        u = jnp.maximum(u, 0.0)
        hidden.append((u * u).astype(BF16))
    out_ref[...] = x1 + jnp.dot(jnp.concatenate(hidden, axis=1), wdn_ref[...],
                                preferred_element_type=F32)


def _out_call(x2d, o_groups, lse_groups, o_gla, proj2d, gbias, wa, wb, wo, g2, wup, wdn, batch, tm):
    t = x2d.shape[0]
    lanes = np.arange(LANES)
    cols = np.arange(GROUP_WIDTH)
    head_expand = jnp.asarray(lanes[:, None] == cols[None, :] // HEAD_DIM, BF16)

    tiles_per_seq = t // batch // tm

    def tok(width):
        return pl.BlockSpec((tm, width), lambda i: (i, 0))

    attn_o = pl.BlockSpec((1, N_PAIRS, tm, LANES),
                          lambda i: (i // tiles_per_seq, 0, i % tiles_per_seq, 0))
    attn_lse = pl.BlockSpec((1, tm, LANES), lambda i: (i // tiles_per_seq, i % tiles_per_seq, 0))

    def const(shape):
        return pl.BlockSpec(shape, lambda i: (0, 0), pipeline_mode=pl.Buffered(1))

    return pl.pallas_call(
        _out_kernel,
        grid=(t // tm,),
        in_specs=[
            tok(D_MODEL),
            attn_o, attn_o, attn_o,
            attn_lse, attn_lse, attn_lse,
            tok(GLA_DV),
            pl.BlockSpec((tm, D_MODEL), lambda i: (i, P_GATE // D_MODEL)),
            pl.BlockSpec((tm, D_MODEL), lambda i: (i, P_GATE // D_MODEL + 1)),
            const((1, 2 * D_MODEL)),
            const((LANES, GROUP_WIDTH)),
            const((GROUP_WIDTH, D_MODEL)),
            const((GLA_DV, D_MODEL)),
            const((D_MODEL, D_MODEL)),
            const((1, D_MODEL)),
            const((D_MODEL, D_FF)),
            const((D_FF, D_MODEL)),
        ],
        out_specs=tok(D_MODEL),
        out_shape=jax.ShapeDtypeStruct((t, D_MODEL), F32),
        compiler_params=pltpu.CompilerParams(
            dimension_semantics=("parallel",),
            vmem_limit_bytes=VMEM_LIMIT_BYTES),
        name="out",
    )(x2d, *o_groups, *lse_groups, o_gla, proj2d, proj2d, gbias, head_expand, wa, wb, wo, g2, wup, wdn)


def _layer(x2d, batch, norm1_g, w_in, gq, gk, gate_up, gate_bias, gla_norm_g, branch_bias,
           w_a, w_b, w_out, norm2_g, w_up, w_down, out_tm=512):
    g1 = norm1_g.reshape(1, D_MODEL)
    w_in_bf16 = w_in.astype(BF16)
    w_gate = w_in_bf16[:, O_GATE:O_GATE + 2 * D_MODEL]
    w_pa = jnp.pad(w_in_bf16[:, O_PA:O_PA + GLA_RANK], ((0, 0), (0, PA_PAD - GLA_RANK)))
    q_gain = jnp.tile(gq, HEADS_PER_GROUP) * (HEAD_DIM ** -0.5 * LOG2_E)
    k_gain = jnp.tile(gk, HEADS_PER_GROUP)
    qk_gain = jnp.stack([q_gain, k_gain]).reshape(2, 1, GROUP_WIDTH)

    proj, pa = _main_proj_call(x2d, g1, w_in_bf16, w_gate, w_pa, tm=1024, rows=512)
    qkv_groups = _attn_proj_call(x2d, g1, w_in_bf16, qk_gain, batch, tm=1024, rows=512)

    o_groups, lse_groups = [], []
    for qkv, (_, dilation) in zip(qkv_groups, ATTN_GROUPS):
        o, lse = _attn_call(qkv, dilation)
        o_groups.append(o)
        lse_groups.append(lse)

    u_pad = jnp.pad(gate_up, ((0, PA_PAD - GLA_RANK), (0, 0))).astype(BF16)
    o_gla = _gla_call(proj, pa, u_pad, gate_bias.reshape(1, GLA_DK),
                      gla_norm_g.reshape(1, GLA_HV), batch)

    return _out_call(x2d, o_groups, lse_groups, o_gla, proj,
                     branch_bias.reshape(1, 2 * D_MODEL),
                     w_a.astype(BF16), w_b.astype(BF16), w_out.astype(BF16),
                     norm2_g.reshape(1, D_MODEL), w_up.astype(BF16), w_down.astype(BF16), batch, out_tm)


def kernel(x, norm1_g, w_in, attn_q_norm_g, attn_k_norm_g, gla_gate_up, gla_gate_bias, gla_out_norm_g, branch_gate_bias, w_attn_branch, w_gla_branch, w_out, norm2_g, w_ff_up, w_ff_down):
    b, s, d = x.shape
    x2d = x.reshape(b * s, d)
    for l in range(norm1_g.shape[0]):
        x2d = _layer(x2d, b, norm1_g[l], w_in[l], attn_q_norm_g[l], attn_k_norm_g[l],
                     gla_gate_up[l], gla_gate_bias[l], gla_out_norm_g[l], branch_gate_bias[l],
                     w_attn_branch[l], w_gla_branch[l], w_out[l], norm2_g[l],
                     w_ff_up[l], w_ff_down[l])
    return x2d.reshape(b, s, d)
```

```python
import functools

import numpy as np
import jax
import jax.numpy as jnp
from jax import lax
from jax.experimental import pallas as pl
from jax.experimental.pallas import tpu as pltpu

F32 = jnp.float32
BF16 = jnp.bfloat16

D_MODEL = 1024
ATTN_GROUPS = ((128, 1), (512, 4), (2048, 16))
N_GROUPS = len(ATTN_GROUPS)
HEADS_PER_GROUP = 8
HEAD_DIM = 64
ATTN_BLOCK = 128
GROUP_WIDTH = HEADS_PER_GROUP * HEAD_DIM
ATTN_WIDTH = 3 * N_GROUPS * GROUP_WIDTH
N_PAIRS = GROUP_WIDTH // 128
ATTN_HEADS_PER_DOT = 4
ATTN_UNROLL = 16

GLA_HEADS = 4
GLA_DK = 512
GLA_DV = 1024
GLA_HK = GLA_DK // GLA_HEADS
GLA_HV = GLA_DV // GLA_HEADS
GLA_RANK = 16
GLA_TAU = 16.0
GLA_CHUNK = 64
GLA_SUB = 8
GLA_STEP = 256
GLA_SUBSTEPS = 4

D_FF = 4 * D_MODEL
EPS = 1e-6
LOG2_E = 1.4426950408889634
LN_2 = 0.6931471805599453

FAST_STRIDE = 4
LANES = 128
VMEM_LIMIT_BYTES = 56 * 1024 * 1024

_ORIG_SIZES = (ATTN_WIDTH, GLA_DK, GLA_DK, GLA_DV, GLA_DV, GLA_RANK, 2 * D_MODEL)
_ORIG_OFF = tuple(int(v) for v in np.cumsum((0,) + _ORIG_SIZES))
O_ATTN, O_GQ, O_GK, O_GV, O_GR, O_PA, O_GATE = _ORIG_OFF[:7]

P_GQ = 0
P_GK = P_GQ + GLA_DK
P_GV = P_GK + GLA_DK
P_GR = P_GV + GLA_DV
P_GATE = P_GR + GLA_DV
P_MAIN = P_GATE + 2 * D_MODEL
GLA_W_BLOCK = (O_GATE - GLA_RANK - O_GQ) // 2
N_KINDS = 3
PA_PAD = LANES


def _rms_norm_rows(x, gain):
    ms = jnp.mean(x * x, axis=-1, keepdims=True)
    return x * lax.rsqrt(ms + EPS) * gain


def _main_proj_kernel(x_ref, g1_ref, wa_ref, wb_ref, wgate_ref, wpa_ref, o_ref, pa_ref, *, rows):
    gain = g1_ref[...]
    for rc in range(x_ref.shape[0] // rows):
        rs = slice(rc * rows, (rc + 1) * rows)
        h = _rms_norm_rows(x_ref[rs, :], gain).astype(BF16)
        pa_ref[rs, :] = jnp.dot(h, wpa_ref[...], preferred_element_type=F32)
        col = 0
        for w_ref in (wa_ref, wb_ref, wgate_ref):
            width = w_ref.shape[1]
            o_ref[rs, col:col + width] = jnp.dot(
                h, w_ref[...], preferred_element_type=F32).astype(BF16)
            col += width


def _main_proj_call(x2d, g1, w_in_bf16, w_gate, w_pa, tm, rows):
    t = x2d.shape[0]
    first_block = O_GQ // GLA_W_BLOCK

    def const(shape, col_block=0):
        return pl.BlockSpec(shape, lambda i: (0, col_block), pipeline_mode=pl.Buffered(1))

    return pl.pallas_call(
        functools.partial(_main_proj_kernel, rows=rows),
        grid=(t // tm,),
        in_specs=[
            pl.BlockSpec((tm, D_MODEL), lambda i: (i, 0)),
            const((1, D_MODEL)),
            const((D_MODEL, GLA_W_BLOCK), first_block),
            const((D_MODEL, GLA_W_BLOCK), first_block + 1),
            const((D_MODEL, 2 * D_MODEL)),
            const((D_MODEL, PA_PAD)),
        ],
        out_specs=[
            pl.BlockSpec((tm, P_MAIN), lambda i: (i, 0)),
            pl.BlockSpec((tm, PA_PAD), lambda i: (i, 0)),
        ],
        out_shape=[
            jax.ShapeDtypeStruct((t, P_MAIN), BF16),
            jax.ShapeDtypeStruct((t, PA_PAD), F32),
        ],
        compiler_params=pltpu.CompilerParams(
            dimension_semantics=("parallel",),
            vmem_limit_bytes=VMEM_LIMIT_BYTES),
        name="proj_main",
    )(x2d, g1, w_in_bf16, w_in_bf16, w_gate, w_pa)


def _qkv_project(h, w_refs, gain_ref, o_ref, chunk, residue_of_block=None):
    d = o_ref.shape[1]
    rows = h.shape[0] // d
    residue_of_block = residue_of_block or list(range(d))
    for kind in range(N_KINDS):
        cols = slice(kind * GROUP_WIDTH, (kind + 1) * GROUP_WIDTH)
        acc = jnp.dot(h, w_refs[kind][...], preferred_element_type=F32)
        if kind < 2:
            sq = acc * acc
            first_head = lax.broadcasted_iota(jnp.int32, (1, LANES), 1) < HEAD_DIM
            scales = []
            for pp in range(N_PAIRS):
                t = sq[:, pp * LANES:(pp + 1) * LANES]
                s_lo = jnp.sum(jnp.where(first_head, t, 0.0), axis=-1, keepdims=True)
                s_hi = jnp.sum(jnp.where(first_head, 0.0, t), axis=-1, keepdims=True)
                scales.append(jnp.where(first_head,
                                        lax.rsqrt(s_lo * (1.0 / HEAD_DIM) + EPS),
                                        lax.rsqrt(s_hi * (1.0 / HEAD_DIM) + EPS)))
            acc = acc * jnp.concatenate(scales, axis=1) * gain_ref[kind]
        y = acc.astype(BF16)
        for j, r in enumerate(residue_of_block):
            o_ref[0, r, chunk * rows:(chunk + 1) * rows, cols] = y[j * rows:(j + 1) * rows, :]


def _attn_proj_kernel(x_ref, g1_ref, *rest):
    n_w = N_KINDS * N_GROUPS
    w_all = rest[:n_w]
    gain_ref, o0_ref, o1_ref, o2_ref, col_scr, col2_scr, perm1_scr, perm2_scr = rest[n_w:]
    w_group = [[w_all[kind * N_GROUPS + g] for kind in range(N_KINDS)] for g in range(N_GROUPS)]
    n_chunks, _, tc, _ = col_scr.shape
    n_col = D_MODEL // LANES
    gain = g1_ref[...]
    for ch in range(n_chunks):
        hf = _rms_norm_rows(x_ref[ch * tc:(ch + 1) * tc, :], gain)
        for c in range(n_col):
            col_scr[ch, c] = hf[:, c * LANES:(c + 1) * LANES]
        _qkv_project(hf.astype(BF16), w_group[0], gain_ref, o0_ref, ch)
        f = o1_ref.shape[1]
        assert o2_ref.shape[1] == f * f
        rows1, rows2 = tc // f, tc // (f * f)
        for a in range(f):
            for c in range(n_col):
                t = col_scr[ch, c, pl.ds(a, rows1, stride=f), :]
                col2_scr[ch, c, a * rows1:(a + 1) * rows1, :] = t
                perm1_scr[ch, a * rows1:(a + 1) * rows1, c * LANES:(c + 1) * LANES] = t.astype(BF16)
        _qkv_project(perm1_scr[ch], w_group[1], gain_ref, o1_ref, ch)
        for a in range(f):
            for b in range(f):
                blk = a * f + b
                for c in range(n_col):
                    perm2_scr[ch, blk * rows2:(blk + 1) * rows2, c * LANES:(c + 1) * LANES] = (
                        col2_scr[ch, c, pl.ds(a * rows1 + b, rows2, stride=f), :].astype(BF16))
        _qkv_project(perm2_scr[ch], w_group[2], gain_ref, o2_ref, ch,
                     residue_of_block=[b * f + a for a in range(f) for b in range(f)])


def _attn_proj_call(x2d, g1, w_in_bf16, qk_gain, batch, tm, rows):
    t = x2d.shape[0]
    s = t // batch
    tiles_per_seq = s // tm
    width = N_KINDS * GROUP_WIDTH
    n_w = N_KINDS * N_GROUPS

    def const(shape):
        return pl.BlockSpec(shape, lambda i: (0,) * len(shape), pipeline_mode=pl.Buffered(1))

    def w_block(j):
        return pl.BlockSpec((D_MODEL, GROUP_WIDTH), lambda i: (0, O_ATTN // GROUP_WIDTH + j),
                            pipeline_mode=pl.Buffered(1))

    return pl.pallas_call(
        _attn_proj_kernel,
        grid=(t // tm,),
        in_specs=[
            pl.BlockSpec((tm, D_MODEL), lambda i: (i, 0)),
            const((1, D_MODEL)),
            *[w_block(j) for j in range(n_w)],
            const((2, 1, GROUP_WIDTH)),
        ],
        out_specs=[
            pl.BlockSpec((1, d, tm // d, width),
                         lambda i: (i // tiles_per_seq, 0, i % tiles_per_seq, 0))
            for _, d in ATTN_GROUPS],
        out_shape=[jax.ShapeDtypeStruct((batch, d, s // d, width), BF16) for _, d in ATTN_GROUPS],
        scratch_shapes=[
            pltpu.VMEM((tm // rows, D_MODEL // LANES, rows, LANES), F32),
            pltpu.VMEM((tm // rows, D_MODEL // LANES, rows, LANES), F32),
            pltpu.VMEM((tm // rows, rows, D_MODEL), BF16),
            pltpu.VMEM((tm // rows, rows, D_MODEL), BF16),
        ],
        compiler_params=pltpu.CompilerParams(
            dimension_semantics=("parallel",),
            vmem_limit_bytes=VMEM_LIMIT_BYTES),
        name="proj_attn",
    )(x2d, g1, *([w_in_bf16] * n_w), qk_gain)


def _attn_kernel(q_ref, kp_ref, kc_ref, vp_ref, vc_ref, o_ref, lse_ref, *stage):
    n = pl.program_id(1)
    d = q_ref.shape[1]
    blk = ATTN_BLOCK
    n_sub = q_ref.shape[2] // blk
    stage_scr = stage[0] if stage else None
    f = FAST_STRIDE

    def put(idx, r, u, tile):
        if stage_scr is None:
            tok = pl.ds(u * blk * d + r, blk, stride=d)
            if idx == N_PAIRS:
                lse_ref[0, tok, :] = tile
            else:
                o_ref[0, idx, tok, :] = tile
        else:
            a, b = r % f, r // f
            stage_scr[idx, a, pl.ds(u * blk * (d // f) + b, blk, stride=d // f), :] = tile
    qi = lax.broadcasted_iota(jnp.int32, (blk, 2 * blk), 0)
    ki = lax.broadcasted_iota(jnp.int32, (blk, 2 * blk), 1)
    band = (ki >= qi) & (ki <= qi + blk)
    band_first = band & ((ki >= blk) | (n > 0))
    lane = lax.broadcasted_iota(jnp.int32, (blk, LANES), 1)
    first_head = lane < HEAD_DIM
    hb = ATTN_HEADS_PER_DOT
    width = hb * HEAD_DIM
    lane_w = lax.broadcasted_iota(jnp.int32, (blk, width), 1)
    head_lanes = [(lane_w >= h * HEAD_DIM) & (lane_w < (h + 1) * HEAD_DIM) for h in range(hb)]
    valid_by_sub = [jnp.concatenate([band_first if u == 0 else band] * hb, axis=0)
                    for u in range(min(n_sub, 2))]

    def unit(r, u):
        rows = slice(u * blk, (u + 1) * blk)
        valid_b = valid_by_sub[min(u, 1)]
        m_tile = jnp.zeros((blk, LANES), F32)
        l_tile = jnp.ones((blk, LANES), F32)
        for g in range(HEADS_PER_GROUP // hb):
            sl = slice(g * width, (g + 1) * width)
            q = q_ref[0, r, rows, sl]
            if u == 0:
                k_prev, v_prev = kp_ref[0, r, :, sl], vp_ref[0, r, :, sl]
            else:
                prev_rows = slice((u - 1) * blk, u * blk)
                k_prev, v_prev = kc_ref[0, r, prev_rows, sl], vc_ref[0, r, prev_rows, sl]
            k = jnp.concatenate([k_prev, kc_ref[0, r, rows, sl]], axis=0)
            v = jnp.concatenate([v_prev, vc_ref[0, r, rows, sl]], axis=0)
            zero = jnp.zeros_like(q)
            q_rows = jnp.concatenate([jnp.where(head_lanes[h], q, zero) for h in range(hb)],
                                     axis=0)
            s = lax.dot_general(q_rows, k, (((1,), (1,)), ((), ())), preferred_element_type=F32)
            s = jnp.where(valid_b, s, -jnp.inf)
            m = jnp.max(s, axis=-1, keepdims=True)
            p = jnp.exp2(s - m)
            l = jnp.sum(p, axis=-1, keepdims=True)
            p = p.astype(BF16)
            inv_l = 1.0 / l
            for h in range(hb):
                head = g * hb + h
                m_tile = jnp.where(lane == head, m[h * blk:(h + 1) * blk], m_tile)
                l_tile = jnp.where(lane == head, l[h * blk:(h + 1) * blk], l_tile)
            for pp in range(hb // 2):
                rows2 = slice(2 * pp * blk, (2 * pp + 2) * blk)
                pv = jnp.dot(p[rows2], v[:, pp * LANES:(pp + 1) * LANES],
                             preferred_element_type=F32) * inv_l[rows2]
                put(g * (hb // 2) + pp, r, u, jnp.where(first_head, pv[:blk], pv[blk:]))
        put(N_PAIRS, r, u, m_tile * LN_2 + jnp.log(l_tile))

    if d * n_sub <= ATTN_UNROLL:
        for r in range(d):
            for u in range(n_sub):
                unit(r, u)
        if stage_scr is not None:
            for a in range(f):
                merged = pl.ds(a, stage_scr.shape[2], stride=f)
                for idx in range(N_PAIRS):
                    o_ref[0, idx, merged, :] = stage_scr[idx, a]
                lse_ref[0, merged, :] = stage_scr[N_PAIRS, a]
    else:
        assert stage_scr is None
        def residue(r, carry):
            for u in range(n_sub):
                unit(r, u)
            return carry
        lax.fori_loop(0, d, residue, 0, unroll=ATTN_UNROLL // n_sub)


def _attn_call(qkv, dilation):
    b, d, sub_len, _ = qkv.shape
    assert d == dilation
    blk = ATTN_BLOCK
    n_sub = max(1, ATTN_UNROLL // d)
    rows = n_sub * blk
    steps = sub_len // rows
    s = sub_len * d

    def cur(kind):
        return lambda bi, n: (bi, 0, n, kind)

    def prev(kind):
        return lambda bi, n: (bi, 0, jnp.maximum(n * n_sub - 1, 0), kind)

    cur_shape = (1, d, rows, GROUP_WIDTH)
    prev_shape = (1, d, blk, GROUP_WIDTH)
    o, lse = pl.pallas_call(
        _attn_kernel,
        grid=(b, steps),
        in_specs=[
            pl.BlockSpec(cur_shape, cur(0)),
            pl.BlockSpec(prev_shape, prev(1)),
            pl.BlockSpec(cur_shape, cur(1)),
            pl.BlockSpec(prev_shape, prev(2)),
            pl.BlockSpec(cur_shape, cur(2)),
        ],
        out_specs=[
            pl.BlockSpec((1, N_PAIRS, d * rows, LANES), lambda bi, n: (bi, 0, n, 0)),
            pl.BlockSpec((1, d * rows, LANES), lambda bi, n: (bi, n, 0)),
        ],
        out_shape=[
            jax.ShapeDtypeStruct((b, N_PAIRS, s, LANES), F32),
            jax.ShapeDtypeStruct((b, s, LANES), F32),
        ],
        scratch_shapes=([pltpu.VMEM((N_PAIRS + 1, FAST_STRIDE, d * rows // FAST_STRIDE, LANES), F32)]
                        if d > FAST_STRIDE else []),
        compiler_params=pltpu.CompilerParams(
            dimension_semantics=("parallel", "arbitrary"),
            vmem_limit_bytes=VMEM_LIMIT_BYTES),
        name=f"attn_d{d}",
    )(qkv, qkv, qkv, qkv, qkv)
    return o, lse


def _log_sigmoid(x):
    return jnp.minimum(x, 0.0) - jnp.log(1.0 + jnp.exp(-jnp.abs(x)))


def _split3(x):
    hi = x.astype(BF16)
    r1 = x - hi.astype(F32)
    mid = r1.astype(BF16)
    lo = (r1 - mid.astype(F32)).astype(BF16)
    return hi, mid, lo


def _gla_pairwise_products(b_scr, qs_scr, kf_scr, pcat_scr):
    sub = GLA_SUB
    tl = lax.broadcasted_iota(jnp.int32, (sub, GLA_HK), 0)
    for pair_i in range(GLA_STEP // (2 * sub)):
        tiles = []
        for r0 in (2 * pair_i * sub, (2 * pair_i + 1) * sub):
            b_blk, q_blk = b_scr[r0:r0 + sub, :], qs_scr[r0:r0 + sub, :]
            row = []
            for s in range(sub):
                d = b_blk - b_scr[r0 + s:r0 + s + 1, :]
                if s > 0:
                    d = jnp.where(tl >= s, d, -jnp.inf)
                row.append(q_blk * kf_scr[r0 + s:r0 + s + 1, :] * jnp.exp2(d))
            tiles.append(row)
        r0 = 2 * pair_i * sub
        for s in range(sub):
            pcat_scr[r0:r0 + 2 * sub, s * GLA_HK:(s + 1) * GLA_HK] = (
                jnp.concatenate([tiles[0][s], tiles[1][s]], axis=0).astype(BF16))


def _gla_reference_factors(b, qs, kf):
    c_len = GLA_CHUNK
    n_chunks = GLA_STEP // c_len
    sub = GLA_SUB

    def z(nrows):
        return jnp.zeros((nrows, GLA_HK), F32)

    qb, kb = [], []
    for r0 in range(0, GLA_STEP, 2 * sub):
        mid = r0 + sub
        b_ref = b[mid - 1:mid]
        qb += [z(sub), qs[mid:mid + sub] * jnp.exp2(b[mid:mid + sub] - b_ref)]
        kb += [kf[r0:mid] * jnp.exp2(b_ref - b[r0:mid]), z(sub)]
    q_block = jnp.concatenate(qb, axis=0).astype(BF16)
    k_block = jnp.concatenate(kb, axis=0).astype(BF16)

    qw, kw = [], []
    for c in range(n_chunks):
        bc, qc, kc = (x[c * c_len:(c + 1) * c_len] for x in (b, qs, kf))
        b15, b31, b47 = bc[15:16], bc[31:32], bc[47:48]
        q1 = jnp.concatenate([z(16), qc[16:32] * jnp.exp2(bc[16:32] - b15), z(32)], axis=0)
        q2 = jnp.concatenate([z(32), qc[32:64] * jnp.exp2(bc[32:64] - b31)], axis=0)
        q3 = jnp.concatenate([z(48), qc[48:64] * jnp.exp2(bc[48:64] - b47)], axis=0)
        k1 = jnp.concatenate([kc[0:16] * jnp.exp2(b15 - bc[0:16]), z(48)], axis=0)
        k2 = jnp.concatenate([kc[0:32] * jnp.exp2(b31 - bc[0:32]), z(32)], axis=0)
        k3 = jnp.concatenate([z(32), kc[32:48] * jnp.exp2(b47 - bc[32:48]), z(16)], axis=0)
        qw.append(jnp.concatenate([q1, q2, q3], axis=1))
        kw.append(jnp.concatenate([k1, k2, k3], axis=1))
    q_within = jnp.concatenate(qw, axis=0).astype(BF16)
    k_within = jnp.concatenate(kw, axis=0).astype(BF16)

    qx, kx = [], []
    for j in range(n_chunks - 1):
        lo, hi = j * c_len, (j + 1) * c_len
        b_ref = b[hi - 1:hi]
        qx.append(jnp.concatenate([z(hi), qs[hi:] * jnp.exp2(b[hi:] - b_ref)], axis=0))
        parts = [kf[lo:hi] * jnp.exp2(b_ref - b[lo:hi])]
        if lo:
            parts.insert(0, z(lo))
        parts.append(z(GLA_STEP - hi))
        kx.append(jnp.concatenate(parts, axis=0))
    q_cross = jnp.concatenate(qx, axis=1).astype(BF16)
    k_cross = jnp.concatenate(kx, axis=1).astype(BF16)
    return (q_block, k_block), (q_within, k_within), (q_cross, k_cross)


def _gla_decay(pa_ref, u_ref, bias_ref, ltri_ref):
    logits = jnp.dot(pa_ref[...].astype(BF16), u_ref[...], preferred_element_type=F32) + bias_ref[...]
    la = _log_sigmoid(logits) * (1.0 / GLA_TAU)
    parts = jnp.dot(ltri_ref[...], jnp.concatenate(_split3(la), axis=1), preferred_element_type=F32)
    return (parts[:, :GLA_DK] + parts[:, GLA_DK:2 * GLA_DK] + parts[:, 2 * GLA_DK:]) * LOG2_E


def _gla_kernel(q_ref, k_ref, v_ref, r_ref, pa_ref, pa_next_ref, u_ref, bias_ref, gn_ref, ltri_ref, e_ref,
                o_ref, st_ref, b_scr, qs_scr, kf_scr, pcat_scr, decay_scr, *, steps_per_seq):
    step = pl.program_id(0)
    n = GLA_STEP
    n_sub = q_ref.shape[0] // n

    @pl.when(lax.rem(step, steps_per_seq) == 0)
    def _():
        st_ref[...] = jnp.zeros_like(st_ref)

    @pl.when(step == 0)
    def _():
        decay_scr[...] = _gla_decay(pa_ref.at[0:n], u_ref, bias_ref, ltri_ref)

    def run(rows, next_pa_ref):
        _gla_substep(decay_scr[...], q_ref.at[rows], k_ref.at[rows], v_ref.at[rows], r_ref.at[rows],
                     gn_ref, e_ref, o_ref.at[rows], st_ref, b_scr, qs_scr, kf_scr, pcat_scr)
        decay_scr[...] = _gla_decay(next_pa_ref, u_ref, bias_ref, ltri_ref)

    def substep(ss, carry):
        rows = pl.ds(pl.multiple_of(ss * n, n), n)
        nxt = pl.ds(pl.multiple_of((ss + 1) * n, n), n)
        run(rows, pa_ref.at[nxt])
        return carry

    lax.fori_loop(0, n_sub - 1, substep, 0)
    run(slice((n_sub - 1) * n, n_sub * n), pa_next_ref)


def _gla_substep(b_all, q_ref, k_ref, v_ref, r_ref, gn_ref, e_ref,
                 o_ref, st_ref, b_scr, qs_scr, kf_scr, pcat_scr):
    n = GLA_STEP
    nt = (((1,), (1,)), ((), ()))

    heads = []
    for h in range(GLA_HEADS):
        kcols = slice(h * GLA_HK, (h + 1) * GLA_HK)
        b = b_all[:, kcols]
        qs = q_ref[:, kcols].astype(F32) * (GLA_HK ** -0.5)
        kf = k_ref[:, kcols].astype(F32)
        b_scr[h], qs_scr[h], kf_scr[h] = b, qs, kf
        _gla_pairwise_products(b_scr.at[h], qs_scr.at[h], kf_scr.at[h], pcat_scr.at[h * n:(h + 1) * n])
        heads.append((b, qs, kf))
    a_diag = jnp.dot(pcat_scr[...], e_ref[...], preferred_element_type=F32)

    ri = lax.broadcasted_iota(jnp.int32, (n, n), 0)
    ci = lax.broadcasted_iota(jnp.int32, (n, n), 1)
    same_sub = (ri // GLA_SUB) == (ci // GLA_SUB)
    same_pair = (ri // (2 * GLA_SUB)) == (ci // (2 * GLA_SUB))
    same_chunk = (ri // GLA_CHUNK) == (ci // GLA_CHUNK)
    gn = gn_ref[...]

    for h, (b, qs, kf) in enumerate(heads):
        vcols = slice(h * GLA_HV, (h + 1) * GLA_HV)
        a_block, a_within, a_cross = (
            lax.dot_general(qf, kf_, nt, preferred_element_type=F32)
            for qf, kf_ in _gla_reference_factors(b, qs, kf))
        a = jnp.where(same_sub, a_diag[h * n:(h + 1) * n],
                      jnp.where(same_pair, a_block, jnp.where(same_chunk, a_within, a_cross)))
        v = v_ref[:, vcols]
        st = st_ref[h]
        b_last = b[n - 1:n]
        q_in = (qs * jnp.exp2(b)).astype(BF16)
        o = (jnp.dot(a.astype(BF16), v, preferred_element_type=F32)
             + lax.dot_general(q_in, st.astype(BF16), nt, preferred_element_type=F32))
        k_st = (kf * jnp.exp2(b_last - b)).astype(BF16)
        upd = lax.dot_general(v, k_st, (((0,), (0,)), ((), ())), preferred_element_type=F32)
        st_ref[h] = st * jnp.exp2(b_last) + upd

        ms = jnp.mean(o * o, axis=-1, keepdims=True)
        y = o * lax.rsqrt(ms + EPS) * gn
        r = r_ref[:, vcols].astype(F32)
        o_ref[:, vcols] = (y * (r * jax.nn.sigmoid(r))).astype(BF16)


def _gla_constants():
    n = GLA_STEP
    idx = np.arange(n)
    ltri = idx[:, None] >= idx[None, :]
    rows = np.arange(GLA_SUB * GLA_HK)
    cols = np.arange(n)
    e = (rows[:, None] // GLA_HK) == (cols[None, :] % GLA_SUB)
    return jnp.asarray(ltri, BF16), jnp.asarray(e, BF16)


def _gla_call(proj2d, pa, u_pad, bias, gn, batch):
    t = proj2d.shape[0]
    n = GLA_STEP
    rows = n * GLA_SUBSTEPS
    steps_per_seq = t // batch // rows
    n_steps = t // rows
    ltri, e = _gla_constants()

    def tok(width, col_block):
        return pl.BlockSpec((rows, width), lambda i: (i, col_block))

    def const(shape):
        return pl.BlockSpec(shape, lambda i: (0, 0))

    pa_next = pl.BlockSpec((n, PA_PAD), lambda i: (jnp.minimum(i + 1, n_steps - 1) * GLA_SUBSTEPS, 0))

    return pl.pallas_call(
        functools.partial(_gla_kernel, steps_per_seq=steps_per_seq),
        grid=(n_steps,),
        in_specs=[
            tok(GLA_DK, P_GQ // GLA_DK),
            tok(GLA_DK, P_GK // GLA_DK),
            tok(GLA_DV, P_GV // GLA_DV),
            tok(GLA_DV, P_GR // GLA_DV),
            tok(PA_PAD, 0),
            pa_next,
            const((PA_PAD, GLA_DK)),
            const((1, GLA_DK)),
            const((1, GLA_HV)),
            const((n, n)),
            const((GLA_SUB * GLA_HK, n)),
        ],
        out_specs=tok(GLA_DV, 0),
        out_shape=jax.ShapeDtypeStruct((t, GLA_DV), BF16),
        scratch_shapes=[
            pltpu.VMEM((GLA_HEADS, GLA_HV, GLA_HK), F32),
            pltpu.VMEM((GLA_HEADS, n, GLA_HK), F32),
            pltpu.VMEM((GLA_HEADS, n, GLA_HK), F32),
            pltpu.VMEM((GLA_HEADS, n, GLA_HK), F32),
            pltpu.VMEM((GLA_HEADS * n, GLA_SUB * GLA_HK), BF16),
            pltpu.VMEM((n, GLA_DK), F32),
        ],
        compiler_params=pltpu.CompilerParams(
            dimension_semantics=("arbitrary",),
            vmem_limit_bytes=VMEM_LIMIT_BYTES),
        name="gla",
    )(proj2d, proj2d, proj2d, proj2d, pa, pa, u_pad, bias, gn, ltri, e)


FF_CHUNK = 1024


def _out_kernel(x_ref, o0_ref, o1_ref, o2_ref, l0_ref, l1_ref, l2_ref, og_ref, gate_a_ref, gate_g_ref, gbias_ref,
                hx_ref, wa_ref, wb_ref, wo_ref, g2_ref, wup_ref, wdn_ref, out_ref):
    l0, l1, l2 = l0_ref[0], l1_ref[0], l2_ref[0]
    mx = jnp.maximum(jnp.maximum(l0, l1), l2)
    e0, e1, e2 = jnp.exp(l0 - mx), jnp.exp(l1 - mx), jnp.exp(l2 - mx)
    inv = 1.0 / (e0 + e1 + e2)
    hx = hx_ref[...]
    def pairs(ref):
        return jnp.concatenate([ref[0, p] for p in range(N_PAIRS)], axis=1)

    def expand(w):
        return jnp.dot(w.astype(BF16), hx, preferred_element_type=F32)

    o_attn = (expand(e0 * inv) * pairs(o0_ref) + expand(e1 * inv) * pairs(o1_ref)
              + expand(e2 * inv) * pairs(o2_ref))
    a = jnp.dot(o_attn.astype(BF16), wa_ref[...], preferred_element_type=F32)
    g = jnp.dot(og_ref[...], wb_ref[...], preferred_element_type=F32)
    gate_a = jax.nn.sigmoid(gate_a_ref[...].astype(F32) + gbias_ref[:, :D_MODEL])
    gate_g = jax.nn.sigmoid(gate_g_ref[...].astype(F32) + gbias_ref[:, D_MODEL:])
    mixed = gate_a * a + gate_g * g
    x1 = x_ref[...] + jnp.dot(mixed.astype(BF16), wo_ref[...], preferred_element_type=F32)

    ms = jnp.mean(x1 * x1, axis=-1, keepdims=True)
    h2 = (x1 * lax.rsqrt(ms + EPS) * g2_ref[...]).astype(BF16)
    hidden = []
    for c in range(D_FF // FF_CHUNK):
        u = jnp.dot(h2, wup_ref[:, c * FF_CHUNK:(c + 1) * FF_CHUNK], preferred_element_type=F32)
        u = jnp.maximum(u, 0.0)
        hidden.append((u * u).astype(BF16))
    out_ref[...] = x1 + jnp.dot(jnp.concatenate(hidden, axis=1), wdn_ref[...],
                                preferred_element_type=F32)


def _out_call(x2d, o_groups, lse_groups, o_gla, proj2d, gbias, wa, wb, wo, g2, wup, wdn, batch, tm):
    t = x2d.shape[0]
    lanes = np.arange(LANES)
    cols = np.arange(GROUP_WIDTH)
    head_expand = jnp.asarray(lanes[:, None] == cols[None, :] // HEAD_DIM, BF16)

    tiles_per_seq = t // batch // tm

    def tok(width):
        return pl.BlockSpec((tm, width), lambda i: (i, 0))

    attn_o = pl.BlockSpec((1, N_PAIRS, tm, LANES),
                          lambda i: (i // tiles_per_seq, 0, i % tiles_per_seq, 0))
    attn_lse = pl.BlockSpec((1, tm, LANES), lambda i: (i // tiles_per_seq, i % tiles_per_seq, 0))

    def const(shape):
        return pl.BlockSpec(shape, lambda i: (0, 0), pipeline_mode=pl.Buffered(1))

    return pl.pallas_call(
        _out_kernel,
        grid=(t // tm,),
        in_specs=[
            tok(D_MODEL),
            attn_o, attn_o, attn_o,
            attn_lse, attn_lse, attn_lse,
            tok(GLA_DV),
            pl.BlockSpec((tm, D_MODEL), lambda i: (i, P_GATE // D_MODEL)),
            pl.BlockSpec((tm, D_MODEL), lambda i: (i, P_GATE // D_MODEL + 1)),
            const((1, 2 * D_MODEL)),
            const((LANES, GROUP_WIDTH)),
            const((GROUP_WIDTH, D_MODEL)),
            const((GLA_DV, D_MODEL)),
            const((D_MODEL, D_MODEL)),
            const((1, D_MODEL)),
            const((D_MODEL, D_FF)),
            const((D_FF, D_MODEL)),
        ],
        out_specs=tok(D_MODEL),
        out_shape=jax.ShapeDtypeStruct((t, D_MODEL), F32),
        compiler_params=pltpu.CompilerParams(
            dimension_semantics=("parallel",),
            vmem_limit_bytes=VMEM_LIMIT_BYTES),
        name="out",
    )(x2d, *o_groups, *lse_groups, o_gla, proj2d, proj2d, gbias, head_expand, wa, wb, wo, g2, wup, wdn)


def _layer(x2d, batch, norm1_g, w_in, gq, gk, gate_up, gate_bias, gla_norm_g, branch_bias,
           w_a, w_b, w_out, norm2_g, w_up, w_down, out_tm=512):
    g1 = norm1_g.reshape(1, D_MODEL)
    w_in_bf16 = w_in.astype(BF16)
    w_gate = w_in_bf16[:, O_GATE:O_GATE + 2 * D_MODEL]
    w_pa = jnp.pad(w_in_bf16[:, O_PA:O_PA + GLA_RANK], ((0, 0), (0, PA_PAD - GLA_RANK)))
    q_gain = jnp.tile(gq, HEADS_PER_GROUP) * (HEAD_DIM ** -0.5 * LOG2_E)
    k_gain = jnp.tile(gk, HEADS_PER_GROUP)
    qk_gain = jnp.stack([q_gain, k_gain]).reshape(2, 1, GROUP_WIDTH)

    proj, pa = _main_proj_call(x2d, g1, w_in_bf16, w_gate, w_pa, tm=1024, rows=512)
    qkv_groups = _attn_proj_call(x2d, g1, w_in_bf16, qk_gain, batch, tm=1024, rows=512)

    o_groups, lse_groups = [], []
    for qkv, (_, dilation) in zip(qkv_groups, ATTN_GROUPS):
        o, lse = _attn_call(qkv, dilation)
        o_groups.append(o)
        lse_groups.append(lse)

    u_pad = jnp.pad(gate_up, ((0, PA_PAD - GLA_RANK), (0, 0))).astype(BF16)
    o_gla = _gla_call(proj, pa, u_pad, gate_bias.reshape(1, GLA_DK),
                      gla_norm_g.reshape(1, GLA_HV), batch)

    return _out_call(x2d, o_groups, lse_groups, o_gla, proj,
                     branch_bias.reshape(1, 2 * D_MODEL),
                     w_a.astype(BF16), w_b.astype(BF16), w_out.astype(BF16),
                     norm2_g.reshape(1, D_MODEL), w_up.astype(BF16), w_down.astype(BF16), batch, out_tm)


def kernel(x, norm1_g, w_in, attn_q_norm_g, attn_k_norm_g, gla_gate_up, gla_gate_bias, gla_out_norm_g, branch_gate_bias, w_attn_branch, w_gla_branch, w_out, norm2_g, w_ff_up, w_ff_down):
    b, s, d = x.shape
    x2d = x.reshape(b * s, d)
    for l in range(norm1_g.shape[0]):
        x2d = _layer(x2d, b, norm1_g[l], w_in[l], attn_q_norm_g[l], attn_k_norm_g[l],
                     gla_gate_up[l], gla_gate_bias[l], gla_out_norm_g[l], branch_gate_bias[l],
                     w_attn_branch[l], w_gla_branch[l], w_out[l], norm2_g[l],
                     w_ff_up[l], w_ff_down[l])
    return x2d.reshape(b, s, d)
```

```python
import functools

import numpy as np
import jax
import jax.numpy as jnp
from jax import lax
from jax.experimental import pallas as pl
from jax.experimental.pallas import tpu as pltpu

F32 = jnp.float32
BF16 = jnp.bfloat16

D_MODEL = 1024
ATTN_GROUPS = ((128, 1), (512, 4), (2048, 16))
N_GROUPS = len(ATTN_GROUPS)
HEADS_PER_GROUP = 8
HEAD_DIM = 64
ATTN_BLOCK = 128
GROUP_WIDTH = HEADS_PER_GROUP * HEAD_DIM
ATTN_WIDTH = 3 * N_GROUPS * GROUP_WIDTH
N_PAIRS = GROUP_WIDTH // 128
ATTN_HEADS_PER_DOT = 4
ATTN_UNROLL = 16

GLA_HEADS = 4
GLA_DK = 512
GLA_DV = 1024
GLA_HK = GLA_DK // GLA_HEADS
GLA_HV = GLA_DV // GLA_HEADS
GLA_RANK = 16
GLA_TAU = 16.0
GLA_CHUNK = 64
GLA_SUB = 8
GLA_STEP = 256
GLA_SUBSTEPS = 4
GLA_SAFE_SPAN = 64.0

D_FF = 4 * D_MODEL
EPS = 1e-6
LOG2_E = 1.4426950408889634
LN_2 = 0.6931471805599453

FAST_STRIDE = 4
LANES = 128
VMEM_LIMIT_BYTES = 56 * 1024 * 1024

_ORIG_SIZES = (ATTN_WIDTH, GLA_DK, GLA_DK, GLA_DV, GLA_DV, GLA_RANK, 2 * D_MODEL)
_ORIG_OFF = tuple(int(v) for v in np.cumsum((0,) + _ORIG_SIZES))
O_ATTN, O_GQ, O_GK, O_GV, O_GR, O_PA, O_GATE = _ORIG_OFF[:7]

P_GQ = 0
P_GK = P_GQ + GLA_DK
P_GV = P_GK + GLA_DK
P_GR = P_GV + GLA_DV
P_GATE = P_GR + GLA_DV
P_MAIN = P_GATE + 2 * D_MODEL
GLA_W_BLOCK = (O_GATE - GLA_RANK - O_GQ) // 2
N_KINDS = 3
PA_PAD = LANES


def _rms_norm_rows(x, gain):
    ms = jnp.mean(x * x, axis=-1, keepdims=True)
    return x * lax.rsqrt(ms + EPS) * gain


def _main_proj_kernel(x_ref, g1_ref, wa_ref, wb_ref, wgate_ref, wpa_ref, o_ref, pa_ref, *, rows):
    gain = g1_ref[...]
    for rc in range(x_ref.shape[0] // rows):
        rs = slice(rc * rows, (rc + 1) * rows)
        h = _rms_norm_rows(x_ref[rs, :], gain).astype(BF16)
        pa_ref[rs, :] = jnp.dot(h, wpa_ref[...], preferred_element_type=F32)
        col = 0
        for w_ref in (wa_ref, wb_ref, wgate_ref):
            width = w_ref.shape[1]
            o_ref[rs, col:col + width] = jnp.dot(
                h, w_ref[...], preferred_element_type=F32).astype(BF16)
            col += width


def _main_proj_call(x2d, g1, w_in_bf16, w_gate, w_pa, tm, rows):
    t = x2d.shape[0]
    first_block = O_GQ // GLA_W_BLOCK

    def const(shape, col_block=0):
        return pl.BlockSpec(shape, lambda i: (0, col_block), pipeline_mode=pl.Buffered(1))

    return pl.pallas_call(
        functools.partial(_main_proj_kernel, rows=rows),
        grid=(t // tm,),
        in_specs=[
            pl.BlockSpec((tm, D_MODEL), lambda i: (i, 0)),
            const((1, D_MODEL)),
            const((D_MODEL, GLA_W_BLOCK), first_block),
            const((D_MODEL, GLA_W_BLOCK), first_block + 1),
            const((D_MODEL, 2 * D_MODEL)),
            const((D_MODEL, PA_PAD)),
        ],
        out_specs=[
            pl.BlockSpec((tm, P_MAIN), lambda i: (i, 0)),
            pl.BlockSpec((tm, PA_PAD), lambda i: (i, 0)),
        ],
        out_shape=[
            jax.ShapeDtypeStruct((t, P_MAIN), BF16),
            jax.ShapeDtypeStruct((t, PA_PAD), F32),
        ],
        compiler_params=pltpu.CompilerParams(
            dimension_semantics=("parallel",),
            vmem_limit_bytes=VMEM_LIMIT_BYTES),
        name="proj_main",
    )(x2d, g1, w_in_bf16, w_in_bf16, w_gate, w_pa)


def _qkv_project(h, w_refs, gain_ref, o_ref, chunk, residue_of_block=None):
    d = o_ref.shape[1]
    rows = h.shape[0] // d
    residue_of_block = residue_of_block or list(range(d))
    for kind in range(N_KINDS):
        cols = slice(kind * GROUP_WIDTH, (kind + 1) * GROUP_WIDTH)
        acc = jnp.dot(h, w_refs[kind][...], preferred_element_type=F32)
        if kind < 2:
            sq = acc * acc
            first_head = lax.broadcasted_iota(jnp.int32, (1, LANES), 1) < HEAD_DIM
            scales = []
            for pp in range(N_PAIRS):
                t = sq[:, pp * LANES:(pp + 1) * LANES]
                s_lo = jnp.sum(jnp.where(first_head, t, 0.0), axis=-1, keepdims=True)
                s_hi = jnp.sum(jnp.where(first_head, 0.0, t), axis=-1, keepdims=True)
                scales.append(jnp.where(first_head,
                                        lax.rsqrt(s_lo * (1.0 / HEAD_DIM) + EPS),
                                        lax.rsqrt(s_hi * (1.0 / HEAD_DIM) + EPS)))
            acc = acc * jnp.concatenate(scales, axis=1) * gain_ref[kind]
        y = acc.astype(BF16)
        for j, r in enumerate(residue_of_block):
            o_ref[0, r, chunk * rows:(chunk + 1) * rows, cols] = y[j * rows:(j + 1) * rows, :]


def _attn_proj_kernel(x_ref, g1_ref, *rest):
    n_w = N_KINDS * N_GROUPS
    w_all = rest[:n_w]
    gain_ref, o0_ref, o1_ref, o2_ref, col_scr, col2_scr, perm1_scr, perm2_scr = rest[n_w:]
    w_group = [[w_all[kind * N_GROUPS + g] for kind in range(N_KINDS)] for g in range(N_GROUPS)]
    n_chunks, _, tc, _ = col_scr.shape
    n_col = D_MODEL // LANES
    gain = g1_ref[...]
    for ch in range(n_chunks):
        hf = _rms_norm_rows(x_ref[ch * tc:(ch + 1) * tc, :], gain)
        for c in range(n_col):
            col_scr[ch, c] = hf[:, c * LANES:(c + 1) * LANES]
        _qkv_project(hf.astype(BF16), w_group[0], gain_ref, o0_ref, ch)
        f = o1_ref.shape[1]
        assert o2_ref.shape[1] == f * f
        rows1, rows2 = tc // f, tc // (f * f)
        for a in range(f):
            for c in range(n_col):
                t = col_scr[ch, c, pl.ds(a, rows1, stride=f), :]
                col2_scr[ch, c, a * rows1:(a + 1) * rows1, :] = t
                perm1_scr[ch, a * rows1:(a + 1) * rows1, c * LANES:(c + 1) * LANES] = t.astype(BF16)
        _qkv_project(perm1_scr[ch], w_group[1], gain_ref, o1_ref, ch)
        for a in range(f):
            for b in range(f):
                blk = a * f + b
                for c in range(n_col):
                    perm2_scr[ch, blk * rows2:(blk + 1) * rows2, c * LANES:(c + 1) * LANES] = (
                        col2_scr[ch, c, pl.ds(a * rows1 + b, rows2, stride=f), :].astype(BF16))
        _qkv_project(perm2_scr[ch], w_group[2], gain_ref, o2_ref, ch,
                     residue_of_block=[b * f + a for a in range(f) for b in range(f)])


def _attn_proj_call(x2d, g1, w_in_bf16, qk_gain, batch, tm, rows):
    t = x2d.shape[0]
    s = t // batch
    tiles_per_seq = s // tm
    width = N_KINDS * GROUP_WIDTH
    n_w = N_KINDS * N_GROUPS

    def const(shape):
        return pl.BlockSpec(shape, lambda i: (0,) * len(shape), pipeline_mode=pl.Buffered(1))

    def w_block(j):
        return pl.BlockSpec((D_MODEL, GROUP_WIDTH), lambda i: (0, O_ATTN // GROUP_WIDTH + j),
                            pipeline_mode=pl.Buffered(1))

    return pl.pallas_call(
        _attn_proj_kernel,
        grid=(t // tm,),
        in_specs=[
            pl.BlockSpec((tm, D_MODEL), lambda i: (i, 0)),
            const((1, D_MODEL)),
            *[w_block(j) for j in range(n_w)],
            const((2, 1, GROUP_WIDTH)),
        ],
        out_specs=[
            pl.BlockSpec((1, d, tm // d, width),
                         lambda i: (i // tiles_per_seq, 0, i % tiles_per_seq, 0))
            for _, d in ATTN_GROUPS],
        out_shape=[jax.ShapeDtypeStruct((batch, d, s // d, width), BF16) for _, d in ATTN_GROUPS],
        scratch_shapes=[
            pltpu.VMEM((tm // rows, D_MODEL // LANES, rows, LANES), F32),
            pltpu.VMEM((tm // rows, D_MODEL // LANES, rows, LANES), F32),
            pltpu.VMEM((tm // rows, rows, D_MODEL), BF16),
            pltpu.VMEM((tm // rows, rows, D_MODEL), BF16),
        ],
        compiler_params=pltpu.CompilerParams(
            dimension_semantics=("parallel",),
            vmem_limit_bytes=VMEM_LIMIT_BYTES),
        name="proj_attn",
    )(x2d, g1, *([w_in_bf16] * n_w), qk_gain)


def _attn_kernel(q_ref, kp_ref, kc_ref, vp_ref, vc_ref, o_ref, lse_ref, *stage):
    n = pl.program_id(1)
    d = q_ref.shape[1]
    blk = ATTN_BLOCK
    n_sub = q_ref.shape[2] // blk
    stage_scr = stage[0] if stage else None
    f = FAST_STRIDE

    def put(idx, r, u, tile):
        if stage_scr is None:
            tok = pl.ds(u * blk * d + r, blk, stride=d)
            if idx == N_PAIRS:
                lse_ref[0, tok, :] = tile
            else:
                o_ref[0, idx, tok, :] = tile
        else:
            a, b = r % f, r // f
            stage_scr[idx, a, pl.ds(u * blk * (d // f) + b, blk, stride=d // f), :] = tile
    qi = lax.broadcasted_iota(jnp.int32, (blk, 2 * blk), 0)
    ki = lax.broadcasted_iota(jnp.int32, (blk, 2 * blk), 1)
    band = (ki >= qi) & (ki <= qi + blk)
    band_first = band & ((ki >= blk) | (n > 0))
    lane = lax.broadcasted_iota(jnp.int32, (blk, LANES), 1)
    first_head = lane < HEAD_DIM
    hb = ATTN_HEADS_PER_DOT
    width = hb * HEAD_DIM
    lane_w = lax.broadcasted_iota(jnp.int32, (blk, width), 1)
    head_lanes = [(lane_w >= h * HEAD_DIM) & (lane_w < (h + 1) * HEAD_DIM) for h in range(hb)]
    valid_by_sub = [jnp.concatenate([band_first if u == 0 else band] * hb, axis=0)
                    for u in range(min(n_sub, 2))]

    def unit(r, u):
        rows = slice(u * blk, (u + 1) * blk)
        valid_b = valid_by_sub[min(u, 1)]
        m_tile = jnp.zeros((blk, LANES), F32)
        l_tile = jnp.ones((blk, LANES), F32)
        for g in range(HEADS_PER_GROUP // hb):
            sl = slice(g * width, (g + 1) * width)
            q = q_ref[0, r, rows, sl]
            if u == 0:
                k_prev, v_prev = kp_ref[0, r, :, sl], vp_ref[0, r, :, sl]
            else:
                prev_rows = slice((u - 1) * blk, u * blk)
                k_prev, v_prev = kc_ref[0, r, prev_rows, sl], vc_ref[0, r, prev_rows, sl]
            k = jnp.concatenate([k_prev, kc_ref[0, r, rows, sl]], axis=0)
            v = jnp.concatenate([v_prev, vc_ref[0, r, rows, sl]], axis=0)
            zero = jnp.zeros_like(q)
            q_rows = jnp.concatenate([jnp.where(head_lanes[h], q, zero) for h in range(hb)],
                                     axis=0)
            s = lax.dot_general(q_rows, k, (((1,), (1,)), ((), ())), preferred_element_type=F32)
            s = jnp.where(valid_b, s, -jnp.inf)
            m = jnp.max(s, axis=-1, keepdims=True)
            p = jnp.exp2(s - m)
            l = jnp.sum(p, axis=-1, keepdims=True)
            p = p.astype(BF16)
            inv_l = 1.0 / l
            for h in range(hb):
                head = g * hb + h
                m_tile = jnp.where(lane == head, m[h * blk:(h + 1) * blk], m_tile)
                l_tile = jnp.where(lane == head, l[h * blk:(h + 1) * blk], l_tile)
            for pp in range(hb // 2):
                rows2 = slice(2 * pp * blk, (2 * pp + 2) * blk)
                pv = jnp.dot(p[rows2], v[:, pp * LANES:(pp + 1) * LANES],
                             preferred_element_type=F32) * inv_l[rows2]
                put(g * (hb // 2) + pp, r, u, jnp.where(first_head, pv[:blk], pv[blk:]))
        put(N_PAIRS, r, u, m_tile * LN_2 + jnp.log(l_tile))

    if d * n_sub <= ATTN_UNROLL:
        for r in range(d):
            for u in range(n_sub):
                unit(r, u)
        if stage_scr is not None:
            for a in range(f):
                merged = pl.ds(a, stage_scr.shape[2], stride=f)
                for idx in range(N_PAIRS):
                    o_ref[0, idx, merged, :] = stage_scr[idx, a]
                lse_ref[0, merged, :] = stage_scr[N_PAIRS, a]
    else:
        assert stage_scr is None
        def residue(r, carry):
            for u in range(n_sub):
                unit(r, u)
            return carry
        lax.fori_loop(0, d, residue, 0, unroll=ATTN_UNROLL // n_sub)


def _attn_call(qkv, dilation):
    b, d, sub_len, _ = qkv.shape
    assert d == dilation
    blk = ATTN_BLOCK
    n_sub = max(1, ATTN_UNROLL // d)
    rows = n_sub * blk
    steps = sub_len // rows
    s = sub_len * d

    def cur(kind):
        return lambda bi, n: (bi, 0, n, kind)

    def prev(kind):
        return lambda bi, n: (bi, 0, jnp.maximum(n * n_sub - 1, 0), kind)

    cur_shape = (1, d, rows, GROUP_WIDTH)
    prev_shape = (1, d, blk, GROUP_WIDTH)
    o, lse = pl.pallas_call(
        _attn_kernel,
        grid=(b, steps),
        in_specs=[
            pl.BlockSpec(cur_shape, cur(0)),
            pl.BlockSpec(prev_shape, prev(1)),
            pl.BlockSpec(cur_shape, cur(1)),
            pl.BlockSpec(prev_shape, prev(2)),
            pl.BlockSpec(cur_shape, cur(2)),
        ],
        out_specs=[
            pl.BlockSpec((1, N_PAIRS, d * rows, LANES), lambda bi, n: (bi, 0, n, 0)),
            pl.BlockSpec((1, d * rows, LANES), lambda bi, n: (bi, n, 0)),
        ],
        out_shape=[
            jax.ShapeDtypeStruct((b, N_PAIRS, s, LANES), F32),
            jax.ShapeDtypeStruct((b, s, LANES), F32),
        ],
        scratch_shapes=([pltpu.VMEM((N_PAIRS + 1, FAST_STRIDE, d * rows // FAST_STRIDE, LANES), F32)]
                        if d > FAST_STRIDE else []),
        compiler_params=pltpu.CompilerParams(
            dimension_semantics=("parallel", "arbitrary"),
            vmem_limit_bytes=VMEM_LIMIT_BYTES),
        name=f"attn_d{d}",
    )(qkv, qkv, qkv, qkv, qkv)
    return o, lse


def _log_sigmoid(x):
    return jnp.minimum(x, 0.0) - jnp.log(1.0 + jnp.exp(-jnp.abs(x)))


def _split3(x):
    hi = x.astype(BF16)
    r1 = x - hi.astype(F32)
    mid = r1.astype(BF16)
    lo = (r1 - mid.astype(F32)).astype(BF16)
    return hi, mid, lo


def _gla_pairwise_products(b_scr, qs_scr, kf_scr, pcat_scr):
    sub = GLA_SUB
    tl = lax.broadcasted_iota(jnp.int32, (sub, GLA_HK), 0)
    for pair_i in range(GLA_STEP // (2 * sub)):
        tiles = []
        for r0 in (2 * pair_i * sub, (2 * pair_i + 1) * sub):
            b_blk, q_blk = b_scr[r0:r0 + sub, :], qs_scr[r0:r0 + sub, :]
            row = []
            for s in range(sub):
                d = b_blk - b_scr[r0 + s:r0 + s + 1, :]
                if s > 0:
                    d = jnp.where(tl >= s, d, -jnp.inf)
                row.append(q_blk * kf_scr[r0 + s:r0 + s + 1, :] * jnp.exp2(d))
            tiles.append(row)
        r0 = 2 * pair_i * sub
        for s in range(sub):
            pcat_scr[r0:r0 + 2 * sub, s * GLA_HK:(s + 1) * GLA_HK] = (
                jnp.concatenate([tiles[0][s], tiles[1][s]], axis=0).astype(BF16))


def _gla_reference_factors(b, qs, kf):
    c_len = GLA_CHUNK
    n_chunks = GLA_STEP // c_len
    sub = GLA_SUB

    def z(nrows):
        return jnp.zeros((nrows, GLA_HK), F32)

    qb, kb = [], []
    for r0 in range(0, GLA_STEP, 2 * sub):
        mid = r0 + sub
        b_ref = b[mid - 1:mid]
        qb += [z(sub), qs[mid:mid + sub] * jnp.exp2(b[mid:mid + sub] - b_ref)]
        kb += [kf[r0:mid] * jnp.exp2(b_ref - b[r0:mid]), z(sub)]
    q_block = jnp.concatenate(qb, axis=0).astype(BF16)
    k_block = jnp.concatenate(kb, axis=0).astype(BF16)

    qw, kw = [], []
    for c in range(n_chunks):
        bc, qc, kc = (x[c * c_len:(c + 1) * c_len] for x in (b, qs, kf))
        b15, b31, b47 = bc[15:16], bc[31:32], bc[47:48]
        q1 = jnp.concatenate([z(16), qc[16:32] * jnp.exp2(bc[16:32] - b15), z(32)], axis=0)
        q2 = jnp.concatenate([z(32), qc[32:64] * jnp.exp2(bc[32:64] - b31)], axis=0)
        q3 = jnp.concatenate([z(48), qc[48:64] * jnp.exp2(bc[48:64] - b47)], axis=0)
        k1 = jnp.concatenate([kc[0:16] * jnp.exp2(b15 - bc[0:16]), z(48)], axis=0)
        k2 = jnp.concatenate([kc[0:32] * jnp.exp2(b31 - bc[0:32]), z(32)], axis=0)
        k3 = jnp.concatenate([z(32), kc[32:48] * jnp.exp2(b47 - bc[32:48]), z(16)], axis=0)
        qw.append(jnp.concatenate([q1, q2, q3], axis=1))
        kw.append(jnp.concatenate([k1, k2, k3], axis=1))
    q_within = jnp.concatenate(qw, axis=0).astype(BF16)
    k_within = jnp.concatenate(kw, axis=0).astype(BF16)

    qx, kx = [], []
    for j in range(n_chunks - 1):
        lo, hi = j * c_len, (j + 1) * c_len
        b_ref = b[hi - 1:hi]
        qx.append(jnp.concatenate([z(hi), qs[hi:] * jnp.exp2(b[hi:] - b_ref)], axis=0))
        parts = [kf[lo:hi] * jnp.exp2(b_ref - b[lo:hi])]
        if lo:
            parts.insert(0, z(lo))
        parts.append(z(GLA_STEP - hi))
        kx.append(jnp.concatenate(parts, axis=0))
    q_cross = jnp.concatenate(qx, axis=1).astype(BF16)
    k_cross = jnp.concatenate(kx, axis=1).astype(BF16)
    return (q_block, k_block), (q_within, k_within), (q_cross, k_cross)


def _gla_decay(pa_ref, u_ref, bias_ref, ltri_ref):
    logits = jnp.dot(pa_ref[...].astype(BF16), u_ref[...], preferred_element_type=F32) + bias_ref[...]
    la = _log_sigmoid(logits) * (1.0 / GLA_TAU)
    parts = jnp.dot(ltri_ref[...], jnp.concatenate(_split3(la), axis=1), preferred_element_type=F32)
    return (parts[:, :GLA_DK] + parts[:, GLA_DK:2 * GLA_DK] + parts[:, 2 * GLA_DK:]) * LOG2_E


def _gla_kernel(q_ref, k_ref, v_ref, r_ref, pa_ref, pa_next_ref, u_ref, bias_ref, gn_ref, ltri_ref, e_ref,
                o_ref, st_ref, b_scr, qs_scr, kf_scr, pcat_scr, decay_scr, *, steps_per_seq):
    step = pl.program_id(0)
    n = GLA_STEP
    n_sub = q_ref.shape[0] // n

    @pl.when(lax.rem(step, steps_per_seq) == 0)
    def _():
        st_ref[...] = jnp.zeros_like(st_ref)

    @pl.when(step == 0)
    def _():
        decay_scr[...] = _gla_decay(pa_ref.at[0:n], u_ref, bias_ref, ltri_ref)

    def run(rows, next_pa_ref):
        def body(small_span):
            _gla_substep(decay_scr[...], q_ref.at[rows], k_ref.at[rows], v_ref.at[rows],
                         r_ref.at[rows], gn_ref, e_ref, o_ref.at[rows], st_ref,
                         b_scr, qs_scr, kf_scr, pcat_scr, small_span=small_span)
            decay_scr[...] = _gla_decay(next_pa_ref, u_ref, bias_ref, ltri_ref)

        small_span = jnp.max(-decay_scr[n - 1:n, :]) <= GLA_SAFE_SPAN
        pl.when(small_span)(lambda: body(True))
        pl.when(jnp.logical_not(small_span))(lambda: body(False))

    def substep(ss, carry):
        rows = pl.ds(pl.multiple_of(ss * n, n), n)
        nxt = pl.ds(pl.multiple_of((ss + 1) * n, n), n)
        run(rows, pa_ref.at[nxt])
        return carry

    lax.fori_loop(0, n_sub - 1, substep, 0)
    run(slice((n_sub - 1) * n, n_sub * n), pa_next_ref)


def _gla_substep(b_all, q_ref, k_ref, v_ref, r_ref, gn_ref, e_ref,
                 o_ref, st_ref, b_scr, qs_scr, kf_scr, pcat_scr, *, small_span):
    n = GLA_STEP
    nt = (((1,), (1,)), ((), ()))

    heads = []
    for h in range(GLA_HEADS):
        kcols = slice(h * GLA_HK, (h + 1) * GLA_HK)
        b = b_all[:, kcols]
        qs = q_ref[:, kcols].astype(F32) * (GLA_HK ** -0.5)
        kf = k_ref[:, kcols].astype(F32)
        if not small_span:
            b_scr[h], qs_scr[h], kf_scr[h] = b, qs, kf
            _gla_pairwise_products(b_scr.at[h], qs_scr.at[h], kf_scr.at[h],
                                   pcat_scr.at[h * n:(h + 1) * n])
        heads.append((b, qs, kf))

    ri = lax.broadcasted_iota(jnp.int32, (n, n), 0)
    ci = lax.broadcasted_iota(jnp.int32, (n, n), 1)
    if small_span:
        causal = ri >= ci
    else:
        a_diag = jnp.dot(pcat_scr[...], e_ref[...], preferred_element_type=F32)
        same_sub = (ri // GLA_SUB) == (ci // GLA_SUB)
        same_pair = (ri // (2 * GLA_SUB)) == (ci // (2 * GLA_SUB))
        same_chunk = (ri // GLA_CHUNK) == (ci // GLA_CHUNK)
    gn = gn_ref[...]

    for h, (b, qs, kf) in enumerate(heads):
        vcols = slice(h * GLA_HV, (h + 1) * GLA_HV)
        q_in = (qs * jnp.exp2(b)).astype(BF16)
        if small_span:
            k_out = (kf * jnp.exp2(-b)).astype(BF16)
            a = jnp.where(causal, lax.dot_general(q_in, k_out, nt, preferred_element_type=F32), 0.0)
        else:
            a_block, a_within, a_cross = (
                lax.dot_general(qf, kf_, nt, preferred_element_type=F32)
                for qf, kf_ in _gla_reference_factors(b, qs, kf))
            a = jnp.where(same_sub, a_diag[h * n:(h + 1) * n],
                          jnp.where(same_pair, a_block, jnp.where(same_chunk, a_within, a_cross)))
        v = v_ref[:, vcols]
        st = st_ref[h]
        b_last = b[n - 1:n]
        o = (jnp.dot(a.astype(BF16), v, preferred_element_type=F32)
             + lax.dot_general(q_in, st.astype(BF16), nt, preferred_element_type=F32))
        k_st = (kf * jnp.exp2(b_last - b)).astype(BF16)
        upd = lax.dot_general(v, k_st, (((0,), (0,)), ((), ())), preferred_element_type=F32)
        st_ref[h] = st * jnp.exp2(b_last) + upd

        ms = jnp.mean(o * o, axis=-1, keepdims=True)
        y = o * lax.rsqrt(ms + EPS) * gn
        r = r_ref[:, vcols].astype(F32)
        o_ref[:, vcols] = (y * (r * jax.nn.sigmoid(r))).astype(BF16)


def _gla_constants():
    n = GLA_STEP
    idx = np.arange(n)
    ltri = idx[:, None] >= idx[None, :]
    rows = np.arange(GLA_SUB * GLA_HK)
    cols = np.arange(n)
    e = (rows[:, None] // GLA_HK) == (cols[None, :] % GLA_SUB)
    return jnp.asarray(ltri, BF16), jnp.asarray(e, BF16)


def _gla_call(proj2d, pa, u_pad, bias, gn, batch):
    t = proj2d.shape[0]
    n = GLA_STEP
    rows = n * GLA_SUBSTEPS
    steps_per_seq = t // batch // rows
    n_steps = t // rows
    ltri, e = _gla_constants()

    def tok(width, col_block):
        return pl.BlockSpec((rows, width), lambda i: (i, col_block))

    def const(shape):
        return pl.BlockSpec(shape, lambda i: (0, 0))

    pa_next = pl.BlockSpec((n, PA_PAD), lambda i: (jnp.minimum(i + 1, n_steps - 1) * GLA_SUBSTEPS, 0))

    return pl.pallas_call(
        functools.partial(_gla_kernel, steps_per_seq=steps_per_seq),
        grid=(n_steps,),
        in_specs=[
            tok(GLA_DK, P_GQ // GLA_DK),
            tok(GLA_DK, P_GK // GLA_DK),
            tok(GLA_DV, P_GV // GLA_DV),
            tok(GLA_DV, P_GR // GLA_DV),
            tok(PA_PAD, 0),
            pa_next,
            const((PA_PAD, GLA_DK)),
            const((1, GLA_DK)),
            const((1, GLA_HV)),
            const((n, n)),
            const((GLA_SUB * GLA_HK, n)),
        ],
        out_specs=tok(GLA_DV, 0),
        out_shape=jax.ShapeDtypeStruct((t, GLA_DV), BF16),
        scratch_shapes=[
            pltpu.VMEM((GLA_HEADS, GLA_HV, GLA_HK), F32),
            pltpu.VMEM((GLA_HEADS, n, GLA_HK), F32),
            pltpu.VMEM((GLA_HEADS, n, GLA_HK), F32),
            pltpu.VMEM((GLA_HEADS, n, GLA_HK), F32),
            pltpu.VMEM((GLA_HEADS * n, GLA_SUB * GLA_HK), BF16),
            pltpu.VMEM((n, GLA_DK), F32),
        ],
        compiler_params=pltpu.CompilerParams(
            dimension_semantics=("arbitrary",),
            vmem_limit_bytes=VMEM_LIMIT_BYTES),
        name="gla",
    )(proj2d, proj2d, proj2d, proj2d, pa, pa, u_pad, bias, gn, ltri, e)


FF_CHUNK = 1024


def _out_kernel(x_ref, o0_ref, o1_ref, o2_ref, l0_ref, l1_ref, l2_ref, og_ref, gate_a_ref, gate_g_ref, gbias_ref,
                hx_ref, wa_ref, wb_ref, wo_ref, g2_ref, wup_ref, wdn_ref, out_ref):
    l0, l1, l2 = l0_ref[0], l1_ref[0], l2_ref[0]
    mx = jnp.maximum(jnp.maximum(l0, l1), l2)
    e0, e1, e2 = jnp.exp(l0 - mx), jnp.exp(l1 - mx), jnp.exp(l2 - mx)
    inv = 1.0 / (e0 + e1 + e2)
    hx = hx_ref[...]
    def pairs(ref):
        return jnp.concatenate([ref[0, p] for p in range(N_PAIRS)], axis=1)

    def expand(w):
        return jnp.dot(w.astype(BF16), hx, preferred_element_type=F32)

    o_attn = (expand(e0 * inv) * pairs(o0_ref) + expand(e1 * inv) * pairs(o1_ref)
              + expand(e2 * inv) * pairs(o2_ref))
    a = jnp.dot(o_attn.astype(BF16), wa_ref[...], preferred_element_type=F32)
    g = jnp.dot(og_ref[...], wb_ref[...], preferred_element_type=F32)
    gate_a = jax.nn.sigmoid(gate_a_ref[...].astype(F32) + gbias_ref[:, :D_MODEL])
    gate_g = jax.nn.sigmoid(gate_g_ref[...].astype(F32) + gbias_ref[:, D_MODEL:])
    mixed = gate_a * a + gate_g * g
    x1 = x_ref[...] + jnp.dot(mixed.astype(BF16), wo_ref[...], preferred_element_type=F32)

    ms = jnp.mean(x1 * x1, axis=-1, keepdims=True)
    h2 = (x1 * lax.rsqrt(ms + EPS) * g2_ref[...]).astype(BF16)
    hidden = []
    for c in range(D_FF // FF_CHUNK):
        u = jnp.dot(h2, wup_ref[:, c * FF_CHUNK:(c + 1) * FF_CHUNK], preferred_element_type=F32)
        u = jnp.maximum(u, 0.0)
        hidden.append((u * u).astype(BF16))
    out_ref[...] = x1 + jnp.dot(jnp.concatenate(hidden, axis=1), wdn_ref[...],
                                preferred_element_type=F32)


def _out_call(x2d, o_groups, lse_groups, o_gla, proj2d, gbias, wa, wb, wo, g2, wup, wdn, batch, tm):
    t = x2d.shape[0]
    lanes = np.arange(LANES)
    cols = np.arange(GROUP_WIDTH)
    head_expand = jnp.asarray(lanes[:, None] == cols[None, :] // HEAD_DIM, BF16)

    tiles_per_seq = t // batch // tm

    def tok(width):
        return pl.BlockSpec((tm, width), lambda i: (i, 0))

    attn_o = pl.BlockSpec((1, N_PAIRS, tm, LANES),
                          lambda i: (i // tiles_per_seq, 0, i % tiles_per_seq, 0))
    attn_lse = pl.BlockSpec((1, tm, LANES), lambda i: (i // tiles_per_seq, i % tiles_per_seq, 0))

    def const(shape):
        return pl.BlockSpec(shape, lambda i: (0, 0), pipeline_mode=pl.Buffered(1))

    return pl.pallas_call(
        _out_kernel,
        grid=(t // tm,),
        in_specs=[
            tok(D_MODEL),
            attn_o, attn_o, attn_o,
            attn_lse, attn_lse, attn_lse,
            tok(GLA_DV),
            pl.BlockSpec((tm, D_MODEL), lambda i: (i, P_GATE // D_MODEL)),
            pl.BlockSpec((tm, D_MODEL), lambda i: (i, P_GATE // D_MODEL + 1)),
            const((1, 2 * D_MODEL)),
            const((LANES, GROUP_WIDTH)),
            const((GROUP_WIDTH, D_MODEL)),
            const((GLA_DV, D_MODEL)),
            const((D_MODEL, D_MODEL)),
            const((1, D_MODEL)),
            const((D_MODEL, D_FF)),
            const((D_FF, D_MODEL)),
        ],
        out_specs=tok(D_MODEL),
        out_shape=jax.ShapeDtypeStruct((t, D_MODEL), F32),
        compiler_params=pltpu.CompilerParams(
            dimension_semantics=("parallel",),
            vmem_limit_bytes=VMEM_LIMIT_BYTES),
        name="out",
    )(x2d, *o_groups, *lse_groups, o_gla, proj2d, proj2d, gbias, head_expand, wa, wb, wo, g2, wup, wdn)


def _layer(x2d, batch, norm1_g, w_in, gq, gk, gate_up, gate_bias, gla_norm_g, branch_bias,
           w_a, w_b, w_out, norm2_g, w_up, w_down, out_tm=512):
    g1 = norm1_g.reshape(1, D_MODEL)
    w_in_bf16 = w_in.astype(BF16)
    w_gate = w_in_bf16[:, O_GATE:O_GATE + 2 * D_MODEL]
    w_pa = jnp.pad(w_in_bf16[:, O_PA:O_PA + GLA_RANK], ((0, 0), (0, PA_PAD - GLA_RANK)))
    q_gain = jnp.tile(gq, HEADS_PER_GROUP) * (HEAD_DIM ** -0.5 * LOG2_E)
    k_gain = jnp.tile(gk, HEADS_PER_GROUP)
    qk_gain = jnp.stack([q_gain, k_gain]).reshape(2, 1, GROUP_WIDTH)

    proj, pa = _main_proj_call(x2d, g1, w_in_bf16, w_gate, w_pa, tm=1024, rows=512)
    qkv_groups = _attn_proj_call(x2d, g1, w_in_bf16, qk_gain, batch, tm=1024, rows=512)

    o_groups, lse_groups = [], []
    for qkv, (_, dilation) in zip(qkv_groups, ATTN_GROUPS):
        o, lse = _attn_call(qkv, dilation)
        o_groups.append(o)
        lse_groups.append(lse)

    u_pad = jnp.pad(gate_up, ((0, PA_PAD - GLA_RANK), (0, 0))).astype(BF16)
    o_gla = _gla_call(proj, pa, u_pad, gate_bias.reshape(1, GLA_DK),
                      gla_norm_g.reshape(1, GLA_HV), batch)

    return _out_call(x2d, o_groups, lse_groups, o_gla, proj,
                     branch_bias.reshape(1, 2 * D_MODEL),
                     w_a.astype(BF16), w_b.astype(BF16), w_out.astype(BF16),
                     norm2_g.reshape(1, D_MODEL), w_up.astype(BF16), w_down.astype(BF16), batch, out_tm)


def kernel(x, norm1_g, w_in, attn_q_norm_g, attn_k_norm_g, gla_gate_up, gla_gate_bias, gla_out_norm_g, branch_gate_bias, w_attn_branch, w_gla_branch, w_out, norm2_g, w_ff_up, w_ff_down):
    b, s, d = x.shape
    x2d = x.reshape(b * s, d)
    for l in range(norm1_g.shape[0]):
        x2d = _layer(x2d, b, norm1_g[l], w_in[l], attn_q_norm_g[l], attn_k_norm_g[l],
                     gla_gate_up[l], gla_gate_bias[l], gla_out_norm_g[l], branch_gate_bias[l],
                     w_attn_branch[l], w_gla_branch[l], w_out[l], norm2_g[l],
                     w_ff_up[l], w_ff_down[l])
    return x2d.reshape(b, s, d)
```

```python
import functools

import numpy as np
import jax
import jax.numpy as jnp
from jax import lax
from jax.experimental import pallas as pl
from jax.experimental.pallas import tpu as pltpu

F32 = jnp.float32
BF16 = jnp.bfloat16

D_MODEL = 1024
ATTN_GROUPS = ((128, 1), (512, 4), (2048, 16))
N_GROUPS = len(ATTN_GROUPS)
HEADS_PER_GROUP = 8
HEAD_DIM = 64
ATTN_BLOCK = 128
GROUP_WIDTH = HEADS_PER_GROUP * HEAD_DIM
ATTN_WIDTH = 3 * N_GROUPS * GROUP_WIDTH
N_PAIRS = GROUP_WIDTH // 128
ATTN_HEADS_PER_DOT = 4
ATTN_UNROLL = 16

GLA_HEADS = 4
GLA_DK = 512
GLA_DV = 1024
GLA_HK = GLA_DK // GLA_HEADS
GLA_HV = GLA_DV // GLA_HEADS
GLA_RANK = 16
GLA_TAU = 16.0
GLA_CHUNK = 64
GLA_SUB = 8
GLA_STEP = 256
GLA_SUBSTEPS = 8
GLA_SAFE_SPAN = 64.0

D_FF = 4 * D_MODEL
EPS = 1e-6
LOG2_E = 1.4426950408889634
LN_2 = 0.6931471805599453

FAST_STRIDE = 4
LANES = 128
VMEM_LIMIT_BYTES = 56 * 1024 * 1024
PROJ_TM = 1024
PROJ_ROWS = 512
OUT_TM = 512

_ORIG_SIZES = (ATTN_WIDTH, GLA_DK, GLA_DK, GLA_DV, GLA_DV, GLA_RANK, 2 * D_MODEL)
_ORIG_OFF = tuple(int(v) for v in np.cumsum((0,) + _ORIG_SIZES))
O_ATTN, O_GQ, O_GK, O_GV, O_GR, O_PA, O_GATE = _ORIG_OFF[:7]

P_GQ = 0
P_GK = P_GQ + GLA_DK
P_GV = P_GK + GLA_DK
P_GR = P_GV + GLA_DV
P_GATE = P_GR + GLA_DV
P_MAIN = P_GATE + 2 * D_MODEL
GLA_W_BLOCK = (O_GATE - GLA_RANK - O_GQ) // 2
N_KINDS = 3
PA_PAD = LANES


def _rms_norm_rows(x, gain):
    ms = jnp.mean(x * x, axis=-1, keepdims=True)
    return x * lax.rsqrt(ms + EPS) * gain


def _main_proj_kernel(x_ref, g1_ref, wa_ref, wb_ref, wgate_ref, wpa_ref, o_ref, pa_ref, *, rows):
    gain = g1_ref[...]
    for rc in range(x_ref.shape[0] // rows):
        rs = slice(rc * rows, (rc + 1) * rows)
        h = _rms_norm_rows(x_ref[rs, :], gain).astype(BF16)
        pa_ref[rs, :] = jnp.dot(h, wpa_ref[...], preferred_element_type=F32)
        col = 0
        for w_ref in (wa_ref, wb_ref, wgate_ref):
            width = w_ref.shape[1]
            o_ref[rs, col:col + width] = jnp.dot(
                h, w_ref[...], preferred_element_type=F32).astype(BF16)
            col += width


def _main_proj_call(x2d, g1, w_in_bf16, w_gate, w_pa, tm, rows):
    t = x2d.shape[0]
    first_block = O_GQ // GLA_W_BLOCK

    def const(shape, col_block=0):
        return pl.BlockSpec(shape, lambda i: (0, col_block), pipeline_mode=pl.Buffered(1))

    return pl.pallas_call(
        functools.partial(_main_proj_kernel, rows=rows),
        grid=(t // tm,),
        in_specs=[
            pl.BlockSpec((tm, D_MODEL), lambda i: (i, 0)),
            const((1, D_MODEL)),
            const((D_MODEL, GLA_W_BLOCK), first_block),
            const((D_MODEL, GLA_W_BLOCK), first_block + 1),
            const((D_MODEL, 2 * D_MODEL)),
            const((D_MODEL, PA_PAD)),
        ],
        out_specs=[
            pl.BlockSpec((tm, P_MAIN), lambda i: (i, 0)),
            pl.BlockSpec((tm, PA_PAD), lambda i: (i, 0)),
        ],
        out_shape=[
            jax.ShapeDtypeStruct((t, P_MAIN), BF16),
            jax.ShapeDtypeStruct((t, PA_PAD), F32),
        ],
        compiler_params=pltpu.CompilerParams(
            dimension_semantics=("parallel",),
            vmem_limit_bytes=VMEM_LIMIT_BYTES),
        name="proj_main",
    )(x2d, g1, w_in_bf16, w_in_bf16, w_gate, w_pa)


def _qkv_project(h, w_refs, gain_ref, o_ref, chunk, residue_of_block=None):
    d = o_ref.shape[1]
    rows = h.shape[0] // d
    residue_of_block = residue_of_block or list(range(d))
    for kind in range(N_KINDS):
        cols = slice(kind * GROUP_WIDTH, (kind + 1) * GROUP_WIDTH)
        acc = jnp.dot(h, w_refs[kind][...], preferred_element_type=F32)
        if kind < 2:
            sq = acc * acc
            first_head = lax.broadcasted_iota(jnp.int32, (1, LANES), 1) < HEAD_DIM
            scales = []
            for pp in range(N_PAIRS):
                t = sq[:, pp * LANES:(pp + 1) * LANES]
                s_lo = jnp.sum(jnp.where(first_head, t, 0.0), axis=-1, keepdims=True)
                s_hi = jnp.sum(jnp.where(first_head, 0.0, t), axis=-1, keepdims=True)
                scales.append(jnp.where(first_head,
                                        lax.rsqrt(s_lo * (1.0 / HEAD_DIM) + EPS),
                                        lax.rsqrt(s_hi * (1.0 / HEAD_DIM) + EPS)))
            acc = acc * jnp.concatenate(scales, axis=1) * gain_ref[kind]
        y = acc.astype(BF16)
        for j, r in enumerate(residue_of_block):
            o_ref[0, r, chunk * rows:(chunk + 1) * rows, cols] = y[j * rows:(j + 1) * rows, :]


def _attn_proj_kernel(x_ref, g1_ref, *rest):
    n_w = N_KINDS * N_GROUPS
    w_all = rest[:n_w]
    gain_ref, o0_ref, o1_ref, o2_ref, col_scr, col2_scr, perm1_scr, perm2_scr = rest[n_w:]
    w_group = [[w_all[kind * N_GROUPS + g] for kind in range(N_KINDS)] for g in range(N_GROUPS)]
    n_chunks, _, tc, _ = col_scr.shape
    n_col = D_MODEL // LANES
    gain = g1_ref[...]
    for ch in range(n_chunks):
        hf = _rms_norm_rows(x_ref[ch * tc:(ch + 1) * tc, :], gain)
        for c in range(n_col):
            col_scr[ch, c] = hf[:, c * LANES:(c + 1) * LANES]
        _qkv_project(hf.astype(BF16), w_group[0], gain_ref, o0_ref, ch)
        f = o1_ref.shape[1]
        assert o2_ref.shape[1] == f * f
        rows1, rows2 = tc // f, tc // (f * f)
        for a in range(f):
            for c in range(n_col):
                t = col_scr[ch, c, pl.ds(a, rows1, stride=f), :]
                col2_scr[ch, c, a * rows1:(a + 1) * rows1, :] = t
                perm1_scr[ch, a * rows1:(a + 1) * rows1, c * LANES:(c + 1) * LANES] = t.astype(BF16)
        _qkv_project(perm1_scr[ch], w_group[1], gain_ref, o1_ref, ch)
        for a in range(f):
            for b in range(f):
                blk = a * f + b
                for c in range(n_col):
                    perm2_scr[ch, blk * rows2:(blk + 1) * rows2, c * LANES:(c + 1) * LANES] = (
                        col2_scr[ch, c, pl.ds(a * rows1 + b, rows2, stride=f), :].astype(BF16))
        _qkv_project(perm2_scr[ch], w_group[2], gain_ref, o2_ref, ch,
                     residue_of_block=[b * f + a for a in range(f) for b in range(f)])


def _attn_proj_call(x2d, g1, w_in_bf16, qk_gain, batch, tm, rows):
    t = x2d.shape[0]
    s = t // batch
    tiles_per_seq = s // tm
    width = N_KINDS * GROUP_WIDTH
    n_w = N_KINDS * N_GROUPS

    def const(shape):
        return pl.BlockSpec(shape, lambda i: (0,) * len(shape), pipeline_mode=pl.Buffered(1))

    def w_block(j):
        return pl.BlockSpec((D_MODEL, GROUP_WIDTH), lambda i: (0, O_ATTN // GROUP_WIDTH + j),
                            pipeline_mode=pl.Buffered(1))

    return pl.pallas_call(
        _attn_proj_kernel,
        grid=(t // tm,),
        in_specs=[
            pl.BlockSpec((tm, D_MODEL), lambda i: (i, 0)),
            const((1, D_MODEL)),
            *[w_block(j) for j in range(n_w)],
            const((2, 1, GROUP_WIDTH)),
        ],
        out_specs=[
            pl.BlockSpec((1, d, tm // d, width),
                         lambda i: (i // tiles_per_seq, 0, i % tiles_per_seq, 0))
            for _, d in ATTN_GROUPS],
        out_shape=[jax.ShapeDtypeStruct((batch, d, s // d, width), BF16) for _, d in ATTN_GROUPS],
        scratch_shapes=[
            pltpu.VMEM((tm // rows, D_MODEL // LANES, rows, LANES), F32),
            pltpu.VMEM((tm // rows, D_MODEL // LANES, rows, LANES), F32),
            pltpu.VMEM((tm // rows, rows, D_MODEL), BF16),
            pltpu.VMEM((tm // rows, rows, D_MODEL), BF16),
        ],
        compiler_params=pltpu.CompilerParams(
            dimension_semantics=("parallel",),
            vmem_limit_bytes=VMEM_LIMIT_BYTES),
        name="proj_attn",
    )(x2d, g1, *([w_in_bf16] * n_w), qk_gain)


def _attn_kernel(q_ref, kp_ref, kc_ref, vp_ref, vc_ref, o_ref, lse_ref, *stage):
    n = pl.program_id(1)
    d = q_ref.shape[1]
    blk = ATTN_BLOCK
    n_sub = q_ref.shape[2] // blk
    stage_scr = stage[0] if stage else None
    f = FAST_STRIDE

    def put(idx, r, u, tile):
        if stage_scr is None:
            tok = pl.ds(u * blk * d + r, blk, stride=d)
            if idx == N_PAIRS:
                lse_ref[0, tok, :] = tile
            else:
                o_ref[0, idx, tok, :] = tile
        else:
            a, b = r % f, r // f
            stage_scr[idx, a, pl.ds(u * blk * (d // f) + b, blk, stride=d // f), :] = tile
    qi = lax.broadcasted_iota(jnp.int32, (blk, 2 * blk), 0)
    ki = lax.broadcasted_iota(jnp.int32, (blk, 2 * blk), 1)
    band = (ki >= qi) & (ki <= qi + blk)
    band_first = band & ((ki >= blk) | (n > 0))
    lane = lax.broadcasted_iota(jnp.int32, (blk, LANES), 1)
    first_head = lane < HEAD_DIM
    hb = ATTN_HEADS_PER_DOT
    width = hb * HEAD_DIM
    lane_w = lax.broadcasted_iota(jnp.int32, (blk, width), 1)
    head_lanes = [(lane_w >= h * HEAD_DIM) & (lane_w < (h + 1) * HEAD_DIM) for h in range(hb)]
    valid_by_sub = [jnp.concatenate([band_first if u == 0 else band] * hb, axis=0)
                    for u in range(min(n_sub, 2))]

    def unit(r, u):
        rows = slice(u * blk, (u + 1) * blk)
        valid_b = valid_by_sub[min(u, 1)]
        m_tile = jnp.zeros((blk, LANES), F32)
        l_tile = jnp.ones((blk, LANES), F32)
        for g in range(HEADS_PER_GROUP // hb):
            sl = slice(g * width, (g + 1) * width)
            q = q_ref[0, r, rows, sl]
            if u == 0:
                k_prev, v_prev = kp_ref[0, r, :, sl], vp_ref[0, r, :, sl]
            else:
                prev_rows = slice((u - 1) * blk, u * blk)
                k_prev, v_prev = kc_ref[0, r, prev_rows, sl], vc_ref[0, r, prev_rows, sl]
            k = jnp.concatenate([k_prev, kc_ref[0, r, rows, sl]], axis=0)
            v = jnp.concatenate([v_prev, vc_ref[0, r, rows, sl]], axis=0)
            zero = jnp.zeros_like(q)
            q_rows = jnp.concatenate([jnp.where(head_lanes[h], q, zero) for h in range(hb)],
                                     axis=0)
            s = lax.dot_general(q_rows, k, (((1,), (1,)), ((), ())), preferred_element_type=F32)
            s = jnp.where(valid_b, s, -jnp.inf)
            m = jnp.max(s, axis=-1, keepdims=True)
            p = jnp.exp2(s - m)
            l = jnp.sum(p, axis=-1, keepdims=True)
            p = p.astype(BF16)
            inv_l = 1.0 / l
            for h in range(hb):
                head = g * hb + h
                m_tile = jnp.where(lane == head, m[h * blk:(h + 1) * blk], m_tile)
                l_tile = jnp.where(lane == head, l[h * blk:(h + 1) * blk], l_tile)
            for pp in range(hb // 2):
                rows2 = slice(2 * pp * blk, (2 * pp + 2) * blk)
                pv = jnp.dot(p[rows2], v[:, pp * LANES:(pp + 1) * LANES],
                             preferred_element_type=F32) * inv_l[rows2]
                put(g * (hb // 2) + pp, r, u, jnp.where(first_head, pv[:blk], pv[blk:]))
        put(N_PAIRS, r, u, m_tile * LN_2 + jnp.log(l_tile))

    if d * n_sub <= ATTN_UNROLL:
        for r in range(d):
            for u in range(n_sub):
                unit(r, u)
        if stage_scr is not None:
            for a in range(f):
                merged = pl.ds(a, stage_scr.shape[2], stride=f)
                for idx in range(N_PAIRS):
                    o_ref[0, idx, merged, :] = stage_scr[idx, a]
                lse_ref[0, merged, :] = stage_scr[N_PAIRS, a]
    else:
        assert stage_scr is None
        def residue(r, carry):
            for u in range(n_sub):
                unit(r, u)
            return carry
        lax.fori_loop(0, d, residue, 0, unroll=ATTN_UNROLL // n_sub)


def _attn_call(qkv, dilation):
    b, d, sub_len, _ = qkv.shape
    assert d == dilation
    blk = ATTN_BLOCK
    n_sub = max(1, ATTN_UNROLL // d)
    rows = n_sub * blk
    steps = sub_len // rows
    s = sub_len * d

    def cur(kind):
        return lambda bi, n: (bi, 0, n, kind)

    def prev(kind):
        return lambda bi, n: (bi, 0, jnp.maximum(n * n_sub - 1, 0), kind)

    cur_shape = (1, d, rows, GROUP_WIDTH)
    prev_shape = (1, d, blk, GROUP_WIDTH)
    o, lse = pl.pallas_call(
        _attn_kernel,
        grid=(b, steps),
        in_specs=[
            pl.BlockSpec(cur_shape, cur(0)),
            pl.BlockSpec(prev_shape, prev(1)),
            pl.BlockSpec(cur_shape, cur(1)),
            pl.BlockSpec(prev_shape, prev(2)),
            pl.BlockSpec(cur_shape, cur(2)),
        ],
        out_specs=[
            pl.BlockSpec((1, N_PAIRS, d * rows, LANES), lambda bi, n: (bi, 0, n, 0)),
            pl.BlockSpec((1, d * rows, LANES), lambda bi, n: (bi, n, 0)),
        ],
        out_shape=[
            jax.ShapeDtypeStruct((b, N_PAIRS, s, LANES), F32),
            jax.ShapeDtypeStruct((b, s, LANES), F32),
        ],
        scratch_shapes=([pltpu.VMEM((N_PAIRS + 1, FAST_STRIDE, d * rows // FAST_STRIDE, LANES), F32)]
                        if d > FAST_STRIDE else []),
        compiler_params=pltpu.CompilerParams(
            dimension_semantics=("parallel", "arbitrary"),
            vmem_limit_bytes=VMEM_LIMIT_BYTES),
        name=f"attn_d{d}",
    )(qkv, qkv, qkv, qkv, qkv)
    return o, lse


def _log_sigmoid(x):
    return jnp.minimum(x, 0.0) - jnp.log(1.0 + jnp.exp(-jnp.abs(x)))


def _split3(x):
    hi = x.astype(BF16)
    r1 = x - hi.astype(F32)
    mid = r1.astype(BF16)
    lo = (r1 - mid.astype(F32)).astype(BF16)
    return hi, mid, lo


def _gla_pairwise_products(b_scr, qs_scr, kf_scr, pcat_scr):
    sub = GLA_SUB
    tl = lax.broadcasted_iota(jnp.int32, (sub, GLA_HK), 0)
    for pair_i in range(GLA_STEP // (2 * sub)):
        tiles = []
        for r0 in (2 * pair_i * sub, (2 * pair_i + 1) * sub):
            b_blk, q_blk = b_scr[r0:r0 + sub, :], qs_scr[r0:r0 + sub, :]
            row = []
            for s in range(sub):
                d = b_blk - b_scr[r0 + s:r0 + s + 1, :]
                if s > 0:
                    d = jnp.where(tl >= s, d, -jnp.inf)
                row.append(q_blk * kf_scr[r0 + s:r0 + s + 1, :] * jnp.exp2(d))
            tiles.append(row)
        r0 = 2 * pair_i * sub
        for s in range(sub):
            pcat_scr[r0:r0 + 2 * sub, s * GLA_HK:(s + 1) * GLA_HK] = (
                jnp.concatenate([tiles[0][s], tiles[1][s]], axis=0).astype(BF16))


def _gla_reference_factors(b, qs, kf):
    c_len = GLA_CHUNK
    n_chunks = GLA_STEP // c_len
    sub = GLA_SUB

    def z(nrows):
        return jnp.zeros((nrows, GLA_HK), F32)

    qb, kb = [], []
    for r0 in range(0, GLA_STEP, 2 * sub):
        mid = r0 + sub
        b_ref = b[mid - 1:mid]
        qb += [z(sub), qs[mid:mid + sub] * jnp.exp2(b[mid:mid + sub] - b_ref)]
        kb += [kf[r0:mid] * jnp.exp2(b_ref - b[r0:mid]), z(sub)]
    q_block = jnp.concatenate(qb, axis=0).astype(BF16)
    k_block = jnp.concatenate(kb, axis=0).astype(BF16)

    qw, kw = [], []
    for c in range(n_chunks):
        bc, qc, kc = (x[c * c_len:(c + 1) * c_len] for x in (b, qs, kf))
        b15, b31, b47 = bc[15:16], bc[31:32], bc[47:48]
        q1 = jnp.concatenate([z(16), qc[16:32] * jnp.exp2(bc[16:32] - b15), z(32)], axis=0)
        q2 = jnp.concatenate([z(32), qc[32:64] * jnp.exp2(bc[32:64] - b31)], axis=0)
        q3 = jnp.concatenate([z(48), qc[48:64] * jnp.exp2(bc[48:64] - b47)], axis=0)
        k1 = jnp.concatenate([kc[0:16] * jnp.exp2(b15 - bc[0:16]), z(48)], axis=0)
        k2 = jnp.concatenate([kc[0:32] * jnp.exp2(b31 - bc[0:32]), z(32)], axis=0)
        k3 = jnp.concatenate([z(32), kc[32:48] * jnp.exp2(b47 - bc[32:48]), z(16)], axis=0)
        qw.append(jnp.concatenate([q1, q2, q3], axis=1))
        kw.append(jnp.concatenate([k1, k2, k3], axis=1))
    q_within = jnp.concatenate(qw, axis=0).astype(BF16)
    k_within = jnp.concatenate(kw, axis=0).astype(BF16)

    qx, kx = [], []
    for j in range(n_chunks - 1):
        lo, hi = j * c_len, (j + 1) * c_len
        b_ref = b[hi - 1:hi]
        qx.append(jnp.concatenate([z(hi), qs[hi:] * jnp.exp2(b[hi:] - b_ref)], axis=0))
        parts = [kf[lo:hi] * jnp.exp2(b_ref - b[lo:hi])]
        if lo:
            parts.insert(0, z(lo))
        parts.append(z(GLA_STEP - hi))
        kx.append(jnp.concatenate(parts, axis=0))
    q_cross = jnp.concatenate(qx, axis=1).astype(BF16)
    k_cross = jnp.concatenate(kx, axis=1).astype(BF16)
    return (q_block, k_block), (q_within, k_within), (q_cross, k_cross)


def _gla_decay(pa_ref, u_ref, bias_ref, ltri_ref):
    logits = jnp.dot(pa_ref[...].astype(BF16), u_ref[...], preferred_element_type=F32) + bias_ref[...]
    la = _log_sigmoid(logits) * (LOG2_E / GLA_TAU)
    parts = jnp.dot(ltri_ref[...], jnp.concatenate(_split3(la), axis=1), preferred_element_type=F32)
    return parts[:, :GLA_DK] + parts[:, GLA_DK:2 * GLA_DK] + parts[:, 2 * GLA_DK:]


def _gla_kernel(q_ref, k_ref, v_ref, r_ref, pa_ref, pa_next_ref, u_ref, bias_ref, gn_ref, ltri_ref, e_ref,
                o_ref, st_ref, b_scr, qs_scr, kf_scr, pcat_scr, decay_scr, *, steps_per_seq):
    step = pl.program_id(0)
    n = GLA_STEP
    n_sub = q_ref.shape[0] // n

    @pl.when(lax.rem(step, steps_per_seq) == 0)
    def _():
        st_ref[...] = jnp.zeros_like(st_ref)

    @pl.when(step == 0)
    def _():
        decay_scr[...] = _gla_decay(pa_ref.at[0:n], u_ref, bias_ref, ltri_ref)

    def run(rows, next_pa_ref):
        def body(small_span):
            _gla_substep(decay_scr[...], q_ref.at[rows], k_ref.at[rows], v_ref.at[rows],
                         r_ref.at[rows], gn_ref, e_ref, o_ref.at[rows], st_ref,
                         b_scr, qs_scr, kf_scr, pcat_scr, small_span=small_span)
            decay_scr[...] = _gla_decay(next_pa_ref, u_ref, bias_ref, ltri_ref)

        small_span = jnp.max(-decay_scr[n - 1:n, :]) <= GLA_SAFE_SPAN
        pl.when(small_span)(lambda: body(True))
        pl.when(jnp.logical_not(small_span))(lambda: body(False))

    def substep(ss, carry):
        rows = pl.ds(pl.multiple_of(ss * n, n), n)
        nxt = pl.ds(pl.multiple_of((ss + 1) * n, n), n)
        run(rows, pa_ref.at[nxt])
        return carry

    lax.fori_loop(0, n_sub - 1, substep, 0)
    run(slice((n_sub - 1) * n, n_sub * n), pa_next_ref)


def _gla_substep(b_all, q_ref, k_ref, v_ref, r_ref, gn_ref, e_ref,
                 o_ref, st_ref, b_scr, qs_scr, kf_scr, pcat_scr, *, small_span):
    n = GLA_STEP
    nt = (((1,), (1,)), ((), ()))

    heads = []
    for h in range(GLA_HEADS):
        kcols = slice(h * GLA_HK, (h + 1) * GLA_HK)
        b = b_all[:, kcols]
        qs = q_ref[:, kcols].astype(F32) * (GLA_HK ** -0.5)
        kf = k_ref[:, kcols].astype(F32)
        if not small_span:
            b_scr[h], qs_scr[h], kf_scr[h] = b, qs, kf
            _gla_pairwise_products(b_scr.at[h], qs_scr.at[h], kf_scr.at[h],
                                   pcat_scr.at[h * n:(h + 1) * n])
        heads.append((b, qs, kf))

    ri = lax.broadcasted_iota(jnp.int32, (n, n), 0)
    ci = lax.broadcasted_iota(jnp.int32, (n, n), 1)
    if small_span:
        causal = ri >= ci
    else:
        a_diag = jnp.dot(pcat_scr[...], e_ref[...], preferred_element_type=F32)
        same_sub = (ri // GLA_SUB) == (ci // GLA_SUB)
        same_pair = (ri // (2 * GLA_SUB)) == (ci // (2 * GLA_SUB))
        same_chunk = (ri // GLA_CHUNK) == (ci // GLA_CHUNK)
    gn = gn_ref[...]

    for h, (b, qs, kf) in enumerate(heads):
        vcols = slice(h * GLA_HV, (h + 1) * GLA_HV)
        q_in = (qs * jnp.exp2(b)).astype(BF16)
        if small_span:
            k_out = (kf * jnp.exp2(-b)).astype(BF16)
            a = jnp.where(causal, lax.dot_general(q_in, k_out, nt, preferred_element_type=F32), 0.0)
        else:
            a_block, a_within, a_cross = (
                lax.dot_general(qf, kf_, nt, preferred_element_type=F32)
                for qf, kf_ in _gla_reference_factors(b, qs, kf))
            a = jnp.where(same_sub, a_diag[h * n:(h + 1) * n],
                          jnp.where(same_pair, a_block, jnp.where(same_chunk, a_within, a_cross)))
        v = v_ref[:, vcols]
        st = st_ref[h]
        b_last = b[n - 1:n]
        o = (jnp.dot(a.astype(BF16), v, preferred_element_type=F32)
             + lax.dot_general(q_in, st.astype(BF16), nt, preferred_element_type=F32))
        k_st = (kf * jnp.exp2(b_last - b)).astype(BF16)
        upd = lax.dot_general(v, k_st, (((0,), (0,)), ((), ())), preferred_element_type=F32)
        st_ref[h] = st * jnp.exp2(b_last) + upd

        ms = jnp.mean(o * o, axis=-1, keepdims=True)
        y = o * lax.rsqrt(ms + EPS) * gn
        r = r_ref[:, vcols].astype(F32)
        o_ref[:, vcols] = (y * (r * jax.nn.sigmoid(r))).astype(BF16)


def _gla_constants():
    n = GLA_STEP
    idx = np.arange(n)
    ltri = idx[:, None] >= idx[None, :]
    rows = np.arange(GLA_SUB * GLA_HK)
    cols = np.arange(n)
    e = (rows[:, None] // GLA_HK) == (cols[None, :] % GLA_SUB)
    return jnp.asarray(ltri, BF16), jnp.asarray(e, BF16)


def _gla_call(proj2d, pa, u_pad, bias, gn, batch):
    t = proj2d.shape[0]
    n = GLA_STEP
    rows = n * GLA_SUBSTEPS
    steps_per_seq = t // batch // rows
    n_steps = t // rows
    ltri, e = _gla_constants()

    def tok(width, col_block):
        return pl.BlockSpec((rows, width), lambda i: (i, col_block))

    def const(shape):
        return pl.BlockSpec(shape, lambda i: (0, 0))

    pa_next = pl.BlockSpec((n, PA_PAD), lambda i: (jnp.minimum(i + 1, n_steps - 1) * GLA_SUBSTEPS, 0))

    return pl.pallas_call(
        functools.partial(_gla_kernel, steps_per_seq=steps_per_seq),
        grid=(n_steps,),
        in_specs=[
            tok(GLA_DK, P_GQ // GLA_DK),
            tok(GLA_DK, P_GK // GLA_DK),
            tok(GLA_DV, P_GV // GLA_DV),
            tok(GLA_DV, P_GR // GLA_DV),
            tok(PA_PAD, 0),
            pa_next,
            const((PA_PAD, GLA_DK)),
            const((1, GLA_DK)),
            const((1, GLA_HV)),
            const((n, n)),
            const((GLA_SUB * GLA_HK, n)),
        ],
        out_specs=tok(GLA_DV, 0),
        out_shape=jax.ShapeDtypeStruct((t, GLA_DV), BF16),
        scratch_shapes=[
            pltpu.VMEM((GLA_HEADS, GLA_HV, GLA_HK), F32),
            pltpu.VMEM((GLA_HEADS, n, GLA_HK), F32),
            pltpu.VMEM((GLA_HEADS, n, GLA_HK), F32),
            pltpu.VMEM((GLA_HEADS, n, GLA_HK), F32),
            pltpu.VMEM((GLA_HEADS * n, GLA_SUB * GLA_HK), BF16),
            pltpu.VMEM((n, GLA_DK), F32),
        ],
        compiler_params=pltpu.CompilerParams(
            dimension_semantics=("arbitrary",),
            vmem_limit_bytes=VMEM_LIMIT_BYTES),
        name="gla",
    )(proj2d, proj2d, proj2d, proj2d, pa, pa, u_pad, bias, gn, ltri, e)


FF_CHUNK = 1024


def _out_kernel(x_ref, o0_ref, o1_ref, o2_ref, l0_ref, l1_ref, l2_ref, og_ref, gate_a_ref, gate_g_ref, gbias_ref,
                hx_ref, wa_ref, wb_ref, wo_ref, g2_ref, wup_ref, wdn_ref, out_ref):
    l0, l1, l2 = l0_ref[0], l1_ref[0], l2_ref[0]
    mx = jnp.maximum(jnp.maximum(l0, l1), l2)
    e0, e1, e2 = jnp.exp(l0 - mx), jnp.exp(l1 - mx), jnp.exp(l2 - mx)
    inv = 1.0 / (e0 + e1 + e2)
    hx = hx_ref[...]
    def pairs(ref):
        return jnp.concatenate([ref[0, p] for p in range(N_PAIRS)], axis=1)

    def expand(w):
        return jnp.dot(w.astype(BF16), hx, preferred_element_type=F32)

    o_attn = (expand(e0 * inv) * pairs(o0_ref) + expand(e1 * inv) * pairs(o1_ref)
              + expand(e2 * inv) * pairs(o2_ref))
    a = jnp.dot(o_attn.astype(BF16), wa_ref[...], preferred_element_type=F32)
    g = jnp.dot(og_ref[...], wb_ref[...], preferred_element_type=F32)
    gate_a = jax.nn.sigmoid(gate_a_ref[...].astype(F32) + gbias_ref[:, :D_MODEL])
    gate_g = jax.nn.sigmoid(gate_g_ref[...].astype(F32) + gbias_ref[:, D_MODEL:])
    mixed = gate_a * a + gate_g * g
    x1 = x_ref[...] + jnp.dot(mixed.astype(BF16), wo_ref[...], preferred_element_type=F32)

    ms = jnp.mean(x1 * x1, axis=-1, keepdims=True)
    h2 = (x1 * lax.rsqrt(ms + EPS) * g2_ref[...]).astype(BF16)
    hidden = []
    for c in range(D_FF // FF_CHUNK):
        u = jnp.dot(h2, wup_ref[:, c * FF_CHUNK:(c + 1) * FF_CHUNK], preferred_element_type=F32)
        u = jnp.maximum(u, 0.0)
        hidden.append((u * u).astype(BF16))
    out_ref[...] = x1 + jnp.dot(jnp.concatenate(hidden, axis=1), wdn_ref[...],
                                preferred_element_type=F32)


def _out_call(x2d, o_groups, lse_groups, o_gla, proj2d, gbias, wa, wb, wo, g2, wup, wdn, batch, tm):
    t = x2d.shape[0]
    lanes = np.arange(LANES)
    cols = np.arange(GROUP_WIDTH)
    head_expand = jnp.asarray(lanes[:, None] == cols[None, :] // HEAD_DIM, BF16)

    tiles_per_seq = t // batch // tm

    def tok(width):
        return pl.BlockSpec((tm, width), lambda i: (i, 0))

    attn_o = pl.BlockSpec((1, N_PAIRS, tm, LANES),
                          lambda i: (i // tiles_per_seq, 0, i % tiles_per_seq, 0))
    attn_lse = pl.BlockSpec((1, tm, LANES), lambda i: (i // tiles_per_seq, i % tiles_per_seq, 0))

    def const(shape):
        return pl.BlockSpec(shape, lambda i: (0, 0), pipeline_mode=pl.Buffered(1))

    return pl.pallas_call(
        _out_kernel,
        grid=(t // tm,),
        in_specs=[
            tok(D_MODEL),
            attn_o, attn_o, attn_o,
            attn_lse, attn_lse, attn_lse,
            tok(GLA_DV),
            pl.BlockSpec((tm, D_MODEL), lambda i: (i, P_GATE // D_MODEL)),
            pl.BlockSpec((tm, D_MODEL), lambda i: (i, P_GATE // D_MODEL + 1)),
            const((1, 2 * D_MODEL)),
            const((LANES, GROUP_WIDTH)),
            const((GROUP_WIDTH, D_MODEL)),
            const((GLA_DV, D_MODEL)),
            const((D_MODEL, D_MODEL)),
            const((1, D_MODEL)),
            const((D_MODEL, D_FF)),
            const((D_FF, D_MODEL)),
        ],
        out_specs=tok(D_MODEL),
        out_shape=jax.ShapeDtypeStruct((t, D_MODEL), F32),
        compiler_params=pltpu.CompilerParams(
            dimension_semantics=("parallel",),
            vmem_limit_bytes=VMEM_LIMIT_BYTES),
        name="out",
    )(x2d, *o_groups, *lse_groups, o_gla, proj2d, proj2d, gbias, head_expand, wa, wb, wo, g2, wup, wdn)


def _layer(x2d, batch, norm1_g, w_in, gq, gk, gate_up, gate_bias, gla_norm_g, branch_bias,
           w_a, w_b, w_out, norm2_g, w_up, w_down):
    g1 = norm1_g.reshape(1, D_MODEL)
    w_in_bf16 = w_in.astype(BF16)
    w_gate = w_in_bf16[:, O_GATE:O_GATE + 2 * D_MODEL]
    w_pa = jnp.pad(w_in_bf16[:, O_PA:O_PA + GLA_RANK], ((0, 0), (0, PA_PAD - GLA_RANK)))
    q_gain = jnp.tile(gq, HEADS_PER_GROUP) * (HEAD_DIM ** -0.5 * LOG2_E)
    k_gain = jnp.tile(gk, HEADS_PER_GROUP)
    qk_gain = jnp.stack([q_gain, k_gain]).reshape(2, 1, GROUP_WIDTH)

    proj, pa = _main_proj_call(x2d, g1, w_in_bf16, w_gate, w_pa, tm=PROJ_TM, rows=PROJ_ROWS)
    qkv_groups = _attn_proj_call(x2d, g1, w_in_bf16, qk_gain, batch, tm=PROJ_TM, rows=PROJ_ROWS)

    o_groups, lse_groups = [], []
    for qkv, (_, dilation) in zip(qkv_groups, ATTN_GROUPS):
        o, lse = _attn_call(qkv, dilation)
        o_groups.append(o)
        lse_groups.append(lse)

    u_pad = jnp.pad(gate_up, ((0, PA_PAD - GLA_RANK), (0, 0))).astype(BF16)
    o_gla = _gla_call(proj, pa, u_pad, gate_bias.reshape(1, GLA_DK),
                      gla_norm_g.reshape(1, GLA_HV), batch)

    return _out_call(x2d, o_groups, lse_groups, o_gla, proj,
                     branch_bias.reshape(1, 2 * D_MODEL),
                     w_a.astype(BF16), w_b.astype(BF16), w_out.astype(BF16),
                     norm2_g.reshape(1, D_MODEL), w_up.astype(BF16), w_down.astype(BF16), batch, OUT_TM)


def kernel(x, norm1_g, w_in, attn_q_norm_g, attn_k_norm_g, gla_gate_up, gla_gate_bias, gla_out_norm_g, branch_gate_bias, w_attn_branch, w_gla_branch, w_out, norm2_g, w_ff_up, w_ff_down):
    b, s, d = x.shape
    x2d = x.reshape(b * s, d)
    for l in range(norm1_g.shape[0]):
        x2d = _layer(x2d, b, norm1_g[l], w_in[l], attn_q_norm_g[l], attn_k_norm_g[l],
                     gla_gate_up[l], gla_gate_bias[l], gla_out_norm_g[l], branch_gate_bias[l],
                     w_attn_branch[l], w_gla_branch[l], w_out[l], norm2_g[l],
                     w_ff_up[l], w_ff_down[l])
    return x2d.reshape(b, s, d)
```

```python
import functools

import numpy as np
import jax
import jax.numpy as jnp
from jax import lax
from jax.experimental import pallas as pl
from jax.experimental.pallas import tpu as pltpu

F32 = jnp.float32
BF16 = jnp.bfloat16

D_MODEL = 1024
ATTN_GROUPS = ((128, 1), (512, 4), (2048, 16))
N_GROUPS = len(ATTN_GROUPS)
HEADS_PER_GROUP = 8
HEAD_DIM = 64
ATTN_BLOCK = 128
GROUP_WIDTH = HEADS_PER_GROUP * HEAD_DIM
ATTN_WIDTH = 3 * N_GROUPS * GROUP_WIDTH
N_PAIRS = GROUP_WIDTH // 128
ATTN_HEADS_PER_DOT = 4
ATTN_UNROLL = 16

GLA_HEADS = 4
GLA_DK = 512
GLA_DV = 1024
GLA_HK = GLA_DK // GLA_HEADS
GLA_HV = GLA_DV // GLA_HEADS
GLA_RANK = 16
GLA_TAU = 16.0
GLA_CHUNK = 64
GLA_SUB = 8
GLA_STEP = 256
GLA_SUBSTEPS = 4
GLA_SAFE_SPAN = 64.0

D_FF = 4 * D_MODEL
EPS = 1e-6
LOG2_E = 1.4426950408889634
LN_2 = 0.6931471805599453

FAST_STRIDE = 4
LANES = 128
VMEM_LIMIT_BYTES = 56 * 1024 * 1024
PROJ_TM = 1024
PROJ_ROWS = 512
OUT_TM = 512

_ORIG_SIZES = (ATTN_WIDTH, GLA_DK, GLA_DK, GLA_DV, GLA_DV, GLA_RANK, 2 * D_MODEL)
_ORIG_OFF = tuple(int(v) for v in np.cumsum((0,) + _ORIG_SIZES))
O_ATTN, O_GQ, O_GK, O_GV, O_GR, O_PA, O_GATE = _ORIG_OFF[:7]

P_GQ = 0
P_GK = P_GQ + GLA_DK
P_GV = P_GK + GLA_DK
P_GR = P_GV + GLA_DV
P_GATE = P_GR + GLA_DV
P_MAIN = P_GATE + 2 * D_MODEL
GLA_W_BLOCK = (O_GATE - GLA_RANK - O_GQ) // 2
N_KINDS = 3
PA_PAD = LANES


def _rms_norm_rows(x, gain):
    ms = jnp.mean(x * x, axis=-1, keepdims=True)
    return x * lax.rsqrt(ms + EPS) * gain


def _main_proj_kernel(x_ref, g1_ref, wa_ref, wb_ref, wgate_ref, wpa_ref, o_ref, pa_ref, *, rows):
    gain = g1_ref[...]
    for rc in range(x_ref.shape[0] // rows):
        rs = slice(rc * rows, (rc + 1) * rows)
        h = _rms_norm_rows(x_ref[rs, :], gain).astype(BF16)
        pa_ref[rs, :] = jnp.dot(h, wpa_ref[...], preferred_element_type=F32)
        col = 0
        for w_ref in (wa_ref, wb_ref, wgate_ref):
            width = w_ref.shape[1]
            o_ref[rs, col:col + width] = jnp.dot(
                h, w_ref[...], preferred_element_type=F32).astype(BF16)
            col += width


def _main_proj_call(x2d, g1, w_in_bf16, w_gate, w_pa, tm, rows):
    t = x2d.shape[0]
    first_block = O_GQ // GLA_W_BLOCK

    def const(shape, col_block=0):
        return pl.BlockSpec(shape, lambda i: (0, col_block), pipeline_mode=pl.Buffered(1))

    return pl.pallas_call(
        functools.partial(_main_proj_kernel, rows=rows),
        grid=(t // tm,),
        in_specs=[
            pl.BlockSpec((tm, D_MODEL), lambda i: (i, 0)),
            const((1, D_MODEL)),
            const((D_MODEL, GLA_W_BLOCK), first_block),
            const((D_MODEL, GLA_W_BLOCK), first_block + 1),
            const((D_MODEL, 2 * D_MODEL)),
            const((D_MODEL, PA_PAD)),
        ],
        out_specs=[
            pl.BlockSpec((tm, P_MAIN), lambda i: (i, 0)),
            pl.BlockSpec((tm, PA_PAD), lambda i: (i, 0)),
        ],
        out_shape=[
            jax.ShapeDtypeStruct((t, P_MAIN), BF16),
            jax.ShapeDtypeStruct((t, PA_PAD), F32),
        ],
        compiler_params=pltpu.CompilerParams(
            dimension_semantics=("parallel",),
            vmem_limit_bytes=VMEM_LIMIT_BYTES),
        name="proj_main",
    )(x2d, g1, w_in_bf16, w_in_bf16, w_gate, w_pa)


def _qkv_project(h, w_refs, gain_ref, o_ref, chunk, residue_of_block=None):
    d = o_ref.shape[1]
    rows = h.shape[0] // d
    residue_of_block = residue_of_block or list(range(d))
    for kind in range(N_KINDS):
        cols = slice(kind * GROUP_WIDTH, (kind + 1) * GROUP_WIDTH)
        acc = jnp.dot(h, w_refs[kind][...], preferred_element_type=F32)
        if kind < 2:
            sq = acc * acc
            first_head = lax.broadcasted_iota(jnp.int32, (1, LANES), 1) < HEAD_DIM
            scales = []
            for pp in range(N_PAIRS):
                t = sq[:, pp * LANES:(pp + 1) * LANES]
                s_lo = jnp.sum(jnp.where(first_head, t, 0.0), axis=-1, keepdims=True)
                s_hi = jnp.sum(jnp.where(first_head, 0.0, t), axis=-1, keepdims=True)
                scales.append(jnp.where(first_head,
                                        lax.rsqrt(s_lo * (1.0 / HEAD_DIM) + EPS),
                                        lax.rsqrt(s_hi * (1.0 / HEAD_DIM) + EPS)))
            acc = acc * jnp.concatenate(scales, axis=1) * gain_ref[kind]
        y = acc.astype(BF16)
        for j, r in enumerate(residue_of_block):
            o_ref[0, r, chunk * rows:(chunk + 1) * rows, cols] = y[j * rows:(j + 1) * rows, :]


def _attn_proj_kernel(x_ref, g1_ref, *rest):
    n_w = N_KINDS * N_GROUPS
    w_all = rest[:n_w]
    gain_ref, o0_ref, o1_ref, o2_ref, col_scr, col2_scr, perm1_scr, perm2_scr = rest[n_w:]
    w_group = [[w_all[kind * N_GROUPS + g] for kind in range(N_KINDS)] for g in range(N_GROUPS)]
    n_chunks, _, tc, _ = col_scr.shape
    n_col = D_MODEL // LANES
    gain = g1_ref[...]
    for ch in range(n_chunks):
        hf = _rms_norm_rows(x_ref[ch * tc:(ch + 1) * tc, :], gain)
        for c in range(n_col):
            col_scr[ch, c] = hf[:, c * LANES:(c + 1) * LANES]
        _qkv_project(hf.astype(BF16), w_group[0], gain_ref, o0_ref, ch)
        f = o1_ref.shape[1]
        assert o2_ref.shape[1] == f * f
        rows1, rows2 = tc // f, tc // (f * f)
        for a in range(f):
            for c in range(n_col):
                t = col_scr[ch, c, pl.ds(a, rows1, stride=f), :]
                col2_scr[ch, c, a * rows1:(a + 1) * rows1, :] = t
                perm1_scr[ch, a * rows1:(a + 1) * rows1, c * LANES:(c + 1) * LANES] = t.astype(BF16)
        _qkv_project(perm1_scr[ch], w_group[1], gain_ref, o1_ref, ch)
        for a in range(f):
            for b in range(f):
                blk = a * f + b
                for c in range(n_col):
                    perm2_scr[ch, blk * rows2:(blk + 1) * rows2, c * LANES:(c + 1) * LANES] = (
                        col2_scr[ch, c, pl.ds(a * rows1 + b, rows2, stride=f), :].astype(BF16))
        _qkv_project(perm2_scr[ch], w_group[2], gain_ref, o2_ref, ch,
                     residue_of_block=[b * f + a for a in range(f) for b in range(f)])


def _attn_proj_call(x2d, g1, w_in_bf16, qk_gain, batch, tm, rows):
    t = x2d.shape[0]
    s = t // batch
    tiles_per_seq = s // tm
    width = N_KINDS * GROUP_WIDTH
    n_w = N_KINDS * N_GROUPS

    def const(shape):
        return pl.BlockSpec(shape, lambda i: (0,) * len(shape), pipeline_mode=pl.Buffered(1))

    def w_block(j):
        return pl.BlockSpec((D_MODEL, GROUP_WIDTH), lambda i: (0, O_ATTN // GROUP_WIDTH + j),
                            pipeline_mode=pl.Buffered(1))

    return pl.pallas_call(
        _attn_proj_kernel,
        grid=(t // tm,),
        in_specs=[
            pl.BlockSpec((tm, D_MODEL), lambda i: (i, 0)),
            const((1, D_MODEL)),
            *[w_block(j) for j in range(n_w)],
            const((2, 1, GROUP_WIDTH)),
        ],
        out_specs=[
            pl.BlockSpec((1, d, tm // d, width),
                         lambda i: (i // tiles_per_seq, 0, i % tiles_per_seq, 0))
            for _, d in ATTN_GROUPS],
        out_shape=[jax.ShapeDtypeStruct((batch, d, s // d, width), BF16) for _, d in ATTN_GROUPS],
        scratch_shapes=[
            pltpu.VMEM((tm // rows, D_MODEL // LANES, rows, LANES), F32),
            pltpu.VMEM((tm // rows, D_MODEL // LANES, rows, LANES), F32),
            pltpu.VMEM((tm // rows, rows, D_MODEL), BF16),
            pltpu.VMEM((tm // rows, rows, D_MODEL), BF16),
        ],
        compiler_params=pltpu.CompilerParams(
            dimension_semantics=("parallel",),
            vmem_limit_bytes=VMEM_LIMIT_BYTES),
        name="proj_attn",
    )(x2d, g1, *([w_in_bf16] * n_w), qk_gain)


def _attn_kernel(q_ref, kp_ref, kc_ref, vp_ref, vc_ref, o_ref, lse_ref, *stage):
    n = pl.program_id(1)
    d = q_ref.shape[1]
    blk = ATTN_BLOCK
    n_sub = q_ref.shape[2] // blk
    stage_scr = stage[0] if stage else None
    f = FAST_STRIDE

    def put(idx, r, u, tile):
        if stage_scr is None:
            tok = pl.ds(u * blk * d + r, blk, stride=d)
            if idx == N_PAIRS:
                lse_ref[0, tok, :] = tile
            else:
                o_ref[0, idx, tok, :] = tile
        else:
            a, b = r % f, r // f
            stage_scr[idx, a, pl.ds(u * blk * (d // f) + b, blk, stride=d // f), :] = tile
    qi = lax.broadcasted_iota(jnp.int32, (blk, 2 * blk), 0)
    ki = lax.broadcasted_iota(jnp.int32, (blk, 2 * blk), 1)
    band = (ki >= qi) & (ki <= qi + blk)
    band_first = band & ((ki >= blk) | (n > 0))
    lane = lax.broadcasted_iota(jnp.int32, (blk, LANES), 1)
    first_head = lane < HEAD_DIM
    hb = ATTN_HEADS_PER_DOT
    width = hb * HEAD_DIM
    lane_w = lax.broadcasted_iota(jnp.int32, (blk, width), 1)
    head_lanes = [(lane_w >= h * HEAD_DIM) & (lane_w < (h + 1) * HEAD_DIM) for h in range(hb)]
    valid_by_sub = [jnp.concatenate([band_first if u == 0 else band] * hb, axis=0)
                    for u in range(min(n_sub, 2))]

    def unit(r, u):
        rows = slice(u * blk, (u + 1) * blk)
        valid_b = valid_by_sub[min(u, 1)]
        m_tile = jnp.zeros((blk, LANES), F32)
        l_tile = jnp.ones((blk, LANES), F32)
        for g in range(HEADS_PER_GROUP // hb):
            sl = slice(g * width, (g + 1) * width)
            q = q_ref[0, r, rows, sl]
            if u == 0:
                k_prev, v_prev = kp_ref[0, r, :, sl], vp_ref[0, r, :, sl]
            else:
                prev_rows = slice((u - 1) * blk, u * blk)
                k_prev, v_prev = kc_ref[0, r, prev_rows, sl], vc_ref[0, r, prev_rows, sl]
            k = jnp.concatenate([k_prev, kc_ref[0, r, rows, sl]], axis=0)
            v = jnp.concatenate([v_prev, vc_ref[0, r, rows, sl]], axis=0)
            zero = jnp.zeros_like(q)
            q_rows = jnp.concatenate([jnp.where(head_lanes[h], q, zero) for h in range(hb)],
                                     axis=0)
            s = lax.dot_general(q_rows, k, (((1,), (1,)), ((), ())), preferred_element_type=F32)
            s = jnp.where(valid_b, s, -jnp.inf)
            m = jnp.max(s, axis=-1, keepdims=True)
            p = jnp.exp2(s - m)
            l = jnp.sum(p, axis=-1, keepdims=True)
            p = p.astype(BF16)
            inv_l = 1.0 / l
            for h in range(hb):
                head = g * hb + h
                m_tile = jnp.where(lane == head, m[h * blk:(h + 1) * blk], m_tile)
                l_tile = jnp.where(lane == head, l[h * blk:(h + 1) * blk], l_tile)
            for pp in range(hb // 2):
                rows2 = slice(2 * pp * blk, (2 * pp + 2) * blk)
                pv = jnp.dot(p[rows2], v[:, pp * LANES:(pp + 1) * LANES],
                             preferred_element_type=F32) * inv_l[rows2]
                put(g * (hb // 2) + pp, r, u, jnp.where(first_head, pv[:blk], pv[blk:]))
        put(N_PAIRS, r, u, m_tile * LN_2 + jnp.log(l_tile))

    if d * n_sub <= ATTN_UNROLL:
        for r in range(d):
            for u in range(n_sub):
                unit(r, u)
        if stage_scr is not None:
            for a in range(f):
                merged = pl.ds(a, stage_scr.shape[2], stride=f)
                for idx in range(N_PAIRS):
                    o_ref[0, idx, merged, :] = stage_scr[idx, a]
                lse_ref[0, merged, :] = stage_scr[N_PAIRS, a]
    else:
        assert stage_scr is None
        def residue(r, carry):
            for u in range(n_sub):
                unit(r, u)
            return carry
        lax.fori_loop(0, d, residue, 0, unroll=ATTN_UNROLL // n_sub)


def _attn_call(qkv, dilation):
    b, d, sub_len, _ = qkv.shape
    assert d == dilation
    blk = ATTN_BLOCK
    n_sub = max(1, ATTN_UNROLL // d)
    rows = n_sub * blk
    steps = sub_len // rows
    s = sub_len * d

    def cur(kind):
        return lambda bi, n: (bi, 0, n, kind)

    def prev(kind):
        return lambda bi, n: (bi, 0, jnp.maximum(n * n_sub - 1, 0), kind)

    cur_shape = (1, d, rows, GROUP_WIDTH)
    prev_shape = (1, d, blk, GROUP_WIDTH)
    o, lse = pl.pallas_call(
        _attn_kernel,
        grid=(b, steps),
        in_specs=[
            pl.BlockSpec(cur_shape, cur(0)),
            pl.BlockSpec(prev_shape, prev(1)),
            pl.BlockSpec(cur_shape, cur(1)),
            pl.BlockSpec(prev_shape, prev(2)),
            pl.BlockSpec(cur_shape, cur(2)),
        ],
        out_specs=[
            pl.BlockSpec((1, N_PAIRS, d * rows, LANES), lambda bi, n: (bi, 0, n, 0)),
            pl.BlockSpec((1, d * rows, LANES), lambda bi, n: (bi, n, 0)),
        ],
        out_shape=[
            jax.ShapeDtypeStruct((b, N_PAIRS, s, LANES), F32),
            jax.ShapeDtypeStruct((b, s, LANES), F32),
        ],
        scratch_shapes=([pltpu.VMEM((N_PAIRS + 1, FAST_STRIDE, d * rows // FAST_STRIDE, LANES), F32)]
                        if d > FAST_STRIDE else []),
        compiler_params=pltpu.CompilerParams(
            dimension_semantics=("parallel", "arbitrary"),
            vmem_limit_bytes=VMEM_LIMIT_BYTES),
        name=f"attn_d{d}",
    )(qkv, qkv, qkv, qkv, qkv)
    return o, lse


def _log_sigmoid(x):
    return jnp.minimum(x, 0.0) - jnp.log(1.0 + jnp.exp(-jnp.abs(x)))


def _split3(x):
    hi = x.astype(BF16)
    r1 = x - hi.astype(F32)
    mid = r1.astype(BF16)
    lo = (r1 - mid.astype(F32)).astype(BF16)
    return hi, mid, lo


def _gla_pairwise_products(b_scr, qs_scr, kf_scr, pcat_scr):
    sub = GLA_SUB
    tl = lax.broadcasted_iota(jnp.int32, (sub, GLA_HK), 0)
    for pair_i in range(GLA_STEP // (2 * sub)):
        tiles = []
        for r0 in (2 * pair_i * sub, (2 * pair_i + 1) * sub):
            b_blk, q_blk = b_scr[r0:r0 + sub, :], qs_scr[r0:r0 + sub, :]
            row = []
            for s in range(sub):
                d = b_blk - b_scr[r0 + s:r0 + s + 1, :]
                if s > 0:
                    d = jnp.where(tl >= s, d, -jnp.inf)
                row.append(q_blk * kf_scr[r0 + s:r0 + s + 1, :] * jnp.exp2(d))
            tiles.append(row)
        r0 = 2 * pair_i * sub
        for s in range(sub):
            pcat_scr[r0:r0 + 2 * sub, s * GLA_HK:(s + 1) * GLA_HK] = (
                jnp.concatenate([tiles[0][s], tiles[1][s]], axis=0).astype(BF16))


def _gla_reference_factors(b, qs, kf):
    c_len = GLA_CHUNK
    n_chunks = GLA_STEP // c_len
    sub = GLA_SUB

    def z(nrows):
        return jnp.zeros((nrows, GLA_HK), F32)

    qb, kb = [], []
    for r0 in range(0, GLA_STEP, 2 * sub):
        mid = r0 + sub
        b_ref = b[mid - 1:mid]
        qb += [z(sub), qs[mid:mid + sub] * jnp.exp2(b[mid:mid + sub] - b_ref)]
        kb += [kf[r0:mid] * jnp.exp2(b_ref - b[r0:mid]), z(sub)]
    q_block = jnp.concatenate(qb, axis=0).astype(BF16)
    k_block = jnp.concatenate(kb, axis=0).astype(BF16)

    qw, kw = [], []
    for c in range(n_chunks):
        bc, qc, kc = (x[c * c_len:(c + 1) * c_len] for x in (b, qs, kf))
        b15, b31, b47 = bc[15:16], bc[31:32], bc[47:48]
        q1 = jnp.concatenate([z(16), qc[16:32] * jnp.exp2(bc[16:32] - b15), z(32)], axis=0)
        q2 = jnp.concatenate([z(32), qc[32:64] * jnp.exp2(bc[32:64] - b31)], axis=0)
        q3 = jnp.concatenate([z(48), qc[48:64] * jnp.exp2(bc[48:64] - b47)], axis=0)
        k1 = jnp.concatenate([kc[0:16] * jnp.exp2(b15 - bc[0:16]), z(48)], axis=0)
        k2 = jnp.concatenate([kc[0:32] * jnp.exp2(b31 - bc[0:32]), z(32)], axis=0)
        k3 = jnp.concatenate([z(32), kc[32:48] * jnp.exp2(b47 - bc[32:48]), z(16)], axis=0)
        qw.append(jnp.concatenate([q1, q2, q3], axis=1))
        kw.append(jnp.concatenate([k1, k2, k3], axis=1))
    q_within = jnp.concatenate(qw, axis=0).astype(BF16)
    k_within = jnp.concatenate(kw, axis=0).astype(BF16)

    qx, kx = [], []
    for j in range(n_chunks - 1):
        lo, hi = j * c_len, (j + 1) * c_len
        b_ref = b[hi - 1:hi]
        qx.append(jnp.concatenate([z(hi), qs[hi:] * jnp.exp2(b[hi:] - b_ref)], axis=0))
        parts = [kf[lo:hi] * jnp.exp2(b_ref - b[lo:hi])]
        if lo:
            parts.insert(0, z(lo))
        parts.append(z(GLA_STEP - hi))
        kx.append(jnp.concatenate(parts, axis=0))
    q_cross = jnp.concatenate(qx, axis=1).astype(BF16)
    k_cross = jnp.concatenate(kx, axis=1).astype(BF16)
    return (q_block, k_block), (q_within, k_within), (q_cross, k_cross)


def _gla_decay(pa_ref, u_ref, bias_ref, ltri_ref):
    logits = jnp.dot(pa_ref[...].astype(BF16), u_ref[...], preferred_element_type=F32) + bias_ref[...]
    la = _log_sigmoid(logits) * (LOG2_E / GLA_TAU)
    parts = jnp.dot(ltri_ref[...], jnp.concatenate(_split3(la), axis=1), preferred_element_type=F32)
    return parts[:, :GLA_DK] + parts[:, GLA_DK:2 * GLA_DK] + parts[:, 2 * GLA_DK:]


def _gla_kernel(q_ref, k_ref, v_ref, r_ref, pa_ref, pa_next_ref, u_ref, bias_ref, gn_ref, ltri_ref, e_ref,
                o_ref, st_ref, b_scr, qs_scr, kf_scr, pcat_scr, decay_scr, *, steps_per_seq):
    step = pl.program_id(0)
    n = GLA_STEP
    n_sub = q_ref.shape[0] // n

    @pl.when(lax.rem(step, steps_per_seq) == 0)
    def _():
        st_ref[...] = jnp.zeros_like(st_ref)

    @pl.when(step == 0)
    def _():
        decay_scr[...] = _gla_decay(pa_ref.at[0:n], u_ref, bias_ref, ltri_ref)

    def run(rows, next_pa_ref):
        def body(small_span):
            _gla_substep(decay_scr[...], q_ref.at[rows], k_ref.at[rows], v_ref.at[rows],
                         r_ref.at[rows], gn_ref, e_ref, o_ref.at[rows], st_ref,
                         b_scr, qs_scr, kf_scr, pcat_scr, small_span=small_span)
            decay_scr[...] = _gla_decay(next_pa_ref, u_ref, bias_ref, ltri_ref)

        small_span = jnp.max(-decay_scr[n - 1:n, :]) <= GLA_SAFE_SPAN
        pl.when(small_span)(lambda: body(True))
        pl.when(jnp.logical_not(small_span))(lambda: body(False))

    def substep(ss, carry):
        rows = pl.ds(pl.multiple_of(ss * n, n), n)
        nxt = pl.ds(pl.multiple_of((ss + 1) * n, n), n)
        run(rows, pa_ref.at[nxt])
        return carry

    lax.fori_loop(0, n_sub - 1, substep, 0)
    run(slice((n_sub - 1) * n, n_sub * n), pa_next_ref)


def _gla_substep(b_all, q_ref, k_ref, v_ref, r_ref, gn_ref, e_ref,
                 o_ref, st_ref, b_scr, qs_scr, kf_scr, pcat_scr, *, small_span):
    n = GLA_STEP
    nt = (((1,), (1,)), ((), ()))

    heads = []
    for h in range(GLA_HEADS):
        kcols = slice(h * GLA_HK, (h + 1) * GLA_HK)
        b = b_all[:, kcols]
        qs = q_ref[:, kcols].astype(F32) * (GLA_HK ** -0.5)
        kf = k_ref[:, kcols].astype(F32)
        if not small_span:
            b_scr[h], qs_scr[h], kf_scr[h] = b, qs, kf
            _gla_pairwise_products(b_scr.at[h], qs_scr.at[h], kf_scr.at[h],
                                   pcat_scr.at[h * n:(h + 1) * n])
        heads.append((b, qs, kf))

    ri = lax.broadcasted_iota(jnp.int32, (n, n), 0)
    ci = lax.broadcasted_iota(jnp.int32, (n, n), 1)
    if small_span:
        causal = ri >= ci
    else:
        a_diag = jnp.dot(pcat_scr[...], e_ref[...], preferred_element_type=F32)
        same_sub = (ri // GLA_SUB) == (ci // GLA_SUB)
        same_pair = (ri // (2 * GLA_SUB)) == (ci // (2 * GLA_SUB))
        same_chunk = (ri // GLA_CHUNK) == (ci // GLA_CHUNK)
    gn = gn_ref[...]

    for h, (b, qs, kf) in enumerate(heads):
        vcols = slice(h * GLA_HV, (h + 1) * GLA_HV)
        q_in = (qs * jnp.exp2(b)).astype(BF16)
        if small_span:
            k_out = (kf * jnp.exp2(-b)).astype(BF16)
            a = jnp.where(causal, lax.dot_general(q_in, k_out, nt, preferred_element_type=F32), 0.0)
        else:
            a_block, a_within, a_cross = (
                lax.dot_general(qf, kf_, nt, preferred_element_type=F32)
                for qf, kf_ in _gla_reference_factors(b, qs, kf))
            a = jnp.where(same_sub, a_diag[h * n:(h + 1) * n],
                          jnp.where(same_pair, a_block, jnp.where(same_chunk, a_within, a_cross)))
        v = v_ref[:, vcols]
        st = st_ref[h]
        b_last = b[n - 1:n]
        o = (jnp.dot(a.astype(BF16), v, preferred_element_type=F32)
             + lax.dot_general(q_in, st.astype(BF16), nt, preferred_element_type=F32))
        k_st = (kf * jnp.exp2(b_last - b)).astype(BF16)
        upd = lax.dot_general(v, k_st, (((0,), (0,)), ((), ())), preferred_element_type=F32)
        st_ref[h] = st * jnp.exp2(b_last) + upd

        ms = jnp.mean(o * o, axis=-1, keepdims=True)
        y = o * lax.rsqrt(ms + EPS) * gn
        r = r_ref[:, vcols].astype(F32)
        o_ref[:, vcols] = (y * (r * jax.nn.sigmoid(r))).astype(BF16)


def _gla_constants():
    n = GLA_STEP
    idx = np.arange(n)
    ltri = idx[:, None] >= idx[None, :]
    rows = np.arange(GLA_SUB * GLA_HK)
    cols = np.arange(n)
    e = (rows[:, None] // GLA_HK) == (cols[None, :] % GLA_SUB)
    return jnp.asarray(ltri, BF16), jnp.asarray(e, BF16)


def _gla_call(proj2d, pa, u_pad, bias, gn, batch):
    t = proj2d.shape[0]
    n = GLA_STEP
    rows = n * GLA_SUBSTEPS
    steps_per_seq = t // batch // rows
    n_steps = t // rows
    ltri, e = _gla_constants()

    def tok(width, col_block):
        return pl.BlockSpec((rows, width), lambda i: (i, col_block))

    def const(shape):
        return pl.BlockSpec(shape, lambda i: (0, 0))

    pa_next = pl.BlockSpec((n, PA_PAD), lambda i: (jnp.minimum(i + 1, n_steps - 1) * GLA_SUBSTEPS, 0))

    return pl.pallas_call(
        functools.partial(_gla_kernel, steps_per_seq=steps_per_seq),
        grid=(n_steps,),
        in_specs=[
            tok(GLA_DK, P_GQ // GLA_DK),
            tok(GLA_DK, P_GK // GLA_DK),
            tok(GLA_DV, P_GV // GLA_DV),
            tok(GLA_DV, P_GR // GLA_DV),
            tok(PA_PAD, 0),
            pa_next,
            const((PA_PAD, GLA_DK)),
            const((1, GLA_DK)),
            const((1, GLA_HV)),
            const((n, n)),
            const((GLA_SUB * GLA_HK, n)),
        ],
        out_specs=tok(GLA_DV, 0),
        out_shape=jax.ShapeDtypeStruct((t, GLA_DV), BF16),
        scratch_shapes=[
            pltpu.VMEM((GLA_HEADS, GLA_HV, GLA_HK), F32),
            pltpu.VMEM((GLA_HEADS, n, GLA_HK), F32),
            pltpu.VMEM((GLA_HEADS, n, GLA_HK), F32),
            pltpu.VMEM((GLA_HEADS, n, GLA_HK), F32),
            pltpu.VMEM((GLA_HEADS * n, GLA_SUB * GLA_HK), BF16),
            pltpu.VMEM((n, GLA_DK), F32),
        ],
        compiler_params=pltpu.CompilerParams(
            dimension_semantics=("arbitrary",),
            vmem_limit_bytes=VMEM_LIMIT_BYTES),
        name="gla",
    )(proj2d, proj2d, proj2d, proj2d, pa, pa, u_pad, bias, gn, ltri, e)


FF_CHUNK = 1024


def _out_kernel(x_ref, o0_ref, o1_ref, o2_ref, l0_ref, l1_ref, l2_ref, og_ref, gate_a_ref, gate_g_ref, gbias_ref,
                hx_ref, wa_ref, wb_ref, wo_ref, g2_ref, wup_ref, wdn_ref, out_ref):
    l0, l1, l2 = l0_ref[0], l1_ref[0], l2_ref[0]
    mx = jnp.maximum(jnp.maximum(l0, l1), l2)
    e0, e1, e2 = jnp.exp(l0 - mx), jnp.exp(l1 - mx), jnp.exp(l2 - mx)
    inv = 1.0 / (e0 + e1 + e2)
    hx = hx_ref[...]
    def pairs(ref):
        return jnp.concatenate([ref[0, p] for p in range(N_PAIRS)], axis=1)

    def expand(w):
        return jnp.dot(w.astype(BF16), hx, preferred_element_type=F32)

    o_attn = (expand(e0 * inv) * pairs(o0_ref) + expand(e1 * inv) * pairs(o1_ref)
              + expand(e2 * inv) * pairs(o2_ref))
    a = jnp.dot(o_attn.astype(BF16), wa_ref[...], preferred_element_type=F32)
    g = jnp.dot(og_ref[...], wb_ref[...], preferred_element_type=F32)
    gate_a = jax.nn.sigmoid(gate_a_ref[...].astype(F32) + gbias_ref[:, :D_MODEL])
    gate_g = jax.nn.sigmoid(gate_g_ref[...].astype(F32) + gbias_ref[:, D_MODEL:])
    mixed = gate_a * a + gate_g * g
    x1 = x_ref[...] + jnp.dot(mixed.astype(BF16), wo_ref[...], preferred_element_type=F32)

    ms = jnp.mean(x1 * x1, axis=-1, keepdims=True)
    h2 = (x1 * lax.rsqrt(ms + EPS) * g2_ref[...]).astype(BF16)
    hidden = []
    for c in range(D_FF // FF_CHUNK):
        u = jnp.dot(h2, wup_ref[:, c * FF_CHUNK:(c + 1) * FF_CHUNK], preferred_element_type=F32)
        u = jnp.maximum(u, 0.0)
        hidden.append((u * u).astype(BF16))
    out_ref[...] = x1 + jnp.dot(jnp.concatenate(hidden, axis=1), wdn_ref[...],
                                preferred_element_type=F32)


def _out_call(x2d, o_groups, lse_groups, o_gla, proj2d, gbias, wa, wb, wo, g2, wup, wdn, batch, tm):
    t = x2d.shape[0]
    lanes = np.arange(LANES)
    cols = np.arange(GROUP_WIDTH)
    head_expand = jnp.asarray(lanes[:, None] == cols[None, :] // HEAD_DIM, BF16)

    tiles_per_seq = t // batch // tm

    def tok(width):
        return pl.BlockSpec((tm, width), lambda i: (i, 0))

    attn_o = pl.BlockSpec((1, N_PAIRS, tm, LANES),
                          lambda i: (i // tiles_per_seq, 0, i % tiles_per_seq, 0))
    attn_lse = pl.BlockSpec((1, tm, LANES), lambda i: (i // tiles_per_seq, i % tiles_per_seq, 0))

    def const(shape):
        return pl.BlockSpec(shape, lambda i: (0, 0), pipeline_mode=pl.Buffered(1))

    return pl.pallas_call(
        _out_kernel,
        grid=(t // tm,),
        in_specs=[
            tok(D_MODEL),
            attn_o, attn_o, attn_o,
            attn_lse, attn_lse, attn_lse,
            tok(GLA_DV),
            pl.BlockSpec((tm, D_MODEL), lambda i: (i, P_GATE // D_MODEL)),
            pl.BlockSpec((tm, D_MODEL), lambda i: (i, P_GATE // D_MODEL + 1)),
            const((1, 2 * D_MODEL)),
            const((LANES, GROUP_WIDTH)),
            const((GROUP_WIDTH, D_MODEL)),
            const((GLA_DV, D_MODEL)),
            const((D_MODEL, D_MODEL)),
            const((1, D_MODEL)),
            const((D_MODEL, D_FF)),
            const((D_FF, D_MODEL)),
        ],
        out_specs=tok(D_MODEL),
        out_shape=jax.ShapeDtypeStruct((t, D_MODEL), F32),
        compiler_params=pltpu.CompilerParams(
            dimension_semantics=("parallel",),
            vmem_limit_bytes=VMEM_LIMIT_BYTES),
        name="out",
    )(x2d, *o_groups, *lse_groups, o_gla, proj2d, proj2d, gbias, head_expand, wa, wb, wo, g2, wup, wdn)


def _layer(x2d, batch, norm1_g, w_in, gq, gk, gate_up, gate_bias, gla_norm_g, branch_bias,
           w_a, w_b, w_out, norm2_g, w_up, w_down):
    g1 = norm1_g.reshape(1, D_MODEL)
    w_in_bf16 = w_in.astype(BF16)
    w_gate = w_in_bf16[:, O_GATE:O_GATE + 2 * D_MODEL]
    w_pa = jnp.pad(w_in_bf16[:, O_PA:O_PA + GLA_RANK], ((0, 0), (0, PA_PAD - GLA_RANK)))
    q_gain = jnp.tile(gq, HEADS_PER_GROUP) * (HEAD_DIM ** -0.5 * LOG2_E)
    k_gain = jnp.tile(gk, HEADS_PER_GROUP)
    qk_gain = jnp.stack([q_gain, k_gain]).reshape(2, 1, GROUP_WIDTH)

    proj, pa = _main_proj_call(x2d, g1, w_in_bf16, w_gate, w_pa, tm=PROJ_TM, rows=PROJ_ROWS)
    qkv_groups = _attn_proj_call(x2d, g1, w_in_bf16, qk_gain, batch, tm=PROJ_TM, rows=PROJ_ROWS)

    o_groups, lse_groups = [], []
    for qkv, (_, dilation) in zip(qkv_groups, ATTN_GROUPS):
        o, lse = _attn_call(qkv, dilation)
        o_groups.append(o)
        lse_groups.append(lse)

    u_pad = jnp.pad(gate_up, ((0, PA_PAD - GLA_RANK), (0, 0))).astype(BF16)
    o_gla = _gla_call(proj, pa, u_pad, gate_bias.reshape(1, GLA_DK),
                      gla_norm_g.reshape(1, GLA_HV), batch)

    return _out_call(x2d, o_groups, lse_groups, o_gla, proj,
                     branch_bias.reshape(1, 2 * D_MODEL),
                     w_a.astype(BF16), w_b.astype(BF16), w_out.astype(BF16),
                     norm2_g.reshape(1, D_MODEL), w_up.astype(BF16), w_down.astype(BF16), batch, OUT_TM)


def kernel(x, norm1_g, w_in, attn_q_norm_g, attn_k_norm_g, gla_gate_up, gla_gate_bias, gla_out_norm_g, branch_gate_bias, w_attn_branch, w_gla_branch, w_out, norm2_g, w_ff_up, w_ff_down):
    b, s, d = x.shape
    x2d = x.reshape(b * s, d)
    for l in range(norm1_g.shape[0]):
        x2d = _layer(x2d, b, norm1_g[l], w_in[l], attn_q_norm_g[l], attn_k_norm_g[l],
                     gla_gate_up[l], gla_gate_bias[l], gla_out_norm_g[l], branch_gate_bias[l],
                     w_attn_branch[l], w_gla_branch[l], w_out[l], norm2_g[l],
                     w_ff_up[l], w_ff_down[l])
    return x2d.reshape(b, s, d)
```

```python
import functools

import numpy as np
import jax
import jax.numpy as jnp
from jax import lax
from jax.experimental import pallas as pl
from jax.experimental.pallas import tpu as pltpu

F32 = jnp.float32
BF16 = jnp.bfloat16

FAST_STRIDE = 4
LANES = 128
VMEM_LIMIT_BYTES = 56 * 1024 * 1024
PROJ_TM = 1024
PROJ_ROWS = 512
OUT_TM = 512

D_MODEL = 1024
ATTN_GROUPS = ((128, 1), (512, 4), (2048, 16))
N_GROUPS = len(ATTN_GROUPS)
HEADS_PER_GROUP = 8
HEAD_DIM = 64
ATTN_BLOCK = 128
GROUP_WIDTH = HEADS_PER_GROUP * HEAD_DIM
ATTN_WIDTH = 3 * N_GROUPS * GROUP_WIDTH
N_PAIRS = GROUP_WIDTH // LANES
ATTN_HEADS_PER_DOT = 4
ATTN_UNROLL = 16

GLA_HEADS = 4
GLA_DK = 512
GLA_DV = 1024
GLA_HK = GLA_DK // GLA_HEADS
GLA_HV = GLA_DV // GLA_HEADS
GLA_RANK = 16
GLA_TAU = 16.0
GLA_CHUNK = 64
GLA_SUB = 8
GLA_STEP = 256
GLA_SUBSTEPS = 4
GLA_SAFE_SPAN = 64.0

D_FF = 4 * D_MODEL
EPS = 1e-6
LOG2_E = 1.4426950408889634
LN_2 = 0.6931471805599453

_ORIG_SIZES = (ATTN_WIDTH, GLA_DK, GLA_DK, GLA_DV, GLA_DV, GLA_RANK, 2 * D_MODEL)
_ORIG_OFF = tuple(int(v) for v in np.cumsum((0,) + _ORIG_SIZES))
O_ATTN, O_GQ, O_GK, O_GV, O_GR, O_PA, O_GATE = _ORIG_OFF[:7]

P_GQ = 0
P_GK = P_GQ + GLA_DK
P_GV = P_GK + GLA_DK
P_GR = P_GV + GLA_DV
P_GATE = P_GR + GLA_DV
P_MAIN = P_GATE + 2 * D_MODEL
GLA_W_BLOCK = (O_GATE - GLA_RANK - O_GQ) // 2
N_KINDS = 3
PA_PAD = LANES


def _rms_norm_rows(x, gain):
    ms = jnp.mean(x * x, axis=-1, keepdims=True)
    return x * lax.rsqrt(ms + EPS) * gain


def _main_proj_kernel(x_ref, g1_ref, wa_ref, wb_ref, wgate_ref, wpa_ref, o_ref, pa_ref, *, rows):
    gain = g1_ref[...]
    for rc in range(x_ref.shape[0] // rows):
        rs = slice(rc * rows, (rc + 1) * rows)
        h = _rms_norm_rows(x_ref[rs, :], gain).astype(BF16)
        pa_ref[rs, :] = jnp.dot(h, wpa_ref[...], preferred_element_type=F32)
        col = 0
        for w_ref in (wa_ref, wb_ref, wgate_ref):
            width = w_ref.shape[1]
            o_ref[rs, col:col + width] = jnp.dot(
                h, w_ref[...], preferred_element_type=F32).astype(BF16)
            col += width


def _main_proj_call(x2d, g1, w_in_bf16, w_gate, w_pa, tm, rows):
    t = x2d.shape[0]
    first_block = O_GQ // GLA_W_BLOCK

    def const(shape, col_block=0):
        return pl.BlockSpec(shape, lambda i: (0, col_block), pipeline_mode=pl.Buffered(1))

    return pl.pallas_call(
        functools.partial(_main_proj_kernel, rows=rows),
        grid=(t // tm,),
        in_specs=[
            pl.BlockSpec((tm, D_MODEL), lambda i: (i, 0)),
            const((1, D_MODEL)),
            const((D_MODEL, GLA_W_BLOCK), first_block),
            const((D_MODEL, GLA_W_BLOCK), first_block + 1),
            const((D_MODEL, 2 * D_MODEL)),
            const((D_MODEL, PA_PAD)),
        ],
        out_specs=[
            pl.BlockSpec((tm, P_MAIN), lambda i: (i, 0)),
            pl.BlockSpec((tm, PA_PAD), lambda i: (i, 0)),
        ],
        out_shape=[
            jax.ShapeDtypeStruct((t, P_MAIN), BF16),
            jax.ShapeDtypeStruct((t, PA_PAD), F32),
        ],
        compiler_params=pltpu.CompilerParams(
            dimension_semantics=("parallel",),
            vmem_limit_bytes=VMEM_LIMIT_BYTES),
        name="proj_main",
    )(x2d, g1, w_in_bf16, w_in_bf16, w_gate, w_pa)


def _qkv_project(h, w_refs, gain_ref, o_ref, chunk, residue_of_block=None):
    d = o_ref.shape[1]
    rows = h.shape[0] // d
    residue_of_block = residue_of_block or list(range(d))
    for kind in range(N_KINDS):
        cols = slice(kind * GROUP_WIDTH, (kind + 1) * GROUP_WIDTH)
        acc = jnp.dot(h, w_refs[kind][...], preferred_element_type=F32)
        if kind < 2:
            sq = acc * acc
            first_head = lax.broadcasted_iota(jnp.int32, (1, LANES), 1) < HEAD_DIM
            scales = []
            for pp in range(N_PAIRS):
                t = sq[:, pp * LANES:(pp + 1) * LANES]
                s_lo = jnp.sum(jnp.where(first_head, t, 0.0), axis=-1, keepdims=True)
                s_hi = jnp.sum(jnp.where(first_head, 0.0, t), axis=-1, keepdims=True)
                scales.append(jnp.where(first_head,
                                        lax.rsqrt(s_lo * (1.0 / HEAD_DIM) + EPS),
                                        lax.rsqrt(s_hi * (1.0 / HEAD_DIM) + EPS)))
            acc = acc * jnp.concatenate(scales, axis=1) * gain_ref[kind]
        y = acc.astype(BF16)
        for j, r in enumerate(residue_of_block):
            o_ref[0, r, chunk * rows:(chunk + 1) * rows, cols] = y[j * rows:(j + 1) * rows, :]


def _attn_proj_kernel(x_ref, g1_ref, *rest):
    n_w = N_KINDS * N_GROUPS
    w_all = rest[:n_w]
    gain_ref, o0_ref, o1_ref, o2_ref, col_scr, col2_scr, perm1_scr, perm2_scr = rest[n_w:]
    w_group = [[w_all[kind * N_GROUPS + g] for kind in range(N_KINDS)] for g in range(N_GROUPS)]
    n_chunks, _, tc, _ = col_scr.shape
    n_col = D_MODEL // LANES
    gain = g1_ref[...]
    for ch in range(n_chunks):
        hf = _rms_norm_rows(x_ref[ch * tc:(ch + 1) * tc, :], gain)
        for c in range(n_col):
            col_scr[ch, c] = hf[:, c * LANES:(c + 1) * LANES]
        _qkv_project(hf.astype(BF16), w_group[0], gain_ref, o0_ref, ch)
        f = o1_ref.shape[1]
        assert o2_ref.shape[1] == f * f
        rows1, rows2 = tc // f, tc // (f * f)
        for a in range(f):
            for c in range(n_col):
                t = col_scr[ch, c, pl.ds(a, rows1, stride=f), :]
                col2_scr[ch, c, a * rows1:(a + 1) * rows1, :] = t
                perm1_scr[ch, a * rows1:(a + 1) * rows1, c * LANES:(c + 1) * LANES] = t.astype(BF16)
        _qkv_project(perm1_scr[ch], w_group[1], gain_ref, o1_ref, ch)
        for a in range(f):
            for b in range(f):
                blk = a * f + b
                for c in range(n_col):
                    perm2_scr[ch, blk * rows2:(blk + 1) * rows2, c * LANES:(c + 1) * LANES] = (
                        col2_scr[ch, c, pl.ds(a * rows1 + b, rows2, stride=f), :].astype(BF16))
        _qkv_project(perm2_scr[ch], w_group[2], gain_ref, o2_ref, ch,
                     residue_of_block=[b * f + a for a in range(f) for b in range(f)])


def _attn_proj_call(x2d, g1, w_in_bf16, qk_gain, batch, tm, rows):
    t = x2d.shape[0]
    s = t // batch
    tiles_per_seq = s // tm
    width = N_KINDS * GROUP_WIDTH
    n_w = N_KINDS * N_GROUPS

    def const(shape):
        return pl.BlockSpec(shape, lambda i: (0,) * len(shape), pipeline_mode=pl.Buffered(1))

    def w_block(j):
        return pl.BlockSpec((D_MODEL, GROUP_WIDTH), lambda i: (0, O_ATTN // GROUP_WIDTH + j),
                            pipeline_mode=pl.Buffered(1))

    return pl.pallas_call(
        _attn_proj_kernel,
        grid=(t // tm,),
        in_specs=[
            pl.BlockSpec((tm, D_MODEL), lambda i: (i, 0)),
            const((1, D_MODEL)),
            *[w_block(j) for j in range(n_w)],
            const((2, 1, GROUP_WIDTH)),
        ],
        out_specs=[
            pl.BlockSpec((1, d, tm // d, width),
                         lambda i: (i // tiles_per_seq, 0, i % tiles_per_seq, 0))
            for _, d in ATTN_GROUPS],
        out_shape=[jax.ShapeDtypeStruct((batch, d, s // d, width), BF16) for _, d in ATTN_GROUPS],
        scratch_shapes=[
            pltpu.VMEM((tm // rows, D_MODEL // LANES, rows, LANES), F32),
            pltpu.VMEM((tm // rows, D_MODEL // LANES, rows, LANES), F32),
            pltpu.VMEM((tm // rows, rows, D_MODEL), BF16),
            pltpu.VMEM((tm // rows, rows, D_MODEL), BF16),
        ],
        compiler_params=pltpu.CompilerParams(
            dimension_semantics=("parallel",),
            vmem_limit_bytes=VMEM_LIMIT_BYTES),
        name="proj_attn",
    )(x2d, g1, *([w_in_bf16] * n_w), qk_gain)


def _attn_kernel(q_ref, kp_ref, kc_ref, vp_ref, vc_ref, o_ref, lse_ref, *stage):
    n = pl.program_id(1)
    d = q_ref.shape[1]
    blk = ATTN_BLOCK
    n_sub = q_ref.shape[2] // blk
    stage_scr = stage[0] if stage else None
    f = FAST_STRIDE

    def put(idx, r, u, tile):
        if stage_scr is None:
            tok = pl.ds(u * blk * d + r, blk, stride=d)
            if idx == N_PAIRS:
                lse_ref[0, tok, :] = tile
            else:
                o_ref[0, idx, tok, :] = tile
        else:
            a, b = r % f, r // f
            stage_scr[idx, a, pl.ds(u * blk * (d // f) + b, blk, stride=d // f), :] = tile
    qi = lax.broadcasted_iota(jnp.int32, (blk, 2 * blk), 0)
    ki = lax.broadcasted_iota(jnp.int32, (blk, 2 * blk), 1)
    band = (ki >= qi) & (ki <= qi + blk)
    band_first = band & ((ki >= blk) | (n > 0))
    lane = lax.broadcasted_iota(jnp.int32, (blk, LANES), 1)
    first_head = lane < HEAD_DIM
    hb = ATTN_HEADS_PER_DOT
    width = hb * HEAD_DIM
    lane_w = lax.broadcasted_iota(jnp.int32, (blk, width), 1)
    head_lanes = [(lane_w >= h * HEAD_DIM) & (lane_w < (h + 1) * HEAD_DIM) for h in range(hb)]
    valid_by_sub = [jnp.concatenate([band_first if u == 0 else band] * hb, axis=0)
                    for u in range(min(n_sub, 2))]

    def unit(r, u):
        rows = slice(u * blk, (u + 1) * blk)
        valid_b = valid_by_sub[min(u, 1)]
        m_tile = jnp.zeros((blk, LANES), F32)
        l_tile = jnp.ones((blk, LANES), F32)
        for g in range(HEADS_PER_GROUP // hb):
            sl = slice(g * width, (g + 1) * width)
            q = q_ref[0, r, rows, sl]
            if u == 0:
                k_prev, v_prev = kp_ref[0, r, :, sl], vp_ref[0, r, :, sl]
            else:
                prev_rows = slice((u - 1) * blk, u * blk)
                k_prev, v_prev = kc_ref[0, r, prev_rows, sl], vc_ref[0, r, prev_rows, sl]
            k = jnp.concatenate([k_prev, kc_ref[0, r, rows, sl]], axis=0)
            v = jnp.concatenate([v_prev, vc_ref[0, r, rows, sl]], axis=0)
            zero = jnp.zeros_like(q)
            q_rows = jnp.concatenate([jnp.where(head_lanes[h], q, zero) for h in range(hb)],
                                     axis=0)
            s = lax.dot_general(q_rows, k, (((1,), (1,)), ((), ())), preferred_element_type=F32)
            s = jnp.where(valid_b, s, -jnp.inf)
            m = jnp.max(s, axis=-1, keepdims=True)
            p = jnp.exp2(s - m)
            l = jnp.sum(p, axis=-1, keepdims=True)
            p = p.astype(BF16)
            inv_l = 1.0 / l
            for h in range(hb):
                head = g * hb + h
                m_tile = jnp.where(lane == head, m[h * blk:(h + 1) * blk], m_tile)
                l_tile = jnp.where(lane == head, l[h * blk:(h + 1) * blk], l_tile)
            for pp in range(hb // 2):
                rows2 = slice(2 * pp * blk, (2 * pp + 2) * blk)
                pv = jnp.dot(p[rows2], v[:, pp * LANES:(pp + 1) * LANES],
                             preferred_element_type=F32) * inv_l[rows2]
                put(g * (hb // 2) + pp, r, u, jnp.where(first_head, pv[:blk], pv[blk:]))
        put(N_PAIRS, r, u, m_tile * LN_2 + jnp.log(l_tile))

    if d * n_sub <= ATTN_UNROLL:
        for r in range(d):
            for u in range(n_sub):
                unit(r, u)
        if stage_scr is not None:
            for a in range(f):
                merged = pl.ds(a, stage_scr.shape[2], stride=f)
                for idx in range(N_PAIRS):
                    o_ref[0, idx, merged, :] = stage_scr[idx, a]
                lse_ref[0, merged, :] = stage_scr[N_PAIRS, a]
    else:
        assert stage_scr is None
        def residue(r, carry):
            for u in range(n_sub):
                unit(r, u)
            return carry
        lax.fori_loop(0, d, residue, 0, unroll=ATTN_UNROLL // n_sub)


def _attn_call(qkv, dilation):
    b, d, sub_len, _ = qkv.shape
    assert d == dilation
    blk = ATTN_BLOCK
    n_sub = max(1, ATTN_UNROLL // d)
    rows = n_sub * blk
    steps = sub_len // rows
    s = sub_len * d

    def cur(kind):
        return lambda bi, n: (bi, 0, n, kind)

    def prev(kind):
        return lambda bi, n: (bi, 0, jnp.maximum(n * n_sub - 1, 0), kind)

    cur_shape = (1, d, rows, GROUP_WIDTH)
    prev_shape = (1, d, blk, GROUP_WIDTH)
    o, lse = pl.pallas_call(
        _attn_kernel,
        grid=(b, steps),
        in_specs=[
            pl.BlockSpec(cur_shape, cur(0)),
            pl.BlockSpec(prev_shape, prev(1)),
            pl.BlockSpec(cur_shape, cur(1)),
            pl.BlockSpec(prev_shape, prev(2)),
            pl.BlockSpec(cur_shape, cur(2)),
        ],
        out_specs=[
            pl.BlockSpec((1, N_PAIRS, d * rows, LANES), lambda bi, n: (bi, 0, n, 0)),
            pl.BlockSpec((1, d * rows, LANES), lambda bi, n: (bi, n, 0)),
        ],
        out_shape=[
            jax.ShapeDtypeStruct((b, N_PAIRS, s, LANES), F32),
            jax.ShapeDtypeStruct((b, s, LANES), F32),
        ],
        scratch_shapes=([pltpu.VMEM((N_PAIRS + 1, FAST_STRIDE, d * rows // FAST_STRIDE, LANES), F32)]
                        if d > FAST_STRIDE else []),
        compiler_params=pltpu.CompilerParams(
            dimension_semantics=("parallel", "arbitrary"),
            vmem_limit_bytes=VMEM_LIMIT_BYTES),
        name=f"attn_d{d}",
    )(qkv, qkv, qkv, qkv, qkv)
    return o, lse


def _log_sigmoid(x):
    return jnp.minimum(x, 0.0) - jnp.log(1.0 + jnp.exp(-jnp.abs(x)))


def _split3(x):
    hi = x.astype(BF16)
    r1 = x - hi.astype(F32)
    mid = r1.astype(BF16)
    lo = (r1 - mid.astype(F32)).astype(BF16)
    return hi, mid, lo


def _gla_pairwise_products(b_scr, qs_scr, kf_scr, pcat_scr):
    sub = GLA_SUB
    tl = lax.broadcasted_iota(jnp.int32, (sub, GLA_HK), 0)
    for pair_i in range(GLA_STEP // (2 * sub)):
        tiles = []
        for r0 in (2 * pair_i * sub, (2 * pair_i + 1) * sub):
            b_blk, q_blk = b_scr[r0:r0 + sub, :], qs_scr[r0:r0 + sub, :]
            row = []
            for s in range(sub):
                d = b_blk - b_scr[r0 + s:r0 + s + 1, :]
                if s > 0:
                    d = jnp.where(tl >= s, d, -jnp.inf)
                row.append(q_blk * kf_scr[r0 + s:r0 + s + 1, :] * jnp.exp2(d))
            tiles.append(row)
        r0 = 2 * pair_i * sub
        for s in range(sub):
            pcat_scr[r0:r0 + 2 * sub, s * GLA_HK:(s + 1) * GLA_HK] = (
                jnp.concatenate([tiles[0][s], tiles[1][s]], axis=0).astype(BF16))


def _gla_reference_factors(b, qs, kf):
    c_len = GLA_CHUNK
    n_chunks = GLA_STEP // c_len
    sub = GLA_SUB

    def z(nrows):
        return jnp.zeros((nrows, GLA_HK), F32)

    qb, kb = [], []
    for r0 in range(0, GLA_STEP, 2 * sub):
        mid = r0 + sub
        b_ref = b[mid - 1:mid]
        qb += [z(sub), qs[mid:mid + sub] * jnp.exp2(b[mid:mid + sub] - b_ref)]
        kb += [kf[r0:mid] * jnp.exp2(b_ref - b[r0:mid]), z(sub)]
    q_block = jnp.concatenate(qb, axis=0).astype(BF16)
    k_block = jnp.concatenate(kb, axis=0).astype(BF16)

    qw, kw = [], []
    for c in range(n_chunks):
        bc, qc, kc = (x[c * c_len:(c + 1) * c_len] for x in (b, qs, kf))
        b15, b31, b47 = bc[15:16], bc[31:32], bc[47:48]
        q1 = jnp.concatenate([z(16), qc[16:32] * jnp.exp2(bc[16:32] - b15), z(32)], axis=0)
        q2 = jnp.concatenate([z(32), qc[32:64] * jnp.exp2(bc[32:64] - b31)], axis=0)
        q3 = jnp.concatenate([z(48), qc[48:64] * jnp.exp2(bc[48:64] - b47)], axis=0)
        k1 = jnp.concatenate([kc[0:16] * jnp.exp2(b15 - bc[0:16]), z(48)], axis=0)
        k2 = jnp.concatenate([kc[0:32] * jnp.exp2(b31 - bc[0:32]), z(32)], axis=0)
        k3 = jnp.concatenate([z(32), kc[32:48] * jnp.exp2(b47 - bc[32:48]), z(16)], axis=0)
        qw.append(jnp.concatenate([q1, q2, q3], axis=1))
        kw.append(jnp.concatenate([k1, k2, k3], axis=1))
    q_within = jnp.concatenate(qw, axis=0).astype(BF16)
    k_within = jnp.concatenate(kw, axis=0).astype(BF16)

    qx, kx = [], []
    for j in range(n_chunks - 1):
        lo, hi = j * c_len, (j + 1) * c_len
        b_ref = b[hi - 1:hi]
        qx.append(jnp.concatenate([z(hi), qs[hi:] * jnp.exp2(b[hi:] - b_ref)], axis=0))
        parts = [kf[lo:hi] * jnp.exp2(b_ref - b[lo:hi])]
        if lo:
            parts.insert(0, z(lo))
        parts.append(z(GLA_STEP - hi))
        kx.append(jnp.concatenate(parts, axis=0))
    q_cross = jnp.concatenate(qx, axis=1).astype(BF16)
    k_cross = jnp.concatenate(kx, axis=1).astype(BF16)
    return (q_block, k_block), (q_within, k_within), (q_cross, k_cross)


def _gla_decay(pa_ref, u_ref, bias_ref, ltri_ref):
    logits = jnp.dot(pa_ref[...].astype(BF16), u_ref[...], preferred_element_type=F32) + bias_ref[...]
    la = _log_sigmoid(logits) * (LOG2_E / GLA_TAU)
    parts = jnp.dot(ltri_ref[...], jnp.concatenate(_split3(la), axis=1), preferred_element_type=F32)
    return parts[:, :GLA_DK] + parts[:, GLA_DK:2 * GLA_DK] + parts[:, 2 * GLA_DK:]


def _gla_kernel(q_ref, k_ref, v_ref, r_ref, pa_ref, pa_next_ref, u_ref, bias_ref, gn_ref, ltri_ref, e_ref,
                o_ref, st_ref, b_scr, qs_scr, kf_scr, pcat_scr, decay_scr, span_scr, *, steps_per_seq):
    step = pl.program_id(0)
    n = GLA_STEP
    n_sub = q_ref.shape[0] // n

    @pl.when(lax.rem(step, steps_per_seq) == 0)
    def _():
        st_ref[...] = jnp.zeros_like(st_ref)

    def prepare(next_pa_ref):
        decay = _gla_decay(next_pa_ref, u_ref, bias_ref, ltri_ref)
        decay_scr[...] = decay
        span_scr[0] = jnp.max(-decay[n - 1:n, :])

    @pl.when(step == 0)
    def _():
        prepare(pa_ref.at[0:n])

    def run(rows, next_pa_ref):
        def body(small_span):
            _gla_substep(decay_scr[...], q_ref.at[rows], k_ref.at[rows], v_ref.at[rows],
                         r_ref.at[rows], gn_ref, e_ref, o_ref.at[rows], st_ref,
                         b_scr, qs_scr, kf_scr, pcat_scr, small_span=small_span)
            prepare(next_pa_ref)

        small_span = span_scr[0] <= GLA_SAFE_SPAN
        pl.when(small_span)(lambda: body(True))
        pl.when(jnp.logical_not(small_span))(lambda: body(False))

    def substep(ss, carry):
        rows = pl.ds(pl.multiple_of(ss * n, n), n)
        nxt = pl.ds(pl.multiple_of((ss + 1) * n, n), n)
        run(rows, pa_ref.at[nxt])
        return carry

    lax.fori_loop(0, n_sub - 1, substep, 0)
    run(slice((n_sub - 1) * n, n_sub * n), pa_next_ref)


def _gla_substep(b_all, q_ref, k_ref, v_ref, r_ref, gn_ref, e_ref,
                 o_ref, st_ref, b_scr, qs_scr, kf_scr, pcat_scr, *, small_span):
    n = GLA_STEP
    nt = (((1,), (1,)), ((), ()))

    heads = []
    for h in range(GLA_HEADS):
        kcols = slice(h * GLA_HK, (h + 1) * GLA_HK)
        b = b_all[:, kcols]
        qs = q_ref[:, kcols].astype(F32) * (GLA_HK ** -0.5)
        kf = k_ref[:, kcols].astype(F32)
        if not small_span:
            b_scr[h], qs_scr[h], kf_scr[h] = b, qs, kf
            _gla_pairwise_products(b_scr.at[h], qs_scr.at[h], kf_scr.at[h],
                                   pcat_scr.at[h * n:(h + 1) * n])
        heads.append((b, qs, kf))

    ri = lax.broadcasted_iota(jnp.int32, (n, n), 0)
    ci = lax.broadcasted_iota(jnp.int32, (n, n), 1)
    if small_span:
        causal = ri >= ci
    else:
        a_diag = jnp.dot(pcat_scr[...], e_ref[...], preferred_element_type=F32)
        same_sub = (ri // GLA_SUB) == (ci // GLA_SUB)
        same_pair = (ri // (2 * GLA_SUB)) == (ci // (2 * GLA_SUB))
        same_chunk = (ri // GLA_CHUNK) == (ci // GLA_CHUNK)
    gn = gn_ref[...]

    for h, (b, qs, kf) in enumerate(heads):
        vcols = slice(h * GLA_HV, (h + 1) * GLA_HV)
        q_in = (qs * jnp.exp2(b)).astype(BF16)
        if small_span:
            k_out = (kf * jnp.exp2(-b)).astype(BF16)
            a = jnp.where(causal, lax.dot_general(q_in, k_out, nt, preferred_element_type=F32), 0.0)
        else:
            a_block, a_within, a_cross = (
                lax.dot_general(qf, kf_, nt, preferred_element_type=F32)
                for qf, kf_ in _gla_reference_factors(b, qs, kf))
            a = jnp.where(same_sub, a_diag[h * n:(h + 1) * n],
                          jnp.where(same_pair, a_block, jnp.where(same_chunk, a_within, a_cross)))
        v = v_ref[:, vcols]
        st = st_ref[h]
        b_last = b[n - 1:n]
        o = (jnp.dot(a.astype(BF16), v, preferred_element_type=F32)
             + lax.dot_general(q_in, st.astype(BF16), nt, preferred_element_type=F32))
        k_st = (kf * jnp.exp2(b_last - b)).astype(BF16)
        upd = lax.dot_general(v, k_st, (((0,), (0,)), ((), ())), preferred_element_type=F32)
        st_ref[h] = st * jnp.exp2(b_last) + upd

        ms = jnp.mean(o * o, axis=-1, keepdims=True)
        y = o * lax.rsqrt(ms + EPS) * gn
        r = r_ref[:, vcols].astype(F32)
        o_ref[:, vcols] = (y * (r * jax.nn.sigmoid(r))).astype(BF16)


def _gla_constants():
    n = GLA_STEP
    idx = np.arange(n)
    ltri = idx[:, None] >= idx[None, :]
    rows = np.arange(GLA_SUB * GLA_HK)
    cols = np.arange(n)
    e = (rows[:, None] // GLA_HK) == (cols[None, :] % GLA_SUB)
    return jnp.asarray(ltri, BF16), jnp.asarray(e, BF16)


def _gla_call(proj2d, pa, u_pad, bias, gn, batch):
    t = proj2d.shape[0]
    n = GLA_STEP
    rows = n * GLA_SUBSTEPS
    steps_per_seq = t // batch // rows
    n_steps = t // rows
    ltri, e = _gla_constants()

    def tok(width, col_block):
        return pl.BlockSpec((rows, width), lambda i: (i, col_block))

    def const(shape):
        return pl.BlockSpec(shape, lambda i: (0, 0))

    pa_next = pl.BlockSpec((n, PA_PAD), lambda i: (jnp.minimum(i + 1, n_steps - 1) * GLA_SUBSTEPS, 0))

    return pl.pallas_call(
        functools.partial(_gla_kernel, steps_per_seq=steps_per_seq),
        grid=(n_steps,),
        in_specs=[
            tok(GLA_DK, P_GQ // GLA_DK),
            tok(GLA_DK, P_GK // GLA_DK),
            tok(GLA_DV, P_GV // GLA_DV),
            tok(GLA_DV, P_GR // GLA_DV),
            tok(PA_PAD, 0),
            pa_next,
            const((PA_PAD, GLA_DK)),
            const((1, GLA_DK)),
            const((1, GLA_HV)),
            const((n, n)),
            const((GLA_SUB * GLA_HK, n)),
        ],
        out_specs=tok(GLA_DV, 0),
        out_shape=jax.ShapeDtypeStruct((t, GLA_DV), BF16),
        scratch_shapes=[
            pltpu.VMEM((GLA_HEADS, GLA_HV, GLA_HK), F32),
            pltpu.VMEM((GLA_HEADS, n, GLA_HK), F32),
            pltpu.VMEM((GLA_HEADS, n, GLA_HK), F32),
            pltpu.VMEM((GLA_HEADS, n, GLA_HK), F32),
            pltpu.VMEM((GLA_HEADS * n, GLA_SUB * GLA_HK), BF16),
            pltpu.VMEM((n, GLA_DK), F32),
            pltpu.SMEM((1,), F32),
        ],
        compiler_params=pltpu.CompilerParams(
            dimension_semantics=("arbitrary",),
            vmem_limit_bytes=VMEM_LIMIT_BYTES),
        name="gla",
    )(proj2d, proj2d, proj2d, proj2d, pa, pa, u_pad, bias, gn, ltri, e)


FF_CHUNK = 1024


def _out_kernel(x_ref, o0_ref, o1_ref, o2_ref, l0_ref, l1_ref, l2_ref, og_ref, gate_a_ref, gate_g_ref, gbias_ref,
                hx_ref, wa_ref, wb_ref, wo_ref, g2_ref, wup_ref, wdn_ref, out_ref):
    l0, l1, l2 = l0_ref[0], l1_ref[0], l2_ref[0]
    mx = jnp.maximum(jnp.maximum(l0, l1), l2)
    e0, e1, e2 = jnp.exp(l0 - mx), jnp.exp(l1 - mx), jnp.exp(l2 - mx)
    inv = 1.0 / (e0 + e1 + e2)
    hx = hx_ref[...]
    def pairs(ref):
        return jnp.concatenate([ref[0, p] for p in range(N_PAIRS)], axis=1)

    def expand(w):
        return jnp.dot(w.astype(BF16), hx, preferred_element_type=F32)

    o_attn = (expand(e0 * inv) * pairs(o0_ref) + expand(e1 * inv) * pairs(o1_ref)
              + expand(e2 * inv) * pairs(o2_ref))
    a = jnp.dot(o_attn.astype(BF16), wa_ref[...], preferred_element_type=F32)
    g = jnp.dot(og_ref[...], wb_ref[...], preferred_element_type=F32)
    gate_a = jax.nn.sigmoid(gate_a_ref[...].astype(F32) + gbias_ref[:, :D_MODEL])
    gate_g = jax.nn.sigmoid(gate_g_ref[...].astype(F32) + gbias_ref[:, D_MODEL:])
    mixed = gate_a * a + gate_g * g
    x1 = x_ref[...] + jnp.dot(mixed.astype(BF16), wo_ref[...], preferred_element_type=F32)

    ms = jnp.mean(x1 * x1, axis=-1, keepdims=True)
    h2 = (x1 * lax.rsqrt(ms + EPS) * g2_ref[...]).astype(BF16)
    hidden = []
    for c in range(D_FF // FF_CHUNK):
        u = jnp.dot(h2, wup_ref[:, c * FF_CHUNK:(c + 1) * FF_CHUNK], preferred_element_type=F32)
        u = jnp.maximum(u, 0.0)
        hidden.append((u * u).astype(BF16))
    out_ref[...] = x1 + jnp.dot(jnp.concatenate(hidden, axis=1), wdn_ref[...],
                                preferred_element_type=F32)


def _out_call(x2d, o_groups, lse_groups, o_gla, proj2d, gbias, wa, wb, wo, g2, wup, wdn, batch, tm):
    t = x2d.shape[0]
    lanes = np.arange(LANES)
    cols = np.arange(GROUP_WIDTH)
    head_expand = jnp.asarray(lanes[:, None] == cols[None, :] // HEAD_DIM, BF16)

    tiles_per_seq = t // batch // tm

    def tok(width):
        return pl.BlockSpec((tm, width), lambda i: (i, 0))

    attn_o = pl.BlockSpec((1, N_PAIRS, tm, LANES),
                          lambda i: (i // tiles_per_seq, 0, i % tiles_per_seq, 0))
    attn_lse = pl.BlockSpec((1, tm, LANES), lambda i: (i // tiles_per_seq, i % tiles_per_seq, 0))

    def const(shape):
        return pl.BlockSpec(shape, lambda i: (0, 0), pipeline_mode=pl.Buffered(1))

    return pl.pallas_call(
        _out_kernel,
        grid=(t // tm,),
        in_specs=[
            tok(D_MODEL),
            attn_o, attn_o, attn_o,
            attn_lse, attn_lse, attn_lse,
            tok(GLA_DV),
            pl.BlockSpec((tm, D_MODEL), lambda i: (i, P_GATE // D_MODEL)),
            pl.BlockSpec((tm, D_MODEL), lambda i: (i, P_GATE // D_MODEL + 1)),
            const((1, 2 * D_MODEL)),
            const((LANES, GROUP_WIDTH)),
            const((GROUP_WIDTH, D_MODEL)),
            const((GLA_DV, D_MODEL)),
            const((D_MODEL, D_MODEL)),
            const((1, D_MODEL)),
            const((D_MODEL, D_FF)),
            const((D_FF, D_MODEL)),
        ],
        out_specs=tok(D_MODEL),
        out_shape=jax.ShapeDtypeStruct((t, D_MODEL), F32),
        compiler_params=pltpu.CompilerParams(
            dimension_semantics=("parallel",),
            vmem_limit_bytes=VMEM_LIMIT_BYTES),
        name="out",
    )(x2d, *o_groups, *lse_groups, o_gla, proj2d, proj2d, gbias, head_expand, wa, wb, wo, g2, wup, wdn)


def _layer(x2d, batch, norm1_g, w_in, gq, gk, gate_up, gate_bias, gla_norm_g, branch_bias,
           w_a, w_b, w_out, norm2_g, w_up, w_down):
    g1 = norm1_g.reshape(1, D_MODEL)
    w_in_bf16 = w_in.astype(BF16)
    w_gate = w_in_bf16[:, O_GATE:O_GATE + 2 * D_MODEL]
    w_pa = jnp.pad(w_in_bf16[:, O_PA:O_PA + GLA_RANK], ((0, 0), (0, PA_PAD - GLA_RANK)))
    q_gain = jnp.tile(gq, HEADS_PER_GROUP) * (HEAD_DIM ** -0.5 * LOG2_E)
    k_gain = jnp.tile(gk, HEADS_PER_GROUP)
    qk_gain = jnp.stack([q_gain, k_gain]).reshape(2, 1, GROUP_WIDTH)

    proj, pa = _main_proj_call(x2d, g1, w_in_bf16, w_gate, w_pa, tm=PROJ_TM, rows=PROJ_ROWS)
    qkv_groups = _attn_proj_call(x2d, g1, w_in_bf16, qk_gain, batch, tm=PROJ_TM, rows=PROJ_ROWS)

    o_groups, lse_groups = [], []
    for qkv, (_, dilation) in zip(qkv_groups, ATTN_GROUPS):
        o, lse = _attn_call(qkv, dilation)
        o_groups.append(o)
        lse_groups.append(lse)

    u_pad = jnp.pad(gate_up, ((0, PA_PAD - GLA_RANK), (0, 0))).astype(BF16)
    o_gla = _gla_call(proj, pa, u_pad, gate_bias.reshape(1, GLA_DK),
                      gla_norm_g.reshape(1, GLA_HV), batch)

    return _out_call(x2d, o_groups, lse_groups, o_gla, proj,
                     branch_bias.reshape(1, 2 * D_MODEL),
                     w_a.astype(BF16), w_b.astype(BF16), w_out.astype(BF16),
                     norm2_g.reshape(1, D_MODEL), w_up.astype(BF16), w_down.astype(BF16), batch, OUT_TM)


def kernel(x, norm1_g, w_in, attn_q_norm_g, attn_k_norm_g, gla_gate_up, gla_gate_bias, gla_out_norm_g, branch_gate_bias, w_attn_branch, w_gla_branch, w_out, norm2_g, w_ff_up, w_ff_down):
    b, s, d = x.shape
    x2d = x.reshape(b * s, d)
    for l in range(norm1_g.shape[0]):
        x2d = _layer(x2d, b, norm1_g[l], w_in[l], attn_q_norm_g[l], attn_k_norm_g[l],
                     gla_gate_up[l], gla_gate_bias[l], gla_out_norm_g[l], branch_gate_bias[l],
                     w_attn_branch[l], w_gla_branch[l], w_out[l], norm2_g[l],
                     w_ff_up[l], w_ff_down[l])
    return x2d.reshape(b, s, d)
```

```python
import functools

import numpy as np
import jax
import jax.numpy as jnp
from jax import lax
from jax.experimental import pallas as pl
from jax.experimental.pallas import tpu as pltpu

F32 = jnp.float32
BF16 = jnp.bfloat16

FAST_STRIDE = 4
LANES = 128
VMEM_LIMIT_BYTES = 56 * 1024 * 1024
PROJ_TM = 1024
PROJ_ROWS = 256
OUT_TM = 512

D_MODEL = 1024
ATTN_GROUPS = ((128, 1), (512, 4), (2048, 16))
N_GROUPS = len(ATTN_GROUPS)
HEADS_PER_GROUP = 8
HEAD_DIM = 64
ATTN_BLOCK = 128
GROUP_WIDTH = HEADS_PER_GROUP * HEAD_DIM
ATTN_WIDTH = 3 * N_GROUPS * GROUP_WIDTH
N_PAIRS = GROUP_WIDTH // LANES
ATTN_HEADS_PER_DOT = 4
ATTN_UNROLL = 16

GLA_HEADS = 4
GLA_DK = 512
GLA_DV = 1024
GLA_HK = GLA_DK // GLA_HEADS
GLA_HV = GLA_DV // GLA_HEADS
GLA_RANK = 16
GLA_TAU = 16.0
GLA_CHUNK = 64
GLA_SUB = 8
GLA_STEP = 256
GLA_SUBSTEPS = 4
GLA_SAFE_SPAN = 64.0

D_FF = 4 * D_MODEL
EPS = 1e-6
LOG2_E = 1.4426950408889634
LN_2 = 0.6931471805599453

_ORIG_SIZES = (ATTN_WIDTH, GLA_DK, GLA_DK, GLA_DV, GLA_DV, GLA_RANK, 2 * D_MODEL)
_ORIG_OFF = tuple(int(v) for v in np.cumsum((0,) + _ORIG_SIZES))
O_ATTN, O_GQ, O_GK, O_GV, O_GR, O_PA, O_GATE = _ORIG_OFF[:7]

P_GQ = 0
P_GK = P_GQ + GLA_DK
P_GV = P_GK + GLA_DK
P_GR = P_GV + GLA_DV
P_GATE = P_GR + GLA_DV
P_MAIN = P_GATE + 2 * D_MODEL
GLA_W_BLOCK = (O_GATE - GLA_RANK - O_GQ) // 2
N_KINDS = 3
PA_PAD = LANES


def _rms_norm_rows(x, gain):
    ms = jnp.mean(x * x, axis=-1, keepdims=True)
    return x * lax.rsqrt(ms + EPS) * gain


def _main_proj_kernel(x_ref, g1_ref, wa_ref, wb_ref, wgate_ref, wpa_ref, o_ref, pa_ref, *, rows):
    gain = g1_ref[...]
    for rc in range(x_ref.shape[0] // rows):
        rs = slice(rc * rows, (rc + 1) * rows)
        h = _rms_norm_rows(x_ref[rs, :], gain).astype(BF16)
        pa_ref[rs, :] = jnp.dot(h, wpa_ref[...], preferred_element_type=F32)
        col = 0
        for w_ref in (wa_ref, wb_ref, wgate_ref):
            width = w_ref.shape[1]
            o_ref[rs, col:col + width] = jnp.dot(
                h, w_ref[...], preferred_element_type=F32).astype(BF16)
            col += width


def _main_proj_call(x2d, g1, w_in_bf16, w_gate, w_pa, tm, rows):
    t = x2d.shape[0]
    first_block = O_GQ // GLA_W_BLOCK

    def const(shape, col_block=0):
        return pl.BlockSpec(shape, lambda i: (0, col_block), pipeline_mode=pl.Buffered(1))

    return pl.pallas_call(
        functools.partial(_main_proj_kernel, rows=rows),
        grid=(t // tm,),
        in_specs=[
            pl.BlockSpec((tm, D_MODEL), lambda i: (i, 0)),
            const((1, D_MODEL)),
            const((D_MODEL, GLA_W_BLOCK), first_block),
            const((D_MODEL, GLA_W_BLOCK), first_block + 1),
            const((D_MODEL, 2 * D_MODEL)),
            const((D_MODEL, PA_PAD)),
        ],
        out_specs=[
            pl.BlockSpec((tm, P_MAIN), lambda i: (i, 0)),
            pl.BlockSpec((tm, PA_PAD), lambda i: (i, 0)),
        ],
        out_shape=[
            jax.ShapeDtypeStruct((t, P_MAIN), BF16),
            jax.ShapeDtypeStruct((t, PA_PAD), F32),
        ],
        compiler_params=pltpu.CompilerParams(
            dimension_semantics=("parallel",),
            vmem_limit_bytes=VMEM_LIMIT_BYTES),
        name="proj_main",
    )(x2d, g1, w_in_bf16, w_in_bf16, w_gate, w_pa)


def _qkv_project(h, w_refs, gain_ref, o_ref, chunk, residue_of_block=None):
    d = o_ref.shape[1]
    rows = h.shape[0] // d
    residue_of_block = residue_of_block or list(range(d))
    for kind in range(N_KINDS):
        cols = slice(kind * GROUP_WIDTH, (kind + 1) * GROUP_WIDTH)
        acc = jnp.dot(h, w_refs[kind][...], preferred_element_type=F32)
        if kind < 2:
            sq = acc * acc
            first_head = lax.broadcasted_iota(jnp.int32, (1, LANES), 1) < HEAD_DIM
            scales = []
            for pp in range(N_PAIRS):
                t = sq[:, pp * LANES:(pp + 1) * LANES]
                s_lo = jnp.sum(jnp.where(first_head, t, 0.0), axis=-1, keepdims=True)
                s_hi = jnp.sum(jnp.where(first_head, 0.0, t), axis=-1, keepdims=True)
                scales.append(jnp.where(first_head,
                                        lax.rsqrt(s_lo * (1.0 / HEAD_DIM) + EPS),
                                        lax.rsqrt(s_hi * (1.0 / HEAD_DIM) + EPS)))
            acc = acc * jnp.concatenate(scales, axis=1) * gain_ref[kind]
        y = acc.astype(BF16)
        for j, r in enumerate(residue_of_block):
            o_ref[0, r, chunk * rows:(chunk + 1) * rows, cols] = y[j * rows:(j + 1) * rows, :]


def _attn_proj_kernel(x_ref, g1_ref, *rest):
    n_w = N_KINDS * N_GROUPS
    w_all = rest[:n_w]
    gain_ref, o0_ref, o1_ref, o2_ref, col_scr, col2_scr, perm1_scr, perm2_scr = rest[n_w:]
    w_group = [[w_all[kind * N_GROUPS + g] for kind in range(N_KINDS)] for g in range(N_GROUPS)]
    n_chunks, _, tc, _ = col_scr.shape
    n_col = D_MODEL // LANES
    gain = g1_ref[...]
    for ch in range(n_chunks):
        hf = _rms_norm_rows(x_ref[ch * tc:(ch + 1) * tc, :], gain)
        for c in range(n_col):
            col_scr[ch, c] = hf[:, c * LANES:(c + 1) * LANES]
        _qkv_project(hf.astype(BF16), w_group[0], gain_ref, o0_ref, ch)
        f = o1_ref.shape[1]
        assert o2_ref.shape[1] == f * f
        rows1, rows2 = tc // f, tc // (f * f)
        for a in range(f):
            for c in range(n_col):
                t = col_scr[ch, c, pl.ds(a, rows1, stride=f), :]
                col2_scr[ch, c, a * rows1:(a + 1) * rows1, :] = t
                perm1_scr[ch, a * rows1:(a + 1) * rows1, c * LANES:(c + 1) * LANES] = t.astype(BF16)
        _qkv_project(perm1_scr[ch], w_group[1], gain_ref, o1_ref, ch)
        for a in range(f):
            for b in range(f):
                blk = a * f + b
                for c in range(n_col):
                    perm2_scr[ch, blk * rows2:(blk + 1) * rows2, c * LANES:(c + 1) * LANES] = (
                        col2_scr[ch, c, pl.ds(a * rows1 + b, rows2, stride=f), :].astype(BF16))
        _qkv_project(perm2_scr[ch], w_group[2], gain_ref, o2_ref, ch,
                     residue_of_block=[b * f + a for a in range(f) for b in range(f)])


def _attn_proj_call(x2d, g1, w_in_bf16, qk_gain, batch, tm, rows):
    t = x2d.shape[0]
    s = t // batch
    tiles_per_seq = s // tm
    width = N_KINDS * GROUP_WIDTH
    n_w = N_KINDS * N_GROUPS

    def const(shape):
        return pl.BlockSpec(shape, lambda i: (0,) * len(shape), pipeline_mode=pl.Buffered(1))

    def w_block(j):
        return pl.BlockSpec((D_MODEL, GROUP_WIDTH), lambda i: (0, O_ATTN // GROUP_WIDTH + j),
                            pipeline_mode=pl.Buffered(1))

    return pl.pallas_call(
        _attn_proj_kernel,
        grid=(t // tm,),
        in_specs=[
            pl.BlockSpec((tm, D_MODEL), lambda i: (i, 0)),
            const((1, D_MODEL)),
            *[w_block(j) for j in range(n_w)],
            const((2, 1, GROUP_WIDTH)),
        ],
        out_specs=[
            pl.BlockSpec((1, d, tm // d, width),
                         lambda i: (i // tiles_per_seq, 0, i % tiles_per_seq, 0))
            for _, d in ATTN_GROUPS],
        out_shape=[jax.ShapeDtypeStruct((batch, d, s // d, width), BF16) for _, d in ATTN_GROUPS],
        scratch_shapes=[
            pltpu.VMEM((tm // rows, D_MODEL // LANES, rows, LANES), F32),
            pltpu.VMEM((tm // rows, D_MODEL // LANES, rows, LANES), F32),
            pltpu.VMEM((tm // rows, rows, D_MODEL), BF16),
            pltpu.VMEM((tm // rows, rows, D_MODEL), BF16),
        ],
        compiler_params=pltpu.CompilerParams(
            dimension_semantics=("parallel",),
            vmem_limit_bytes=VMEM_LIMIT_BYTES),
        name="proj_attn",
    )(x2d, g1, *([w_in_bf16] * n_w), qk_gain)


def _attn_kernel(q_ref, kp_ref, kc_ref, vp_ref, vc_ref, o_ref, lse_ref, *stage):
    n = pl.program_id(1)
    d = q_ref.shape[1]
    blk = ATTN_BLOCK
    n_sub = q_ref.shape[2] // blk
    stage_scr = stage[0] if stage else None
    f = FAST_STRIDE

    def put(idx, r, u, tile):
        if stage_scr is None:
            tok = pl.ds(u * blk * d + r, blk, stride=d)
            if idx == N_PAIRS:
                lse_ref[0, tok, :] = tile
            else:
                o_ref[0, idx, tok, :] = tile
        else:
            a, b = r % f, r // f
            stage_scr[idx, a, pl.ds(u * blk * (d // f) + b, blk, stride=d // f), :] = tile
    qi = lax.broadcasted_iota(jnp.int32, (blk, 2 * blk), 0)
    ki = lax.broadcasted_iota(jnp.int32, (blk, 2 * blk), 1)
    band = (ki >= qi) & (ki <= qi + blk)
    band_first = band & ((ki >= blk) | (n > 0))
    lane = lax.broadcasted_iota(jnp.int32, (blk, LANES), 1)
    first_head = lane < HEAD_DIM
    hb = ATTN_HEADS_PER_DOT
    width = hb * HEAD_DIM
    lane_w = lax.broadcasted_iota(jnp.int32, (blk, width), 1)
    head_lanes = [(lane_w >= h * HEAD_DIM) & (lane_w < (h + 1) * HEAD_DIM) for h in range(hb)]
    valid_by_sub = [jnp.concatenate([band_first if u == 0 else band] * hb, axis=0)
                    for u in range(min(n_sub, 2))]

    def unit(r, u):
        rows = slice(u * blk, (u + 1) * blk)
        valid_b = valid_by_sub[min(u, 1)]
        m_tile = jnp.zeros((blk, LANES), F32)
        l_tile = jnp.ones((blk, LANES), F32)
        for g in range(HEADS_PER_GROUP // hb):
            sl = slice(g * width, (g + 1) * width)
            q = q_ref[0, r, rows, sl]
            if u == 0:
                k_prev, v_prev = kp_ref[0, r, :, sl], vp_ref[0, r, :, sl]
            else:
                prev_rows = slice((u - 1) * blk, u * blk)
                k_prev, v_prev = kc_ref[0, r, prev_rows, sl], vc_ref[0, r, prev_rows, sl]
            k = jnp.concatenate([k_prev, kc_ref[0, r, rows, sl]], axis=0)
            v = jnp.concatenate([v_prev, vc_ref[0, r, rows, sl]], axis=0)
            zero = jnp.zeros_like(q)
            q_rows = jnp.concatenate([jnp.where(head_lanes[h], q, zero) for h in range(hb)],
                                     axis=0)
            s = lax.dot_general(q_rows, k, (((1,), (1,)), ((), ())), preferred_element_type=F32)
            s = jnp.where(valid_b, s, -jnp.inf)
            m = jnp.max(s, axis=-1, keepdims=True)
            p = jnp.exp2(s - m)
            l = jnp.sum(p, axis=-1, keepdims=True)
            p = p.astype(BF16)
            inv_l = 1.0 / l
            for h in range(hb):
                head = g * hb + h
                m_tile = jnp.where(lane == head, m[h * blk:(h + 1) * blk], m_tile)
                l_tile = jnp.where(lane == head, l[h * blk:(h + 1) * blk], l_tile)
            for pp in range(hb // 2):
                rows2 = slice(2 * pp * blk, (2 * pp + 2) * blk)
                pv = jnp.dot(p[rows2], v[:, pp * LANES:(pp + 1) * LANES],
                             preferred_element_type=F32) * inv_l[rows2]
                put(g * (hb // 2) + pp, r, u, jnp.where(first_head, pv[:blk], pv[blk:]))
        put(N_PAIRS, r, u, m_tile * LN_2 + jnp.log(l_tile))

    if d * n_sub <= ATTN_UNROLL:
        for r in range(d):
            for u in range(n_sub):
                unit(r, u)
        if stage_scr is not None:
            for a in range(f):
                merged = pl.ds(a, stage_scr.shape[2], stride=f)
                for idx in range(N_PAIRS):
                    o_ref[0, idx, merged, :] = stage_scr[idx, a]
                lse_ref[0, merged, :] = stage_scr[N_PAIRS, a]
    else:
        assert stage_scr is None
        def residue(r, carry):
            for u in range(n_sub):
                unit(r, u)
            return carry
        lax.fori_loop(0, d, residue, 0, unroll=ATTN_UNROLL // n_sub)


def _attn_call(qkv, dilation):
    b, d, sub_len, _ = qkv.shape
    assert d == dilation
    blk = ATTN_BLOCK
    n_sub = max(1, ATTN_UNROLL // d)
    rows = n_sub * blk
    steps = sub_len // rows
    s = sub_len * d

    def cur(kind):
        return lambda bi, n: (bi, 0, n, kind)

    def prev(kind):
        return lambda bi, n: (bi, 0, jnp.maximum(n * n_sub - 1, 0), kind)

    cur_shape = (1, d, rows, GROUP_WIDTH)
    prev_shape = (1, d, blk, GROUP_WIDTH)
    o, lse = pl.pallas_call(
        _attn_kernel,
        grid=(b, steps),
        in_specs=[
            pl.BlockSpec(cur_shape, cur(0)),
            pl.BlockSpec(prev_shape, prev(1)),
            pl.BlockSpec(cur_shape, cur(1)),
            pl.BlockSpec(prev_shape, prev(2)),
            pl.BlockSpec(cur_shape, cur(2)),
        ],
        out_specs=[
            pl.BlockSpec((1, N_PAIRS, d * rows, LANES), lambda bi, n: (bi, 0, n, 0)),
            pl.BlockSpec((1, d * rows, LANES), lambda bi, n: (bi, n, 0)),
        ],
        out_shape=[
            jax.ShapeDtypeStruct((b, N_PAIRS, s, LANES), F32),
            jax.ShapeDtypeStruct((b, s, LANES), F32),
        ],
        scratch_shapes=([pltpu.VMEM((N_PAIRS + 1, FAST_STRIDE, d * rows // FAST_STRIDE, LANES), F32)]
                        if d > FAST_STRIDE else []),
        compiler_params=pltpu.CompilerParams(
            dimension_semantics=("parallel", "arbitrary"),
            vmem_limit_bytes=VMEM_LIMIT_BYTES),
        name=f"attn_d{d}",
    )(qkv, qkv, qkv, qkv, qkv)
    return o, lse


def _log_sigmoid(x):
    return jnp.minimum(x, 0.0) - jnp.log(1.0 + jnp.exp(-jnp.abs(x)))


def _split3(x):
    hi = x.astype(BF16)
    r1 = x - hi.astype(F32)
    mid = r1.astype(BF16)
    lo = (r1 - mid.astype(F32)).astype(BF16)
    return hi, mid, lo


def _gla_pairwise_products(b_scr, qs_scr, kf_scr, pcat_scr):
    sub = GLA_SUB
    tl = lax.broadcasted_iota(jnp.int32, (sub, GLA_HK), 0)
    for pair_i in range(GLA_STEP // (2 * sub)):
        tiles = []
        for r0 in (2 * pair_i * sub, (2 * pair_i + 1) * sub):
            b_blk, q_blk = b_scr[r0:r0 + sub, :], qs_scr[r0:r0 + sub, :]
            row = []
            for s in range(sub):
                d = b_blk - b_scr[r0 + s:r0 + s + 1, :]
                if s > 0:
                    d = jnp.where(tl >= s, d, -jnp.inf)
                row.append(q_blk * kf_scr[r0 + s:r0 + s + 1, :] * jnp.exp2(d))
            tiles.append(row)
        r0 = 2 * pair_i * sub
        for s in range(sub):
            pcat_scr[r0:r0 + 2 * sub, s * GLA_HK:(s + 1) * GLA_HK] = (
                jnp.concatenate([tiles[0][s], tiles[1][s]], axis=0).astype(BF16))


def _gla_reference_factors(b, qs, kf):
    c_len = GLA_CHUNK
    n_chunks = GLA_STEP // c_len
    sub = GLA_SUB

    def z(nrows):
        return jnp.zeros((nrows, GLA_HK), F32)

    qb, kb = [], []
    for r0 in range(0, GLA_STEP, 2 * sub):
        mid = r0 + sub
        b_ref = b[mid - 1:mid]
        qb += [z(sub), qs[mid:mid + sub] * jnp.exp2(b[mid:mid + sub] - b_ref)]
        kb += [kf[r0:mid] * jnp.exp2(b_ref - b[r0:mid]), z(sub)]
    q_block = jnp.concatenate(qb, axis=0).astype(BF16)
    k_block = jnp.concatenate(kb, axis=0).astype(BF16)

    qw, kw = [], []
    for c in range(n_chunks):
        bc, qc, kc = (x[c * c_len:(c + 1) * c_len] for x in (b, qs, kf))
        b15, b31, b47 = bc[15:16], bc[31:32], bc[47:48]
        q1 = jnp.concatenate([z(16), qc[16:32] * jnp.exp2(bc[16:32] - b15), z(32)], axis=0)
        q2 = jnp.concatenate([z(32), qc[32:64] * jnp.exp2(bc[32:64] - b31)], axis=0)
        q3 = jnp.concatenate([z(48), qc[48:64] * jnp.exp2(bc[48:64] - b47)], axis=0)
        k1 = jnp.concatenate([kc[0:16] * jnp.exp2(b15 - bc[0:16]), z(48)], axis=0)
        k2 = jnp.concatenate([kc[0:32] * jnp.exp2(b31 - bc[0:32]), z(32)], axis=0)
        k3 = jnp.concatenate([z(32), kc[32:48] * jnp.exp2(b47 - bc[32:48]), z(16)], axis=0)
        qw.append(jnp.concatenate([q1, q2, q3], axis=1))
        kw.append(jnp.concatenate([k1, k2, k3], axis=1))
    q_within = jnp.concatenate(qw, axis=0).astype(BF16)
    k_within = jnp.concatenate(kw, axis=0).astype(BF16)

    qx, kx = [], []
    for j in range(n_chunks - 1):
        lo, hi = j * c_len, (j + 1) * c_len
        b_ref = b[hi - 1:hi]
        qx.append(jnp.concatenate([z(hi), qs[hi:] * jnp.exp2(b[hi:] - b_ref)], axis=0))
        parts = [kf[lo:hi] * jnp.exp2(b_ref - b[lo:hi])]
        if lo:
            parts.insert(0, z(lo))
        parts.append(z(GLA_STEP - hi))
        kx.append(jnp.concatenate(parts, axis=0))
    q_cross = jnp.concatenate(qx, axis=1).astype(BF16)
    k_cross = jnp.concatenate(kx, axis=1).astype(BF16)
    return (q_block, k_block), (q_within, k_within), (q_cross, k_cross)


def _gla_decay(pa_ref, u_ref, bias_ref, ltri_ref):
    logits = jnp.dot(pa_ref[...].astype(BF16), u_ref[...], preferred_element_type=F32) + bias_ref[...]
    la = _log_sigmoid(logits) * (LOG2_E / GLA_TAU)
    parts = jnp.dot(ltri_ref[...], jnp.concatenate(_split3(la), axis=1), preferred_element_type=F32)
    return parts[:, :GLA_DK] + parts[:, GLA_DK:2 * GLA_DK] + parts[:, 2 * GLA_DK:]


def _gla_kernel(q_ref, k_ref, v_ref, r_ref, pa_ref, pa_next_ref, u_ref, bias_ref, gn_ref, ltri_ref, e_ref,
                o_ref, st_ref, b_scr, qs_scr, kf_scr, pcat_scr, decay_scr, span_scr, *, steps_per_seq):
    step = pl.program_id(0)
    n = GLA_STEP
    n_sub = q_ref.shape[0] // n

    @pl.when(lax.rem(step, steps_per_seq) == 0)
    def _():
        st_ref[...] = jnp.zeros_like(st_ref)

    def prepare(next_pa_ref):
        decay = _gla_decay(next_pa_ref, u_ref, bias_ref, ltri_ref)
        decay_scr[...] = decay
        span_scr[0] = jnp.max(-decay[n - 1:n, :])

    @pl.when(step == 0)
    def _():
        prepare(pa_ref.at[0:n])

    def run(rows, next_pa_ref):
        def body(small_span):
            _gla_substep(decay_scr[...], q_ref.at[rows], k_ref.at[rows], v_ref.at[rows],
                         r_ref.at[rows], gn_ref, e_ref, o_ref.at[rows], st_ref,
                         b_scr, qs_scr, kf_scr, pcat_scr, small_span=small_span)
            prepare(next_pa_ref)

        small_span = span_scr[0] <= GLA_SAFE_SPAN
        pl.when(small_span)(lambda: body(True))
        pl.when(jnp.logical_not(small_span))(lambda: body(False))

    def substep(ss, carry):
        rows = pl.ds(pl.multiple_of(ss * n, n), n)
        nxt = pl.ds(pl.multiple_of((ss + 1) * n, n), n)
        run(rows, pa_ref.at[nxt])
        return carry

    lax.fori_loop(0, n_sub - 1, substep, 0)
    run(slice((n_sub - 1) * n, n_sub * n), pa_next_ref)


def _gla_substep(b_all, q_ref, k_ref, v_ref, r_ref, gn_ref, e_ref,
                 o_ref, st_ref, b_scr, qs_scr, kf_scr, pcat_scr, *, small_span):
    n = GLA_STEP
    nt = (((1,), (1,)), ((), ()))

    heads = []
    for h in range(GLA_HEADS):
        kcols = slice(h * GLA_HK, (h + 1) * GLA_HK)
        b = b_all[:, kcols]
        qs = q_ref[:, kcols].astype(F32) * (GLA_HK ** -0.5)
        kf = k_ref[:, kcols].astype(F32)
        if not small_span:
            b_scr[h], qs_scr[h], kf_scr[h] = b, qs, kf
            _gla_pairwise_products(b_scr.at[h], qs_scr.at[h], kf_scr.at[h],
                                   pcat_scr.at[h * n:(h + 1) * n])
        heads.append((b, qs, kf))

    ri = lax.broadcasted_iota(jnp.int32, (n, n), 0)
    ci = lax.broadcasted_iota(jnp.int32, (n, n), 1)
    if small_span:
        causal = ri >= ci
    else:
        a_diag = jnp.dot(pcat_scr[...], e_ref[...], preferred_element_type=F32)
        same_sub = (ri // GLA_SUB) == (ci // GLA_SUB)
        same_pair = (ri // (2 * GLA_SUB)) == (ci // (2 * GLA_SUB))
        same_chunk = (ri // GLA_CHUNK) == (ci // GLA_CHUNK)
    gn = gn_ref[...]

    for h, (b, qs, kf) in enumerate(heads):
        vcols = slice(h * GLA_HV, (h + 1) * GLA_HV)
        q_in = (qs * jnp.exp2(b)).astype(BF16)
        if small_span:
            k_out = (kf * jnp.exp2(-b)).astype(BF16)
            a = jnp.where(causal, lax.dot_general(q_in, k_out, nt, preferred_element_type=F32), 0.0)
        else:
            a_block, a_within, a_cross = (
                lax.dot_general(qf, kf_, nt, preferred_element_type=F32)
                for qf, kf_ in _gla_reference_factors(b, qs, kf))
            a = jnp.where(same_sub, a_diag[h * n:(h + 1) * n],
                          jnp.where(same_pair, a_block, jnp.where(same_chunk, a_within, a_cross)))
        v = v_ref[:, vcols]
        st = st_ref[h]
        b_last = b[n - 1:n]
        o = (jnp.dot(a.astype(BF16), v, preferred_element_type=F32)
             + lax.dot_general(q_in, st.astype(BF16), nt, preferred_element_type=F32))
        k_st = (kf * jnp.exp2(b_last - b)).astype(BF16)
        upd = lax.dot_general(v, k_st, (((0,), (0,)), ((), ())), preferred_element_type=F32)
        st_ref[h] = st * jnp.exp2(b_last) + upd

        ms = jnp.mean(o * o, axis=-1, keepdims=True)
        y = o * lax.rsqrt(ms + EPS) * gn
        r = r_ref[:, vcols].astype(F32)
        o_ref[:, vcols] = (y * (r * jax.nn.sigmoid(r))).astype(BF16)


def _gla_constants():
    n = GLA_STEP
    idx = np.arange(n)
    ltri = idx[:, None] >= idx[None, :]
    rows = np.arange(GLA_SUB * GLA_HK)
    cols = np.arange(n)
    e = (rows[:, None] // GLA_HK) == (cols[None, :] % GLA_SUB)
    return jnp.asarray(ltri, BF16), jnp.asarray(e, BF16)


def _gla_call(proj2d, pa, u_pad, bias, gn, batch):
    t = proj2d.shape[0]
    n = GLA_STEP
    rows = n * GLA_SUBSTEPS
    steps_per_seq = t // batch // rows
    n_steps = t // rows
    ltri, e = _gla_constants()

    def tok(width, col_block):
        return pl.BlockSpec((rows, width), lambda i: (i, col_block))

    def const(shape):
        return pl.BlockSpec(shape, lambda i: (0, 0))

    pa_next = pl.BlockSpec((n, PA_PAD), lambda i: (jnp.minimum(i + 1, n_steps - 1) * GLA_SUBSTEPS, 0))

    return pl.pallas_call(
        functools.partial(_gla_kernel, steps_per_seq=steps_per_seq),
        grid=(n_steps,),
        in_specs=[
            tok(GLA_DK, P_GQ // GLA_DK),
            tok(GLA_DK, P_GK // GLA_DK),
            tok(GLA_DV, P_GV // GLA_DV),
            tok(GLA_DV, P_GR // GLA_DV),
            tok(PA_PAD, 0),
            pa_next,
            const((PA_PAD, GLA_DK)),
            const((1, GLA_DK)),
            const((1, GLA_HV)),
            const((n, n)),
            const((GLA_SUB * GLA_HK, n)),
        ],
        out_specs=tok(GLA_DV, 0),
        out_shape=jax.ShapeDtypeStruct((t, GLA_DV), BF16),
        scratch_shapes=[
            pltpu.VMEM((GLA_HEADS, GLA_HV, GLA_HK), F32),
            pltpu.VMEM((GLA_HEADS, n, GLA_HK), F32),
            pltpu.VMEM((GLA_HEADS, n, GLA_HK), F32),
            pltpu.VMEM((GLA_HEADS, n, GLA_HK), F32),
            pltpu.VMEM((GLA_HEADS * n, GLA_SUB * GLA_HK), BF16),
            pltpu.VMEM((n, GLA_DK), F32),
            pltpu.SMEM((1,), F32),
        ],
        compiler_params=pltpu.CompilerParams(
            dimension_semantics=("arbitrary",),
            vmem_limit_bytes=VMEM_LIMIT_BYTES),
        name="gla",
    )(proj2d, proj2d, proj2d, proj2d, pa, pa, u_pad, bias, gn, ltri, e)


FF_CHUNK = 1024


def _out_kernel(x_ref, o0_ref, o1_ref, o2_ref, l0_ref, l1_ref, l2_ref, og_ref, gate_a_ref, gate_g_ref, gbias_ref,
                hx_ref, wa_ref, wb_ref, wo_ref, g2_ref, wup_ref, wdn_ref, out_ref):
    l0, l1, l2 = l0_ref[0], l1_ref[0], l2_ref[0]
    mx = jnp.maximum(jnp.maximum(l0, l1), l2)
    e0, e1, e2 = jnp.exp(l0 - mx), jnp.exp(l1 - mx), jnp.exp(l2 - mx)
    inv = 1.0 / (e0 + e1 + e2)
    hx = hx_ref[...]
    def pairs(ref):
        return jnp.concatenate([ref[0, p] for p in range(N_PAIRS)], axis=1)

    def expand(w):
        return jnp.dot(w.astype(BF16), hx, preferred_element_type=F32)

    o_attn = (expand(e0 * inv) * pairs(o0_ref) + expand(e1 * inv) * pairs(o1_ref)
              + expand(e2 * inv) * pairs(o2_ref))
    a = jnp.dot(o_attn.astype(BF16), wa_ref[...], preferred_element_type=F32)
    g = jnp.dot(og_ref[...], wb_ref[...], preferred_element_type=F32)
    gate_a = jax.nn.sigmoid(gate_a_ref[...].astype(F32) + gbias_ref[:, :D_MODEL])
    gate_g = jax.nn.sigmoid(gate_g_ref[...].astype(F32) + gbias_ref[:, D_MODEL:])
    mixed = gate_a * a + gate_g * g
    x1 = x_ref[...] + jnp.dot(mixed.astype(BF16), wo_ref[...], preferred_element_type=F32)

    ms = jnp.mean(x1 * x1, axis=-1, keepdims=True)
    h2 = (x1 * lax.rsqrt(ms + EPS) * g2_ref[...]).astype(BF16)
    hidden = []
    for c in range(D_FF // FF_CHUNK):
        u = jnp.dot(h2, wup_ref[:, c * FF_CHUNK:(c + 1) * FF_CHUNK], preferred_element_type=F32)
        u = jnp.maximum(u, 0.0)
        hidden.append((u * u).astype(BF16))
    out_ref[...] = x1 + jnp.dot(jnp.concatenate(hidden, axis=1), wdn_ref[...],
                                preferred_element_type=F32)


def _out_call(x2d, o_groups, lse_groups, o_gla, proj2d, gbias, wa, wb, wo, g2, wup, wdn, batch, tm):
    t = x2d.shape[0]
    lanes = np.arange(LANES)
    cols = np.arange(GROUP_WIDTH)
    head_expand = jnp.asarray(lanes[:, None] == cols[None, :] // HEAD_DIM, BF16)

    tiles_per_seq = t // batch // tm

    def tok(width):
        return pl.BlockSpec((tm, width), lambda i: (i, 0))

    attn_o = pl.BlockSpec((1, N_PAIRS, tm, LANES),
                          lambda i: (i // tiles_per_seq, 0, i % tiles_per_seq, 0))
    attn_lse = pl.BlockSpec((1, tm, LANES), lambda i: (i // tiles_per_seq, i % tiles_per_seq, 0))

    def const(shape):
        return pl.BlockSpec(shape, lambda i: (0, 0), pipeline_mode=pl.Buffered(1))

    return pl.pallas_call(
        _out_kernel,
        grid=(t // tm,),
        in_specs=[
            tok(D_MODEL),
            attn_o, attn_o, attn_o,
            attn_lse, attn_lse, attn_lse,
            tok(GLA_DV),
            pl.BlockSpec((tm, D_MODEL), lambda i: (i, P_GATE // D_MODEL)),
            pl.BlockSpec((tm, D_MODEL), lambda i: (i, P_GATE // D_MODEL + 1)),
            const((1, 2 * D_MODEL)),
            const((LANES, GROUP_WIDTH)),
            const((GROUP_WIDTH, D_MODEL)),
            const((GLA_DV, D_MODEL)),
            const((D_MODEL, D_MODEL)),
            const((1, D_MODEL)),
            const((D_MODEL, D_FF)),
            const((D_FF, D_MODEL)),
        ],
        out_specs=tok(D_MODEL),
        out_shape=jax.ShapeDtypeStruct((t, D_MODEL), F32),
        compiler_params=pltpu.CompilerParams(
            dimension_semantics=("parallel",),
            vmem_limit_bytes=VMEM_LIMIT_BYTES),
        name="out",
    )(x2d, *o_groups, *lse_groups, o_gla, proj2d, proj2d, gbias, head_expand, wa, wb, wo, g2, wup, wdn)


def _layer(x2d, batch, norm1_g, w_in, gq, gk, gate_up, gate_bias, gla_norm_g, branch_bias,
           w_a, w_b, w_out, norm2_g, w_up, w_down):
    g1 = norm1_g.reshape(1, D_MODEL)
    w_in_bf16 = w_in.astype(BF16)
    w_gate = w_in_bf16[:, O_GATE:O_GATE + 2 * D_MODEL]
    w_pa = jnp.pad(w_in_bf16[:, O_PA:O_PA + GLA_RANK], ((0, 0), (0, PA_PAD - GLA_RANK)))
    q_gain = jnp.tile(gq, HEADS_PER_GROUP) * (HEAD_DIM ** -0.5 * LOG2_E)
    k_gain = jnp.tile(gk, HEADS_PER_GROUP)
    qk_gain = jnp.stack([q_gain, k_gain]).reshape(2, 1, GROUP_WIDTH)

    proj, pa = _main_proj_call(x2d, g1, w_in_bf16, w_gate, w_pa, tm=PROJ_TM, rows=PROJ_ROWS)
    qkv_groups = _attn_proj_call(x2d, g1, w_in_bf16, qk_gain, batch, tm=PROJ_TM, rows=PROJ_ROWS)

    o_groups, lse_groups = [], []
    for qkv, (_, dilation) in zip(qkv_groups, ATTN_GROUPS):
        o, lse = _attn_call(qkv, dilation)
        o_groups.append(o)
        lse_groups.append(lse)

    u_pad = jnp.pad(gate_up, ((0, PA_PAD - GLA_RANK), (0, 0))).astype(BF16)
    o_gla = _gla_call(proj, pa, u_pad, gate_bias.reshape(1, GLA_DK),
                      gla_norm_g.reshape(1, GLA_HV), batch)

    return _out_call(x2d, o_groups, lse_groups, o_gla, proj,
                     branch_bias.reshape(1, 2 * D_MODEL),
                     w_a.astype(BF16), w_b.astype(BF16), w_out.astype(BF16),
                     norm2_g.reshape(1, D_MODEL), w_up.astype(BF16), w_down.astype(BF16), batch, OUT_TM)


def kernel(x, norm1_g, w_in, attn_q_norm_g, attn_k_norm_g, gla_gate_up, gla_gate_bias, gla_out_norm_g, branch_gate_bias, w_attn_branch, w_gla_branch, w_out, norm2_g, w_ff_up, w_ff_down):
    b, s, d = x.shape
    x2d = x.reshape(b * s, d)
    for l in range(norm1_g.shape[0]):
        x2d = _layer(x2d, b, norm1_g[l], w_in[l], attn_q_norm_g[l], attn_k_norm_g[l],
                     gla_gate_up[l], gla_gate_bias[l], gla_out_norm_g[l], branch_gate_bias[l],
                     w_attn_branch[l], w_gla_branch[l], w_out[l], norm2_g[l],
                     w_ff_up[l], w_ff_down[l])
    return x2d.reshape(b, s, d)
```

```python
import functools

import numpy as np
import jax
import jax.numpy as jnp
from jax import lax
from jax.experimental import pallas as pl
from jax.experimental.pallas import tpu as pltpu

F32 = jnp.float32
BF16 = jnp.bfloat16

FAST_STRIDE = 4
LANES = 128
VMEM_LIMIT_BYTES = 56 * 1024 * 1024
PROJ_TM = 1024
PROJ_ROWS = 512
OUT_TM = 512

D_MODEL = 1024
ATTN_GROUPS = ((128, 1), (512, 4), (2048, 16))
N_GROUPS = len(ATTN_GROUPS)
HEADS_PER_GROUP = 8
HEAD_DIM = 64
ATTN_BLOCK = 128
GROUP_WIDTH = HEADS_PER_GROUP * HEAD_DIM
ATTN_WIDTH = 3 * N_GROUPS * GROUP_WIDTH
N_PAIRS = GROUP_WIDTH // LANES
N_STATS = 2
ATTN_HEADS_PER_DOT = 4
ATTN_UNROLL = 16

GLA_HEADS = 4
GLA_DK = 512
GLA_DV = 1024
GLA_HK = GLA_DK // GLA_HEADS
GLA_HV = GLA_DV // GLA_HEADS
GLA_RANK = 16
GLA_TAU = 16.0
GLA_CHUNK = 64
GLA_SUB = 8
GLA_STEP = 256
GLA_SUBSTEPS = 4
GLA_SAFE_SPAN = 64.0

D_FF = 4 * D_MODEL
EPS = 1e-6
LOG2_E = 1.4426950408889634

_ORIG_SIZES = (ATTN_WIDTH, GLA_DK, GLA_DK, GLA_DV, GLA_DV, GLA_RANK, 2 * D_MODEL)
_ORIG_OFF = tuple(int(v) for v in np.cumsum((0,) + _ORIG_SIZES))
O_ATTN, O_GQ, O_GK, O_GV, O_GR, O_PA, O_GATE = _ORIG_OFF[:7]

P_GQ = 0
P_GK = P_GQ + GLA_DK
P_GV = P_GK + GLA_DK
P_GR = P_GV + GLA_DV
P_GATE = P_GR + GLA_DV
P_MAIN = P_GATE + 2 * D_MODEL
GLA_W_BLOCK = (O_GATE - GLA_RANK - O_GQ) // 2
N_KINDS = 3
PA_PAD = LANES


def _rms_norm_rows(x, gain):
    ms = jnp.mean(x * x, axis=-1, keepdims=True)
    return x * lax.rsqrt(ms + EPS) * gain


def _main_proj_kernel(x_ref, g1_ref, wa_ref, wb_ref, wgate_ref, wpa_ref, o_ref, pa_ref, *, rows):
    gain = g1_ref[...]
    for rc in range(x_ref.shape[0] // rows):
        rs = slice(rc * rows, (rc + 1) * rows)
        h = _rms_norm_rows(x_ref[rs, :], gain).astype(BF16)
        pa_ref[rs, :] = jnp.dot(h, wpa_ref[...], preferred_element_type=F32)
        col = 0
        for w_ref in (wa_ref, wb_ref, wgate_ref):
            width = w_ref.shape[1]
            o_ref[rs, col:col + width] = jnp.dot(
                h, w_ref[...], preferred_element_type=F32).astype(BF16)
            col += width


def _main_proj_call(x2d, g1, w_in_bf16, w_gate, w_pa, tm, rows):
    t = x2d.shape[0]
    first_block = O_GQ // GLA_W_BLOCK

    def const(shape, col_block=0):
        return pl.BlockSpec(shape, lambda i: (0, col_block), pipeline_mode=pl.Buffered(1))

    return pl.pallas_call(
        functools.partial(_main_proj_kernel, rows=rows),
        grid=(t // tm,),
        in_specs=[
            pl.BlockSpec((tm, D_MODEL), lambda i: (i, 0)),
            const((1, D_MODEL)),
            const((D_MODEL, GLA_W_BLOCK), first_block),
            const((D_MODEL, GLA_W_BLOCK), first_block + 1),
            const((D_MODEL, 2 * D_MODEL)),
            const((D_MODEL, PA_PAD)),
        ],
        out_specs=[
            pl.BlockSpec((tm, P_MAIN), lambda i: (i, 0)),
            pl.BlockSpec((tm, PA_PAD), lambda i: (i, 0)),
        ],
        out_shape=[
            jax.ShapeDtypeStruct((t, P_MAIN), BF16),
            jax.ShapeDtypeStruct((t, PA_PAD), F32),
        ],
        compiler_params=pltpu.CompilerParams(
            dimension_semantics=("parallel",),
            vmem_limit_bytes=VMEM_LIMIT_BYTES),
        name="proj_main",
    )(x2d, g1, w_in_bf16, w_in_bf16, w_gate, w_pa)


def _qkv_project(h, w_refs, gain_ref, o_ref, chunk, residue_of_block=None):
    d = o_ref.shape[1]
    rows = h.shape[0] // d
    residue_of_block = residue_of_block or list(range(d))
    for kind in range(N_KINDS):
        cols = slice(kind * GROUP_WIDTH, (kind + 1) * GROUP_WIDTH)
        acc = jnp.dot(h, w_refs[kind][...], preferred_element_type=F32)
        if kind < 2:
            sq = acc * acc
            first_head = lax.broadcasted_iota(jnp.int32, (1, LANES), 1) < HEAD_DIM
            scales = []
            for pp in range(N_PAIRS):
                t = sq[:, pp * LANES:(pp + 1) * LANES]
                s_lo = jnp.sum(jnp.where(first_head, t, 0.0), axis=-1, keepdims=True)
                s_hi = jnp.sum(jnp.where(first_head, 0.0, t), axis=-1, keepdims=True)
                scales.append(jnp.where(first_head,
                                        lax.rsqrt(s_lo * (1.0 / HEAD_DIM) + EPS),
                                        lax.rsqrt(s_hi * (1.0 / HEAD_DIM) + EPS)))
            acc = acc * jnp.concatenate(scales, axis=1) * gain_ref[kind]
        y = acc.astype(BF16)
        for j, r in enumerate(residue_of_block):
            o_ref[0, r, chunk * rows:(chunk + 1) * rows, cols] = y[j * rows:(j + 1) * rows, :]


def _attn_proj_kernel(x_ref, g1_ref, *rest):
    n_w = N_KINDS * N_GROUPS
    w_all = rest[:n_w]
    gain_ref, o0_ref, o1_ref, o2_ref, col_scr, col2_scr, perm1_scr, perm2_scr = rest[n_w:]
    w_group = [[w_all[kind * N_GROUPS + g] for kind in range(N_KINDS)] for g in range(N_GROUPS)]
    n_chunks, _, tc, _ = col_scr.shape
    n_col = D_MODEL // LANES
    gain = g1_ref[...]
    for ch in range(n_chunks):
        hf = _rms_norm_rows(x_ref[ch * tc:(ch + 1) * tc, :], gain)
        for c in range(n_col):
            col_scr[ch, c] = hf[:, c * LANES:(c + 1) * LANES]
        _qkv_project(hf.astype(BF16), w_group[0], gain_ref, o0_ref, ch)
        f = o1_ref.shape[1]
        assert o2_ref.shape[1] == f * f
        rows1, rows2 = tc // f, tc // (f * f)
        for a in range(f):
            for c in range(n_col):
                t = col_scr[ch, c, pl.ds(a, rows1, stride=f), :]
                col2_scr[ch, c, a * rows1:(a + 1) * rows1, :] = t
                perm1_scr[ch, a * rows1:(a + 1) * rows1, c * LANES:(c + 1) * LANES] = t.astype(BF16)
        _qkv_project(perm1_scr[ch], w_group[1], gain_ref, o1_ref, ch)
        for a in range(f):
            for b in range(f):
                blk = a * f + b
                for c in range(n_col):
                    perm2_scr[ch, blk * rows2:(blk + 1) * rows2, c * LANES:(c + 1) * LANES] = (
                        col2_scr[ch, c, pl.ds(a * rows1 + b, rows2, stride=f), :].astype(BF16))
        _qkv_project(perm2_scr[ch], w_group[2], gain_ref, o2_ref, ch,
                     residue_of_block=[b * f + a for a in range(f) for b in range(f)])


def _attn_proj_call(x2d, g1, w_in_bf16, qk_gain, batch, tm, rows):
    t = x2d.shape[0]
    s = t // batch
    tiles_per_seq = s // tm
    width = N_KINDS * GROUP_WIDTH
    n_w = N_KINDS * N_GROUPS

    def const(shape):
        return pl.BlockSpec(shape, lambda i: (0,) * len(shape), pipeline_mode=pl.Buffered(1))

    def w_block(j):
        return pl.BlockSpec((D_MODEL, GROUP_WIDTH), lambda i: (0, O_ATTN // GROUP_WIDTH + j),
                            pipeline_mode=pl.Buffered(1))

    return pl.pallas_call(
        _attn_proj_kernel,
        grid=(t // tm,),
        in_specs=[
            pl.BlockSpec((tm, D_MODEL), lambda i: (i, 0)),
            const((1, D_MODEL)),
            *[w_block(j) for j in range(n_w)],
            const((2, 1, GROUP_WIDTH)),
        ],
        out_specs=[
            pl.BlockSpec((1, d, tm // d, width),
                         lambda i: (i // tiles_per_seq, 0, i % tiles_per_seq, 0))
            for _, d in ATTN_GROUPS],
        out_shape=[jax.ShapeDtypeStruct((batch, d, s // d, width), BF16) for _, d in ATTN_GROUPS],
        scratch_shapes=[
            pltpu.VMEM((tm // rows, D_MODEL // LANES, rows, LANES), F32),
            pltpu.VMEM((tm // rows, D_MODEL // LANES, rows, LANES), F32),
            pltpu.VMEM((tm // rows, rows, D_MODEL), BF16),
            pltpu.VMEM((tm // rows, rows, D_MODEL), BF16),
        ],
        compiler_params=pltpu.CompilerParams(
            dimension_semantics=("parallel",),
            vmem_limit_bytes=VMEM_LIMIT_BYTES),
        name="proj_attn",
    )(x2d, g1, *([w_in_bf16] * n_w), qk_gain)


def _attn_kernel(q_ref, kp_ref, kc_ref, vp_ref, vc_ref, o_ref, stat_ref, *stage):
    n = pl.program_id(1)
    d = q_ref.shape[1]
    blk = ATTN_BLOCK
    n_sub = q_ref.shape[2] // blk
    stage_scr = stage[0] if stage else None
    f = FAST_STRIDE

    def put(idx, r, u, tile):
        if stage_scr is None:
            tok = pl.ds(u * blk * d + r, blk, stride=d)
            if idx >= N_PAIRS:
                stat_ref[0, idx - N_PAIRS, tok, :] = tile
            else:
                o_ref[0, idx, tok, :] = tile
        else:
            a, b = r % f, r // f
            stage_scr[idx, a, pl.ds(u * blk * (d // f) + b, blk, stride=d // f), :] = tile
    qi = lax.broadcasted_iota(jnp.int32, (blk, 2 * blk), 0)
    ki = lax.broadcasted_iota(jnp.int32, (blk, 2 * blk), 1)
    band = (ki >= qi) & (ki <= qi + blk)
    band_first = band & ((ki >= blk) | (n > 0))
    lane = lax.broadcasted_iota(jnp.int32, (blk, LANES), 1)
    first_head = lane < HEAD_DIM
    hb = ATTN_HEADS_PER_DOT
    width = hb * HEAD_DIM
    lane_w = lax.broadcasted_iota(jnp.int32, (blk, width), 1)
    head_lanes = [(lane_w >= h * HEAD_DIM) & (lane_w < (h + 1) * HEAD_DIM) for h in range(hb)]
    valid_by_sub = [jnp.concatenate([band_first if u == 0 else band] * hb, axis=0)
                    for u in range(min(n_sub, 2))]

    def unit(r, u):
        rows = slice(u * blk, (u + 1) * blk)
        valid_b = valid_by_sub[min(u, 1)]
        m_tile = jnp.zeros((blk, LANES), F32)
        l_tile = jnp.ones((blk, LANES), F32)
        for g in range(HEADS_PER_GROUP // hb):
            sl = slice(g * width, (g + 1) * width)
            q = q_ref[0, r, rows, sl]
            if u == 0:
                k_prev, v_prev = kp_ref[0, r, :, sl], vp_ref[0, r, :, sl]
            else:
                prev_rows = slice((u - 1) * blk, u * blk)
                k_prev, v_prev = kc_ref[0, r, prev_rows, sl], vc_ref[0, r, prev_rows, sl]
            k = jnp.concatenate([k_prev, kc_ref[0, r, rows, sl]], axis=0)
            v = jnp.concatenate([v_prev, vc_ref[0, r, rows, sl]], axis=0)
            zero = jnp.zeros_like(q)
            q_rows = jnp.concatenate([jnp.where(head_lanes[h], q, zero) for h in range(hb)],
                                     axis=0)
            s = lax.dot_general(q_rows, k, (((1,), (1,)), ((), ())), preferred_element_type=F32)
            s = jnp.where(valid_b, s, -jnp.inf)
            m = jnp.max(s, axis=-1, keepdims=True)
            p = jnp.exp2(s - m)
            l = jnp.sum(p, axis=-1, keepdims=True)
            p = p.astype(BF16)
            for h in range(hb):
                head = g * hb + h
                m_tile = jnp.where(lane == head, m[h * blk:(h + 1) * blk], m_tile)
                l_tile = jnp.where(lane == head, l[h * blk:(h + 1) * blk], l_tile)
            for pp in range(hb // 2):
                rows2 = slice(2 * pp * blk, (2 * pp + 2) * blk)
                pv = jnp.dot(p[rows2], v[:, pp * LANES:(pp + 1) * LANES],
                             preferred_element_type=F32)
                put(g * (hb // 2) + pp, r, u, jnp.where(first_head, pv[:blk], pv[blk:]))
        put(N_PAIRS, r, u, m_tile)
        put(N_PAIRS + 1, r, u, l_tile)

    if d * n_sub <= ATTN_UNROLL:
        for r in range(d):
            for u in range(n_sub):
                unit(r, u)
        if stage_scr is not None:
            for a in range(f):
                merged = pl.ds(a, stage_scr.shape[2], stride=f)
                for idx in range(N_PAIRS):
                    o_ref[0, idx, merged, :] = stage_scr[idx, a]
                for j in range(N_STATS):
                    stat_ref[0, j, merged, :] = stage_scr[N_PAIRS + j, a]
    else:
        assert stage_scr is None
        def residue(r, carry):
            for u in range(n_sub):
                unit(r, u)
            return carry
        lax.fori_loop(0, d, residue, 0, unroll=ATTN_UNROLL // n_sub)


def _attn_call(qkv, dilation):
    b, d, sub_len, _ = qkv.shape
    assert d == dilation
    blk = ATTN_BLOCK
    n_sub = max(1, ATTN_UNROLL // d)
    rows = n_sub * blk
    steps = sub_len // rows
    s = sub_len * d

    def cur(kind):
        return lambda bi, n: (bi, 0, n, kind)

    def prev(kind):
        return lambda bi, n: (bi, 0, jnp.maximum(n * n_sub - 1, 0), kind)

    cur_shape = (1, d, rows, GROUP_WIDTH)
    prev_shape = (1, d, blk, GROUP_WIDTH)
    o, stats = pl.pallas_call(
        _attn_kernel,
        grid=(b, steps),
        in_specs=[
            pl.BlockSpec(cur_shape, cur(0)),
            pl.BlockSpec(prev_shape, prev(1)),
            pl.BlockSpec(cur_shape, cur(1)),
            pl.BlockSpec(prev_shape, prev(2)),
            pl.BlockSpec(cur_shape, cur(2)),
        ],
        out_specs=[
            pl.BlockSpec((1, N_PAIRS, d * rows, LANES), lambda bi, n: (bi, 0, n, 0)),
            pl.BlockSpec((1, N_STATS, d * rows, LANES), lambda bi, n: (bi, 0, n, 0)),
        ],
        out_shape=[
            jax.ShapeDtypeStruct((b, N_PAIRS, s, LANES), F32),
            jax.ShapeDtypeStruct((b, N_STATS, s, LANES), F32),
        ],
        scratch_shapes=([pltpu.VMEM((N_PAIRS + N_STATS, FAST_STRIDE, d * rows // FAST_STRIDE, LANES), F32)]
                        if d > FAST_STRIDE else []),
        compiler_params=pltpu.CompilerParams(
            dimension_semantics=("parallel", "arbitrary"),
            vmem_limit_bytes=VMEM_LIMIT_BYTES),
        name=f"attn_d{d}",
    )(qkv, qkv, qkv, qkv, qkv)
    return o, stats


def _log_sigmoid(x):
    return jnp.minimum(x, 0.0) - jnp.log(1.0 + jnp.exp(-jnp.abs(x)))


def _split3(x):
    hi = x.astype(BF16)
    r1 = x - hi.astype(F32)
    mid = r1.astype(BF16)
    lo = (r1 - mid.astype(F32)).astype(BF16)
    return hi, mid, lo


def _gla_pairwise_products(b_scr, qs_scr, kf_scr, pcat_scr):
    sub = GLA_SUB
    tl = lax.broadcasted_iota(jnp.int32, (sub, GLA_HK), 0)
    for pair_i in range(GLA_STEP // (2 * sub)):
        tiles = []
        for r0 in (2 * pair_i * sub, (2 * pair_i + 1) * sub):
            b_blk, q_blk = b_scr[r0:r0 + sub, :], qs_scr[r0:r0 + sub, :]
            row = []
            for s in range(sub):
                d = b_blk - b_scr[r0 + s:r0 + s + 1, :]
                if s > 0:
                    d = jnp.where(tl >= s, d, -jnp.inf)
                row.append(q_blk * kf_scr[r0 + s:r0 + s + 1, :] * jnp.exp2(d))
            tiles.append(row)
        r0 = 2 * pair_i * sub
        for s in range(sub):
            pcat_scr[r0:r0 + 2 * sub, s * GLA_HK:(s + 1) * GLA_HK] = (
                jnp.concatenate([tiles[0][s], tiles[1][s]], axis=0).astype(BF16))


def _gla_reference_factors(b, qs, kf):
    c_len = GLA_CHUNK
    n_chunks = GLA_STEP // c_len
    sub = GLA_SUB

    def z(nrows):
        return jnp.zeros((nrows, GLA_HK), F32)

    qb, kb = [], []
    for r0 in range(0, GLA_STEP, 2 * sub):
        mid = r0 + sub
        b_ref = b[mid - 1:mid]
        qb += [z(sub), qs[mid:mid + sub] * jnp.exp2(b[mid:mid + sub] - b_ref)]
        kb += [kf[r0:mid] * jnp.exp2(b_ref - b[r0:mid]), z(sub)]
    q_block = jnp.concatenate(qb, axis=0).astype(BF16)
    k_block = jnp.concatenate(kb, axis=0).astype(BF16)

    qw, kw = [], []
    for c in range(n_chunks):
        bc, qc, kc = (x[c * c_len:(c + 1) * c_len] for x in (b, qs, kf))
        b15, b31, b47 = bc[15:16], bc[31:32], bc[47:48]
        q1 = jnp.concatenate([z(16), qc[16:32] * jnp.exp2(bc[16:32] - b15), z(32)], axis=0)
        q2 = jnp.concatenate([z(32), qc[32:64] * jnp.exp2(bc[32:64] - b31)], axis=0)
        q3 = jnp.concatenate([z(48), qc[48:64] * jnp.exp2(bc[48:64] - b47)], axis=0)
        k1 = jnp.concatenate([kc[0:16] * jnp.exp2(b15 - bc[0:16]), z(48)], axis=0)
        k2 = jnp.concatenate([kc[0:32] * jnp.exp2(b31 - bc[0:32]), z(32)], axis=0)
        k3 = jnp.concatenate([z(32), kc[32:48] * jnp.exp2(b47 - bc[32:48]), z(16)], axis=0)
        qw.append(jnp.concatenate([q1, q2, q3], axis=1))
        kw.append(jnp.concatenate([k1, k2, k3], axis=1))
    q_within = jnp.concatenate(qw, axis=0).astype(BF16)
    k_within = jnp.concatenate(kw, axis=0).astype(BF16)

    qx, kx = [], []
    for j in range(n_chunks - 1):
        lo, hi = j * c_len, (j + 1) * c_len
        b_ref = b[hi - 1:hi]
        qx.append(jnp.concatenate([z(hi), qs[hi:] * jnp.exp2(b[hi:] - b_ref)], axis=0))
        parts = [kf[lo:hi] * jnp.exp2(b_ref - b[lo:hi])]
        if lo:
            parts.insert(0, z(lo))
        parts.append(z(GLA_STEP - hi))
        kx.append(jnp.concatenate(parts, axis=0))
    q_cross = jnp.concatenate(qx, axis=1).astype(BF16)
    k_cross = jnp.concatenate(kx, axis=1).astype(BF16)
    return (q_block, k_block), (q_within, k_within), (q_cross, k_cross)


def _gla_decay(pa_ref, u_ref, bias_ref, ltri_ref):
    logits = jnp.dot(pa_ref[...].astype(BF16), u_ref[...], preferred_element_type=F32) + bias_ref[...]
    la = _log_sigmoid(logits) * (LOG2_E / GLA_TAU)
    parts = jnp.dot(ltri_ref[...], jnp.concatenate(_split3(la), axis=1), preferred_element_type=F32)
    return parts[:, :GLA_DK] + parts[:, GLA_DK:2 * GLA_DK] + parts[:, 2 * GLA_DK:]


def _gla_kernel(q_ref, k_ref, v_ref, r_ref, pa_ref, pa_next_ref, u_ref, bias_ref, gn_ref, ltri_ref, e_ref,
                o_ref, st_ref, b_scr, qs_scr, kf_scr, pcat_scr, decay_scr, span_scr, *, steps_per_seq):
    step = pl.program_id(0)
    n = GLA_STEP
    n_sub = q_ref.shape[0] // n

    @pl.when(lax.rem(step, steps_per_seq) == 0)
    def _():
        st_ref[...] = jnp.zeros_like(st_ref)

    def prepare(next_pa_ref):
        decay = _gla_decay(next_pa_ref, u_ref, bias_ref, ltri_ref)
        decay_scr[...] = decay
        span_scr[0] = jnp.max(-decay[n - 1:n, :])

    @pl.when(step == 0)
    def _():
        prepare(pa_ref.at[0:n])

    def run(rows, next_pa_ref):
        def body(small_span):
            _gla_substep(decay_scr[...], q_ref.at[rows], k_ref.at[rows], v_ref.at[rows],
                         r_ref.at[rows], gn_ref, e_ref, o_ref.at[rows], st_ref,
                         b_scr, qs_scr, kf_scr, pcat_scr, small_span=small_span)
            prepare(next_pa_ref)

        small_span = span_scr[0] <= GLA_SAFE_SPAN
        pl.when(small_span)(lambda: body(True))
        pl.when(jnp.logical_not(small_span))(lambda: body(False))

    def substep(ss, carry):
        rows = pl.ds(pl.multiple_of(ss * n, n), n)
        nxt = pl.ds(pl.multiple_of((ss + 1) * n, n), n)
        run(rows, pa_ref.at[nxt])
        return carry

    lax.fori_loop(0, n_sub - 1, substep, 0)
    run(slice((n_sub - 1) * n, n_sub * n), pa_next_ref)


def _gla_substep(b_all, q_ref, k_ref, v_ref, r_ref, gn_ref, e_ref,
                 o_ref, st_ref, b_scr, qs_scr, kf_scr, pcat_scr, *, small_span):
    n = GLA_STEP
    nt = (((1,), (1,)), ((), ()))

    heads = []
    for h in range(GLA_HEADS):
        kcols = slice(h * GLA_HK, (h + 1) * GLA_HK)
        b = b_all[:, kcols]
        qs = q_ref[:, kcols].astype(F32) * (GLA_HK ** -0.5)
        kf = k_ref[:, kcols].astype(F32)
        if not small_span:
            b_scr[h], qs_scr[h], kf_scr[h] = b, qs, kf
            _gla_pairwise_products(b_scr.at[h], qs_scr.at[h], kf_scr.at[h],
                                   pcat_scr.at[h * n:(h + 1) * n])
        heads.append((b, qs, kf))

    ri = lax.broadcasted_iota(jnp.int32, (n, n), 0)
    ci = lax.broadcasted_iota(jnp.int32, (n, n), 1)
    if small_span:
        causal = ri >= ci
    else:
        a_diag = jnp.dot(pcat_scr[...], e_ref[...], preferred_element_type=F32)
        same_sub = (ri // GLA_SUB) == (ci // GLA_SUB)
        same_pair = (ri // (2 * GLA_SUB)) == (ci // (2 * GLA_SUB))
        same_chunk = (ri // GLA_CHUNK) == (ci // GLA_CHUNK)
    gn = gn_ref[...]

    for h, (b, qs, kf) in enumerate(heads):
        vcols = slice(h * GLA_HV, (h + 1) * GLA_HV)
        q_in = (qs * jnp.exp2(b)).astype(BF16)
        b_last = b[n - 1:n]
        decay_all = jnp.exp2(b_last)
        if small_span:
            k_out = kf * jnp.exp2(-b)
            a = lax.dot_general(q_in, k_out.astype(BF16), nt, preferred_element_type=F32)
            a = jnp.where(causal, a.astype(BF16), 0.0)
            k_st = (k_out * decay_all).astype(BF16)
        else:
            a_block, a_within, a_cross = (
                lax.dot_general(qf, kf_, nt, preferred_element_type=F32)
                for qf, kf_ in _gla_reference_factors(b, qs, kf))
            a = jnp.where(same_sub, a_diag[h * n:(h + 1) * n],
                          jnp.where(same_pair, a_block, jnp.where(same_chunk, a_within, a_cross)))
            a = a.astype(BF16)
            k_st = (kf * jnp.exp2(b_last - b)).astype(BF16)
        v = v_ref[:, vcols]
        st = st_ref[h]
        o = (jnp.dot(a, v, preferred_element_type=F32)
             + lax.dot_general(q_in, st.astype(BF16), nt, preferred_element_type=F32))
        upd = lax.dot_general(v, k_st, (((0,), (0,)), ((), ())), preferred_element_type=F32)
        st_ref[h] = st * decay_all + upd

        ms = jnp.mean(o * o, axis=-1, keepdims=True)
        y = o * lax.rsqrt(ms + EPS) * gn
        r = r_ref[:, vcols].astype(F32)
        o_ref[:, vcols] = (y * (r * jax.nn.sigmoid(r))).astype(BF16)


def _gla_constants():
    n = GLA_STEP
    idx = np.arange(n)
    ltri = idx[:, None] >= idx[None, :]
    rows = np.arange(GLA_SUB * GLA_HK)
    cols = np.arange(n)
    e = (rows[:, None] // GLA_HK) == (cols[None, :] % GLA_SUB)
    return jnp.asarray(ltri, BF16), jnp.asarray(e, BF16)


def _gla_call(proj2d, pa, u_pad, bias, gn, batch):
    t = proj2d.shape[0]
    n = GLA_STEP
    rows = n * GLA_SUBSTEPS
    steps_per_seq = t // batch // rows
    n_steps = t // rows
    ltri, e = _gla_constants()

    def tok(width, col_block):
        return pl.BlockSpec((rows, width), lambda i: (i, col_block))

    def const(shape):
        return pl.BlockSpec(shape, lambda i: (0, 0))

    pa_next = pl.BlockSpec((n, PA_PAD), lambda i: (jnp.minimum(i + 1, n_steps - 1) * GLA_SUBSTEPS, 0))

    return pl.pallas_call(
        functools.partial(_gla_kernel, steps_per_seq=steps_per_seq),
        grid=(n_steps,),
        in_specs=[
            tok(GLA_DK, P_GQ // GLA_DK),
            tok(GLA_DK, P_GK // GLA_DK),
            tok(GLA_DV, P_GV // GLA_DV),
            tok(GLA_DV, P_GR // GLA_DV),
            tok(PA_PAD, 0),
            pa_next,
            const((PA_PAD, GLA_DK)),
            const((1, GLA_DK)),
            const((1, GLA_HV)),
            const((n, n)),
            const((GLA_SUB * GLA_HK, n)),
        ],
        out_specs=tok(GLA_DV, 0),
        out_shape=jax.ShapeDtypeStruct((t, GLA_DV), BF16),
        scratch_shapes=[
            pltpu.VMEM((GLA_HEADS, GLA_HV, GLA_HK), F32),
            pltpu.VMEM((GLA_HEADS, n, GLA_HK), F32),
            pltpu.VMEM((GLA_HEADS, n, GLA_HK), F32),
            pltpu.VMEM((GLA_HEADS, n, GLA_HK), F32),
            pltpu.VMEM((GLA_HEADS * n, GLA_SUB * GLA_HK), BF16),
            pltpu.VMEM((n, GLA_DK), F32),
            pltpu.SMEM((1,), F32),
        ],
        compiler_params=pltpu.CompilerParams(
            dimension_semantics=("arbitrary",),
            vmem_limit_bytes=VMEM_LIMIT_BYTES),
        name="gla",
    )(proj2d, proj2d, proj2d, proj2d, pa, pa, u_pad, bias, gn, ltri, e)


FF_CHUNK = 1024


def _out_kernel(x_ref, o0_ref, o1_ref, o2_ref, l0_ref, l1_ref, l2_ref, og_ref, gate_a_ref, gate_g_ref, gbias_ref,
                hx_ref, wa_ref, wb_ref, wo_ref, g2_ref, wup_ref, wdn_ref, out_ref):
    m0, m1, m2 = l0_ref[0, 0], l1_ref[0, 0], l2_ref[0, 0]
    mx = jnp.maximum(jnp.maximum(m0, m1), m2)
    e0, e1, e2 = jnp.exp2(m0 - mx), jnp.exp2(m1 - mx), jnp.exp2(m2 - mx)
    inv = 1.0 / (e0 * l0_ref[0, 1] + e1 * l1_ref[0, 1] + e2 * l2_ref[0, 1])
    hx = hx_ref[...]
    def pairs(ref):
        return jnp.concatenate([ref[0, p] for p in range(N_PAIRS)], axis=1)

    def expand(w):
        return jnp.dot(w.astype(BF16), hx, preferred_element_type=F32)

    o_attn = (expand(e0 * inv) * pairs(o0_ref) + expand(e1 * inv) * pairs(o1_ref)
              + expand(e2 * inv) * pairs(o2_ref))
    a = jnp.dot(o_attn.astype(BF16), wa_ref[...], preferred_element_type=F32)
    g = jnp.dot(og_ref[...], wb_ref[...], preferred_element_type=F32)
    gate_a = jax.nn.sigmoid(gate_a_ref[...].astype(F32) + gbias_ref[:, :D_MODEL])
    gate_g = jax.nn.sigmoid(gate_g_ref[...].astype(F32) + gbias_ref[:, D_MODEL:])
    mixed = gate_a * a + gate_g * g
    x1 = x_ref[...] + jnp.dot(mixed.astype(BF16), wo_ref[...], preferred_element_type=F32)

    ms = jnp.mean(x1 * x1, axis=-1, keepdims=True)
    h2 = (x1 * lax.rsqrt(ms + EPS) * g2_ref[...]).astype(BF16)
    hidden = []
    for c in range(D_FF // FF_CHUNK):
        u = jnp.dot(h2, wup_ref[:, c * FF_CHUNK:(c + 1) * FF_CHUNK], preferred_element_type=F32)
        u = jnp.maximum(u, 0.0)
        hidden.append((u * u).astype(BF16))
    out_ref[...] = x1 + jnp.dot(jnp.concatenate(hidden, axis=1), wdn_ref[...],
                                preferred_element_type=F32)


def _out_call(x2d, o_groups, lse_groups, o_gla, proj2d, gbias, wa, wb, wo, g2, wup, wdn, batch, tm):
    t = x2d.shape[0]
    lanes = np.arange(LANES)
    cols = np.arange(GROUP_WIDTH)
    head_expand = jnp.asarray(lanes[:, None] == cols[None, :] // HEAD_DIM, BF16)

    tiles_per_seq = t // batch // tm

    def tok(width):
        return pl.BlockSpec((tm, width), lambda i: (i, 0))

    attn_o = pl.BlockSpec((1, N_PAIRS, tm, LANES),
                          lambda i: (i // tiles_per_seq, 0, i % tiles_per_seq, 0))
    attn_lse = pl.BlockSpec((1, N_STATS, tm, LANES),
                            lambda i: (i // tiles_per_seq, 0, i % tiles_per_seq, 0))

    def const(shape):
        return pl.BlockSpec(shape, lambda i: (0, 0), pipeline_mode=pl.Buffered(1))

    return pl.pallas_call(
        _out_kernel,
        grid=(t // tm,),
        in_specs=[
            tok(D_MODEL),
            attn_o, attn_o, attn_o,
            attn_lse, attn_lse, attn_lse,
            tok(GLA_DV),
            pl.BlockSpec((tm, D_MODEL), lambda i: (i, P_GATE // D_MODEL)),
            pl.BlockSpec((tm, D_MODEL), lambda i: (i, P_GATE // D_MODEL + 1)),
            const((1, 2 * D_MODEL)),
            const((LANES, GROUP_WIDTH)),
            const((GROUP_WIDTH, D_MODEL)),
            const((GLA_DV, D_MODEL)),
            const((D_MODEL, D_MODEL)),
            const((1, D_MODEL)),
            const((D_MODEL, D_FF)),
            const((D_FF, D_MODEL)),
        ],
        out_specs=tok(D_MODEL),
        out_shape=jax.ShapeDtypeStruct((t, D_MODEL), F32),
        compiler_params=pltpu.CompilerParams(
            dimension_semantics=("parallel",),
            vmem_limit_bytes=VMEM_LIMIT_BYTES),
        name="out",
    )(x2d, *o_groups, *lse_groups, o_gla, proj2d, proj2d, gbias, head_expand, wa, wb, wo, g2, wup, wdn)


def _layer(x2d, batch, norm1_g, w_in, gq, gk, gate_up, gate_bias, gla_norm_g, branch_bias,
           w_a, w_b, w_out, norm2_g, w_up, w_down):
    g1 = norm1_g.reshape(1, D_MODEL)
    w_in_bf16 = w_in.astype(BF16)
    w_gate = w_in_bf16[:, O_GATE:O_GATE + 2 * D_MODEL]
    w_pa = jnp.pad(w_in_bf16[:, O_PA:O_PA + GLA_RANK], ((0, 0), (0, PA_PAD - GLA_RANK)))
    q_gain = jnp.tile(gq, HEADS_PER_GROUP) * (HEAD_DIM ** -0.5 * LOG2_E)
    k_gain = jnp.tile(gk, HEADS_PER_GROUP)
    qk_gain = jnp.stack([q_gain, k_gain]).reshape(2, 1, GROUP_WIDTH)

    proj, pa = _main_proj_call(x2d, g1, w_in_bf16, w_gate, w_pa, tm=PROJ_TM, rows=PROJ_ROWS)
    qkv_groups = _attn_proj_call(x2d, g1, w_in_bf16, qk_gain, batch, tm=PROJ_TM, rows=PROJ_ROWS)

    o_groups, lse_groups = [], []
    for qkv, (_, dilation) in zip(qkv_groups, ATTN_GROUPS):
        o, lse = _attn_call(qkv, dilation)
        o_groups.append(o)
        lse_groups.append(lse)

    u_pad = jnp.pad(gate_up, ((0, PA_PAD - GLA_RANK), (0, 0))).astype(BF16)
    o_gla = _gla_call(proj, pa, u_pad, gate_bias.reshape(1, GLA_DK),
                      gla_norm_g.reshape(1, GLA_HV), batch)

    return _out_call(x2d, o_groups, lse_groups, o_gla, proj,
                     branch_bias.reshape(1, 2 * D_MODEL),
                     w_a.astype(BF16), w_b.astype(BF16), w_out.astype(BF16),
                     norm2_g.reshape(1, D_MODEL), w_up.astype(BF16), w_down.astype(BF16), batch, OUT_TM)


def kernel(x, norm1_g, w_in, attn_q_norm_g, attn_k_norm_g, gla_gate_up, gla_gate_bias, gla_out_norm_g, branch_gate_bias, w_attn_branch, w_gla_branch, w_out, norm2_g, w_ff_up, w_ff_down):
    b, s, d = x.shape
    x2d = x.reshape(b * s, d)
    for l in range(norm1_g.shape[0]):
        x2d = _layer(x2d, b, norm1_g[l], w_in[l], attn_q_norm_g[l], attn_k_norm_g[l],
                     gla_gate_up[l], gla_gate_bias[l], gla_out_norm_g[l], branch_gate_bias[l],
                     w_attn_branch[l], w_gla_branch[l], w_out[l], norm2_g[l],
                     w_ff_up[l], w_ff_down[l])
    return x2d.reshape(b, s, d)
```

```python
import functools

import numpy as np
import jax
import jax.numpy as jnp
from jax import lax
from jax.experimental import pallas as pl
from jax.experimental.pallas import tpu as pltpu

F32 = jnp.float32
BF16 = jnp.bfloat16

FAST_STRIDE = 4
LANES = 128
VMEM_LIMIT_BYTES = 56 * 1024 * 1024
PROJ_TM = 1024
PROJ_ROWS = 512
OUT_TM = 512

D_MODEL = 1024
ATTN_GROUPS = ((128, 1), (512, 4), (2048, 16))
N_GROUPS = len(ATTN_GROUPS)
HEADS_PER_GROUP = 8
HEAD_DIM = 64
ATTN_BLOCK = 128
GROUP_WIDTH = HEADS_PER_GROUP * HEAD_DIM
ATTN_WIDTH = 3 * N_GROUPS * GROUP_WIDTH
N_PAIRS = GROUP_WIDTH // LANES
N_STATS = 2
ATTN_HEADS_PER_DOT = 4
ATTN_UNROLL = 16

GLA_HEADS = 4
GLA_DK = 512
GLA_DV = 1024
GLA_HK = GLA_DK // GLA_HEADS
GLA_HV = GLA_DV // GLA_HEADS
GLA_RANK = 16
GLA_TAU = 16.0
GLA_CHUNK = 64
GLA_SUB = 8
GLA_STEP = 256
GLA_SUBSTEPS = 4
GLA_SAFE_SPAN = 64.0

D_FF = 4 * D_MODEL
EPS = 1e-6
LOG2_E = 1.4426950408889634

_ORIG_SIZES = (ATTN_WIDTH, GLA_DK, GLA_DK, GLA_DV, GLA_DV, GLA_RANK, 2 * D_MODEL)
_ORIG_OFF = tuple(int(v) for v in np.cumsum((0,) + _ORIG_SIZES))
O_ATTN, O_GQ, O_GK, O_GV, O_GR, O_PA, O_GATE = _ORIG_OFF[:7]

P_GQ = 0
P_GK = P_GQ + GLA_DK
P_GV = P_GK + GLA_DK
P_GR = P_GV + GLA_DV
P_GATE = P_GR + GLA_DV
P_MAIN = P_GATE + 2 * D_MODEL
GLA_W_BLOCK = (O_GATE - GLA_RANK - O_GQ) // 2
N_KINDS = 3
PA_PAD = LANES


def _rms_norm_rows(x, gain):
    ms = jnp.mean(x * x, axis=-1, keepdims=True)
    return x * lax.rsqrt(ms + EPS) * gain


def _main_proj_kernel(x_ref, g1_ref, wa_ref, wb_ref, wgate_ref, wpa_ref, o_ref, pa_ref, *, rows):
    gain = g1_ref[...]
    for rc in range(x_ref.shape[0] // rows):
        rs = slice(rc * rows, (rc + 1) * rows)
        h = _rms_norm_rows(x_ref[rs, :], gain).astype(BF16)
        pa_ref[rs, :] = jnp.dot(h, wpa_ref[...], preferred_element_type=F32)
        col = 0
        for w_ref in (wa_ref, wb_ref, wgate_ref):
            width = w_ref.shape[1]
            o_ref[rs, col:col + width] = jnp.dot(
                h, w_ref[...], preferred_element_type=F32).astype(BF16)
            col += width


def _main_proj_call(x2d, g1, w_in_bf16, w_gate, w_pa, tm, rows):
    t = x2d.shape[0]
    first_block = O_GQ // GLA_W_BLOCK

    def const(shape, col_block=0):
        return pl.BlockSpec(shape, lambda i: (0, col_block), pipeline_mode=pl.Buffered(1))

    return pl.pallas_call(
        functools.partial(_main_proj_kernel, rows=rows),
        grid=(t // tm,),
        in_specs=[
            pl.BlockSpec((tm, D_MODEL), lambda i: (i, 0)),
            const((1, D_MODEL)),
            const((D_MODEL, GLA_W_BLOCK), first_block),
            const((D_MODEL, GLA_W_BLOCK), first_block + 1),
            const((D_MODEL, 2 * D_MODEL)),
            const((D_MODEL, PA_PAD)),
        ],
        out_specs=[
            pl.BlockSpec((tm, P_MAIN), lambda i: (i, 0)),
            pl.BlockSpec((tm, PA_PAD), lambda i: (i, 0)),
        ],
        out_shape=[
            jax.ShapeDtypeStruct((t, P_MAIN), BF16),
            jax.ShapeDtypeStruct((t, PA_PAD), F32),
        ],
        compiler_params=pltpu.CompilerParams(
            dimension_semantics=("parallel",),
            vmem_limit_bytes=VMEM_LIMIT_BYTES),
        name="proj_main",
    )(x2d, g1, w_in_bf16, w_in_bf16, w_gate, w_pa)


def _qkv_project(h, w_refs, gain_ref, o_ref, chunk, residue_of_block=None):
    d = o_ref.shape[1]
    rows = h.shape[0] // d
    residue_of_block = residue_of_block or list(range(d))
    for kind in range(N_KINDS):
        cols = slice(kind * GROUP_WIDTH, (kind + 1) * GROUP_WIDTH)
        acc = jnp.dot(h, w_refs[kind][...], preferred_element_type=F32)
        if kind < 2:
            sq = acc * acc
            first_head = lax.broadcasted_iota(jnp.int32, (1, LANES), 1) < HEAD_DIM
            scales = []
            for pp in range(N_PAIRS):
                t = sq[:, pp * LANES:(pp + 1) * LANES]
                s_lo = jnp.sum(jnp.where(first_head, t, 0.0), axis=-1, keepdims=True)
                s_hi = jnp.sum(jnp.where(first_head, 0.0, t), axis=-1, keepdims=True)
                scales.append(jnp.where(first_head,
                                        lax.rsqrt(s_lo * (1.0 / HEAD_DIM) + EPS),
                                        lax.rsqrt(s_hi * (1.0 / HEAD_DIM) + EPS)))
            acc = acc * jnp.concatenate(scales, axis=1) * gain_ref[kind]
        y = acc.astype(BF16)
        for j, r in enumerate(residue_of_block):
            o_ref[0, r, chunk * rows:(chunk + 1) * rows, cols] = y[j * rows:(j + 1) * rows, :]


def _attn_proj_kernel(x_ref, g1_ref, *rest):
    n_w = N_KINDS * N_GROUPS
    w_all = rest[:n_w]
    gain_ref, o0_ref, o1_ref, o2_ref, col_scr, col2_scr, perm1_scr, perm2_scr = rest[n_w:]
    w_group = [[w_all[kind * N_GROUPS + g] for kind in range(N_KINDS)] for g in range(N_GROUPS)]
    n_chunks, _, tc, _ = col_scr.shape
    n_col = D_MODEL // LANES
    gain = g1_ref[...]
    for ch in range(n_chunks):
        hf = _rms_norm_rows(x_ref[ch * tc:(ch + 1) * tc, :], gain)
        for c in range(n_col):
            col_scr[ch, c] = hf[:, c * LANES:(c + 1) * LANES]
        _qkv_project(hf.astype(BF16), w_group[0], gain_ref, o0_ref, ch)
        f = o1_ref.shape[1]
        assert o2_ref.shape[1] == f * f
        rows1, rows2 = tc // f, tc // (f * f)
        for a in range(f):
            for c in range(n_col):
                t = col_scr[ch, c, pl.ds(a, rows1, stride=f), :]
                col2_scr[ch, c, a * rows1:(a + 1) * rows1, :] = t
                perm1_scr[ch, a * rows1:(a + 1) * rows1, c * LANES:(c + 1) * LANES] = t.astype(BF16)
        _qkv_project(perm1_scr[ch], w_group[1], gain_ref, o1_ref, ch)
        for a in range(f):
            for b in range(f):
                blk = a * f + b
                for c in range(n_col):
                    perm2_scr[ch, blk * rows2:(blk + 1) * rows2, c * LANES:(c + 1) * LANES] = (
                        col2_scr[ch, c, pl.ds(a * rows1 + b, rows2, stride=f), :].astype(BF16))
        _qkv_project(perm2_scr[ch], w_group[2], gain_ref, o2_ref, ch,
                     residue_of_block=[b * f + a for a in range(f) for b in range(f)])


def _attn_proj_call(x2d, g1, w_in_bf16, qk_gain, batch, tm, rows):
    t = x2d.shape[0]
    s = t // batch
    tiles_per_seq = s // tm
    width = N_KINDS * GROUP_WIDTH
    n_w = N_KINDS * N_GROUPS

    def const(shape):
        return pl.BlockSpec(shape, lambda i: (0,) * len(shape), pipeline_mode=pl.Buffered(1))

    def w_block(j):
        return pl.BlockSpec((D_MODEL, GROUP_WIDTH), lambda i: (0, O_ATTN // GROUP_WIDTH + j),
                            pipeline_mode=pl.Buffered(1))

    return pl.pallas_call(
        _attn_proj_kernel,
        grid=(t // tm,),
        in_specs=[
            pl.BlockSpec((tm, D_MODEL), lambda i: (i, 0)),
            const((1, D_MODEL)),
            *[w_block(j) for j in range(n_w)],
            const((2, 1, GROUP_WIDTH)),
        ],
        out_specs=[
            pl.BlockSpec((1, d, tm // d, width),
                         lambda i: (i // tiles_per_seq, 0, i % tiles_per_seq, 0))
            for _, d in ATTN_GROUPS],
        out_shape=[jax.ShapeDtypeStruct((batch, d, s // d, width), BF16) for _, d in ATTN_GROUPS],
        scratch_shapes=[
            pltpu.VMEM((tm // rows, D_MODEL // LANES, rows, LANES), F32),
            pltpu.VMEM((tm // rows, D_MODEL // LANES, rows, LANES), F32),
            pltpu.VMEM((tm // rows, rows, D_MODEL), BF16),
            pltpu.VMEM((tm // rows, rows, D_MODEL), BF16),
        ],
        compiler_params=pltpu.CompilerParams(
            dimension_semantics=("parallel",),
            vmem_limit_bytes=VMEM_LIMIT_BYTES),
        name="proj_attn",
    )(x2d, g1, *([w_in_bf16] * n_w), qk_gain)


def _attn_kernel(q_ref, kp_ref, kc_ref, vp_ref, vc_ref, o_ref, stat_ref, *stage):
    n = pl.program_id(1)
    d = q_ref.shape[1]
    blk = ATTN_BLOCK
    n_sub = q_ref.shape[2] // blk
    stage_scr = stage[0] if stage else None
    f = FAST_STRIDE

    def put(idx, r, u, tile):
        if stage_scr is None:
            tok = pl.ds(u * blk * d + r, blk, stride=d)
            if idx >= N_PAIRS:
                stat_ref[0, idx - N_PAIRS, tok, :] = tile
            else:
                o_ref[0, idx, tok, :] = tile
        else:
            a, b = r % f, r // f
            stage_scr[idx, a, pl.ds(u * blk * (d // f) + b, blk, stride=d // f), :] = tile
    qi = lax.broadcasted_iota(jnp.int32, (blk, 2 * blk), 0)
    ki = lax.broadcasted_iota(jnp.int32, (blk, 2 * blk), 1)
    band = (ki >= qi) & (ki <= qi + blk)
    band_first = band & ((ki >= blk) | (n > 0))
    lane = lax.broadcasted_iota(jnp.int32, (blk, LANES), 1)
    first_head = lane < HEAD_DIM
    hb = ATTN_HEADS_PER_DOT
    width = hb * HEAD_DIM
    lane_w = lax.broadcasted_iota(jnp.int32, (blk, width), 1)
    head_lanes = [(lane_w >= h * HEAD_DIM) & (lane_w < (h + 1) * HEAD_DIM) for h in range(hb)]
    valid_by_sub = [jnp.concatenate([band_first if u == 0 else band] * hb, axis=0)
                    for u in range(min(n_sub, 2))]
    ones = jnp.ones((2 * blk, LANES), BF16)

    def unit(r, u):
        rows = slice(u * blk, (u + 1) * blk)
        valid_b = valid_by_sub[min(u, 1)]
        m_tile = jnp.zeros((blk, LANES), F32)
        l_tile = jnp.ones((blk, LANES), F32)
        for g in range(HEADS_PER_GROUP // hb):
            sl = slice(g * width, (g + 1) * width)
            q = q_ref[0, r, rows, sl]
            if u == 0:
                k_prev, v_prev = kp_ref[0, r, :, sl], vp_ref[0, r, :, sl]
            else:
                prev_rows = slice((u - 1) * blk, u * blk)
                k_prev, v_prev = kc_ref[0, r, prev_rows, sl], vc_ref[0, r, prev_rows, sl]
            k = jnp.concatenate([k_prev, kc_ref[0, r, rows, sl]], axis=0)
            v = jnp.concatenate([v_prev, vc_ref[0, r, rows, sl]], axis=0)
            zero = jnp.zeros_like(q)
            q_rows = jnp.concatenate([jnp.where(head_lanes[h], q, zero) for h in range(hb)],
                                     axis=0)
            s = lax.dot_general(q_rows, k, (((1,), (1,)), ((), ())), preferred_element_type=F32)
            s = jnp.where(valid_b, s, -jnp.inf)
            m = jnp.max(s, axis=-1, keepdims=True)
            p = jnp.exp2(s - m).astype(BF16)
            for pp in range(hb // 2):
                rows2 = slice(2 * pp * blk, (2 * pp + 2) * blk)
                pv = jnp.dot(p[rows2], jnp.concatenate([v[:, pp * LANES:(pp + 1) * LANES], ones], axis=1),
                             preferred_element_type=F32)
                put(g * (hb // 2) + pp, r, u, jnp.where(first_head, pv[:blk, :LANES], pv[blk:, :LANES]))
                for hh in range(2):
                    h = 2 * pp + hh
                    head = g * hb + h
                    m_tile = jnp.where(lane == head, m[h * blk:(h + 1) * blk], m_tile)
                    l_tile = jnp.where(lane == head, pv[hh * blk:(hh + 1) * blk, LANES:], l_tile)
        put(N_PAIRS, r, u, m_tile)
        put(N_PAIRS + 1, r, u, l_tile)

    if d * n_sub <= ATTN_UNROLL:
        for r in range(d):
            for u in range(n_sub):
                unit(r, u)
        if stage_scr is not None:
            for a in range(f):
                merged = pl.ds(a, stage_scr.shape[2], stride=f)
                for idx in range(N_PAIRS):
                    o_ref[0, idx, merged, :] = stage_scr[idx, a]
                for j in range(N_STATS):
                    stat_ref[0, j, merged, :] = stage_scr[N_PAIRS + j, a]
    else:
        assert stage_scr is None
        def residue(r, carry):
            for u in range(n_sub):
                unit(r, u)
            return carry
        lax.fori_loop(0, d, residue, 0, unroll=ATTN_UNROLL // n_sub)


def _attn_call(qkv, dilation):
    b, d, sub_len, _ = qkv.shape
    assert d == dilation
    blk = ATTN_BLOCK
    n_sub = max(1, ATTN_UNROLL // d)
    rows = n_sub * blk
    steps = sub_len // rows
    s = sub_len * d

    def cur(kind):
        return lambda bi, n: (bi, 0, n, kind)

    def prev(kind):
        return lambda bi, n: (bi, 0, jnp.maximum(n * n_sub - 1, 0), kind)

    cur_shape = (1, d, rows, GROUP_WIDTH)
    prev_shape = (1, d, blk, GROUP_WIDTH)
    o, stats = pl.pallas_call(
        _attn_kernel,
        grid=(b, steps),
        in_specs=[
            pl.BlockSpec(cur_shape, cur(0)),
            pl.BlockSpec(prev_shape, prev(1)),
            pl.BlockSpec(cur_shape, cur(1)),
            pl.BlockSpec(prev_shape, prev(2)),
            pl.BlockSpec(cur_shape, cur(2)),
        ],
        out_specs=[
            pl.BlockSpec((1, N_PAIRS, d * rows, LANES), lambda bi, n: (bi, 0, n, 0)),
            pl.BlockSpec((1, N_STATS, d * rows, LANES), lambda bi, n: (bi, 0, n, 0)),
        ],
        out_shape=[
            jax.ShapeDtypeStruct((b, N_PAIRS, s, LANES), F32),
            jax.ShapeDtypeStruct((b, N_STATS, s, LANES), F32),
        ],
        scratch_shapes=([pltpu.VMEM((N_PAIRS + N_STATS, FAST_STRIDE, d * rows // FAST_STRIDE, LANES), F32)]
                        if d > FAST_STRIDE else []),
        compiler_params=pltpu.CompilerParams(
            dimension_semantics=("parallel", "arbitrary"),
            vmem_limit_bytes=VMEM_LIMIT_BYTES),
        name=f"attn_d{d}",
    )(qkv, qkv, qkv, qkv, qkv)
    return o, stats


def _log_sigmoid(x):
    return jnp.minimum(x, 0.0) - jnp.log(1.0 + jnp.exp(-jnp.abs(x)))


def _split3(x):
    hi = x.astype(BF16)
    r1 = x - hi.astype(F32)
    mid = r1.astype(BF16)
    lo = (r1 - mid.astype(F32)).astype(BF16)
    return hi, mid, lo


def _gla_pairwise_products(b_scr, qs_scr, kf_scr, pcat_scr):
    sub = GLA_SUB
    tl = lax.broadcasted_iota(jnp.int32, (sub, GLA_HK), 0)
    for pair_i in range(GLA_STEP // (2 * sub)):
        tiles = []
        for r0 in (2 * pair_i * sub, (2 * pair_i + 1) * sub):
            b_blk, q_blk = b_scr[r0:r0 + sub, :], qs_scr[r0:r0 + sub, :]
            row = []
            for s in range(sub):
                d = b_blk - b_scr[r0 + s:r0 + s + 1, :]
                if s > 0:
                    d = jnp.where(tl >= s, d, -jnp.inf)
                row.append(q_blk * kf_scr[r0 + s:r0 + s + 1, :] * jnp.exp2(d))
            tiles.append(row)
        r0 = 2 * pair_i * sub
        for s in range(sub):
            pcat_scr[r0:r0 + 2 * sub, s * GLA_HK:(s + 1) * GLA_HK] = (
                jnp.concatenate([tiles[0][s], tiles[1][s]], axis=0).astype(BF16))


def _gla_reference_factors(b, qs, kf):
    c_len = GLA_CHUNK
    n_chunks = GLA_STEP // c_len
    sub = GLA_SUB

    def z(nrows):
        return jnp.zeros((nrows, GLA_HK), F32)

    qb, kb = [], []
    for r0 in range(0, GLA_STEP, 2 * sub):
        mid = r0 + sub
        b_ref = b[mid - 1:mid]
        qb += [z(sub), qs[mid:mid + sub] * jnp.exp2(b[mid:mid + sub] - b_ref)]
        kb += [kf[r0:mid] * jnp.exp2(b_ref - b[r0:mid]), z(sub)]
    q_block = jnp.concatenate(qb, axis=0).astype(BF16)
    k_block = jnp.concatenate(kb, axis=0).astype(BF16)

    qw, kw = [], []
    for c in range(n_chunks):
        bc, qc, kc = (x[c * c_len:(c + 1) * c_len] for x in (b, qs, kf))
        b15, b31, b47 = bc[15:16], bc[31:32], bc[47:48]
        q1 = jnp.concatenate([z(16), qc[16:32] * jnp.exp2(bc[16:32] - b15), z(32)], axis=0)
        q2 = jnp.concatenate([z(32), qc[32:64] * jnp.exp2(bc[32:64] - b31)], axis=0)
        q3 = jnp.concatenate([z(48), qc[48:64] * jnp.exp2(bc[48:64] - b47)], axis=0)
        k1 = jnp.concatenate([kc[0:16] * jnp.exp2(b15 - bc[0:16]), z(48)], axis=0)
        k2 = jnp.concatenate([kc[0:32] * jnp.exp2(b31 - bc[0:32]), z(32)], axis=0)
        k3 = jnp.concatenate([z(32), kc[32:48] * jnp.exp2(b47 - bc[32:48]), z(16)], axis=0)
        qw.append(jnp.concatenate([q1, q2, q3], axis=1))
        kw.append(jnp.concatenate([k1, k2, k3], axis=1))
    q_within = jnp.concatenate(qw, axis=0).astype(BF16)
    k_within = jnp.concatenate(kw, axis=0).astype(BF16)

    qx, kx = [], []
    for j in range(n_chunks - 1):
        lo, hi = j * c_len, (j + 1) * c_len
        b_ref = b[hi - 1:hi]
        qx.append(jnp.concatenate([z(hi), qs[hi:] * jnp.exp2(b[hi:] - b_ref)], axis=0))
        parts = [kf[lo:hi] * jnp.exp2(b_ref - b[lo:hi])]
        if lo:
            parts.insert(0, z(lo))
        parts.append(z(GLA_STEP - hi))
        kx.append(jnp.concatenate(parts, axis=0))
    q_cross = jnp.concatenate(qx, axis=1).astype(BF16)
    k_cross = jnp.concatenate(kx, axis=1).astype(BF16)
    return (q_block, k_block), (q_within, k_within), (q_cross, k_cross)


def _gla_decay(pa_ref, u_ref, bias_ref, ltri_ref):
    logits = jnp.dot(pa_ref[...].astype(BF16), u_ref[...], preferred_element_type=F32) + bias_ref[...]
    la = _log_sigmoid(logits) * (LOG2_E / GLA_TAU)
    parts = jnp.dot(ltri_ref[...], jnp.concatenate(_split3(la), axis=1), preferred_element_type=F32)
    return parts[:, :GLA_DK] + parts[:, GLA_DK:2 * GLA_DK] + parts[:, 2 * GLA_DK:]


def _gla_kernel(q_ref, k_ref, v_ref, r_ref, pa_ref, pa_next_ref, u_ref, bias_ref, gn_ref, ltri_ref, e_ref,
                o_ref, st_ref, b_scr, qs_scr, kf_scr, pcat_scr, decay_scr, span_scr, *, steps_per_seq):
    step = pl.program_id(0)
    n = GLA_STEP
    n_sub = q_ref.shape[0] // n

    @pl.when(lax.rem(step, steps_per_seq) == 0)
    def _():
        st_ref[...] = jnp.zeros_like(st_ref)

    def prepare(next_pa_ref):
        decay = _gla_decay(next_pa_ref, u_ref, bias_ref, ltri_ref)
        decay_scr[...] = decay
        span_scr[0] = jnp.max(-decay[n - 1:n, :])

    @pl.when(step == 0)
    def _():
        prepare(pa_ref.at[0:n])

    def run(rows, next_pa_ref):
        def body(small_span):
            _gla_substep(decay_scr[...], q_ref.at[rows], k_ref.at[rows], v_ref.at[rows],
                         r_ref.at[rows], gn_ref, e_ref, o_ref.at[rows], st_ref,
                         b_scr, qs_scr, kf_scr, pcat_scr, small_span=small_span)
            prepare(next_pa_ref)

        small_span = span_scr[0] <= GLA_SAFE_SPAN
        pl.when(small_span)(lambda: body(True))
        pl.when(jnp.logical_not(small_span))(lambda: body(False))

    def substep(ss, carry):
        rows = pl.ds(pl.multiple_of(ss * n, n), n)
        nxt = pl.ds(pl.multiple_of((ss + 1) * n, n), n)
        run(rows, pa_ref.at[nxt])
        return carry

    lax.fori_loop(0, n_sub - 1, substep, 0)
    run(slice((n_sub - 1) * n, n_sub * n), pa_next_ref)


def _gla_substep(b_all, q_ref, k_ref, v_ref, r_ref, gn_ref, e_ref,
                 o_ref, st_ref, b_scr, qs_scr, kf_scr, pcat_scr, *, small_span):
    n = GLA_STEP
    nt = (((1,), (1,)), ((), ()))

    heads = []
    for h in range(GLA_HEADS):
        kcols = slice(h * GLA_HK, (h + 1) * GLA_HK)
        b = b_all[:, kcols]
        qs = q_ref[:, kcols].astype(F32) * (GLA_HK ** -0.5)
        kf = k_ref[:, kcols].astype(F32)
        if not small_span:
            b_scr[h], qs_scr[h], kf_scr[h] = b, qs, kf
            _gla_pairwise_products(b_scr.at[h], qs_scr.at[h], kf_scr.at[h],
                                   pcat_scr.at[h * n:(h + 1) * n])
        heads.append((b, qs, kf))

    ri = lax.broadcasted_iota(jnp.int32, (n, n), 0)
    ci = lax.broadcasted_iota(jnp.int32, (n, n), 1)
    if small_span:
        causal = ri >= ci
    else:
        a_diag = jnp.dot(pcat_scr[...], e_ref[...], preferred_element_type=F32)
        same_sub = (ri // GLA_SUB) == (ci // GLA_SUB)
        same_pair = (ri // (2 * GLA_SUB)) == (ci // (2 * GLA_SUB))
        same_chunk = (ri // GLA_CHUNK) == (ci // GLA_CHUNK)
    gn = gn_ref[...]

    for h, (b, qs, kf) in enumerate(heads):
        vcols = slice(h * GLA_HV, (h + 1) * GLA_HV)
        q_in = (qs * jnp.exp2(b)).astype(BF16)
        b_last = b[n - 1:n]
        decay_all = jnp.exp2(b_last)
        if small_span:
            k_out = kf * jnp.exp2(-b)
            a = lax.dot_general(q_in, k_out.astype(BF16), nt, preferred_element_type=F32)
            a = jnp.where(causal, a.astype(BF16), 0.0)
            k_st = (k_out * decay_all).astype(BF16)
        else:
            a_block, a_within, a_cross = (
                lax.dot_general(qf, kf_, nt, preferred_element_type=F32)
                for qf, kf_ in _gla_reference_factors(b, qs, kf))
            a = jnp.where(same_sub, a_diag[h * n:(h + 1) * n],
                          jnp.where(same_pair, a_block, jnp.where(same_chunk, a_within, a_cross)))
            a = a.astype(BF16)
            k_st = (kf * jnp.exp2(b_last - b)).astype(BF16)
        v = v_ref[:, vcols]
        st = st_ref[h]
        o = (jnp.dot(a, v, preferred_element_type=F32)
             + lax.dot_general(q_in, st.astype(BF16), nt, preferred_element_type=F32))
        upd = lax.dot_general(v, k_st, (((0,), (0,)), ((), ())), preferred_element_type=F32)
        st_ref[h] = st * decay_all + upd

        ms = jnp.mean(o * o, axis=-1, keepdims=True)
        y = o * lax.rsqrt(ms + EPS) * gn
        r = r_ref[:, vcols].astype(F32)
        o_ref[:, vcols] = (y * (r * jax.nn.sigmoid(r))).astype(BF16)


def _gla_constants():
    n = GLA_STEP
    idx = np.arange(n)
    ltri = idx[:, None] >= idx[None, :]
    rows = np.arange(GLA_SUB * GLA_HK)
    cols = np.arange(n)
    e = (rows[:, None] // GLA_HK) == (cols[None, :] % GLA_SUB)
    return jnp.asarray(ltri, BF16), jnp.asarray(e, BF16)


def _gla_call(proj2d, pa, u_pad, bias, gn, batch):
    t = proj2d.shape[0]
    n = GLA_STEP
    rows = n * GLA_SUBSTEPS
    steps_per_seq = t // batch // rows
    n_steps = t // rows
    ltri, e = _gla_constants()

    def tok(width, col_block):
        return pl.BlockSpec((rows, width), lambda i: (i, col_block))

    def const(shape):
        return pl.BlockSpec(shape, lambda i: (0, 0))

    pa_next = pl.BlockSpec((n, PA_PAD), lambda i: (jnp.minimum(i + 1, n_steps - 1) * GLA_SUBSTEPS, 0))

    return pl.pallas_call(
        functools.partial(_gla_kernel, steps_per_seq=steps_per_seq),
        grid=(n_steps,),
        in_specs=[
            tok(GLA_DK, P_GQ // GLA_DK),
            tok(GLA_DK, P_GK // GLA_DK),
            tok(GLA_DV, P_GV // GLA_DV),
            tok(GLA_DV, P_GR // GLA_DV),
            tok(PA_PAD, 0),
            pa_next,
            const((PA_PAD, GLA_DK)),
            const((1, GLA_DK)),
            const((1, GLA_HV)),
            const((n, n)),
            const((GLA_SUB * GLA_HK, n)),
        ],
        out_specs=tok(GLA_DV, 0),
        out_shape=jax.ShapeDtypeStruct((t, GLA_DV), BF16),
        scratch_shapes=[
            pltpu.VMEM((GLA_HEADS, GLA_HV, GLA_HK), F32),
            pltpu.VMEM((GLA_HEADS, n, GLA_HK), F32),
            pltpu.VMEM((GLA_HEADS, n, GLA_HK), F32),
            pltpu.VMEM((GLA_HEADS, n, GLA_HK), F32),
            pltpu.VMEM((GLA_HEADS * n, GLA_SUB * GLA_HK), BF16),
            pltpu.VMEM((n, GLA_DK), F32),
            pltpu.SMEM((1,), F32),
        ],
        compiler_params=pltpu.CompilerParams(
            dimension_semantics=("arbitrary",),
            vmem_limit_bytes=VMEM_LIMIT_BYTES),
        name="gla",
    )(proj2d, proj2d, proj2d, proj2d, pa, pa, u_pad, bias, gn, ltri, e)


FF_CHUNK = 1024


def _out_kernel(x_ref, o0_ref, o1_ref, o2_ref, l0_ref, l1_ref, l2_ref, og_ref, gate_a_ref, gate_g_ref, gbias_ref,
                hx_ref, wa_ref, wb_ref, wo_ref, g2_ref, wup_ref, wdn_ref, out_ref):
    m0, m1, m2 = l0_ref[0, 0], l1_ref[0, 0], l2_ref[0, 0]
    mx = jnp.maximum(jnp.maximum(m0, m1), m2)
    e0, e1, e2 = jnp.exp2(m0 - mx), jnp.exp2(m1 - mx), jnp.exp2(m2 - mx)
    inv = 1.0 / (e0 * l0_ref[0, 1] + e1 * l1_ref[0, 1] + e2 * l2_ref[0, 1])
    hx = hx_ref[...]
    def pairs(ref):
        return jnp.concatenate([ref[0, p] for p in range(N_PAIRS)], axis=1)

    def expand(w):
        return jnp.dot(w.astype(BF16), hx, preferred_element_type=F32)

    o_attn = (expand(e0 * inv) * pairs(o0_ref) + expand(e1 * inv) * pairs(o1_ref)
              + expand(e2 * inv) * pairs(o2_ref))
    a = jnp.dot(o_attn.astype(BF16), wa_ref[...], preferred_element_type=F32)
    g = jnp.dot(og_ref[...], wb_ref[...], preferred_element_type=F32)
    gate_a = jax.nn.sigmoid(gate_a_ref[...].astype(F32) + gbias_ref[:, :D_MODEL])
    gate_g = jax.nn.sigmoid(gate_g_ref[...].astype(F32) + gbias_ref[:, D_MODEL:])
    mixed = gate_a * a + gate_g * g
    x1 = x_ref[...] + jnp.dot(mixed.astype(BF16), wo_ref[...], preferred_element_type=F32)

    ms = jnp.mean(x1 * x1, axis=-1, keepdims=True)
    h2 = (x1 * lax.rsqrt(ms + EPS) * g2_ref[...]).astype(BF16)
    hidden = []
    for c in range(D_FF // FF_CHUNK):
        u = jnp.dot(h2, wup_ref[:, c * FF_CHUNK:(c + 1) * FF_CHUNK], preferred_element_type=F32)
        u = jnp.maximum(u, 0.0)
        hidden.append((u * u).astype(BF16))
    out_ref[...] = x1 + jnp.dot(jnp.concatenate(hidden, axis=1), wdn_ref[...],
                                preferred_element_type=F32)


def _out_call(x2d, o_groups, lse_groups, o_gla, proj2d, gbias, wa, wb, wo, g2, wup, wdn, batch, tm):
    t = x2d.shape[0]
    lanes = np.arange(LANES)
    cols = np.arange(GROUP_WIDTH)
    head_expand = jnp.asarray(lanes[:, None] == cols[None, :] // HEAD_DIM, BF16)

    tiles_per_seq = t // batch // tm

    def tok(width):
        return pl.BlockSpec((tm, width), lambda i: (i, 0))

    attn_o = pl.BlockSpec((1, N_PAIRS, tm, LANES),
                          lambda i: (i // tiles_per_seq, 0, i % tiles_per_seq, 0))
    attn_lse = pl.BlockSpec((1, N_STATS, tm, LANES),
                            lambda i: (i // tiles_per_seq, 0, i % tiles_per_seq, 0))

    def const(shape):
        return pl.BlockSpec(shape, lambda i: (0, 0), pipeline_mode=pl.Buffered(1))

    return pl.pallas_call(
        _out_kernel,
        grid=(t // tm,),
        in_specs=[
            tok(D_MODEL),
            attn_o, attn_o, attn_o,
            attn_lse, attn_lse, attn_lse,
            tok(GLA_DV),
            pl.BlockSpec((tm, D_MODEL), lambda i: (i, P_GATE // D_MODEL)),
            pl.BlockSpec((tm, D_MODEL), lambda i: (i, P_GATE // D_MODEL + 1)),
            const((1, 2 * D_MODEL)),
            const((LANES, GROUP_WIDTH)),
            const((GROUP_WIDTH, D_MODEL)),
            const((GLA_DV, D_MODEL)),
            const((D_MODEL, D_MODEL)),
            const((1, D_MODEL)),
            const((D_MODEL, D_FF)),
            const((D_FF, D_MODEL)),
        ],
        out_specs=tok(D_MODEL),
        out_shape=jax.ShapeDtypeStruct((t, D_MODEL), F32),
        compiler_params=pltpu.CompilerParams(
            dimension_semantics=("parallel",),
            vmem_limit_bytes=VMEM_LIMIT_BYTES),
        name="out",
    )(x2d, *o_groups, *lse_groups, o_gla, proj2d, proj2d, gbias, head_expand, wa, wb, wo, g2, wup, wdn)


def _layer(x2d, batch, norm1_g, w_in, gq, gk, gate_up, gate_bias, gla_norm_g, branch_bias,
           w_a, w_b, w_out, norm2_g, w_up, w_down):
    g1 = norm1_g.reshape(1, D_MODEL)
    w_in_bf16 = w_in.astype(BF16)
    w_gate = w_in_bf16[:, O_GATE:O_GATE + 2 * D_MODEL]
    w_pa = jnp.pad(w_in_bf16[:, O_PA:O_PA + GLA_RANK], ((0, 0), (0, PA_PAD - GLA_RANK)))
    q_gain = jnp.tile(gq, HEADS_PER_GROUP) * (HEAD_DIM ** -0.5 * LOG2_E)
    k_gain = jnp.tile(gk, HEADS_PER_GROUP)
    qk_gain = jnp.stack([q_gain, k_gain]).reshape(2, 1, GROUP_WIDTH)

    proj, pa = _main_proj_call(x2d, g1, w_in_bf16, w_gate, w_pa, tm=PROJ_TM, rows=PROJ_ROWS)
    qkv_groups = _attn_proj_call(x2d, g1, w_in_bf16, qk_gain, batch, tm=PROJ_TM, rows=PROJ_ROWS)

    o_groups, lse_groups = [], []
    for qkv, (_, dilation) in zip(qkv_groups, ATTN_GROUPS):
        o, lse = _attn_call(qkv, dilation)
        o_groups.append(o)
        lse_groups.append(lse)

    u_pad = jnp.pad(gate_up, ((0, PA_PAD - GLA_RANK), (0, 0))).astype(BF16)
    o_gla = _gla_call(proj, pa, u_pad, gate_bias.reshape(1, GLA_DK),
                      gla_norm_g.reshape(1, GLA_HV), batch)

    return _out_call(x2d, o_groups, lse_groups, o_gla, proj,
                     branch_bias.reshape(1, 2 * D_MODEL),
                     w_a.astype(BF16), w_b.astype(BF16), w_out.astype(BF16),
                     norm2_g.reshape(1, D_MODEL), w_up.astype(BF16), w_down.astype(BF16), batch, OUT_TM)


def kernel(x, norm1_g, w_in, attn_q_norm_g, attn_k_norm_g, gla_gate_up, gla_gate_bias, gla_out_norm_g, branch_gate_bias, w_attn_branch, w_gla_branch, w_out, norm2_g, w_ff_up, w_ff_down):
    b, s, d = x.shape
    x2d = x.reshape(b * s, d)
    for l in range(norm1_g.shape[0]):
        x2d = _layer(x2d, b, norm1_g[l], w_in[l], attn_q_norm_g[l], attn_k_norm_g[l],
                     gla_gate_up[l], gla_gate_bias[l], gla_out_norm_g[l], branch_gate_bias[l],
                     w_attn_branch[l], w_gla_branch[l], w_out[l], norm2_g[l],
                     w_ff_up[l], w_ff_down[l])
    return x2d.reshape(b, s, d)
```
